```python
import jax, jax.numpy as jnp
from jax import lax
import numpy as np

D_MODEL = 2048
BATCH = 4
SEQ = 2048
DEPTH = 2
DEC_BATCH = 8
DEC_SEQ = 4
PAST_LEN = 16384
PAGE_SIZE = 128

HEAD_DIM = 128
ROT_DIM = HEAD_DIM // 4
ROPE_THETA = 500000.0
NORM_EPS = 1e-6
MOBA_HEADS = 8
MOBA_KV_HEADS = 2
MOBA_GROUP = MOBA_HEADS // MOBA_KV_HEADS
MOBA_BLOCK = 256
MOBA_TOPK = 3
MOBA_QCHUNK = 16
NSA_HEADS = 8
NSA_KV_HEADS = 2
NSA_GROUP = NSA_HEADS // NSA_KV_HEADS
CMP_LEN = 32
CMP_STRIDE = 16
CMP_HIDDEN = 128
SLC_BLOCK = 64
SLC_TOPN = 16
WINDOW = 512
NSA_QCHUNK = 16
WIN_QBLOCK = 128
N_GROUPS = 4
EXPERTS_PER_GROUP = 8
N_EXPERTS = N_GROUPS * EXPERTS_PER_GROUP
EXPERT_TOPK = 2
D_EXPERT = 512
MOE_BLOCK = 128

NEG_INF = -1e30
BIG = 1e30
F32 = jnp.float32

kernel_name = 'hybrid_moba_nsa_hmoe_adaln_step'


def in_proj_sizes():
    return (MOBA_HEADS * HEAD_DIM, MOBA_KV_HEADS * HEAD_DIM, MOBA_KV_HEADS * HEAD_DIM,
            NSA_HEADS * HEAD_DIM, 6 * NSA_KV_HEADS * HEAD_DIM, 3 * NSA_HEADS, D_MODEL, D_MODEL)


def rms_norm(x, g):
    xf = x.astype(F32)
    y = xf * lax.rsqrt(jnp.mean(xf * xf, axis=-1, keepdims=True) + NORM_EPS)
    return (y * g.astype(F32)).astype(x.dtype)


def apply_rope(x, pos):
    half = ROT_DIM // 2
    inv_freq = ROPE_THETA ** (-jnp.arange(half, dtype=F32) / half)
    ang = pos.astype(F32)[:, None] * inv_freq[None, :]
    ang = ang.reshape((ang.shape[0],) + (1,) * (x.ndim - 3) + (half,))
    cos, sin = jnp.cos(ang), jnp.sin(ang)
    xr = x[..., :ROT_DIM].astype(F32)
    x1, x2 = xr[..., :half], xr[..., half:]
    rot = jnp.concatenate([x1 * cos - x2 * sin, x2 * cos + x1 * sin], axis=-1)
    return jnp.concatenate([rot.astype(x.dtype), x[..., ROT_DIM:]], axis=-1)


def query_block(s, pref):
    return pref if s % pref == 0 else s


def gather_pages(pool, page_table):
    g = pool[page_table]
    return g.reshape((g.shape[0], g.shape[1] * g.shape[2]) + g.shape[3:])


def moba_attention(q, k_all, v_all, q0):
    B, S, KVH, G, dh = q.shape
    L = k_all.shape[1]
    nb = -(-L // MOBA_BLOCK)
    pad = ((0, 0), (0, nb * MOBA_BLOCK - L), (0, 0), (0, 0))
    kb = jnp.pad(k_all, pad).reshape(B, nb, MOBA_BLOCK, KVH, dh).transpose(0, 3, 1, 2, 4)
    vb = jnp.pad(v_all, pad).reshape(B, nb, MOBA_BLOCK, KVH, dh).transpose(0, 3, 1, 2, 4)
    k_mean = jnp.mean(kb, axis=3, dtype=F32)
    n_sel = min(MOBA_TOPK, nb)
    qc = query_block(S, MOBA_QCHUNK)
    nq = S // qc
    scale = HEAD_DIM ** -0.5
    bi = jnp.arange(B)[:, None, None, None, None]
    gi = jnp.arange(KVH)[None, None, :, None, None]
    blk_ids = jnp.arange(nb)
    in_blk = jnp.arange(MOBA_BLOCK)

    def one_chunk(args):
        qch, start = args
        t = start + jnp.arange(qc)
        own = t // MOBA_BLOCK
        gate = jnp.einsum('bqghd,bgnd->bqghn', qch.astype(F32), k_mean)
        gate = jnp.where((blk_ids[None, :] < own[:, None])[None, :, None, None, :], gate, NEG_INF)
        _, sel = lax.top_k(gate, n_sel)
        slot_ok = (jnp.arange(n_sel)[None, :] < own[:, None])[None, :, None, None, :, None]
        k_sel = kb[bi, gi, sel]
        v_sel = vb[bi, gi, sel]
        s_sel = jnp.einsum('bqghd,bqghskd->bqghsk', qch, k_sel, preferred_element_type=F32) * scale
        s_sel = jnp.where(slot_ok, s_sel, NEG_INF).reshape(B, qc, KVH, G, n_sel * MOBA_BLOCK)
        k_own = kb[:, :, own]
        v_own = vb[:, :, own]
        s_own = jnp.einsum('bqghd,bgqkd->bqghk', qch, k_own, preferred_element_type=F32) * scale
        causal = (own[:, None] * MOBA_BLOCK + in_blk[None, :]) <= t[:, None]
        s_own = jnp.where(causal[None, :, None, None, :], s_own, NEG_INF)
        p = jax.nn.softmax(jnp.concatenate([s_sel, s_own], axis=-1), axis=-1).astype(v_all.dtype)
        p_sel = p[..., :n_sel * MOBA_BLOCK].reshape(B, qc, KVH, G, n_sel, MOBA_BLOCK)
        p_own = p[..., n_sel * MOBA_BLOCK:]
        return (jnp.einsum('bqghsk,bqghskd->bqghd', p_sel, v_sel)
                + jnp.einsum('bqghk,bgqkd->bqghd', p_own, v_own))

    q_chunks = q.reshape(B, nq, qc, KVH, G, dh).transpose(1, 0, 2, 3, 4, 5)
    o = lax.map(one_chunk, (q_chunks, q0 + qc * jnp.arange(nq)))
    return o.transpose(1, 0, 2, 3, 4, 5).reshape(B, S, KVH * G * dh)


def compress_rows(x, w_pos, w1, b1, w2, b2):
    B, L, KVH, dh = x.shape
    r = CMP_LEN // CMP_STRIDE
    n_chunks = L // CMP_STRIDE
    n_cmp = n_chunks - r + 1
    ch = x[:, :n_chunks * CMP_STRIDE].reshape(B, n_chunks, CMP_STRIDE, KVH, dh)
    blocks = jnp.concatenate([ch[:, j:j + n_cmp] for j in range(r)], axis=2)
    blocks = (blocks + w_pos[None, None, :, None, :]).transpose(0, 1, 3, 2, 4)
    flat = blocks.reshape(B, n_cmp, KVH, CMP_LEN * dh)
    return jax.nn.gelu(flat @ w1 + b1) @ w2 + b2


def cmp_to_slc_map(n_cmp, n_slc):
    lo = np.arange(n_cmp) * CMP_STRIDE
    blo = np.arange(n_slc) * SLC_BLOCK
    m = (lo[:, None] < blo[None, :] + SLC_BLOCK) & (lo[:, None] + CMP_LEN > blo[None, :])
    return jnp.asarray(m.astype(np.float32))


def nsa_compressed_selected(q, k_cmp, v_cmp, k_slc, v_slc, q0):
    B, S, KVH, G, dh = q.shape
    L = k_slc.shape[1]
    n_cmp = k_cmp.shape[1]
    n_slc = -(-L // SLC_BLOCK)
    n_top = min(SLC_TOPN, n_slc)
    pad = ((0, 0), (0, n_slc * SLC_BLOCK - L), (0, 0), (0, 0))
    kb = jnp.pad(k_slc, pad).reshape(B, n_slc, SLC_BLOCK, KVH, dh).transpose(0, 3, 1, 2, 4)
    vb = jnp.pad(v_slc, pad).reshape(B, n_slc, SLC_BLOCK, KVH, dh).transpose(0, 3, 1, 2, 4)
    sel_map = cmp_to_slc_map(n_cmp, n_slc)
    cmp_last = jnp.arange(n_cmp) * CMP_STRIDE + CMP_LEN - 1
    blk_ids = jnp.arange(n_slc)
    in_blk = jnp.arange(SLC_BLOCK)
    qc = query_block(S, NSA_QCHUNK)
    nq = S // qc
    scale = HEAD_DIM ** -0.5
    bi = jnp.arange(B)[:, None, None, None]
    gi = jnp.arange(KVH)[None, None, :, None]

    def one_chunk(args):
        qch, start = args
        t = start + jnp.arange(qc)
        avail = (cmp_last[None, :] <= t[:, None])[None, :, None, None, :]
        s_c = jnp.einsum('bqghd,bngd->bqghn', qch, k_cmp, preferred_element_type=F32) * scale
        p_c = jnp.where(avail, jax.nn.softmax(jnp.where(avail, s_c, NEG_INF), axis=-1), 0.0)
        o_c = jnp.einsum('bqghn,bngd->bqghd', p_c.astype(v_cmp.dtype), v_cmp)
        imp = jnp.einsum('bqghn,nj->bqgj', p_c, sel_map)
        cur = (t // SLC_BLOCK)[:, None]
        forced = (blk_ids == 0) | (blk_ids == cur) | (blk_ids == cur - 1)
        imp = jnp.where(forced[None, :, None, :], BIG, imp)
        imp = jnp.where((blk_ids > cur)[None, :, None, :], NEG_INF, imp)
        _, sel = lax.top_k(imp, n_top)
        k_sel = kb[bi, gi, sel]
        v_sel = vb[bi, gi, sel]
        key_pos = sel[..., None] * SLC_BLOCK + in_blk
        ok = key_pos <= t[None, :, None, None, None]
        s_s = jnp.einsum('bqghd,bqgskd->bqghsk', qch, k_sel, preferred_element_type=F32) * scale
        s_s = jnp.where(ok[:, :, :, None], s_s, NEG_INF).reshape(B, qc, KVH, G, n_top * SLC_BLOCK)
        p_s = jax.nn.softmax(s_s, axis=-1).astype(v_slc.dtype).reshape(B, qc, KVH, G, n_top, SLC_BLOCK)
        o_s = jnp.einsum('bqghsk,bqgskd->bqghd', p_s, v_sel)
        return o_c, o_s

    q_chunks = q.reshape(B, nq, qc, KVH, G, dh).transpose(1, 0, 2, 3, 4, 5)
    o_c, o_s = lax.map(one_chunk, (q_chunks, q0 + qc * jnp.arange(nq)))
    o_c = o_c.transpose(1, 0, 2, 3, 4, 5).reshape(B, S, KVH, G, dh)
    o_s = o_s.transpose(1, 0, 2, 3, 4, 5).reshape(B, S, KVH, G, dh)
    return o_c, o_s


def window_attention(q, k_win, v_win, q0, k0):
    B, S, KVH, G, dh = q.shape
    qb = query_block(S, WIN_QBLOCK)
    nq = S // qb
    span = WINDOW + qb
    pad = ((0, 0), (WINDOW, 0), (0, 0), (0, 0))
    kp = jnp.pad(k_win, pad)
    vp = jnp.pad(v_win, pad)
    scale = HEAD_DIM ** -0.5

    def one_block(args):
        qblk, i = args
        s0 = q0 - k0 + i * qb
        kk = lax.dynamic_slice_in_dim(kp, s0, span, axis=1)
        vv = lax.dynamic_slice_in_dim(vp, s0, span, axis=1)
        t = q0 + i * qb + jnp.arange(qb)
        kpos = k0 - WINDOW + s0 + jnp.arange(span)
        ok = ((kpos[None, :] > t[:, None] - WINDOW) & (kpos[None, :] <= t[:, None])
              & (kpos[None, :] >= k0))
        s = jnp.einsum('bqghd,bkgd->bqghk', qblk, kk, preferred_element_type=F32) * scale
        s = jnp.where(ok[None, :, None, None, :], s, NEG_INF)
        p = jax.nn.softmax(s, axis=-1).astype(vv.dtype)
        return jnp.einsum('bqghk,bkgd->bqghd', p, vv)

    q_blocks = q.reshape(B, nq, qb, KVH, G, dh).transpose(1, 0, 2, 3, 4, 5)
    o = lax.map(one_block, (q_blocks, jnp.arange(nq)))
    return o.transpose(1, 0, 2, 3, 4, 5).reshape(B, S, KVH, G, dh)


def token_mixer(h, q0, past, lp):
    B, S, _ = h.shape
    cuts = [int(c) for c in np.cumsum(in_proj_sizes())[:-1]]
    q_a, k_a, v_a, q_b, kv_b, g_b, g_ma, g_mb = jnp.split(h @ lp['w_in'], cuts, axis=-1)
    pos = q0 + jnp.arange(S)
    q_a = apply_rope(q_a.reshape(B, S, MOBA_KV_HEADS, MOBA_GROUP, HEAD_DIM), pos)
    k_a = apply_rope(k_a.reshape(B, S, MOBA_KV_HEADS, HEAD_DIM), pos)
    v_a = v_a.reshape(B, S, MOBA_KV_HEADS, HEAD_DIM)
    q_b = apply_rope(q_b.reshape(B, S, NSA_KV_HEADS, NSA_GROUP, HEAD_DIM), pos)
    kv_b = kv_b.reshape(B, S, 3, 2, NSA_KV_HEADS, HEAD_DIM)
    new_moba = jnp.stack([k_a, v_a], axis=2)
    new_cmp = kv_b[:, :, 0]
    new_slc = jnp.stack([apply_rope(kv_b[:, :, 1, 0], pos), kv_b[:, :, 1, 1]], axis=2)
    new_win = jnp.stack([apply_rope(kv_b[:, :, 2, 0], pos), kv_b[:, :, 2, 1]], axis=2)
    if past is None:
        moba_kv, cmp_kv, slc_kv, win_kv = new_moba, new_cmp, new_slc, new_win
    else:
        moba_kv = jnp.concatenate([past[0], new_moba], axis=1)
        cmp_kv = jnp.concatenate([past[1], new_cmp], axis=1)
        slc_kv = jnp.concatenate([past[2], new_slc], axis=1)
        win_kv = jnp.concatenate([past[3], new_win], axis=1)
    k0 = q0 + S - win_kv.shape[1]
    win_state = win_kv[:, win_kv.shape[1] - min(WINDOW, win_kv.shape[1]):]

    o_a = moba_attention(q_a, moba_kv[:, :, 0], moba_kv[:, :, 1], q0)
    k_c = compress_rows(cmp_kv[:, :, 0], lp['cmp_pos'][0], lp['cmp_w1'][0], lp['cmp_b1'][0],
                        lp['cmp_w2'][0], lp['cmp_b2'][0])
    v_c = compress_rows(cmp_kv[:, :, 1], lp['cmp_pos'][1], lp['cmp_w1'][1], lp['cmp_b1'][1],
                        lp['cmp_w2'][1], lp['cmp_b2'][1])
    o_c, o_s = nsa_compressed_selected(q_b, k_c, v_c, slc_kv[:, :, 0], slc_kv[:, :, 1], q0)
    o_w = window_attention(q_b, win_kv[:, :, 0], win_kv[:, :, 1], q0, k0)
    gates = jax.nn.sigmoid(g_b.reshape(B, S, 3, NSA_KV_HEADS, NSA_GROUP, 1))
    o_b = (gates[:, :, 0] * o_c + gates[:, :, 1] * o_s + gates[:, :, 2] * o_w).reshape(B, S, NSA_HEADS * HEAD_DIM)
    merged = jax.nn.sigmoid(g_ma) * (o_a @ lp['w_pa']) + jax.nn.sigmoid(g_mb) * (o_b @ lp['w_pb'])
    return merged @ lp['w_out'], (new_moba, new_cmp, new_slc, win_state)


def routed_experts(x, expert_ids, weights, w_gate, w_up, w_down):
    T, D = x.shape
    A = T * EXPERT_TOPK
    flat_e = expert_ids.reshape(A)
    order = jnp.argsort(flat_e)
    e_sorted = flat_e[order]
    tok_sorted = order // EXPERT_TOPK
    counts = jnp.zeros((N_EXPERTS,), jnp.int32).at[flat_e].add(1)
    padded = (counts + MOE_BLOCK - 1) // MOE_BLOCK * MOE_BLOCK
    pad_end = jnp.cumsum(padded)
    pad_start = pad_end - padded
    start = jnp.cumsum(counts) - counts
    dest = pad_start[e_sorted] + jnp.arange(A) - start[e_sorted]
    n_blk = -(-(A + N_EXPERTS * (MOE_BLOCK - 1)) // MOE_BLOCK)
    row_tok = jnp.zeros((n_blk * MOE_BLOCK,), jnp.int32).at[dest].set(tok_sorted)
    blk_expert = jnp.minimum(jnp.searchsorted(pad_end, jnp.arange(n_blk) * MOE_BLOCK, side='right'),
                             N_EXPERTS - 1)
    xb = x[row_tok].reshape(n_blk, MOE_BLOCK, D)

    def expert_block(args):
        xe, e = args
        return (jax.nn.silu(xe @ w_gate[e]) * (xe @ w_up[e])) @ w_down[e]

    yb = lax.map(expert_block, (xb, blk_expert)).reshape(n_blk * MOE_BLOCK, D)
    contrib = yb[dest] * weights.reshape(A)[order][:, None].astype(x.dtype)
    return jnp.zeros((T, D), x.dtype).at[tok_sorted].add(contrib)


def hierarchical_moe(h, lp):
    B, S, D = h.shape
    x = h.reshape(B * S, D)
    g_prob = jax.nn.softmax((x @ lp['w_rg']).astype(F32) + lp['b_rg'].astype(F32), axis=-1)
    g_w, g_idx = lax.top_k(g_prob, 1)
    e_logit = ((x @ lp['w_re']).astype(F32) + lp['b_re'].astype(F32)).reshape(-1, N_GROUPS, EXPERTS_PER_GROUP)
    e_logit = jnp.take_along_axis(e_logit, g_idx[:, :, None], axis=1)[:, 0]
    e_val, e_idx = lax.top_k(e_logit, EXPERT_TOPK)
    weights = g_w * jax.nn.softmax(e_val, axis=-1)
    expert_ids = g_idx * EXPERTS_PER_GROUP + e_idx
    y = routed_experts(x, expert_ids, weights, lp['w_gate'], lp['w_up'], lp['w_down'])
    return y.reshape(B, S, D)


def decoder_layer(x, c, q0, past, lp):
    ada = jax.nn.silu(c) @ lp['w_ada'] + lp['b_ada']
    sh1, sc1, g1, sh2, sc2, g2 = jnp.split(ada[:, None, :], 6, axis=-1)
    h = rms_norm(x, lp['norm_attn']) * (1 + sc1) + sh1
    mix, rows = token_mixer(h, q0, past, lp)
    x = x + g1 * mix
    h = rms_norm(x, lp['norm_ffn']) * (1 + sc2) + sh2
    x = x + g2 * hierarchical_moe(h, lp)
    return x, rows


def setup_inputs(seed: int = 0) -> dict:
    keys = jax.random.split(jax.random.key(seed), 32)

    def nrm(i, shape, scale=1.0):
        return jax.random.normal(keys[i], shape, F32) * scale

    n_pages = PAST_LEN // PAGE_SIZE
    n_pool = (5 * DEC_BATCH * n_pages) // 4
    w_buf = min(WINDOW, PAST_LEN)
    d_in = sum(in_proj_sizes())
    page_table = jax.random.permutation(keys[6], n_pool)[:DEC_BATCH * n_pages]
    page_table = page_table.reshape(DEC_BATCH, n_pages).astype(jnp.int32)
    return {
        'x_prompt': nrm(0, (BATCH, SEQ, D_MODEL)),
        'x_sample': nrm(1, (DEC_BATCH, DEC_SEQ, D_MODEL)),
        'cache_moba_kv': nrm(2, (DEPTH, n_pool, PAGE_SIZE, 2, MOBA_KV_HEADS, HEAD_DIM)),
        'cache_cmp_kv': nrm(3, (DEPTH, n_pool, PAGE_SIZE, 2, NSA_KV_HEADS, HEAD_DIM)),
        'cache_slc_kv': nrm(4, (DEPTH, n_pool, PAGE_SIZE, 2, NSA_KV_HEADS, HEAD_DIM)),
        'state_win_kv': nrm(5, (DEPTH, DEC_BATCH, w_buf, 2, NSA_KV_HEADS, HEAD_DIM)),
        'page_table': page_table,
        'c_prompt': nrm(7, (BATCH, D_MODEL)),
        'c_sample': nrm(8, (DEC_BATCH, D_MODEL)),
        'w_in': nrm(9, (DEPTH, D_MODEL, d_in), D_MODEL ** -0.5),
        'w_pa': nrm(10, (DEPTH, MOBA_HEADS * HEAD_DIM, D_MODEL), (MOBA_HEADS * HEAD_DIM) ** -0.5),
        'w_pb': nrm(11, (DEPTH, NSA_HEADS * HEAD_DIM, D_MODEL), (NSA_HEADS * HEAD_DIM) ** -0.5),
        'w_out': nrm(12, (DEPTH, D_MODEL, D_MODEL), D_MODEL ** -0.5),
        'cmp_pos': nrm(13, (DEPTH, 2, CMP_LEN, HEAD_DIM), 0.1),
        'cmp_w1': nrm(14, (DEPTH, 2, CMP_LEN * HEAD_DIM, CMP_HIDDEN), (CMP_LEN * HEAD_DIM) ** -0.5),
        'cmp_b1': nrm(15, (DEPTH, 2, CMP_HIDDEN), 0.01),
        'cmp_w2': nrm(16, (DEPTH, 2, CMP_HIDDEN, HEAD_DIM), CMP_HIDDEN ** -0.5),
        'cmp_b2': nrm(17, (DEPTH, 2, HEAD_DIM), 0.01),
        'norm_attn': 1.0 + nrm(18, (DEPTH, D_MODEL), 0.05),
        'norm_ffn': 1.0 + nrm(19, (DEPTH, D_MODEL), 0.05),
        'norm_final': 1.0 + nrm(20, (D_MODEL,), 0.05),
        'w_ada': nrm(21, (DEPTH, D_MODEL, 6 * D_MODEL), 0.5 * D_MODEL ** -0.5),
        'b_ada': nrm(22, (DEPTH, 6 * D_MODEL), 0.01),
        'w_rg': nrm(23, (DEPTH, D_MODEL, N_GROUPS), D_MODEL ** -0.5),
        'b_rg': nrm(24, (DEPTH, N_GROUPS), 0.01),
        'w_re': nrm(25, (DEPTH, D_MODEL, N_EXPERTS), D_MODEL ** -0.5),
        'b_re': nrm(26, (DEPTH, N_EXPERTS), 0.01),
        'w_gate': nrm(27, (DEPTH, N_EXPERTS, D_MODEL, D_EXPERT), D_MODEL ** -0.5),
        'w_up': nrm(28, (DEPTH, N_EXPERTS, D_MODEL, D_EXPERT), D_MODEL ** -0.5),
        'w_down': nrm(29, (DEPTH, N_EXPERTS, D_EXPERT, D_MODEL), D_EXPERT ** -0.5),
    }


def reference(x_prompt, x_sample, cache_moba_kv, cache_cmp_kv, cache_slc_kv, state_win_kv,
              page_table, c_prompt, c_sample, w_in, w_pa, w_pb, w_out, cmp_pos, cmp_w1, cmp_b1,
              cmp_w2, cmp_b2, norm_attn, norm_ffn, norm_final, w_ada, b_ada, w_rg, b_rg, w_re,
              b_re, w_gate, w_up, w_down):
    past_len = page_table.shape[1] * cache_moba_kv.shape[2]
    y_p, y_s = x_prompt, x_sample
    rows_p, rows_s = [], []
    for l in range(DEPTH):
        lp = {'w_in': w_in[l], 'w_pa': w_pa[l], 'w_pb': w_pb[l], 'w_out': w_out[l],
              'cmp_pos': cmp_pos[l], 'cmp_w1': cmp_w1[l], 'cmp_b1': cmp_b1[l],
              'cmp_w2': cmp_w2[l], 'cmp_b2': cmp_b2[l], 'norm_attn': norm_attn[l],
              'norm_ffn': norm_ffn[l], 'w_ada': w_ada[l], 'b_ada': b_ada[l],
              'w_rg': w_rg[l], 'b_rg': b_rg[l], 'w_re': w_re[l], 'b_re': b_re[l],
              'w_gate': w_gate[l], 'w_up': w_up[l], 'w_down': w_down[l]}
        y_p, r_p = decoder_layer(y_p, c_prompt, 0, None, lp)
        past = (gather_pages(cache_moba_kv[l], page_table),
                gather_pages(cache_cmp_kv[l], page_table),
                gather_pages(cache_slc_kv[l], page_table),
                state_win_kv[l])
        y_s, r_s = decoder_layer(y_s, c_sample, past_len, past, lp)
        rows_p.append(r_p)
        rows_s.append(r_s)
    y_prompt = rms_norm(y_p, norm_final)
    y_sample = rms_norm(y_s, norm_final)
    return (y_prompt, y_sample,
            jnp.stack([r[0] for r in rows_p]), jnp.stack([r[0] for r in rows_s]),
            jnp.stack([r[1] for r in rows_p]), jnp.stack([r[1] for r in rows_s]),
            jnp.stack([r[2] for r in rows_p]), jnp.stack([r[2] for r in rows_s]),
            jnp.stack([r[3] for r in rows_p]), jnp.stack([r[3] for r in rows_s]))
```

```python
import functools

import jax
import jax.numpy as jnp
import numpy as np
from jax import lax
from jax.experimental import pallas as pl
from jax.experimental.pallas import tpu as pltpu

D_MODEL = 2048
DEPTH = 2
HEAD_DIM = 128
ROT_DIM = HEAD_DIM // 4
ROPE_THETA = 500000.0
NORM_EPS = 1e-6
MOBA_HEADS = 8
MOBA_KV_HEADS = 2
MOBA_GROUP = MOBA_HEADS // MOBA_KV_HEADS
MOBA_BLOCK = 256
MOBA_TOPK = 3
MOBA_QCHUNK = 16
NSA_HEADS = 8
NSA_KV_HEADS = 2
NSA_GROUP = NSA_HEADS // NSA_KV_HEADS
CMP_LEN = 32
CMP_STRIDE = 16
CMP_HIDDEN = 128
SLC_BLOCK = 64
SLC_TOPN = 16
WINDOW = 512
NSA_QCHUNK = 16
WIN_QBLOCK = 128
N_GROUPS = 4
EXPERTS_PER_GROUP = 8
N_EXPERTS = N_GROUPS * EXPERTS_PER_GROUP
EXPERT_TOPK = 2
D_EXPERT = 512
MOE_BLOCK = 128

NEG_INF = -1e30
BIG = 1e30
F32 = jnp.float32
BF16 = jnp.bfloat16

LANE = 128
VMEM_LIMIT_BYTES = 48 * 1024 * 1024


def _round_up(n, m):
    return (n + m - 1) // m * m


def _mm_kernel(x_ref, w_ref, o_ref):
    o_ref[...] = jnp.dot(x_ref[...].astype(BF16), w_ref[...].astype(BF16),
                         preferred_element_type=F32)


def _mm(x, w, tm=512, tn=512):
    m, k = x.shape
    n = w.shape[1]
    n_pad = _round_up(n, LANE)
    if n_pad != n:
        w = jnp.pad(w, ((0, 0), (0, n_pad - n)))
    tm = min(tm, m)
    tn = min(tn, n_pad)
    while n_pad % tn:
        tn -= LANE
    assert m % tm == 0
    out = pl.pallas_call(
        _mm_kernel,
        grid=(m // tm, n_pad // tn),
        in_specs=[pl.BlockSpec((tm, k), lambda i, j: (i, 0)),
                  pl.BlockSpec((k, tn), lambda i, j: (0, j))],
        out_specs=pl.BlockSpec((tm, tn), lambda i, j: (i, j)),
        out_shape=jax.ShapeDtypeStruct((m, n_pad), F32),
        compiler_params=pltpu.CompilerParams(
            dimension_semantics=("arbitrary", "arbitrary"),
            vmem_limit_bytes=VMEM_LIMIT_BYTES),
        name="dense_mm",
    )(x, w)
    return out[:, :n] if n_pad != n else out


def _mm3(x, w):
    b, s, k = x.shape
    return _mm(x.reshape(b * s, k), w).reshape(b, s, w.shape[1])


def in_proj_sizes():
    return (MOBA_HEADS * HEAD_DIM, MOBA_KV_HEADS * HEAD_DIM, MOBA_KV_HEADS * HEAD_DIM,
            NSA_HEADS * HEAD_DIM, 6 * NSA_KV_HEADS * HEAD_DIM, 3 * NSA_HEADS, D_MODEL, D_MODEL)


def rms_norm(x, g):
    xf = x.astype(F32)
    y = xf * lax.rsqrt(jnp.mean(xf * xf, axis=-1, keepdims=True) + NORM_EPS)
    return (y * g.astype(F32)).astype(x.dtype)


def apply_rope(x, pos):
    half = ROT_DIM // 2
    inv_freq = ROPE_THETA ** (-jnp.arange(half, dtype=F32) / half)
    ang = pos.astype(F32)[:, None] * inv_freq[None, :]
    ang = ang.reshape((ang.shape[0],) + (1,) * (x.ndim - 3) + (half,))
    cos, sin = jnp.cos(ang), jnp.sin(ang)
    xr = x[..., :ROT_DIM].astype(F32)
    x1, x2 = xr[..., :half], xr[..., half:]
    rot = jnp.concatenate([x1 * cos - x2 * sin, x2 * cos + x1 * sin], axis=-1)
    return jnp.concatenate([rot.astype(x.dtype), x[..., ROT_DIM:]], axis=-1)


def query_block(s, pref):
    return pref if s % pref == 0 else s


def gather_pages(pool, page_table):
    g = pool[page_table]
    return g.reshape((g.shape[0], g.shape[1] * g.shape[2]) + g.shape[3:])


def moba_attention(q, k_all, v_all, q0):
    B, S, KVH, G, dh = q.shape
    L = k_all.shape[1]
    nb = -(-L // MOBA_BLOCK)
    pad = ((0, 0), (0, nb * MOBA_BLOCK - L), (0, 0), (0, 0))
    kb = jnp.pad(k_all, pad).reshape(B, nb, MOBA_BLOCK, KVH, dh).transpose(0, 3, 1, 2, 4)
    vb = jnp.pad(v_all, pad).reshape(B, nb, MOBA_BLOCK, KVH, dh).transpose(0, 3, 1, 2, 4)
    k_mean = jnp.mean(kb, axis=3, dtype=F32)
    n_sel = min(MOBA_TOPK, nb)
    qc = query_block(S, MOBA_QCHUNK)
    nq = S // qc
    scale = HEAD_DIM ** -0.5
    bi = jnp.arange(B)[:, None, None, None, None]
    gi = jnp.arange(KVH)[None, None, :, None, None]
    blk_ids = jnp.arange(nb)
    in_blk = jnp.arange(MOBA_BLOCK)

    def one_chunk(args):
        qch, start = args
        t = start + jnp.arange(qc)
        own = t // MOBA_BLOCK
        gate = jnp.einsum('bqghd,bgnd->bqghn', qch.astype(F32), k_mean)
        gate = jnp.where((blk_ids[None, :] < own[:, None])[None, :, None, None, :], gate, NEG_INF)
        _, sel = lax.top_k(gate, n_sel)
        slot_ok = (jnp.arange(n_sel)[None, :] < own[:, None])[None, :, None, None, :, None]
        k_sel = kb[bi, gi, sel]
        v_sel = vb[bi, gi, sel]
        s_sel = jnp.einsum('bqghd,bqghskd->bqghsk', qch, k_sel, preferred_element_type=F32) * scale
        s_sel = jnp.where(slot_ok, s_sel, NEG_INF).reshape(B, qc, KVH, G, n_sel * MOBA_BLOCK)
        k_own = kb[:, :, own]
        v_own = vb[:, :, own]
        s_own = jnp.einsum('bqghd,bgqkd->bqghk', qch, k_own, preferred_element_type=F32) * scale
        causal = (own[:, None] * MOBA_BLOCK + in_blk[None, :]) <= t[:, None]
        s_own = jnp.where(causal[None, :, None, None, :], s_own, NEG_INF)
        p = jax.nn.softmax(jnp.concatenate([s_sel, s_own], axis=-1), axis=-1).astype(v_all.dtype)
        p_sel = p[..., :n_sel * MOBA_BLOCK].reshape(B, qc, KVH, G, n_sel, MOBA_BLOCK)
        p_own = p[..., n_sel * MOBA_BLOCK:]
        return (jnp.einsum('bqghsk,bqghskd->bqghd', p_sel, v_sel)
                + jnp.einsum('bqghk,bgqkd->bqghd', p_own, v_own))

    q_chunks = q.reshape(B, nq, qc, KVH, G, dh).transpose(1, 0, 2, 3, 4, 5)
    o = lax.map(one_chunk, (q_chunks, q0 + qc * jnp.arange(nq)))
    return o.transpose(1, 0, 2, 3, 4, 5).reshape(B, S, KVH * G * dh)


def compress_rows(x, w_pos, w1, b1, w2, b2):
    B, L, KVH, dh = x.shape
    r = CMP_LEN // CMP_STRIDE
    n_chunks = L // CMP_STRIDE
    n_cmp = n_chunks - r + 1
    ch = x[:, :n_chunks * CMP_STRIDE].reshape(B, n_chunks, CMP_STRIDE, KVH, dh)
    blocks = jnp.concatenate([ch[:, j:j + n_cmp] for j in range(r)], axis=2)
    blocks = (blocks + w_pos[None, None, :, None, :]).transpose(0, 1, 3, 2, 4)
    flat = blocks.reshape(B, n_cmp, KVH, CMP_LEN * dh)
    return jax.nn.gelu(flat @ w1 + b1) @ w2 + b2


def cmp_to_slc_map(n_cmp, n_slc):
    lo = np.arange(n_cmp) * CMP_STRIDE
    blo = np.arange(n_slc) * SLC_BLOCK
    m = (lo[:, None] < blo[None, :] + SLC_BLOCK) & (lo[:, None] + CMP_LEN > blo[None, :])
    return jnp.asarray(m.astype(np.float32))


def nsa_compressed_selected(q, k_cmp, v_cmp, k_slc, v_slc, q0):
    B, S, KVH, G, dh = q.shape
    L = k_slc.shape[1]
    n_cmp = k_cmp.shape[1]
    n_slc = -(-L // SLC_BLOCK)
    n_top = min(SLC_TOPN, n_slc)
    pad = ((0, 0), (0, n_slc * SLC_BLOCK - L), (0, 0), (0, 0))
    kb = jnp.pad(k_slc, pad).reshape(B, n_slc, SLC_BLOCK, KVH, dh).transpose(0, 3, 1, 2, 4)
    vb = jnp.pad(v_slc, pad).reshape(B, n_slc, SLC_BLOCK, KVH, dh).transpose(0, 3, 1, 2, 4)
    sel_map = cmp_to_slc_map(n_cmp, n_slc)
    cmp_last = jnp.arange(n_cmp) * CMP_STRIDE + CMP_LEN - 1
    blk_ids = jnp.arange(n_slc)
    in_blk = jnp.arange(SLC_BLOCK)
    qc = query_block(S, NSA_QCHUNK)
    nq = S // qc
    scale = HEAD_DIM ** -0.5
    bi = jnp.arange(B)[:, None, None, None]
    gi = jnp.arange(KVH)[None, None, :, None]

    def one_chunk(args):
        qch, start = args
        t = start + jnp.arange(qc)
        avail = (cmp_last[None, :] <= t[:, None])[None, :, None, None, :]
        s_c = jnp.einsum('bqghd,bngd->bqghn', qch, k_cmp, preferred_element_type=F32) * scale
        p_c = jnp.where(avail, jax.nn.softmax(jnp.where(avail, s_c, NEG_INF), axis=-1), 0.0)
        o_c = jnp.einsum('bqghn,bngd->bqghd', p_c.astype(v_cmp.dtype), v_cmp)
        imp = jnp.einsum('bqghn,nj->bqgj', p_c, sel_map)
        cur = (t // SLC_BLOCK)[:, None]
        forced = (blk_ids == 0) | (blk_ids == cur) | (blk_ids == cur - 1)
        imp = jnp.where(forced[None, :, None, :], BIG, imp)
        imp = jnp.where((blk_ids > cur)[None, :, None, :], NEG_INF, imp)
        _, sel = lax.top_k(imp, n_top)
        k_sel = kb[bi, gi, sel]
        v_sel = vb[bi, gi, sel]
        key_pos = sel[..., None] * SLC_BLOCK + in_blk
        ok = key_pos <= t[None, :, None, None, None]
        s_s = jnp.einsum('bqghd,bqgskd->bqghsk', qch, k_sel, preferred_element_type=F32) * scale
        s_s = jnp.where(ok[:, :, :, None], s_s, NEG_INF).reshape(B, qc, KVH, G, n_top * SLC_BLOCK)
        p_s = jax.nn.softmax(s_s, axis=-1).astype(v_slc.dtype).reshape(B, qc, KVH, G, n_top, SLC_BLOCK)
        o_s = jnp.einsum('bqghsk,bqgskd->bqghd', p_s, v_sel)
        return o_c, o_s

    q_chunks = q.reshape(B, nq, qc, KVH, G, dh).transpose(1, 0, 2, 3, 4, 5)
    o_c, o_s = lax.map(one_chunk, (q_chunks, q0 + qc * jnp.arange(nq)))
    o_c = o_c.transpose(1, 0, 2, 3, 4, 5).reshape(B, S, KVH, G, dh)
    o_s = o_s.transpose(1, 0, 2, 3, 4, 5).reshape(B, S, KVH, G, dh)
    return o_c, o_s


def window_attention(q, k_win, v_win, q0, k0):
    B, S, KVH, G, dh = q.shape
    qb = query_block(S, WIN_QBLOCK)
    nq = S // qb
    span = WINDOW + qb
    pad = ((0, 0), (WINDOW, 0), (0, 0), (0, 0))
    kp = jnp.pad(k_win, pad)
    vp = jnp.pad(v_win, pad)
    scale = HEAD_DIM ** -0.5

    def one_block(args):
        qblk, i = args
        s0 = q0 - k0 + i * qb
        kk = lax.dynamic_slice_in_dim(kp, s0, span, axis=1)
        vv = lax.dynamic_slice_in_dim(vp, s0, span, axis=1)
        t = q0 + i * qb + jnp.arange(qb)
        kpos = k0 - WINDOW + s0 + jnp.arange(span)
        ok = ((kpos[None, :] > t[:, None] - WINDOW) & (kpos[None, :] <= t[:, None])
              & (kpos[None, :] >= k0))
        s = jnp.einsum('bqghd,bkgd->bqghk', qblk, kk, preferred_element_type=F32) * scale
        s = jnp.where(ok[None, :, None, None, :], s, NEG_INF)
        p = jax.nn.softmax(s, axis=-1).astype(vv.dtype)
        return jnp.einsum('bqghk,bkgd->bqghd', p, vv)

    q_blocks = q.reshape(B, nq, qb, KVH, G, dh).transpose(1, 0, 2, 3, 4, 5)
    o = lax.map(one_block, (q_blocks, jnp.arange(nq)))
    return o.transpose(1, 0, 2, 3, 4, 5).reshape(B, S, KVH, G, dh)


def token_mixer(h, q0, past, lp):
    B, S, _ = h.shape
    cuts = [int(c) for c in np.cumsum(in_proj_sizes())[:-1]]
    q_a, k_a, v_a, q_b, kv_b, g_b, g_ma, g_mb = jnp.split(_mm3(h, lp['w_in']), cuts, axis=-1)
    pos = q0 + jnp.arange(S)
    q_a = apply_rope(q_a.reshape(B, S, MOBA_KV_HEADS, MOBA_GROUP, HEAD_DIM), pos)
    k_a = apply_rope(k_a.reshape(B, S, MOBA_KV_HEADS, HEAD_DIM), pos)
    v_a = v_a.reshape(B, S, MOBA_KV_HEADS, HEAD_DIM)
    q_b = apply_rope(q_b.reshape(B, S, NSA_KV_HEADS, NSA_GROUP, HEAD_DIM), pos)
    kv_b = kv_b.reshape(B, S, 3, 2, NSA_KV_HEADS, HEAD_DIM)
    new_moba = jnp.stack([k_a, v_a], axis=2)
    new_cmp = kv_b[:, :, 0]
    new_slc = jnp.stack([apply_rope(kv_b[:, :, 1, 0], pos), kv_b[:, :, 1, 1]], axis=2)
    new_win = jnp.stack([apply_rope(kv_b[:, :, 2, 0], pos), kv_b[:, :, 2, 1]], axis=2)
    if past is None:
        moba_kv, cmp_kv, slc_kv, win_kv = new_moba, new_cmp, new_slc, new_win
    else:
        moba_kv = jnp.concatenate([past[0], new_moba], axis=1)
        cmp_kv = jnp.concatenate([past[1], new_cmp], axis=1)
        slc_kv = jnp.concatenate([past[2], new_slc], axis=1)
        win_kv = jnp.concatenate([past[3], new_win], axis=1)
    k0 = q0 + S - win_kv.shape[1]
    win_state = win_kv[:, win_kv.shape[1] - min(WINDOW, win_kv.shape[1]):]

    o_a = moba_attention(q_a, moba_kv[:, :, 0], moba_kv[:, :, 1], q0)
    k_c = compress_rows(cmp_kv[:, :, 0], lp['cmp_pos'][0], lp['cmp_w1'][0], lp['cmp_b1'][0],
                        lp['cmp_w2'][0], lp['cmp_b2'][0])
    v_c = compress_rows(cmp_kv[:, :, 1], lp['cmp_pos'][1], lp['cmp_w1'][1], lp['cmp_b1'][1],
                        lp['cmp_w2'][1], lp['cmp_b2'][1])
    o_c, o_s = nsa_compressed_selected(q_b, k_c, v_c, slc_kv[:, :, 0], slc_kv[:, :, 1], q0)
    o_w = window_attention(q_b, win_kv[:, :, 0], win_kv[:, :, 1], q0, k0)
    gates = jax.nn.sigmoid(g_b.reshape(B, S, 3, NSA_KV_HEADS, NSA_GROUP, 1))
    o_b = (gates[:, :, 0] * o_c + gates[:, :, 1] * o_s + gates[:, :, 2] * o_w).reshape(B, S, NSA_HEADS * HEAD_DIM)
    merged = (jax.nn.sigmoid(g_ma) * _mm3(o_a, lp['w_pa'])
              + jax.nn.sigmoid(g_mb) * _mm3(o_b, lp['w_pb']))
    return _mm3(merged, lp['w_out']), (new_moba, new_cmp, new_slc, win_state)


def routed_experts(x, expert_ids, weights, w_gate, w_up, w_down):
    T, D = x.shape
    A = T * EXPERT_TOPK
    flat_e = expert_ids.reshape(A)
    order = jnp.argsort(flat_e)
    e_sorted = flat_e[order]
    tok_sorted = order // EXPERT_TOPK
    counts = jnp.zeros((N_EXPERTS,), jnp.int32).at[flat_e].add(1)
    padded = (counts + MOE_BLOCK - 1) // MOE_BLOCK * MOE_BLOCK
    pad_end = jnp.cumsum(padded)
    pad_start = pad_end - padded
    start = jnp.cumsum(counts) - counts
    dest = pad_start[e_sorted] + jnp.arange(A) - start[e_sorted]
    n_blk = -(-(A + N_EXPERTS * (MOE_BLOCK - 1)) // MOE_BLOCK)
    row_tok = jnp.zeros((n_blk * MOE_BLOCK,), jnp.int32).at[dest].set(tok_sorted)
    blk_expert = jnp.minimum(jnp.searchsorted(pad_end, jnp.arange(n_blk) * MOE_BLOCK, side='right'),
                             N_EXPERTS - 1)
    xb = x[row_tok].reshape(n_blk, MOE_BLOCK, D)

    def expert_block(args):
        xe, e = args
        return (jax.nn.silu(xe @ w_gate[e]) * (xe @ w_up[e])) @ w_down[e]

    yb = lax.map(expert_block, (xb, blk_expert)).reshape(n_blk * MOE_BLOCK, D)
    contrib = yb[dest] * weights.reshape(A)[order][:, None].astype(x.dtype)
    return jnp.zeros((T, D), x.dtype).at[tok_sorted].add(contrib)


def hierarchical_moe(h, lp):
    B, S, D = h.shape
    x = h.reshape(B * S, D)
    g_prob = jax.nn.softmax((x @ lp['w_rg']).astype(F32) + lp['b_rg'].astype(F32), axis=-1)
    g_w, g_idx = lax.top_k(g_prob, 1)
    e_logit = ((x @ lp['w_re']).astype(F32) + lp['b_re'].astype(F32)).reshape(-1, N_GROUPS, EXPERTS_PER_GROUP)
    e_logit = jnp.take_along_axis(e_logit, g_idx[:, :, None], axis=1)[:, 0]
    e_val, e_idx = lax.top_k(e_logit, EXPERT_TOPK)
    weights = g_w * jax.nn.softmax(e_val, axis=-1)
    expert_ids = g_idx * EXPERTS_PER_GROUP + e_idx
    y = routed_experts(x, expert_ids, weights, lp['w_gate'], lp['w_up'], lp['w_down'])
    return y.reshape(B, S, D)


def decoder_layer(x, c, q0, past, lp):
    ada = _mm(jax.nn.silu(c), lp['w_ada'], tn=2048) + lp['b_ada']
    sh1, sc1, g1, sh2, sc2, g2 = jnp.split(ada[:, None, :], 6, axis=-1)
    h = rms_norm(x, lp['norm_attn']) * (1 + sc1) + sh1
    mix, rows = token_mixer(h, q0, past, lp)
    x = x + g1 * mix
    h = rms_norm(x, lp['norm_ffn']) * (1 + sc2) + sh2
    x = x + g2 * hierarchical_moe(h, lp)
    return x, rows


def kernel(x_prompt, x_sample, cache_moba_kv, cache_cmp_kv, cache_slc_kv, state_win_kv,
           page_table, c_prompt, c_sample, w_in, w_pa, w_pb, w_out, cmp_pos, cmp_w1, cmp_b1,
           cmp_w2, cmp_b2, norm_attn, norm_ffn, norm_final, w_ada, b_ada, w_rg, b_rg, w_re,
           b_re, w_gate, w_up, w_down):
    past_len = page_table.shape[1] * cache_moba_kv.shape[2]
    y_p, y_s = x_prompt, x_sample
    rows_p, rows_s = [], []
    for l in range(DEPTH):
        lp = {'w_in': w_in[l], 'w_pa': w_pa[l], 'w_pb': w_pb[l], 'w_out': w_out[l],
              'cmp_pos': cmp_pos[l], 'cmp_w1': cmp_w1[l], 'cmp_b1': cmp_b1[l],
              'cmp_w2': cmp_w2[l], 'cmp_b2': cmp_b2[l], 'norm_attn': norm_attn[l],
              'norm_ffn': norm_ffn[l], 'w_ada': w_ada[l], 'b_ada': b_ada[l],
              'w_rg': w_rg[l], 'b_rg': b_rg[l], 'w_re': w_re[l], 'b_re': b_re[l],
              'w_gate': w_gate[l], 'w_up': w_up[l], 'w_down': w_down[l]}
        y_p, r_p = decoder_layer(y_p, c_prompt, 0, None, lp)
        past = (gather_pages(cache_moba_kv[l], page_table),
                gather_pages(cache_cmp_kv[l], page_table),
                gather_pages(cache_slc_kv[l], page_table),
                state_win_kv[l])
        y_s, r_s = decoder_layer(y_s, c_sample, past_len, past, lp)
        rows_p.append(r_p)
        rows_s.append(r_s)
    y_prompt = rms_norm(y_p, norm_final)
    y_sample = rms_norm(y_s, norm_final)
    return (y_prompt, y_sample,
            jnp.stack([r[0] for r in rows_p]), jnp.stack([r[0] for r in rows_s]),
            jnp.stack([r[1] for r in rows_p]), jnp.stack([r[1] for r in rows_s]),
            jnp.stack([r[2] for r in rows_p]), jnp.stack([r[2] for r in rows_s]),
            jnp.stack([r[3] for r in rows_p]), jnp.stack([r[3] for r in rows_s]))
```

```python
import functools

import jax
import jax.numpy as jnp
import numpy as np
from jax import lax
from jax.experimental import pallas as pl
from jax.experimental.pallas import tpu as pltpu

D_MODEL = 2048
DEPTH = 2
HEAD_DIM = 128
ROT_DIM = HEAD_DIM // 4
ROPE_THETA = 500000.0
NORM_EPS = 1e-6
MOBA_HEADS = 8
MOBA_KV_HEADS = 2
MOBA_GROUP = MOBA_HEADS // MOBA_KV_HEADS
MOBA_BLOCK = 256
MOBA_TOPK = 3
MOBA_QCHUNK = 16
NSA_HEADS = 8
NSA_KV_HEADS = 2
NSA_GROUP = NSA_HEADS // NSA_KV_HEADS
CMP_LEN = 32
CMP_STRIDE = 16
CMP_HIDDEN = 128
SLC_BLOCK = 64
SLC_TOPN = 16
WINDOW = 512
NSA_QCHUNK = 16
WIN_QBLOCK = 128
N_GROUPS = 4
EXPERTS_PER_GROUP = 8
N_EXPERTS = N_GROUPS * EXPERTS_PER_GROUP
EXPERT_TOPK = 2
D_EXPERT = 512
MOE_BLOCK = 128

NEG_INF = -1e30
BIG = 1e30
F32 = jnp.float32
BF16 = jnp.bfloat16

LANE = 128
VMEM_LIMIT_BYTES = 48 * 1024 * 1024


def _round_up(n, m):
    return (n + m - 1) // m * m


def _mm_kernel(x_ref, w_ref, o_ref):
    o_ref[...] = jnp.dot(x_ref[...].astype(BF16), w_ref[...].astype(BF16),
                         preferred_element_type=F32)


def _mm(x, w, tm=512, tn=512):
    m, k = x.shape
    n = w.shape[1]
    n_pad = _round_up(n, LANE)
    if n_pad != n:
        w = jnp.pad(w, ((0, 0), (0, n_pad - n)))
    tm = min(tm, m)
    tn = min(tn, n_pad)
    while n_pad % tn:
        tn -= LANE
    assert m % tm == 0
    out = pl.pallas_call(
        _mm_kernel,
        grid=(m // tm, n_pad // tn),
        in_specs=[pl.BlockSpec((tm, k), lambda i, j: (i, 0)),
                  pl.BlockSpec((k, tn), lambda i, j: (0, j))],
        out_specs=pl.BlockSpec((tm, tn), lambda i, j: (i, j)),
        out_shape=jax.ShapeDtypeStruct((m, n_pad), F32),
        compiler_params=pltpu.CompilerParams(
            dimension_semantics=("arbitrary", "arbitrary"),
            vmem_limit_bytes=VMEM_LIMIT_BYTES),
        name="dense_mm",
    )(x, w)
    return out[:, :n] if n_pad != n else out


def _mm3(x, w):
    b, s, k = x.shape
    return _mm(x.reshape(b * s, k), w).reshape(b, s, w.shape[1])


ATT_SCALE = HEAD_DIM ** -0.5
_NT = (((1,), (1,)), ((), ()))


def _flash_update(qb, k, v, mask, carry):
    m, l, acc = carry
    s = lax.dot_general(qb, k.astype(BF16), _NT, preferred_element_type=F32) * ATT_SCALE
    s = jnp.where(mask, s, NEG_INF)
    m_new = jnp.maximum(m, jnp.max(s, axis=-1, keepdims=True))
    alpha = jnp.exp(m - m_new)
    p = jnp.exp(s - m_new)
    l = alpha * l + jnp.sum(p, axis=-1, keepdims=True)
    acc = alpha * acc + jnp.dot(p.astype(BF16), v.astype(BF16), preferred_element_type=F32)
    return m_new, l, acc


def _flash_init(tq):
    return (jnp.full((tq, 1), -jnp.inf, F32), jnp.zeros((tq, 1), F32),
            jnp.zeros((tq, HEAD_DIM), F32))


def _moba_prompt_kernel(q_ref, k_ref, v_ref, o_ref, kmean_scr, sel_scr, *, nb):
    i = pl.program_id(2)
    tq = MOBA_BLOCK

    @pl.when(i == 0)
    def _():
        kmean_scr[...] = jnp.zeros_like(kmean_scr)
        for j in range(nb):
            kmean_scr[j:j + 1, :] = jnp.mean(k_ref[0, j * tq:(j + 1) * tq, :], axis=0, keepdims=True)

    lane = lax.broadcasted_iota(jnp.int32, (tq, LANE), 1)
    row = lax.broadcasted_iota(jnp.int32, (tq, tq), 0)
    col = lax.broadcasted_iota(jnp.int32, (tq, tq), 1)
    own0 = pl.multiple_of(i * tq, tq)
    for g in range(MOBA_GROUP):
        qg = q_ref[0, :, g * HEAD_DIM:(g + 1) * HEAD_DIM]
        gate = lax.dot_general(qg, kmean_scr[...], _NT, precision=lax.Precision.HIGHEST,
                               preferred_element_type=F32)
        gate = jnp.where(lane < i, gate, NEG_INF)
        for j in range(nb):
            gj = gate[:, j:j + 1]
            better = ((gate > gj) | ((gate == gj) & (lane < j))) & (lane < nb)
            cnt = jnp.sum(jnp.where(better, 1.0, 0.0), axis=-1, keepdims=True)
            selj = jnp.where((cnt < MOBA_TOPK) & (j < i), 1.0, 0.0)
            sel_scr[g, j] = jnp.broadcast_to(selj, (tq, LANE))

        qb = qg.astype(BF16)
        carry = _flash_update(qb, k_ref[0, pl.ds(own0, tq), :], v_ref[0, pl.ds(own0, tq), :],
                              col <= row, _flash_init(tq))

        def body(j, carry, qb=qb, g=g):
            off = pl.multiple_of(j * tq, tq)
            selb = sel_scr[g, j]
            mask = jnp.concatenate([selb, selb], axis=1) > 0.5
            return _flash_update(qb, k_ref[0, pl.ds(off, tq), :], v_ref[0, pl.ds(off, tq), :],
                                 mask, carry)

        m, l, acc = lax.fori_loop(0, i, body, carry)
        o_ref[0, :, g * HEAD_DIM:(g + 1) * HEAD_DIM] = acc / l


def _moba_prompt(q, kv):
    b, s, _ = q.shape
    nb = s // MOBA_BLOCK
    assert s % MOBA_BLOCK == 0 and nb <= LANE
    gw = MOBA_GROUP * HEAD_DIM
    return pl.pallas_call(
        functools.partial(_moba_prompt_kernel, nb=nb),
        grid=(b, MOBA_KV_HEADS, nb),
        in_specs=[pl.BlockSpec((1, MOBA_BLOCK, gw), lambda bi, h, i: (bi, i, h)),
                  pl.BlockSpec((1, s, HEAD_DIM), lambda bi, h, i: (bi, 0, h)),
                  pl.BlockSpec((1, s, HEAD_DIM), lambda bi, h, i: (bi, 0, MOBA_KV_HEADS + h))],
        out_specs=pl.BlockSpec((1, MOBA_BLOCK, gw), lambda bi, h, i: (bi, i, h)),
        out_shape=jax.ShapeDtypeStruct(q.shape, F32),
        scratch_shapes=[pltpu.VMEM((LANE, HEAD_DIM), F32),
                        pltpu.VMEM((MOBA_GROUP, nb, MOBA_BLOCK, LANE), F32)],
        compiler_params=pltpu.CompilerParams(
            dimension_semantics=("arbitrary", "arbitrary", "arbitrary"),
            vmem_limit_bytes=VMEM_LIMIT_BYTES),
        name="moba_prompt",
    )(q, kv, kv)


KV_COLS = 2 * NSA_KV_HEADS * HEAD_DIM


def _compress_kernel(x_ref, pos_ref, w1_ref, b1_ref, w2_ref, b2_ref, o_ref, *, nch):
    for kv in range(2):
        for h in range(NSA_KV_HEADS):
            c = kv * NSA_KV_HEADS + h
            first = jnp.zeros((nch, CMP_HIDDEN), F32)
            second = jnp.zeros((nch, CMP_HIDDEN), F32)
            for r in range(CMP_STRIDE):
                lo = r * KV_COLS + c * HEAD_DIM
                xr = x_ref[0, :, lo:lo + HEAD_DIM]
                r2 = CMP_STRIDE + r
                first += jnp.dot((xr + pos_ref[kv, r:r + 1, :]).astype(BF16),
                                 w1_ref[kv, r * HEAD_DIM:(r + 1) * HEAD_DIM, :].astype(BF16),
                                 preferred_element_type=F32)
                second += jnp.dot((xr + pos_ref[kv, r2:r2 + 1, :]).astype(BF16),
                                  w1_ref[kv, r2 * HEAD_DIM:(r2 + 1) * HEAD_DIM, :].astype(BF16),
                                  preferred_element_type=F32)
            pre = first + pltpu.roll(second, nch - 1, axis=0) + b1_ref[kv:kv + 1, :]
            hdn = jax.nn.gelu(pre)
            o_ref[0, c] = (jnp.dot(hdn.astype(BF16), w2_ref[kv].astype(BF16),
                                   preferred_element_type=F32) + b2_ref[kv:kv + 1, :])


def _compress_prompt(kv_cmp, pos, w1, b1, w2, b2):
    b, s, _ = kv_cmp.shape
    nch = s // CMP_STRIDE
    x = kv_cmp.reshape(b, nch, CMP_STRIDE * KV_COLS)
    full = lambda shape: pl.BlockSpec(shape, lambda bi: (0,) * len(shape))
    return pl.pallas_call(
        functools.partial(_compress_kernel, nch=nch),
        grid=(b,),
        in_specs=[pl.BlockSpec((1, nch, CMP_STRIDE * KV_COLS), lambda bi: (bi, 0, 0)),
                  full(pos.shape), full(w1.shape), full(b1.shape), full(w2.shape), full(b2.shape)],
        out_specs=pl.BlockSpec((1, 2 * NSA_KV_HEADS, nch, HEAD_DIM), lambda bi: (bi, 0, 0, 0)),
        out_shape=jax.ShapeDtypeStruct((b, 2 * NSA_KV_HEADS, nch, HEAD_DIM), F32),
        compiler_params=pltpu.CompilerParams(dimension_semantics=("arbitrary",),
                                             vmem_limit_bytes=VMEM_LIMIT_BYTES),
        name="nsa_compress",
    )(x, pos, w1, b1, w2, b2)


NSA_TQ = 256
NSA_KB = 256


def _nsa_prompt_kernel(q_ref, kc_ref, vc_ref, ks_ref, vs_ref, kw_ref, vw_ref, gb_ref, o_ref,
                       mask_scr, *, s_len):
    kvh = pl.program_id(1)
    i = pl.program_id(2)
    tq, kb = NSA_TQ, NSA_KB
    n_cmp = s_len // CMP_STRIDE - CMP_LEN // CMP_STRIDE + 1
    n_slc = s_len // SLC_BLOCK
    n_top = min(SLC_TOPN, n_slc)
    nkb = s_len // kb
    dh = HEAD_DIM

    lane = lax.broadcasted_iota(jnp.int32, (tq, LANE), 1)
    t_col = i * tq + lax.broadcasted_iota(jnp.int32, (tq, 1), 0)
    avail = (lane * CMP_STRIDE + (CMP_LEN - 1) <= t_col) & (lane < n_cmp)
    nn = lax.broadcasted_iota(jnp.int32, (LANE, LANE), 0)
    jj = lax.broadcasted_iota(jnp.int32, (LANE, LANE), 1)
    sel_map = jnp.where((nn * CMP_STRIDE < jj * SLC_BLOCK + SLC_BLOCK)
                        & (nn * CMP_STRIDE + CMP_LEN > jj * SLC_BLOCK)
                        & (nn < n_cmp) & (jj < n_slc), 1.0, 0.0)
    gates = jax.nn.sigmoid(gb_ref[0])

    def gate_col(branch, g):
        c = branch * NSA_HEADS + kvh * NSA_GROUP + g
        return jnp.sum(jnp.where(lane == c, gates, 0.0), axis=-1, keepdims=True)

    imp = jnp.zeros((tq, LANE), F32)
    kc = kc_ref[0, 0].astype(BF16)
    vc = vc_ref[0, 0].astype(BF16)
    for g in range(NSA_GROUP):
        qb = q_ref[0, :, g * dh:(g + 1) * dh].astype(BF16)
        s_c = lax.dot_general(qb, kc, _NT, preferred_element_type=F32) * ATT_SCALE
        s_c = jnp.where(avail, s_c, NEG_INF)
        e = jnp.where(avail, jnp.exp(s_c - jnp.max(s_c, axis=-1, keepdims=True)), 0.0)
        den = jnp.sum(e, axis=-1, keepdims=True)
        p_c = e * jnp.where(den > 0.0, 1.0 / den, 0.0)
        o_c = jnp.dot(p_c.astype(BF16), vc, preferred_element_type=F32)
        imp = imp + jnp.dot(p_c, sel_map, precision=lax.Precision.HIGHEST,
                            preferred_element_type=F32)
        o_ref[0, :, g * dh:(g + 1) * dh] = gate_col(0, g) * o_c

    n_rows = _round_up(n_slc, 8)
    imp_t = imp.T[:n_rows]
    jio = lax.broadcasted_iota(jnp.int32, (n_rows, tq), 0)
    cur = (i * tq + lax.broadcasted_iota(jnp.int32, (n_rows, tq), 1)) // SLC_BLOCK
    forced = (jio == 0) | (jio == cur) | (jio == cur - 1)
    imp_t = jnp.where(forced, BIG, imp_t)
    imp_t = jnp.where(jio > cur, NEG_INF, imp_t)
    imp_t = jnp.where(jio < n_slc, imp_t, -jnp.inf)
    cnt = jnp.zeros((n_rows, tq), F32)
    for r in range(n_slc):
        vr = imp_t[r:r + 1, :]
        cnt = cnt + jnp.where((vr > imp_t) | ((vr == imp_t) & (r < jio)), 1.0, 0.0)
    sel_t = jnp.where((cnt < n_top) & (jio < n_slc), 1.0, 0.0)
    if n_rows < LANE:
        sel_t = jnp.concatenate([sel_t, jnp.zeros((LANE - n_rows, tq), F32)], axis=0)
    sel = sel_t.T.astype(BF16)
    erow = lax.broadcasted_iota(jnp.int32, (LANE, kb), 0)
    ecol = lax.broadcasted_iota(jnp.int32, (LANE, kb), 1)
    for jb in range(nkb):
        expand = jnp.where((jb * kb + ecol) // SLC_BLOCK == erow, 1.0, 0.0).astype(BF16)
        mask_scr[jb] = jnp.dot(sel, expand, preferred_element_type=F32)

    rowk = lax.broadcasted_iota(jnp.int32, (tq, kb), 0)
    colk = lax.broadcasted_iota(jnp.int32, (tq, kb), 1)
    causal = colk <= rowk
    own0 = pl.multiple_of(i * kb, kb)
    for g in range(NSA_GROUP):
        qb = q_ref[0, :, g * dh:(g + 1) * dh].astype(BF16)

        carry = _flash_update(qb, ks_ref[0, pl.ds(own0, kb), :], vs_ref[0, pl.ds(own0, kb), :],
                              (mask_scr[i] > 0.5) & causal, _flash_init(tq))

        def slc_body(j, carry, qb=qb):
            off = pl.multiple_of(j * kb, kb)
            return _flash_update(qb, ks_ref[0, pl.ds(off, kb), :], vs_ref[0, pl.ds(off, kb), :],
                                 mask_scr[j] > 0.5, carry)

        _, l, acc = lax.fori_loop(0, i, slc_body, carry)
        o_s = acc / l

        carry = _flash_update(qb, kw_ref[0, pl.ds(own0, kb), :], vw_ref[0, pl.ds(own0, kb), :],
                              causal, _flash_init(tq))

        def win_body(j, carry, qb=qb):
            off = pl.multiple_of(j * kb, kb)
            mask = (j * kb + colk) > (i * tq + rowk - WINDOW)
            return _flash_update(qb, kw_ref[0, pl.ds(off, kb), :], vw_ref[0, pl.ds(off, kb), :],
                                 mask, carry)

        _, l, acc = lax.fori_loop(jnp.maximum(i - WINDOW // kb, 0), i, win_body, carry)
        o_w = acc / l
        o_ref[0, :, g * dh:(g + 1) * dh] += gate_col(1, g) * o_s + gate_col(2, g) * o_w


def _nsa_prompt(q, kvc, kv_slc, kv_win, g_b):
    b, s, _ = q.shape
    assert s % NSA_TQ == 0 and NSA_TQ == NSA_KB and s // CMP_STRIDE == LANE
    gw = NSA_GROUP * HEAD_DIM
    nkh = NSA_KV_HEADS
    seq_k = pl.BlockSpec((1, s, HEAD_DIM), lambda bi, h, i: (bi, 0, h))
    seq_v = pl.BlockSpec((1, s, HEAD_DIM), lambda bi, h, i: (bi, 0, nkh + h))
    return pl.pallas_call(
        functools.partial(_nsa_prompt_kernel, s_len=s),
        grid=(b, nkh, s // NSA_TQ),
        in_specs=[pl.BlockSpec((1, NSA_TQ, gw), lambda bi, h, i: (bi, i, h)),
                  pl.BlockSpec((1, 1, LANE, HEAD_DIM), lambda bi, h, i: (bi, h, 0, 0)),
                  pl.BlockSpec((1, 1, LANE, HEAD_DIM), lambda bi, h, i: (bi, nkh + h, 0, 0)),
                  seq_k, seq_v, seq_k, seq_v,
                  pl.BlockSpec((1, NSA_TQ, LANE), lambda bi, h, i: (bi, i, 0))],
        out_specs=pl.BlockSpec((1, NSA_TQ, gw), lambda bi, h, i: (bi, i, h)),
        out_shape=jax.ShapeDtypeStruct(q.shape, F32),
        scratch_shapes=[pltpu.VMEM((s // NSA_KB, NSA_TQ, NSA_KB), F32)],
        compiler_params=pltpu.CompilerParams(
            dimension_semantics=("arbitrary", "arbitrary", "arbitrary"),
            vmem_limit_bytes=VMEM_LIMIT_BYTES),
        name="nsa_prompt",
    )(q, kvc, kvc, kv_slc, kv_slc, kv_win, kv_win, g_b)


def in_proj_sizes():
    return (MOBA_HEADS * HEAD_DIM, MOBA_KV_HEADS * HEAD_DIM, MOBA_KV_HEADS * HEAD_DIM,
            NSA_HEADS * HEAD_DIM, 6 * NSA_KV_HEADS * HEAD_DIM, 3 * NSA_HEADS, D_MODEL, D_MODEL)


def rms_norm(x, g):
    xf = x.astype(F32)
    y = xf * lax.rsqrt(jnp.mean(xf * xf, axis=-1, keepdims=True) + NORM_EPS)
    return (y * g.astype(F32)).astype(x.dtype)


def apply_rope(x, pos):
    half = ROT_DIM // 2
    inv_freq = ROPE_THETA ** (-jnp.arange(half, dtype=F32) / half)
    ang = pos.astype(F32)[:, None] * inv_freq[None, :]
    ang = ang.reshape((ang.shape[0],) + (1,) * (x.ndim - 3) + (half,))
    cos, sin = jnp.cos(ang), jnp.sin(ang)
    xr = x[..., :ROT_DIM].astype(F32)
    x1, x2 = xr[..., :half], xr[..., half:]
    rot = jnp.concatenate([x1 * cos - x2 * sin, x2 * cos + x1 * sin], axis=-1)
    return jnp.concatenate([rot.astype(x.dtype), x[..., ROT_DIM:]], axis=-1)


def query_block(s, pref):
    return pref if s % pref == 0 else s


def gather_pages(pool, page_table):
    g = pool[page_table]
    return g.reshape((g.shape[0], g.shape[1] * g.shape[2]) + g.shape[3:])


def moba_attention(q, k_all, v_all, q0):
    B, S, KVH, G, dh = q.shape
    L = k_all.shape[1]
    nb = -(-L // MOBA_BLOCK)
    pad = ((0, 0), (0, nb * MOBA_BLOCK - L), (0, 0), (0, 0))
    kb = jnp.pad(k_all, pad).reshape(B, nb, MOBA_BLOCK, KVH, dh).transpose(0, 3, 1, 2, 4)
    vb = jnp.pad(v_all, pad).reshape(B, nb, MOBA_BLOCK, KVH, dh).transpose(0, 3, 1, 2, 4)
    k_mean = jnp.mean(kb, axis=3, dtype=F32)
    n_sel = min(MOBA_TOPK, nb)
    qc = query_block(S, MOBA_QCHUNK)
    nq = S // qc
    scale = HEAD_DIM ** -0.5
    bi = jnp.arange(B)[:, None, None, None, None]
    gi = jnp.arange(KVH)[None, None, :, None, None]
    blk_ids = jnp.arange(nb)
    in_blk = jnp.arange(MOBA_BLOCK)

    def one_chunk(args):
        qch, start = args
        t = start + jnp.arange(qc)
        own = t // MOBA_BLOCK
        gate = jnp.einsum('bqghd,bgnd->bqghn', qch.astype(F32), k_mean)
        gate = jnp.where((blk_ids[None, :] < own[:, None])[None, :, None, None, :], gate, NEG_INF)
        _, sel = lax.top_k(gate, n_sel)
        slot_ok = (jnp.arange(n_sel)[None, :] < own[:, None])[None, :, None, None, :, None]
        k_sel = kb[bi, gi, sel]
        v_sel = vb[bi, gi, sel]
        s_sel = jnp.einsum('bqghd,bqghskd->bqghsk', qch, k_sel, preferred_element_type=F32) * scale
        s_sel = jnp.where(slot_ok, s_sel, NEG_INF).reshape(B, qc, KVH, G, n_sel * MOBA_BLOCK)
        k_own = kb[:, :, own]
        v_own = vb[:, :, own]
        s_own = jnp.einsum('bqghd,bgqkd->bqghk', qch, k_own, preferred_element_type=F32) * scale
        causal = (own[:, None] * MOBA_BLOCK + in_blk[None, :]) <= t[:, None]
        s_own = jnp.where(causal[None, :, None, None, :], s_own, NEG_INF)
        p = jax.nn.softmax(jnp.concatenate([s_sel, s_own], axis=-1), axis=-1).astype(v_all.dtype)
        p_sel = p[..., :n_sel * MOBA_BLOCK].reshape(B, qc, KVH, G, n_sel, MOBA_BLOCK)
        p_own = p[..., n_sel * MOBA_BLOCK:]
        return (jnp.einsum('bqghsk,bqghskd->bqghd', p_sel, v_sel)
                + jnp.einsum('bqghk,bgqkd->bqghd', p_own, v_own))

    q_chunks = q.reshape(B, nq, qc, KVH, G, dh).transpose(1, 0, 2, 3, 4, 5)
    o = lax.map(one_chunk, (q_chunks, q0 + qc * jnp.arange(nq)))
    return o.transpose(1, 0, 2, 3, 4, 5).reshape(B, S, KVH * G * dh)


def compress_rows(x, w_pos, w1, b1, w2, b2):
    B, L, KVH, dh = x.shape
    r = CMP_LEN // CMP_STRIDE
    n_chunks = L // CMP_STRIDE
    n_cmp = n_chunks - r + 1
    ch = x[:, :n_chunks * CMP_STRIDE].reshape(B, n_chunks, CMP_STRIDE, KVH, dh)
    blocks = jnp.concatenate([ch[:, j:j + n_cmp] for j in range(r)], axis=2)
    blocks = (blocks + w_pos[None, None, :, None, :]).transpose(0, 1, 3, 2, 4)
    flat = blocks.reshape(B, n_cmp, KVH, CMP_LEN * dh)
    return jax.nn.gelu(flat @ w1 + b1) @ w2 + b2


def cmp_to_slc_map(n_cmp, n_slc):
    lo = np.arange(n_cmp) * CMP_STRIDE
    blo = np.arange(n_slc) * SLC_BLOCK
    m = (lo[:, None] < blo[None, :] + SLC_BLOCK) & (lo[:, None] + CMP_LEN > blo[None, :])
    return jnp.asarray(m.astype(np.float32))


def nsa_compressed_selected(q, k_cmp, v_cmp, k_slc, v_slc, q0):
    B, S, KVH, G, dh = q.shape
    L = k_slc.shape[1]
    n_cmp = k_cmp.shape[1]
    n_slc = -(-L // SLC_BLOCK)
    n_top = min(SLC_TOPN, n_slc)
    pad = ((0, 0), (0, n_slc * SLC_BLOCK - L), (0, 0), (0, 0))
    kb = jnp.pad(k_slc, pad).reshape(B, n_slc, SLC_BLOCK, KVH, dh).transpose(0, 3, 1, 2, 4)
    vb = jnp.pad(v_slc, pad).reshape(B, n_slc, SLC_BLOCK, KVH, dh).transpose(0, 3, 1, 2, 4)
    sel_map = cmp_to_slc_map(n_cmp, n_slc)
    cmp_last = jnp.arange(n_cmp) * CMP_STRIDE + CMP_LEN - 1
    blk_ids = jnp.arange(n_slc)
    in_blk = jnp.arange(SLC_BLOCK)
    qc = query_block(S, NSA_QCHUNK)
    nq = S // qc
    scale = HEAD_DIM ** -0.5
    bi = jnp.arange(B)[:, None, None, None]
    gi = jnp.arange(KVH)[None, None, :, None]

    def one_chunk(args):
        qch, start = args
        t = start + jnp.arange(qc)
        avail = (cmp_last[None, :] <= t[:, None])[None, :, None, None, :]
        s_c = jnp.einsum('bqghd,bngd->bqghn', qch, k_cmp, preferred_element_type=F32) * scale
        p_c = jnp.where(avail, jax.nn.softmax(jnp.where(avail, s_c, NEG_INF), axis=-1), 0.0)
        o_c = jnp.einsum('bqghn,bngd->bqghd', p_c.astype(v_cmp.dtype), v_cmp)
        imp = jnp.einsum('bqghn,nj->bqgj', p_c, sel_map)
        cur = (t // SLC_BLOCK)[:, None]
        forced = (blk_ids == 0) | (blk_ids == cur) | (blk_ids == cur - 1)
        imp = jnp.where(forced[None, :, None, :], BIG, imp)
        imp = jnp.where((blk_ids > cur)[None, :, None, :], NEG_INF, imp)
        _, sel = lax.top_k(imp, n_top)
        k_sel = kb[bi, gi, sel]
        v_sel = vb[bi, gi, sel]
        key_pos = sel[..., None] * SLC_BLOCK + in_blk
        ok = key_pos <= t[None, :, None, None, None]
        s_s = jnp.einsum('bqghd,bqgskd->bqghsk', qch, k_sel, preferred_element_type=F32) * scale
        s_s = jnp.where(ok[:, :, :, None], s_s, NEG_INF).reshape(B, qc, KVH, G, n_top * SLC_BLOCK)
        p_s = jax.nn.softmax(s_s, axis=-1).astype(v_slc.dtype).reshape(B, qc, KVH, G, n_top, SLC_BLOCK)
        o_s = jnp.einsum('bqghsk,bqgskd->bqghd', p_s, v_sel)
        return o_c, o_s

    q_chunks = q.reshape(B, nq, qc, KVH, G, dh).transpose(1, 0, 2, 3, 4, 5)
    o_c, o_s = lax.map(one_chunk, (q_chunks, q0 + qc * jnp.arange(nq)))
    o_c = o_c.transpose(1, 0, 2, 3, 4, 5).reshape(B, S, KVH, G, dh)
    o_s = o_s.transpose(1, 0, 2, 3, 4, 5).reshape(B, S, KVH, G, dh)
    return o_c, o_s


def window_attention(q, k_win, v_win, q0, k0):
    B, S, KVH, G, dh = q.shape
    qb = query_block(S, WIN_QBLOCK)
    nq = S // qb
    span = WINDOW + qb
    pad = ((0, 0), (WINDOW, 0), (0, 0), (0, 0))
    kp = jnp.pad(k_win, pad)
    vp = jnp.pad(v_win, pad)
    scale = HEAD_DIM ** -0.5

    def one_block(args):
        qblk, i = args
        s0 = q0 - k0 + i * qb
        kk = lax.dynamic_slice_in_dim(kp, s0, span, axis=1)
        vv = lax.dynamic_slice_in_dim(vp, s0, span, axis=1)
        t = q0 + i * qb + jnp.arange(qb)
        kpos = k0 - WINDOW + s0 + jnp.arange(span)
        ok = ((kpos[None, :] > t[:, None] - WINDOW) & (kpos[None, :] <= t[:, None])
              & (kpos[None, :] >= k0))
        s = jnp.einsum('bqghd,bkgd->bqghk', qblk, kk, preferred_element_type=F32) * scale
        s = jnp.where(ok[None, :, None, None, :], s, NEG_INF)
        p = jax.nn.softmax(s, axis=-1).astype(vv.dtype)
        return jnp.einsum('bqghk,bkgd->bqghd', p, vv)

    q_blocks = q.reshape(B, nq, qb, KVH, G, dh).transpose(1, 0, 2, 3, 4, 5)
    o = lax.map(one_block, (q_blocks, jnp.arange(nq)))
    return o.transpose(1, 0, 2, 3, 4, 5).reshape(B, S, KVH, G, dh)


def token_mixer(h, q0, past, lp):
    B, S, _ = h.shape
    cuts = [int(c) for c in np.cumsum(in_proj_sizes())[:-1]]
    q_a, k_a, v_a, q_b, kv_b, g_b, g_ma, g_mb = jnp.split(_mm3(h, lp['w_in']), cuts, axis=-1)
    pos = q0 + jnp.arange(S)
    q_a = apply_rope(q_a.reshape(B, S, MOBA_KV_HEADS, MOBA_GROUP, HEAD_DIM), pos)
    k_a = apply_rope(k_a.reshape(B, S, MOBA_KV_HEADS, HEAD_DIM), pos)
    v_a = v_a.reshape(B, S, MOBA_KV_HEADS, HEAD_DIM)
    q_b = apply_rope(q_b.reshape(B, S, NSA_KV_HEADS, NSA_GROUP, HEAD_DIM), pos)
    kv_b = kv_b.reshape(B, S, 3, 2, NSA_KV_HEADS, HEAD_DIM)
    new_moba = jnp.stack([k_a, v_a], axis=2)
    new_cmp = kv_b[:, :, 0]
    new_slc = jnp.stack([apply_rope(kv_b[:, :, 1, 0], pos), kv_b[:, :, 1, 1]], axis=2)
    new_win = jnp.stack([apply_rope(kv_b[:, :, 2, 0], pos), kv_b[:, :, 2, 1]], axis=2)
    if past is None:
        o_a = _moba_prompt(q_a.reshape(B, S, -1), new_moba.reshape(B, S, -1))
        kvc = _compress_prompt(new_cmp.reshape(B, S, -1), lp['cmp_pos'], lp['cmp_w1'], lp['cmp_b1'],
                               lp['cmp_w2'], lp['cmp_b2'])
        o_b = _nsa_prompt(q_b.reshape(B, S, -1), kvc, new_slc.reshape(B, S, -1),
                          new_win.reshape(B, S, -1),
                          jnp.pad(g_b, ((0, 0), (0, 0), (0, LANE - g_b.shape[-1]))))
        merged = (jax.nn.sigmoid(g_ma) * _mm3(o_a, lp['w_pa'])
                  + jax.nn.sigmoid(g_mb) * _mm3(o_b, lp['w_pb']))
        win_state = new_win[:, S - min(WINDOW, S):]
        return _mm3(merged, lp['w_out']), (new_moba, new_cmp, new_slc, win_state)
    else:
        moba_kv = jnp.concatenate([past[0], new_moba], axis=1)
        cmp_kv = jnp.concatenate([past[1], new_cmp], axis=1)
        slc_kv = jnp.concatenate([past[2], new_slc], axis=1)
        win_kv = jnp.concatenate([past[3], new_win], axis=1)
    k0 = q0 + S - win_kv.shape[1]
    win_state = win_kv[:, win_kv.shape[1] - min(WINDOW, win_kv.shape[1]):]

    o_a = moba_attention(q_a, moba_kv[:, :, 0], moba_kv[:, :, 1], q0)
    k_c = compress_rows(cmp_kv[:, :, 0], lp['cmp_pos'][0], lp['cmp_w1'][0], lp['cmp_b1'][0],
                        lp['cmp_w2'][0], lp['cmp_b2'][0])
    v_c = compress_rows(cmp_kv[:, :, 1], lp['cmp_pos'][1], lp['cmp_w1'][1], lp['cmp_b1'][1],
                        lp['cmp_w2'][1], lp['cmp_b2'][1])
    o_c, o_s = nsa_compressed_selected(q_b, k_c, v_c, slc_kv[:, :, 0], slc_kv[:, :, 1], q0)
    o_w = window_attention(q_b, win_kv[:, :, 0], win_kv[:, :, 1], q0, k0)
    gates = jax.nn.sigmoid(g_b.reshape(B, S, 3, NSA_KV_HEADS, NSA_GROUP, 1))
    o_b = (gates[:, :, 0] * o_c + gates[:, :, 1] * o_s + gates[:, :, 2] * o_w).reshape(B, S, NSA_HEADS * HEAD_DIM)
    merged = (jax.nn.sigmoid(g_ma) * _mm3(o_a, lp['w_pa'])
              + jax.nn.sigmoid(g_mb) * _mm3(o_b, lp['w_pb']))
    return _mm3(merged, lp['w_out']), (new_moba, new_cmp, new_slc, win_state)


def routed_experts(x, expert_ids, weights, w_gate, w_up, w_down):
    T, D = x.shape
    A = T * EXPERT_TOPK
    flat_e = expert_ids.reshape(A)
    order = jnp.argsort(flat_e)
    e_sorted = flat_e[order]
    tok_sorted = order // EXPERT_TOPK
    counts = jnp.zeros((N_EXPERTS,), jnp.int32).at[flat_e].add(1)
    padded = (counts + MOE_BLOCK - 1) // MOE_BLOCK * MOE_BLOCK
    pad_end = jnp.cumsum(padded)
    pad_start = pad_end - padded
    start = jnp.cumsum(counts) - counts
    dest = pad_start[e_sorted] + jnp.arange(A) - start[e_sorted]
    n_blk = -(-(A + N_EXPERTS * (MOE_BLOCK - 1)) // MOE_BLOCK)
    row_tok = jnp.zeros((n_blk * MOE_BLOCK,), jnp.int32).at[dest].set(tok_sorted)
    blk_expert = jnp.minimum(jnp.searchsorted(pad_end, jnp.arange(n_blk) * MOE_BLOCK, side='right'),
                             N_EXPERTS - 1)
    xb = x[row_tok].reshape(n_blk, MOE_BLOCK, D)

    def expert_block(args):
        xe, e = args
        return (jax.nn.silu(xe @ w_gate[e]) * (xe @ w_up[e])) @ w_down[e]

    yb = lax.map(expert_block, (xb, blk_expert)).reshape(n_blk * MOE_BLOCK, D)
    contrib = yb[dest] * weights.reshape(A)[order][:, None].astype(x.dtype)
    return jnp.zeros((T, D), x.dtype).at[tok_sorted].add(contrib)


def hierarchical_moe(h, lp):
    B, S, D = h.shape
    x = h.reshape(B * S, D)
    g_prob = jax.nn.softmax((x @ lp['w_rg']).astype(F32) + lp['b_rg'].astype(F32), axis=-1)
    g_w, g_idx = lax.top_k(g_prob, 1)
    e_logit = ((x @ lp['w_re']).astype(F32) + lp['b_re'].astype(F32)).reshape(-1, N_GROUPS, EXPERTS_PER_GROUP)
    e_logit = jnp.take_along_axis(e_logit, g_idx[:, :, None], axis=1)[:, 0]
    e_val, e_idx = lax.top_k(e_logit, EXPERT_TOPK)
    weights = g_w * jax.nn.softmax(e_val, axis=-1)
    expert_ids = g_idx * EXPERTS_PER_GROUP + e_idx
    y = routed_experts(x, expert_ids, weights, lp['w_gate'], lp['w_up'], lp['w_down'])
    return y.reshape(B, S, D)


def decoder_layer(x, c, q0, past, lp):
    ada = _mm(jax.nn.silu(c), lp['w_ada'], tn=2048) + lp['b_ada']
    sh1, sc1, g1, sh2, sc2, g2 = jnp.split(ada[:, None, :], 6, axis=-1)
    h = rms_norm(x, lp['norm_attn']) * (1 + sc1) + sh1
    mix, rows = token_mixer(h, q0, past, lp)
    x = x + g1 * mix
    h = rms_norm(x, lp['norm_ffn']) * (1 + sc2) + sh2
    x = x + g2 * hierarchical_moe(h, lp)
    return x, rows


def kernel(x_prompt, x_sample, cache_moba_kv, cache_cmp_kv, cache_slc_kv, state_win_kv,
           page_table, c_prompt, c_sample, w_in, w_pa, w_pb, w_out, cmp_pos, cmp_w1, cmp_b1,
           cmp_w2, cmp_b2, norm_attn, norm_ffn, norm_final, w_ada, b_ada, w_rg, b_rg, w_re,
           b_re, w_gate, w_up, w_down):
    past_len = page_table.shape[1] * cache_moba_kv.shape[2]
    y_p, y_s = x_prompt, x_sample
    rows_p, rows_s = [], []
    for l in range(DEPTH):
        lp = {'w_in': w_in[l], 'w_pa': w_pa[l], 'w_pb': w_pb[l], 'w_out': w_out[l],
              'cmp_pos': cmp_pos[l], 'cmp_w1': cmp_w1[l], 'cmp_b1': cmp_b1[l],
              'cmp_w2': cmp_w2[l], 'cmp_b2': cmp_b2[l], 'norm_attn': norm_attn[l],
              'norm_ffn': norm_ffn[l], 'w_ada': w_ada[l], 'b_ada': b_ada[l],
              'w_rg': w_rg[l], 'b_rg': b_rg[l], 'w_re': w_re[l], 'b_re': b_re[l],
              'w_gate': w_gate[l], 'w_up': w_up[l], 'w_down': w_down[l]}
        y_p, r_p = decoder_layer(y_p, c_prompt, 0, None, lp)
        past = (gather_pages(cache_moba_kv[l], page_table),
                gather_pages(cache_cmp_kv[l], page_table),
                gather_pages(cache_slc_kv[l], page_table),
                state_win_kv[l])
        y_s, r_s = decoder_layer(y_s, c_sample, past_len, past, lp)
        rows_p.append(r_p)
        rows_s.append(r_s)
    y_prompt = rms_norm(y_p, norm_final)
    y_sample = rms_norm(y_s, norm_final)
    return (y_prompt, y_sample,
            jnp.stack([r[0] for r in rows_p]), jnp.stack([r[0] for r in rows_s]),
            jnp.stack([r[1] for r in rows_p]), jnp.stack([r[1] for r in rows_s]),
            jnp.stack([r[2] for r in rows_p]), jnp.stack([r[2] for r in rows_s]),
            jnp.stack([r[3] for r in rows_p]), jnp.stack([r[3] for r in rows_s]))
```

```python
import functools

import jax
import jax.numpy as jnp
import numpy as np
from jax import lax
from jax.experimental import pallas as pl
from jax.experimental.pallas import tpu as pltpu

D_MODEL = 2048
DEPTH = 2
HEAD_DIM = 128
ROT_DIM = HEAD_DIM // 4
ROPE_THETA = 500000.0
NORM_EPS = 1e-6
MOBA_HEADS = 8
MOBA_KV_HEADS = 2
MOBA_GROUP = MOBA_HEADS // MOBA_KV_HEADS
MOBA_BLOCK = 256
MOBA_TOPK = 3
MOBA_QCHUNK = 16
NSA_HEADS = 8
NSA_KV_HEADS = 2
NSA_GROUP = NSA_HEADS // NSA_KV_HEADS
CMP_LEN = 32
CMP_STRIDE = 16
CMP_HIDDEN = 128
SLC_BLOCK = 64
SLC_TOPN = 16
WINDOW = 512
NSA_QCHUNK = 16
WIN_QBLOCK = 128
N_GROUPS = 4
EXPERTS_PER_GROUP = 8
N_EXPERTS = N_GROUPS * EXPERTS_PER_GROUP
EXPERT_TOPK = 2
D_EXPERT = 512
MOE_BLOCK = 128

NEG_INF = -1e30
BIG = 1e30
F32 = jnp.float32
BF16 = jnp.bfloat16

LANE = 128
VMEM_LIMIT_BYTES = 48 * 1024 * 1024


def _round_up(n, m):
    return (n + m - 1) // m * m


QKV_COLS = 4096
GB_COLS = 3 * NSA_HEADS
GATE_COLS = 2 * D_MODEL
PROJ_COLS = QKV_COLS + GATE_COLS
COL_QA, COL_KA, COL_VA, COL_QB = 0, 8, 10, 12
COL_CMP, COL_SLC, COL_WIN = 20, 24, 28


def _mod_spec(mod, tm, rows_per_mod):
    d = mod.shape[-1]
    if mod.shape[1] == 1:
        return pl.BlockSpec((1, 1, d), lambda i, *_: (i * tm // rows_per_mod, 0, 0))
    return pl.BlockSpec((1, tm, d), lambda i, *_: (i, 0, 0))


def _ada_kernel(c_ref, w_ref, b_ref, o_ref):
    c = jax.nn.silu(c_ref[...]).astype(BF16)
    o_ref[...] = jnp.dot(c, w_ref[...].astype(BF16), preferred_element_type=F32) + b_ref[...]


def _ada(c, w_ada, b_ada, tn=1024):
    m, k = c.shape
    n = w_ada.shape[1]
    return pl.pallas_call(
        _ada_kernel,
        grid=(n // tn,),
        in_specs=[pl.BlockSpec((m, k), lambda j: (0, 0)),
                  pl.BlockSpec((k, tn), lambda j: (0, j)),
                  pl.BlockSpec((1, tn), lambda j: (0, j))],
        out_specs=pl.BlockSpec((m, tn), lambda j: (0, j)),
        out_shape=jax.ShapeDtypeStruct((m, n), F32),
        compiler_params=pltpu.CompilerParams(dimension_semantics=("arbitrary",),
                                             vmem_limit_bytes=VMEM_LIMIT_BYTES),
        name="ada_ln",
    )(c, w_ada, b_ada.reshape(1, n))


def _rope_tables(pos):
    half = ROT_DIM // 2
    inv_freq = ROPE_THETA ** (-jnp.arange(half, dtype=F32) / half)
    ang = pos.astype(F32)[:, None] * inv_freq[None, :]
    cos, sin = jnp.cos(ang), jnp.sin(ang)
    n = pos.shape[0]
    cos_t = jnp.concatenate([cos, cos, jnp.ones((n, HEAD_DIM - ROT_DIM), F32)], axis=1)
    sin_t = jnp.concatenate([-sin, sin, jnp.zeros((n, HEAD_DIM - ROT_DIM), F32)], axis=1)
    return cos_t, sin_t


def _rope_column_mask():
    m = np.zeros((1, PROJ_COLS), np.float32)
    for lo, hi in ((0, 1280), (1536, 2560), (3072, 3328), (3584, 3840)):
        m[0, lo:hi] = 1.0
    return jnp.asarray(m)


def _in_proj_kernel(x_ref, g_ref, sc_ref, sh_ref, cos_ref, sin_ref, rmask_ref, w_ref, wgb_ref,
                    o_ref, ogb_ref, hb_scr, *, tn):
    j = pl.program_id(1)

    @pl.when(j == 0)
    def _():
        x = x_ref[...]
        y = x * lax.rsqrt(jnp.mean(x * x, axis=-1, keepdims=True) + NORM_EPS) * g_ref[...]
        h = (y * (1.0 + sc_ref[0]) + sh_ref[0]).astype(BF16)
        hb_scr[...] = h
        ogb_ref[...] = jnp.dot(h, wgb_ref[...], preferred_element_type=F32)

    acc = jnp.dot(hb_scr[...], w_ref[...], preferred_element_type=F32)

    @pl.when(j < QKV_COLS // tn)
    def _():
        lane = lax.broadcasted_iota(jnp.int32, (acc.shape[0], HEAD_DIM), 1)
        cos_t, sin_t = cos_ref[...], sin_ref[...]
        for hh in range(tn // HEAD_DIM):
            blk = acc[:, hh * HEAD_DIM:(hh + 1) * HEAD_DIM]
            on = rmask_ref[:, hh * HEAD_DIM:(hh + 1) * HEAD_DIM] > 0.5
            partner = jnp.where(lane < ROT_DIM // 2,
                                pltpu.roll(blk, HEAD_DIM - ROT_DIM // 2, axis=1),
                                pltpu.roll(blk, ROT_DIM // 2, axis=1))
            o_ref[:, hh * HEAD_DIM:(hh + 1) * HEAD_DIM] = (
                blk * jnp.where(on, cos_t, 1.0) + partner * jnp.where(on, sin_t, 0.0))

    @pl.when(j >= QKV_COLS // tn)
    def _():
        o_ref[...] = jax.nn.sigmoid(acc)


def _in_proj(x, g, sc, sh, cos_t, sin_t, w_main, w_gb, tm, rows_per_mod, tn=1024):
    t, d = x.shape
    rmask = _rope_column_mask()
    return pl.pallas_call(
        functools.partial(_in_proj_kernel, tn=tn),
        grid=(t // tm, PROJ_COLS // tn),
        in_specs=[pl.BlockSpec((tm, d), lambda i, j: (i, 0)),
                  pl.BlockSpec((1, d), lambda i, j: (0, 0)),
                  _mod_spec(sc, tm, rows_per_mod), _mod_spec(sh, tm, rows_per_mod),
                  pl.BlockSpec((tm, HEAD_DIM), lambda i, j: (i, 0)),
                  pl.BlockSpec((tm, HEAD_DIM), lambda i, j: (i, 0)),
                  pl.BlockSpec((1, tn), lambda i, j: (0, j)),
                  pl.BlockSpec((d, tn), lambda i, j: (0, j)),
                  pl.BlockSpec((d, LANE), lambda i, j: (0, 0))],
        out_specs=[pl.BlockSpec((tm, tn), lambda i, j: (i, j)),
                   pl.BlockSpec((tm, LANE), lambda i, j: (i, 0))],
        out_shape=[jax.ShapeDtypeStruct((t, PROJ_COLS), F32),
                   jax.ShapeDtypeStruct((t, LANE), F32)],
        scratch_shapes=[pltpu.VMEM((tm, d), BF16)],
        compiler_params=pltpu.CompilerParams(dimension_semantics=("arbitrary", "arbitrary"),
                                             vmem_limit_bytes=VMEM_LIMIT_BYTES),
        name="in_proj",
    )(x, g.reshape(1, d), sc, sh, cos_t, sin_t, rmask, w_main, w_gb)


def _out_proj_kernel(oa_ref, ob_ref, gma_ref, gmb_ref, x_ref, g1_ref, gf_ref, sc_ref, sh_ref,
                     wpa_ref, wpb_ref, wout_ref, wr_ref, br_ref, xo_ref, h_ref, lg_ref):
    pa = jnp.dot(oa_ref[...].astype(BF16), wpa_ref[...], preferred_element_type=F32)
    pb = jnp.dot(ob_ref[...].astype(BF16), wpb_ref[...], preferred_element_type=F32)
    merged = (gma_ref[...] * pa + gmb_ref[...] * pb).astype(BF16)
    mix = jnp.dot(merged, wout_ref[...], preferred_element_type=F32)
    x = x_ref[...] + g1_ref[0] * mix
    xo_ref[...] = x
    y = x * lax.rsqrt(jnp.mean(x * x, axis=-1, keepdims=True) + NORM_EPS) * gf_ref[...]
    h = y * (1.0 + sc_ref[0]) + sh_ref[0]
    h_ref[...] = h
    lg_ref[...] = jnp.dot(h, wr_ref[...], precision=lax.Precision.HIGHEST,
                          preferred_element_type=F32) + br_ref[...]


def _out_proj(o_a, o_b, proj, x, g1, gf, sc, sh, w_pa, w_pb, w_out, w_r, b_r, tm, rows_per_mod):
    t, d = x.shape
    da = o_a.shape[1]
    const = lambda shape: pl.BlockSpec(shape, lambda i: (0,) * len(shape),
                                       pipeline_mode=pl.Buffered(1))
    gate_blk = QKV_COLS // d
    return pl.pallas_call(
        _out_proj_kernel,
        grid=(t // tm,),
        in_specs=[pl.BlockSpec((tm, da), lambda i: (i, 0)),
                  pl.BlockSpec((tm, da), lambda i: (i, 0)),
                  pl.BlockSpec((tm, d), lambda i: (i, gate_blk)),
                  pl.BlockSpec((tm, d), lambda i: (i, gate_blk + 1)),
                  pl.BlockSpec((tm, d), lambda i: (i, 0)),
                  _mod_spec(g1, tm, rows_per_mod),
                  const((1, d)),
                  _mod_spec(sc, tm, rows_per_mod), _mod_spec(sh, tm, rows_per_mod),
                  const((da, d)), const((da, d)), const((d, d)), const((d, LANE)), const((1, LANE))],
        out_specs=[pl.BlockSpec((tm, d), lambda i: (i, 0)),
                   pl.BlockSpec((tm, d), lambda i: (i, 0)),
                   pl.BlockSpec((tm, LANE), lambda i: (i, 0))],
        out_shape=[jax.ShapeDtypeStruct((t, d), F32), jax.ShapeDtypeStruct((t, d), F32),
                   jax.ShapeDtypeStruct((t, LANE), F32)],
        compiler_params=pltpu.CompilerParams(dimension_semantics=("arbitrary",),
                                             vmem_limit_bytes=VMEM_LIMIT_BYTES),
        name="out_proj",
    )(o_a, o_b, proj, proj, x, g1, gf.reshape(1, d), sc, sh, w_pa, w_pb, w_out, w_r, b_r)


def _moe_ffn_kernel(blk_e_ref, row_tok_ref, nused_ref, x_hbm, roww_ref, wg_ref, wu_ref, wd_ref,
                    o_ref, xbuf, sem, wg_bf, wu_bf, wd_bf, *, bm):
    i = pl.program_id(0)
    n_used = nused_ref[0]

    def issue(blk, slot):
        def one(r, c):
            tok = row_tok_ref[blk * bm + r]
            pltpu.make_async_copy(x_hbm.at[pl.ds(tok, 1)], xbuf.at[slot, pl.ds(r, 1)],
                                  sem.at[slot]).start()
            return c
        lax.fori_loop(0, bm, one, 0, unroll=8)

    def wait(slot):
        pltpu.make_async_copy(x_hbm.at[pl.ds(0, bm)], xbuf.at[slot], sem.at[slot]).wait()

    slot = i % 2

    @pl.when((i == 0) & (n_used > 0))
    def _():
        issue(0, 0)

    @pl.when(i < n_used)
    def _():
        wait(slot)

        @pl.when(i + 1 < n_used)
        def _():
            issue(i + 1, 1 - slot)

        prev = blk_e_ref[jnp.maximum(i - 1, 0)]

        @pl.when((i == 0) | (blk_e_ref[i] != prev))
        def _():
            wg_bf[...] = wg_ref[0].astype(BF16)
            wu_bf[...] = wu_ref[0].astype(BF16)
            wd_bf[...] = wd_ref[0].astype(BF16)

        x = xbuf[slot].astype(BF16)
        hg = jnp.dot(x, wg_bf[...], preferred_element_type=F32)
        hu = jnp.dot(x, wu_bf[...], preferred_element_type=F32)
        act = (jax.nn.silu(hg) * hu).astype(BF16)
        y = jnp.dot(act, wd_bf[...], preferred_element_type=F32)
        o_ref[...] = y * roww_ref[...]

    @pl.when(i >= n_used)
    def _():
        o_ref[...] = jnp.zeros_like(o_ref)


def _moe_ffn(x, blk_expert, row_tok, n_used, row_w, w_gate, w_up, w_down, bm):
    n_rows = row_tok.shape[0]
    n_blk = n_rows // bm
    d = x.shape[1]
    de = w_gate.shape[2]
    grid_spec = pltpu.PrefetchScalarGridSpec(
        num_scalar_prefetch=3,
        grid=(n_blk,),
        in_specs=[pl.BlockSpec(memory_space=pl.ANY),
                  pl.BlockSpec((bm, 1), lambda i, be, rt, nu: (i, 0)),
                  pl.BlockSpec((1, d, de), lambda i, be, rt, nu: (be[i], 0, 0)),
                  pl.BlockSpec((1, d, de), lambda i, be, rt, nu: (be[i], 0, 0)),
                  pl.BlockSpec((1, de, d), lambda i, be, rt, nu: (be[i], 0, 0))],
        out_specs=pl.BlockSpec((bm, d), lambda i, be, rt, nu: (i, 0)),
        scratch_shapes=[pltpu.VMEM((2, bm, d), F32),
                        pltpu.SemaphoreType.DMA((2,)),
                        pltpu.VMEM((d, de), BF16), pltpu.VMEM((d, de), BF16),
                        pltpu.VMEM((de, d), BF16)])
    return pl.pallas_call(
        functools.partial(_moe_ffn_kernel, bm=bm),
        grid_spec=grid_spec,
        out_shape=jax.ShapeDtypeStruct((n_rows, d), F32),
        compiler_params=pltpu.CompilerParams(dimension_semantics=("arbitrary",),
                                             vmem_limit_bytes=VMEM_LIMIT_BYTES),
        name="moe_ffn",
    )(blk_expert, row_tok, n_used, x, row_w, w_gate, w_up, w_down)


def _moe_combine_kernel(d0_ref, d1_ref, x_ref, g2_ref, nf_ref, yb_hbm, o_ref, ybuf, sem,
                        *, tm, final_norm):
    i = pl.program_id(0)
    n = pl.num_programs(0)

    def issue(blk, slot):
        def one(r, c):
            t = blk * tm + r
            pltpu.make_async_copy(yb_hbm.at[pl.ds(d0_ref[t], 1)], ybuf.at[slot, 0, pl.ds(r, 1)],
                                  sem.at[slot]).start()
            pltpu.make_async_copy(yb_hbm.at[pl.ds(d1_ref[t], 1)], ybuf.at[slot, 1, pl.ds(r, 1)],
                                  sem.at[slot]).start()
            return c
        lax.fori_loop(0, tm, one, 0, unroll=8)

    def wait(slot):
        pltpu.make_async_copy(yb_hbm.at[pl.ds(0, tm)], ybuf.at[slot, 0], sem.at[slot]).wait()
        pltpu.make_async_copy(yb_hbm.at[pl.ds(0, tm)], ybuf.at[slot, 1], sem.at[slot]).wait()

    slot = i % 2

    @pl.when(i == 0)
    def _():
        issue(0, 0)

    wait(slot)

    @pl.when(i + 1 < n)
    def _():
        issue(i + 1, 1 - slot)

    y = x_ref[...] + g2_ref[0] * (ybuf[slot, 0] + ybuf[slot, 1])
    if final_norm:
        y = y * lax.rsqrt(jnp.mean(y * y, axis=-1, keepdims=True) + NORM_EPS) * nf_ref[...]
    o_ref[...] = y


def _moe_combine(x, g2, norm_final, yb, d0, d1, tm, rows_per_mod, final_norm):
    t, d = x.shape
    grid_spec = pltpu.PrefetchScalarGridSpec(
        num_scalar_prefetch=2,
        grid=(t // tm,),
        in_specs=[pl.BlockSpec((tm, d), lambda i, a, b: (i, 0)),
                  _mod_spec(g2, tm, rows_per_mod),
                  pl.BlockSpec((1, d), lambda i, a, b: (0, 0)),
                  pl.BlockSpec(memory_space=pl.ANY)],
        out_specs=pl.BlockSpec((tm, d), lambda i, a, b: (i, 0)),
        scratch_shapes=[pltpu.VMEM((2, 2, tm, d), F32), pltpu.SemaphoreType.DMA((2,))])
    return pl.pallas_call(
        functools.partial(_moe_combine_kernel, tm=tm, final_norm=final_norm),
        grid_spec=grid_spec,
        out_shape=jax.ShapeDtypeStruct((t, d), F32),
        compiler_params=pltpu.CompilerParams(dimension_semantics=("arbitrary",),
                                             vmem_limit_bytes=VMEM_LIMIT_BYTES),
        name="moe_combine",
    )(d0, d1, x, g2, norm_final.reshape(1, d), yb)


def _route(logits, bm):
    t = logits.shape[0]
    a = t * EXPERT_TOPK
    g_prob = jax.nn.softmax(logits[:, :N_GROUPS], axis=-1)
    g_w, g_idx = lax.top_k(g_prob, 1)
    e_logit = logits[:, N_GROUPS:N_GROUPS + N_EXPERTS].reshape(t, N_GROUPS, EXPERTS_PER_GROUP)
    e_logit = jnp.take_along_axis(e_logit, g_idx[:, :, None], axis=1)[:, 0]
    e_val, e_idx = lax.top_k(e_logit, EXPERT_TOPK)
    weights = (g_w * jax.nn.softmax(e_val, axis=-1)).reshape(a)
    flat_e = (g_idx * EXPERTS_PER_GROUP + e_idx).reshape(a)
    order = jnp.argsort(flat_e).astype(jnp.int32)
    e_sorted = flat_e[order]
    counts = jnp.zeros((N_EXPERTS,), jnp.int32).at[flat_e].add(1)
    padded = (counts + bm - 1) // bm * bm
    pad_end = jnp.cumsum(padded)
    pad_start = pad_end - padded
    start = jnp.cumsum(counts) - counts
    dest_sorted = pad_start[e_sorted] + jnp.arange(a, dtype=jnp.int32) - start[e_sorted]
    n_blk = -(-(a + N_EXPERTS * (bm - 1)) // bm)
    n_rows = n_blk * bm
    row_asg = jnp.zeros((n_rows,), jnp.int32).at[dest_sorted].set(order)
    row_valid = jnp.zeros((n_rows,), F32).at[dest_sorted].set(1.0)
    row_tok = row_asg // EXPERT_TOPK
    row_w = (weights[row_asg] * row_valid).reshape(n_rows, 1)
    blk_expert = jnp.minimum(jnp.searchsorted(pad_end, jnp.arange(n_blk) * bm, side='right'),
                             N_EXPERTS - 1).astype(jnp.int32)
    n_used = (pad_end[-1] // bm).astype(jnp.int32).reshape(1)
    dest = jnp.zeros((a,), jnp.int32).at[order].set(dest_sorted).reshape(t, EXPERT_TOPK)
    return blk_expert, row_tok, n_used, row_w, dest[:, 0], dest[:, 1]


ATT_SCALE = HEAD_DIM ** -0.5
_NT = (((1,), (1,)), ((), ()))


def _flash_update(qb, k, v, mask, carry):
    m, l, acc = carry
    s = lax.dot_general(qb, k.astype(BF16), _NT, preferred_element_type=F32) * ATT_SCALE
    s = jnp.where(mask, s, NEG_INF)
    m_new = jnp.maximum(m, jnp.max(s, axis=-1, keepdims=True))
    alpha = jnp.exp(m - m_new)
    p = jnp.exp(s - m_new)
    l = alpha * l + jnp.sum(p, axis=-1, keepdims=True)
    acc = alpha * acc + jnp.dot(p.astype(BF16), v.astype(BF16), preferred_element_type=F32)
    return m_new, l, acc


def _flash_init(tq):
    return (jnp.full((tq, 1), -jnp.inf, F32), jnp.zeros((tq, 1), F32),
            jnp.zeros((tq, HEAD_DIM), F32))


def _moba_prompt_kernel(q_ref, k_ref, v_ref, o_ref, kmean_scr, sel_scr, *, nb):
    i = pl.program_id(2)
    tq = MOBA_BLOCK

    @pl.when(i == 0)
    def _():
        kmean_scr[...] = jnp.zeros_like(kmean_scr)
        for j in range(nb):
            kmean_scr[j:j + 1, :] = jnp.mean(k_ref[0, j * tq:(j + 1) * tq, :], axis=0, keepdims=True)

    lane = lax.broadcasted_iota(jnp.int32, (tq, LANE), 1)
    row = lax.broadcasted_iota(jnp.int32, (tq, tq), 0)
    col = lax.broadcasted_iota(jnp.int32, (tq, tq), 1)
    own0 = pl.multiple_of(i * tq, tq)
    for g in range(MOBA_GROUP):
        qg = q_ref[0, :, g * HEAD_DIM:(g + 1) * HEAD_DIM]
        gate = lax.dot_general(qg, kmean_scr[...], _NT, precision=lax.Precision.HIGHEST,
                               preferred_element_type=F32)
        gate = jnp.where(lane < i, gate, NEG_INF)
        for j in range(nb):
            gj = gate[:, j:j + 1]
            better = ((gate > gj) | ((gate == gj) & (lane < j))) & (lane < nb)
            cnt = jnp.sum(jnp.where(better, 1.0, 0.0), axis=-1, keepdims=True)
            selj = jnp.where((cnt < MOBA_TOPK) & (j < i), 1.0, 0.0)
            sel_scr[g, j] = jnp.broadcast_to(selj, (tq, LANE))

        qb = qg.astype(BF16)
        carry = _flash_update(qb, k_ref[0, pl.ds(own0, tq), :], v_ref[0, pl.ds(own0, tq), :],
                              col <= row, _flash_init(tq))

        def body(j, carry, qb=qb, g=g):
            off = pl.multiple_of(j * tq, tq)
            selb = sel_scr[g, j]
            mask = jnp.concatenate([selb, selb], axis=1) > 0.5
            return _flash_update(qb, k_ref[0, pl.ds(off, tq), :], v_ref[0, pl.ds(off, tq), :],
                                 mask, carry)

        m, l, acc = lax.fori_loop(0, i, body, carry)
        o_ref[0, :, g * HEAD_DIM:(g + 1) * HEAD_DIM] = acc / l


def _moba_prompt(proj):
    b, s, _ = proj.shape
    nb = s // MOBA_BLOCK
    assert s % MOBA_BLOCK == 0 and nb <= LANE
    gw = MOBA_GROUP * HEAD_DIM
    return pl.pallas_call(
        functools.partial(_moba_prompt_kernel, nb=nb),
        grid=(b, MOBA_KV_HEADS, nb),
        in_specs=[pl.BlockSpec((1, MOBA_BLOCK, gw), lambda bi, h, i: (bi, i, COL_QA // MOBA_GROUP + h)),
                  pl.BlockSpec((1, s, HEAD_DIM), lambda bi, h, i: (bi, 0, COL_KA + h)),
                  pl.BlockSpec((1, s, HEAD_DIM), lambda bi, h, i: (bi, 0, COL_VA + h))],
        out_specs=pl.BlockSpec((1, MOBA_BLOCK, gw), lambda bi, h, i: (bi, i, h)),
        out_shape=jax.ShapeDtypeStruct((b, s, MOBA_HEADS * HEAD_DIM), F32),
        scratch_shapes=[pltpu.VMEM((LANE, HEAD_DIM), F32),
                        pltpu.VMEM((MOBA_GROUP, nb, MOBA_BLOCK, LANE), F32)],
        compiler_params=pltpu.CompilerParams(
            dimension_semantics=("arbitrary", "arbitrary", "arbitrary"),
            vmem_limit_bytes=VMEM_LIMIT_BYTES),
        name="moba_prompt",
    )(proj, proj, proj)


KV_COLS = 2 * NSA_KV_HEADS * HEAD_DIM


def _compress_kernel(x0_ref, x1_ref, x2_ref, x3_ref, pos_ref, w1_ref, b1_ref, w2_ref, b2_ref,
                     o_ref, *, nch):
    x_refs = (x0_ref, x1_ref, x2_ref, x3_ref)
    for kv in range(2):
        for h in range(NSA_KV_HEADS):
            c = kv * NSA_KV_HEADS + h
            first = jnp.zeros((nch, CMP_HIDDEN), F32)
            second = jnp.zeros((nch, CMP_HIDDEN), F32)
            for r in range(CMP_STRIDE):
                xr = x_refs[c][0, pl.ds(r, nch, stride=CMP_STRIDE), :]
                r2 = CMP_STRIDE + r
                first += jnp.dot((xr + pos_ref[kv, r:r + 1, :]).astype(BF16),
                                 w1_ref[kv, r * HEAD_DIM:(r + 1) * HEAD_DIM, :].astype(BF16),
                                 preferred_element_type=F32)
                second += jnp.dot((xr + pos_ref[kv, r2:r2 + 1, :]).astype(BF16),
                                  w1_ref[kv, r2 * HEAD_DIM:(r2 + 1) * HEAD_DIM, :].astype(BF16),
                                  preferred_element_type=F32)
            pre = first + pltpu.roll(second, nch - 1, axis=0) + b1_ref[kv:kv + 1, :]
            hdn = jax.nn.gelu(pre)
            o_ref[0, c] = (jnp.dot(hdn.astype(BF16), w2_ref[kv].astype(BF16),
                                   preferred_element_type=F32) + b2_ref[kv:kv + 1, :])


def _compress_prompt(proj, pos, w1, b1, w2, b2):
    b, s, _ = proj.shape
    nch = s // CMP_STRIDE
    full = lambda shape: pl.BlockSpec(shape, lambda bi: (0,) * len(shape))
    return pl.pallas_call(
        functools.partial(_compress_kernel, nch=nch),
        grid=(b,),
        in_specs=[pl.BlockSpec((1, s, HEAD_DIM), functools.partial(lambda bi, c: (bi, 0, COL_CMP + c), c=c))
                  for c in range(2 * NSA_KV_HEADS)] + [
                  full(pos.shape), full(w1.shape), full(b1.shape), full(w2.shape), full(b2.shape)],
        out_specs=pl.BlockSpec((1, 2 * NSA_KV_HEADS, nch, HEAD_DIM), lambda bi: (bi, 0, 0, 0)),
        out_shape=jax.ShapeDtypeStruct((b, 2 * NSA_KV_HEADS, nch, HEAD_DIM), F32),
        compiler_params=pltpu.CompilerParams(dimension_semantics=("arbitrary",),
                                             vmem_limit_bytes=VMEM_LIMIT_BYTES),
        name="nsa_compress",
    )(proj, proj, proj, proj, pos, w1, b1, w2, b2)


NSA_TQ = 256
NSA_KB = 256


def _nsa_prompt_kernel(q_ref, kc_ref, vc_ref, ks_ref, vs_ref, kw_ref, vw_ref, gb_ref, o_ref,
                       mask_scr, *, s_len):
    kvh = pl.program_id(1)
    i = pl.program_id(2)
    tq, kb = NSA_TQ, NSA_KB
    n_cmp = s_len // CMP_STRIDE - CMP_LEN // CMP_STRIDE + 1
    n_slc = s_len // SLC_BLOCK
    n_top = min(SLC_TOPN, n_slc)
    nkb = s_len // kb
    dh = HEAD_DIM

    lane = lax.broadcasted_iota(jnp.int32, (tq, LANE), 1)
    t_col = i * tq + lax.broadcasted_iota(jnp.int32, (tq, 1), 0)
    avail = (lane * CMP_STRIDE + (CMP_LEN - 1) <= t_col) & (lane < n_cmp)
    nn = lax.broadcasted_iota(jnp.int32, (LANE, LANE), 0)
    jj = lax.broadcasted_iota(jnp.int32, (LANE, LANE), 1)
    sel_map = jnp.where((nn * CMP_STRIDE < jj * SLC_BLOCK + SLC_BLOCK)
                        & (nn * CMP_STRIDE + CMP_LEN > jj * SLC_BLOCK)
                        & (nn < n_cmp) & (jj < n_slc), 1.0, 0.0)
    gates = jax.nn.sigmoid(gb_ref[0])

    def gate_col(branch, g):
        c = branch * NSA_HEADS + kvh * NSA_GROUP + g
        return jnp.sum(jnp.where(lane == c, gates, 0.0), axis=-1, keepdims=True)

    imp = jnp.zeros((tq, LANE), F32)
    kc = kc_ref[0, 0].astype(BF16)
    vc = vc_ref[0, 0].astype(BF16)
    for g in range(NSA_GROUP):
        qb = q_ref[0, :, g * dh:(g + 1) * dh].astype(BF16)
        s_c = lax.dot_general(qb, kc, _NT, preferred_element_type=F32) * ATT_SCALE
        s_c = jnp.where(avail, s_c, NEG_INF)
        e = jnp.where(avail, jnp.exp(s_c - jnp.max(s_c, axis=-1, keepdims=True)), 0.0)
        den = jnp.sum(e, axis=-1, keepdims=True)
        p_c = e * jnp.where(den > 0.0, 1.0 / den, 0.0)
        o_c = jnp.dot(p_c.astype(BF16), vc, preferred_element_type=F32)
        imp = imp + jnp.dot(p_c, sel_map, precision=lax.Precision.HIGHEST,
                            preferred_element_type=F32)
        o_ref[0, :, g * dh:(g + 1) * dh] = gate_col(0, g) * o_c

    n_rows = _round_up(n_slc, 8)
    imp_t = imp.T[:n_rows]
    jio = lax.broadcasted_iota(jnp.int32, (n_rows, tq), 0)
    cur = (i * tq + lax.broadcasted_iota(jnp.int32, (n_rows, tq), 1)) // SLC_BLOCK
    forced = (jio == 0) | (jio == cur) | (jio == cur - 1)
    imp_t = jnp.where(forced, BIG, imp_t)
    imp_t = jnp.where(jio > cur, NEG_INF, imp_t)
    imp_t = jnp.where(jio < n_slc, imp_t, -jnp.inf)
    cnt = jnp.zeros((n_rows, tq), F32)
    for r in range(n_slc):
        vr = imp_t[r:r + 1, :]
        cnt = cnt + jnp.where((vr > imp_t) | ((vr == imp_t) & (r < jio)), 1.0, 0.0)
    sel_t = jnp.where((cnt < n_top) & (jio < n_slc), 1.0, 0.0)
    if n_rows < LANE:
        sel_t = jnp.concatenate([sel_t, jnp.zeros((LANE - n_rows, tq), F32)], axis=0)
    sel = sel_t.T.astype(BF16)
    erow = lax.broadcasted_iota(jnp.int32, (LANE, kb), 0)
    ecol = lax.broadcasted_iota(jnp.int32, (LANE, kb), 1)
    for jb in range(nkb):
        expand = jnp.where((jb * kb + ecol) // SLC_BLOCK == erow, 1.0, 0.0).astype(BF16)
        mask_scr[jb] = jnp.dot(sel, expand, preferred_element_type=F32)

    rowk = lax.broadcasted_iota(jnp.int32, (tq, kb), 0)
    colk = lax.broadcasted_iota(jnp.int32, (tq, kb), 1)
    causal = colk <= rowk
    own0 = pl.multiple_of(i * kb, kb)
    for g in range(NSA_GROUP):
        qb = q_ref[0, :, g * dh:(g + 1) * dh].astype(BF16)

        carry = _flash_update(qb, ks_ref[0, pl.ds(own0, kb), :], vs_ref[0, pl.ds(own0, kb), :],
                              (mask_scr[i] > 0.5) & causal, _flash_init(tq))

        def slc_body(j, carry, qb=qb):
            off = pl.multiple_of(j * kb, kb)
            return _flash_update(qb, ks_ref[0, pl.ds(off, kb), :], vs_ref[0, pl.ds(off, kb), :],
                                 mask_scr[j] > 0.5, carry)

        _, l, acc = lax.fori_loop(0, i, slc_body, carry)
        o_s = acc / l

        carry = _flash_update(qb, kw_ref[0, pl.ds(own0, kb), :], vw_ref[0, pl.ds(own0, kb), :],
                              causal, _flash_init(tq))

        def win_body(j, carry, qb=qb):
            off = pl.multiple_of(j * kb, kb)
            mask = (j * kb + colk) > (i * tq + rowk - WINDOW)
            return _flash_update(qb, kw_ref[0, pl.ds(off, kb), :], vw_ref[0, pl.ds(off, kb), :],
                                 mask, carry)

        _, l, acc = lax.fori_loop(jnp.maximum(i - WINDOW // kb, 0), i, win_body, carry)
        o_w = acc / l
        o_ref[0, :, g * dh:(g + 1) * dh] += gate_col(1, g) * o_s + gate_col(2, g) * o_w


def _nsa_prompt(proj, kvc, g_b):
    b, s, _ = proj.shape
    assert s % NSA_TQ == 0 and NSA_TQ == NSA_KB and s // CMP_STRIDE == LANE
    gw = NSA_GROUP * HEAD_DIM
    nkh = NSA_KV_HEADS
    seq = lambda col: pl.BlockSpec((1, s, HEAD_DIM), lambda bi, h, i: (bi, 0, col + h))
    return pl.pallas_call(
        functools.partial(_nsa_prompt_kernel, s_len=s),
        grid=(b, nkh, s // NSA_TQ),
        in_specs=[pl.BlockSpec((1, NSA_TQ, gw), lambda bi, h, i: (bi, i, COL_QB // NSA_GROUP + h)),
                  pl.BlockSpec((1, 1, LANE, HEAD_DIM), lambda bi, h, i: (bi, h, 0, 0)),
                  pl.BlockSpec((1, 1, LANE, HEAD_DIM), lambda bi, h, i: (bi, nkh + h, 0, 0)),
                  seq(COL_SLC), seq(COL_SLC + nkh), seq(COL_WIN), seq(COL_WIN + nkh),
                  pl.BlockSpec((1, NSA_TQ, LANE), lambda bi, h, i: (bi, i, 0))],
        out_specs=pl.BlockSpec((1, NSA_TQ, gw), lambda bi, h, i: (bi, i, h)),
        out_shape=jax.ShapeDtypeStruct((b, s, NSA_HEADS * HEAD_DIM), F32),
        scratch_shapes=[pltpu.VMEM((s // NSA_KB, NSA_TQ, NSA_KB), F32)],
        compiler_params=pltpu.CompilerParams(
            dimension_semantics=("arbitrary", "arbitrary", "arbitrary"),
            vmem_limit_bytes=VMEM_LIMIT_BYTES),
        name="nsa_prompt",
    )(proj, kvc, kvc, proj, proj, proj, proj, g_b)


def in_proj_sizes():
    return (MOBA_HEADS * HEAD_DIM, MOBA_KV_HEADS * HEAD_DIM, MOBA_KV_HEADS * HEAD_DIM,
            NSA_HEADS * HEAD_DIM, 6 * NSA_KV_HEADS * HEAD_DIM, 3 * NSA_HEADS, D_MODEL, D_MODEL)


def rms_norm(x, g):
    xf = x.astype(F32)
    y = xf * lax.rsqrt(jnp.mean(xf * xf, axis=-1, keepdims=True) + NORM_EPS)
    return (y * g.astype(F32)).astype(x.dtype)


def apply_rope(x, pos):
    half = ROT_DIM // 2
    inv_freq = ROPE_THETA ** (-jnp.arange(half, dtype=F32) / half)
    ang = pos.astype(F32)[:, None] * inv_freq[None, :]
    ang = ang.reshape((ang.shape[0],) + (1,) * (x.ndim - 3) + (half,))
    cos, sin = jnp.cos(ang), jnp.sin(ang)
    xr = x[..., :ROT_DIM].astype(F32)
    x1, x2 = xr[..., :half], xr[..., half:]
    rot = jnp.concatenate([x1 * cos - x2 * sin, x2 * cos + x1 * sin], axis=-1)
    return jnp.concatenate([rot.astype(x.dtype), x[..., ROT_DIM:]], axis=-1)


def query_block(s, pref):
    return pref if s % pref == 0 else s


def gather_pages(pool, page_table):
    g = pool[page_table]
    return g.reshape((g.shape[0], g.shape[1] * g.shape[2]) + g.shape[3:])


def moba_attention(q, k_all, v_all, q0):
    B, S, KVH, G, dh = q.shape
    L = k_all.shape[1]
    nb = -(-L // MOBA_BLOCK)
    pad = ((0, 0), (0, nb * MOBA_BLOCK - L), (0, 0), (0, 0))
    kb = jnp.pad(k_all, pad).reshape(B, nb, MOBA_BLOCK, KVH, dh).transpose(0, 3, 1, 2, 4)
    vb = jnp.pad(v_all, pad).reshape(B, nb, MOBA_BLOCK, KVH, dh).transpose(0, 3, 1, 2, 4)
    k_mean = jnp.mean(kb, axis=3, dtype=F32)
    n_sel = min(MOBA_TOPK, nb)
    qc = query_block(S, MOBA_QCHUNK)
    nq = S // qc
    scale = HEAD_DIM ** -0.5
    bi = jnp.arange(B)[:, None, None, None, None]
    gi = jnp.arange(KVH)[None, None, :, None, None]
    blk_ids = jnp.arange(nb)
    in_blk = jnp.arange(MOBA_BLOCK)

    def one_chunk(args):
        qch, start = args
        t = start + jnp.arange(qc)
        own = t // MOBA_BLOCK
        gate = jnp.einsum('bqghd,bgnd->bqghn', qch.astype(F32), k_mean)
        gate = jnp.where((blk_ids[None, :] < own[:, None])[None, :, None, None, :], gate, NEG_INF)
        _, sel = lax.top_k(gate, n_sel)
        slot_ok = (jnp.arange(n_sel)[None, :] < own[:, None])[None, :, None, None, :, None]
        k_sel = kb[bi, gi, sel]
        v_sel = vb[bi, gi, sel]
        s_sel = jnp.einsum('bqghd,bqghskd->bqghsk', qch, k_sel, preferred_element_type=F32) * scale
        s_sel = jnp.where(slot_ok, s_sel, NEG_INF).reshape(B, qc, KVH, G, n_sel * MOBA_BLOCK)
        k_own = kb[:, :, own]
        v_own = vb[:, :, own]
        s_own = jnp.einsum('bqghd,bgqkd->bqghk', qch, k_own, preferred_element_type=F32) * scale
        causal = (own[:, None] * MOBA_BLOCK + in_blk[None, :]) <= t[:, None]
        s_own = jnp.where(causal[None, :, None, None, :], s_own, NEG_INF)
        p = jax.nn.softmax(jnp.concatenate([s_sel, s_own], axis=-1), axis=-1).astype(v_all.dtype)
        p_sel = p[..., :n_sel * MOBA_BLOCK].reshape(B, qc, KVH, G, n_sel, MOBA_BLOCK)
        p_own = p[..., n_sel * MOBA_BLOCK:]
        return (jnp.einsum('bqghsk,bqghskd->bqghd', p_sel, v_sel)
                + jnp.einsum('bqghk,bgqkd->bqghd', p_own, v_own))

    q_chunks = q.reshape(B, nq, qc, KVH, G, dh).transpose(1, 0, 2, 3, 4, 5)
    o = lax.map(one_chunk, (q_chunks, q0 + qc * jnp.arange(nq)))
    return o.transpose(1, 0, 2, 3, 4, 5).reshape(B, S, KVH * G * dh)


def compress_rows(x, w_pos, w1, b1, w2, b2):
    B, L, KVH, dh = x.shape
    r = CMP_LEN // CMP_STRIDE
    n_chunks = L // CMP_STRIDE
    n_cmp = n_chunks - r + 1
    ch = x[:, :n_chunks * CMP_STRIDE].reshape(B, n_chunks, CMP_STRIDE, KVH, dh)
    blocks = jnp.concatenate([ch[:, j:j + n_cmp] for j in range(r)], axis=2)
    blocks = (blocks + w_pos[None, None, :, None, :]).transpose(0, 1, 3, 2, 4)
    flat = blocks.reshape(B, n_cmp, KVH, CMP_LEN * dh)
    return jax.nn.gelu(flat @ w1 + b1) @ w2 + b2


def cmp_to_slc_map(n_cmp, n_slc):
    lo = np.arange(n_cmp) * CMP_STRIDE
    blo = np.arange(n_slc) * SLC_BLOCK
    m = (lo[:, None] < blo[None, :] + SLC_BLOCK) & (lo[:, None] + CMP_LEN > blo[None, :])
    return jnp.asarray(m.astype(np.float32))


def nsa_compressed_selected(q, k_cmp, v_cmp, k_slc, v_slc, q0):
    B, S, KVH, G, dh = q.shape
    L = k_slc.shape[1]
    n_cmp = k_cmp.shape[1]
    n_slc = -(-L // SLC_BLOCK)
    n_top = min(SLC_TOPN, n_slc)
    pad = ((0, 0), (0, n_slc * SLC_BLOCK - L), (0, 0), (0, 0))
    kb = jnp.pad(k_slc, pad).reshape(B, n_slc, SLC_BLOCK, KVH, dh).transpose(0, 3, 1, 2, 4)
    vb = jnp.pad(v_slc, pad).reshape(B, n_slc, SLC_BLOCK, KVH, dh).transpose(0, 3, 1, 2, 4)
    sel_map = cmp_to_slc_map(n_cmp, n_slc)
    cmp_last = jnp.arange(n_cmp) * CMP_STRIDE + CMP_LEN - 1
    blk_ids = jnp.arange(n_slc)
    in_blk = jnp.arange(SLC_BLOCK)
    qc = query_block(S, NSA_QCHUNK)
    nq = S // qc
    scale = HEAD_DIM ** -0.5
    bi = jnp.arange(B)[:, None, None, None]
    gi = jnp.arange(KVH)[None, None, :, None]

    def one_chunk(args):
        qch, start = args
        t = start + jnp.arange(qc)
        avail = (cmp_last[None, :] <= t[:, None])[None, :, None, None, :]
        s_c = jnp.einsum('bqghd,bngd->bqghn', qch, k_cmp, preferred_element_type=F32) * scale
        p_c = jnp.where(avail, jax.nn.softmax(jnp.where(avail, s_c, NEG_INF), axis=-1), 0.0)
        o_c = jnp.einsum('bqghn,bngd->bqghd', p_c.astype(v_cmp.dtype), v_cmp)
        imp = jnp.einsum('bqghn,nj->bqgj', p_c, sel_map)
        cur = (t // SLC_BLOCK)[:, None]
        forced = (blk_ids == 0) | (blk_ids == cur) | (blk_ids == cur - 1)
        imp = jnp.where(forced[None, :, None, :], BIG, imp)
        imp = jnp.where((blk_ids > cur)[None, :, None, :], NEG_INF, imp)
        _, sel = lax.top_k(imp, n_top)
        k_sel = kb[bi, gi, sel]
        v_sel = vb[bi, gi, sel]
        key_pos = sel[..., None] * SLC_BLOCK + in_blk
        ok = key_pos <= t[None, :, None, None, None]
        s_s = jnp.einsum('bqghd,bqgskd->bqghsk', qch, k_sel, preferred_element_type=F32) * scale
        s_s = jnp.where(ok[:, :, :, None], s_s, NEG_INF).reshape(B, qc, KVH, G, n_top * SLC_BLOCK)
        p_s = jax.nn.softmax(s_s, axis=-1).astype(v_slc.dtype).reshape(B, qc, KVH, G, n_top, SLC_BLOCK)
        o_s = jnp.einsum('bqghsk,bqgskd->bqghd', p_s, v_sel)
        return o_c, o_s

    q_chunks = q.reshape(B, nq, qc, KVH, G, dh).transpose(1, 0, 2, 3, 4, 5)
    o_c, o_s = lax.map(one_chunk, (q_chunks, q0 + qc * jnp.arange(nq)))
    o_c = o_c.transpose(1, 0, 2, 3, 4, 5).reshape(B, S, KVH, G, dh)
    o_s = o_s.transpose(1, 0, 2, 3, 4, 5).reshape(B, S, KVH, G, dh)
    return o_c, o_s


def window_attention(q, k_win, v_win, q0, k0):
    B, S, KVH, G, dh = q.shape
    qb = query_block(S, WIN_QBLOCK)
    nq = S // qb
    span = WINDOW + qb
    pad = ((0, 0), (WINDOW, 0), (0, 0), (0, 0))
    kp = jnp.pad(k_win, pad)
    vp = jnp.pad(v_win, pad)
    scale = HEAD_DIM ** -0.5

    def one_block(args):
        qblk, i = args
        s0 = q0 - k0 + i * qb
        kk = lax.dynamic_slice_in_dim(kp, s0, span, axis=1)
        vv = lax.dynamic_slice_in_dim(vp, s0, span, axis=1)
        t = q0 + i * qb + jnp.arange(qb)
        kpos = k0 - WINDOW + s0 + jnp.arange(span)
        ok = ((kpos[None, :] > t[:, None] - WINDOW) & (kpos[None, :] <= t[:, None])
              & (kpos[None, :] >= k0))
        s = jnp.einsum('bqghd,bkgd->bqghk', qblk, kk, preferred_element_type=F32) * scale
        s = jnp.where(ok[None, :, None, None, :], s, NEG_INF)
        p = jax.nn.softmax(s, axis=-1).astype(vv.dtype)
        return jnp.einsum('bqghk,bkgd->bqghd', p, vv)

    q_blocks = q.reshape(B, nq, qb, KVH, G, dh).transpose(1, 0, 2, 3, 4, 5)
    o = lax.map(one_block, (q_blocks, jnp.arange(nq)))
    return o.transpose(1, 0, 2, 3, 4, 5).reshape(B, S, KVH, G, dh)


def token_mixer(h, q0, past, lp):
    B, S, _ = h.shape
    cuts = [int(c) for c in np.cumsum(in_proj_sizes())[:-1]]
    q_a, k_a, v_a, q_b, kv_b, g_b, g_ma, g_mb = jnp.split(_mm3(h, lp['w_in']), cuts, axis=-1)
    pos = q0 + jnp.arange(S)
    q_a = apply_rope(q_a.reshape(B, S, MOBA_KV_HEADS, MOBA_GROUP, HEAD_DIM), pos)
    k_a = apply_rope(k_a.reshape(B, S, MOBA_KV_HEADS, HEAD_DIM), pos)
    v_a = v_a.reshape(B, S, MOBA_KV_HEADS, HEAD_DIM)
    q_b = apply_rope(q_b.reshape(B, S, NSA_KV_HEADS, NSA_GROUP, HEAD_DIM), pos)
    kv_b = kv_b.reshape(B, S, 3, 2, NSA_KV_HEADS, HEAD_DIM)
    new_moba = jnp.stack([k_a, v_a], axis=2)
    new_cmp = kv_b[:, :, 0]
    new_slc = jnp.stack([apply_rope(kv_b[:, :, 1, 0], pos), kv_b[:, :, 1, 1]], axis=2)
    new_win = jnp.stack([apply_rope(kv_b[:, :, 2, 0], pos), kv_b[:, :, 2, 1]], axis=2)
    if past is None:
        o_a = _moba_prompt(q_a.reshape(B, S, -1), new_moba.reshape(B, S, -1))
        kvc = _compress_prompt(new_cmp.reshape(B, S, -1), lp['cmp_pos'], lp['cmp_w1'], lp['cmp_b1'],
                               lp['cmp_w2'], lp['cmp_b2'])
        o_b = _nsa_prompt(q_b.reshape(B, S, -1), kvc, new_slc.reshape(B, S, -1),
                          new_win.reshape(B, S, -1),
                          jnp.pad(g_b, ((0, 0), (0, 0), (0, LANE - g_b.shape[-1]))))
        merged = (jax.nn.sigmoid(g_ma) * _mm3(o_a, lp['w_pa'])
                  + jax.nn.sigmoid(g_mb) * _mm3(o_b, lp['w_pb']))
        win_state = new_win[:, S - min(WINDOW, S):]
        return _mm3(merged, lp['w_out']), (new_moba, new_cmp, new_slc, win_state)
    else:
        moba_kv = jnp.concatenate([past[0], new_moba], axis=1)
        cmp_kv = jnp.concatenate([past[1], new_cmp], axis=1)
        slc_kv = jnp.concatenate([past[2], new_slc], axis=1)
        win_kv = jnp.concatenate([past[3], new_win], axis=1)
    k0 = q0 + S - win_kv.shape[1]
    win_state = win_kv[:, win_kv.shape[1] - min(WINDOW, win_kv.shape[1]):]

    o_a = moba_attention(q_a, moba_kv[:, :, 0], moba_kv[:, :, 1], q0)
    k_c = compress_rows(cmp_kv[:, :, 0], lp['cmp_pos'][0], lp['cmp_w1'][0], lp['cmp_b1'][0],
                        lp['cmp_w2'][0], lp['cmp_b2'][0])
    v_c = compress_rows(cmp_kv[:, :, 1], lp['cmp_pos'][1], lp['cmp_w1'][1], lp['cmp_b1'][1],
                        lp['cmp_w2'][1], lp['cmp_b2'][1])
    o_c, o_s = nsa_compressed_selected(q_b, k_c, v_c, slc_kv[:, :, 0], slc_kv[:, :, 1], q0)
    o_w = window_attention(q_b, win_kv[:, :, 0], win_kv[:, :, 1], q0, k0)
    gates = jax.nn.sigmoid(g_b.reshape(B, S, 3, NSA_KV_HEADS, NSA_GROUP, 1))
    o_b = (gates[:, :, 0] * o_c + gates[:, :, 1] * o_s + gates[:, :, 2] * o_w).reshape(B, S, NSA_HEADS * HEAD_DIM)
    merged = (jax.nn.sigmoid(g_ma) * _mm3(o_a, lp['w_pa'])
              + jax.nn.sigmoid(g_mb) * _mm3(o_b, lp['w_pb']))
    return _mm3(merged, lp['w_out']), (new_moba, new_cmp, new_slc, win_state)


def routed_experts(x, expert_ids, weights, w_gate, w_up, w_down):
    T, D = x.shape
    A = T * EXPERT_TOPK
    flat_e = expert_ids.reshape(A)
    order = jnp.argsort(flat_e)
    e_sorted = flat_e[order]
    tok_sorted = order // EXPERT_TOPK
    counts = jnp.zeros((N_EXPERTS,), jnp.int32).at[flat_e].add(1)
    padded = (counts + MOE_BLOCK - 1) // MOE_BLOCK * MOE_BLOCK
    pad_end = jnp.cumsum(padded)
    pad_start = pad_end - padded
    start = jnp.cumsum(counts) - counts
    dest = pad_start[e_sorted] + jnp.arange(A) - start[e_sorted]
    n_blk = -(-(A + N_EXPERTS * (MOE_BLOCK - 1)) // MOE_BLOCK)
    row_tok = jnp.zeros((n_blk * MOE_BLOCK,), jnp.int32).at[dest].set(tok_sorted)
    blk_expert = jnp.minimum(jnp.searchsorted(pad_end, jnp.arange(n_blk) * MOE_BLOCK, side='right'),
                             N_EXPERTS - 1)
    xb = x[row_tok].reshape(n_blk, MOE_BLOCK, D)

    def expert_block(args):
        xe, e = args
        return (jax.nn.silu(xe @ w_gate[e]) * (xe @ w_up[e])) @ w_down[e]

    yb = lax.map(expert_block, (xb, blk_expert)).reshape(n_blk * MOE_BLOCK, D)
    contrib = yb[dest] * weights.reshape(A)[order][:, None].astype(x.dtype)
    return jnp.zeros((T, D), x.dtype).at[tok_sorted].add(contrib)


def hierarchical_moe(h, lp):
    B, S, D = h.shape
    x = h.reshape(B * S, D)
    g_prob = jax.nn.softmax((x @ lp['w_rg']).astype(F32) + lp['b_rg'].astype(F32), axis=-1)
    g_w, g_idx = lax.top_k(g_prob, 1)
    e_logit = ((x @ lp['w_re']).astype(F32) + lp['b_re'].astype(F32)).reshape(-1, N_GROUPS, EXPERTS_PER_GROUP)
    e_logit = jnp.take_along_axis(e_logit, g_idx[:, :, None], axis=1)[:, 0]
    e_val, e_idx = lax.top_k(e_logit, EXPERT_TOPK)
    weights = g_w * jax.nn.softmax(e_val, axis=-1)
    expert_ids = g_idx * EXPERTS_PER_GROUP + e_idx
    y = routed_experts(x, expert_ids, weights, lp['w_gate'], lp['w_up'], lp['w_down'])
    return y.reshape(B, S, D)


def decoder_layer(x, c, q0, past, lp):
    ada = _mm(jax.nn.silu(c), lp['w_ada'], tn=2048) + lp['b_ada']
    sh1, sc1, g1, sh2, sc2, g2 = jnp.split(ada[:, None, :], 6, axis=-1)
    h = rms_norm(x, lp['norm_attn']) * (1 + sc1) + sh1
    mix, rows = token_mixer(h, q0, past, lp)
    x = x + g1 * mix
    h = rms_norm(x, lp['norm_ffn']) * (1 + sc2) + sh2
    x = x + g2 * hierarchical_moe(h, lp)
    return x, rows


def _unused_kernel(x_prompt, x_sample, cache_moba_kv, cache_cmp_kv, cache_slc_kv, state_win_kv,
           page_table, c_prompt, c_sample, w_in, w_pa, w_pb, w_out, cmp_pos, cmp_w1, cmp_b1,
           cmp_w2, cmp_b2, norm_attn, norm_ffn, norm_final, w_ada, b_ada, w_rg, b_rg, w_re,
           b_re, w_gate, w_up, w_down):
    past_len = page_table.shape[1] * cache_moba_kv.shape[2]
    y_p, y_s = x_prompt, x_sample
    rows_p, rows_s = [], []
    for l in range(DEPTH):
        lp = {'w_in': w_in[l], 'w_pa': w_pa[l], 'w_pb': w_pb[l], 'w_out': w_out[l],
              'cmp_pos': cmp_pos[l], 'cmp_w1': cmp_w1[l], 'cmp_b1': cmp_b1[l],
              'cmp_w2': cmp_w2[l], 'cmp_b2': cmp_b2[l], 'norm_attn': norm_attn[l],
              'norm_ffn': norm_ffn[l], 'w_ada': w_ada[l], 'b_ada': b_ada[l],
              'w_rg': w_rg[l], 'b_rg': b_rg[l], 'w_re': w_re[l], 'b_re': b_re[l],
              'w_gate': w_gate[l], 'w_up': w_up[l], 'w_down': w_down[l]}
        y_p, r_p = decoder_layer(y_p, c_prompt, 0, None, lp)
        past = (gather_pages(cache_moba_kv[l], page_table),
                gather_pages(cache_cmp_kv[l], page_table),
                gather_pages(cache_slc_kv[l], page_table),
                state_win_kv[l])
        y_s, r_s = decoder_layer(y_s, c_sample, past_len, past, lp)
        rows_p.append(r_p)
        rows_s.append(r_s)
    y_prompt = rms_norm(y_p, norm_final)
    y_sample = rms_norm(y_s, norm_final)
    return (y_prompt, y_sample,
            jnp.stack([r[0] for r in rows_p]), jnp.stack([r[0] for r in rows_s]),
            jnp.stack([r[1] for r in rows_p]), jnp.stack([r[1] for r in rows_s]),
            jnp.stack([r[2] for r in rows_p]), jnp.stack([r[2] for r in rows_s]),
            jnp.stack([r[3] for r in rows_p]), jnp.stack([r[3] for r in rows_s]))


PROMPT_TM_IN = 512
PROMPT_TM_OUT = 256
MOE_TM = 128
SAMPLE_MOE_BLOCK = 8


def _kv_rows(proj, col, b, s):
    lo = col * HEAD_DIM
    return proj[:, lo:lo + KV_COLS].reshape(b, s, 2, NSA_KV_HEADS, HEAD_DIM)


def _sample_mixer(proj, g_b, past, b, s, q0):
    q_a = proj[:, :MOBA_HEADS * HEAD_DIM].reshape(b, s, MOBA_KV_HEADS, MOBA_GROUP, HEAD_DIM)
    q_b = proj[:, COL_QB * HEAD_DIM:COL_CMP * HEAD_DIM].reshape(b, s, NSA_KV_HEADS, NSA_GROUP, HEAD_DIM)
    new_moba = _kv_rows(proj, COL_KA, b, s)
    new_cmp = _kv_rows(proj, COL_CMP, b, s)
    new_slc = _kv_rows(proj, COL_SLC, b, s)
    new_win = _kv_rows(proj, COL_WIN, b, s)
    moba_kv = jnp.concatenate([past[0], new_moba], axis=1)
    cmp_kv = jnp.concatenate([past[1], new_cmp], axis=1)
    slc_kv = jnp.concatenate([past[2], new_slc], axis=1)
    win_kv = jnp.concatenate([past[3], new_win], axis=1)
    k0 = q0 + s - win_kv.shape[1]
    win_state = win_kv[:, win_kv.shape[1] - min(WINDOW, win_kv.shape[1]):]
    lpc = past[4]
    o_a = moba_attention(q_a, moba_kv[:, :, 0], moba_kv[:, :, 1], q0)
    k_c = compress_rows(cmp_kv[:, :, 0], lpc['cmp_pos'][0], lpc['cmp_w1'][0], lpc['cmp_b1'][0],
                        lpc['cmp_w2'][0], lpc['cmp_b2'][0])
    v_c = compress_rows(cmp_kv[:, :, 1], lpc['cmp_pos'][1], lpc['cmp_w1'][1], lpc['cmp_b1'][1],
                        lpc['cmp_w2'][1], lpc['cmp_b2'][1])
    o_c, o_s = nsa_compressed_selected(q_b, k_c, v_c, slc_kv[:, :, 0], slc_kv[:, :, 1], q0)
    o_w = window_attention(q_b, win_kv[:, :, 0], win_kv[:, :, 1], q0, k0)
    gates = jax.nn.sigmoid(g_b[:, :GB_COLS].reshape(b, s, 3, NSA_KV_HEADS, NSA_GROUP, 1))
    o_b = (gates[:, :, 0] * o_c + gates[:, :, 1] * o_s + gates[:, :, 2] * o_w)
    return (o_a.reshape(b * s, -1), o_b.reshape(b * s, -1),
            (new_moba, new_cmp, new_slc, win_state))


def _moe_block(x, h, logits, g2, norm_final, w_gate, w_up, w_down, bm, tm, rows_per_mod, final_norm):
    blk_expert, row_tok, n_used, row_w, d0, d1 = _route(logits, bm)
    yb = _moe_ffn(h, blk_expert, row_tok, n_used, row_w, w_gate, w_up, w_down, bm)
    return _moe_combine(x, g2, norm_final, yb, d0, d1, tm, rows_per_mod, final_norm)


def kernel(x_prompt, x_sample, cache_moba_kv, cache_cmp_kv, cache_slc_kv, state_win_kv,
           page_table, c_prompt, c_sample, w_in, w_pa, w_pb, w_out, cmp_pos, cmp_w1, cmp_b1,
           cmp_w2, cmp_b2, norm_attn, norm_ffn, norm_final, w_ada, b_ada, w_rg, b_rg, w_re,
           b_re, w_gate, w_up, w_down):
    bp, sp, d = x_prompt.shape
    bs, ss, _ = x_sample.shape
    tp, ts = bp * sp, bs * ss
    past_len = page_table.shape[1] * cache_moba_kv.shape[2]
    xp = x_prompt.reshape(tp, d)
    xs = x_sample.reshape(ts, d)
    cos_p, sin_p = _rope_tables(jnp.tile(jnp.arange(sp), bp))
    cos_s, sin_s = _rope_tables(jnp.tile(past_len + jnp.arange(ss), bs))
    n_c = _round_up(bp + bs, 8)
    c_all = jnp.pad(jnp.concatenate([c_prompt, c_sample], axis=0), ((0, n_c - bp - bs), (0, 0)))
    rows_p, rows_s = [], []
    for l in range(DEPTH):
        last = l == DEPTH - 1
        ada = _ada(c_all, w_ada[l], b_ada[l]).reshape(n_c, 6, d)
        mod_p = [ada[:bp, k].reshape(bp, 1, d) for k in range(6)]
        mod_s = [jnp.repeat(ada[bp:bp + bs, k], ss, axis=0).reshape(1, ts, d) for k in range(6)]
        w_main = jnp.concatenate([w_in[l][:, :QKV_COLS], w_in[l][:, QKV_COLS + GB_COLS:]],
                                 axis=1).astype(BF16)
        w_gb = jnp.pad(w_in[l][:, QKV_COLS:QKV_COLS + GB_COLS],
                       ((0, 0), (0, LANE - GB_COLS))).astype(BF16)
        w_pa_b, w_pb_b, w_out_b = w_pa[l].astype(BF16), w_pb[l].astype(BF16), w_out[l].astype(BF16)
        w_r = jnp.pad(jnp.concatenate([w_rg[l], w_re[l]], axis=1),
                      ((0, 0), (0, LANE - N_GROUPS - N_EXPERTS)))
        b_r = jnp.pad(jnp.concatenate([b_rg[l], b_re[l]]),
                      (0, LANE - N_GROUPS - N_EXPERTS)).reshape(1, LANE)

        sh1, sc1, g1, sh2, sc2, g2 = mod_p
        proj, g_b = _in_proj(xp, norm_attn[l], sc1, sh1, cos_p, sin_p, w_main, w_gb,
                             PROMPT_TM_IN, sp)
        proj3 = proj.reshape(bp, sp, PROJ_COLS)
        o_a = _moba_prompt(proj3)
        kvc = _compress_prompt(proj3, cmp_pos[l], cmp_w1[l], cmp_b1[l], cmp_w2[l], cmp_b2[l])
        o_b = _nsa_prompt(proj3, kvc, g_b.reshape(bp, sp, LANE))
        xp, h2, logits = _out_proj(o_a.reshape(tp, -1), o_b.reshape(tp, -1), proj, xp, g1,
                                   norm_ffn[l], sc2, sh2, w_pa_b, w_pb_b, w_out_b, w_r, b_r,
                                   PROMPT_TM_OUT, sp)
        xp = _moe_block(xp, h2, logits, g2, norm_final, w_gate[l], w_up[l], w_down[l],
                        MOE_BLOCK, MOE_TM, sp, last)
        new_win = _kv_rows(proj, COL_WIN, bp, sp)
        rows_p.append((_kv_rows(proj, COL_KA, bp, sp), _kv_rows(proj, COL_CMP, bp, sp),
                       _kv_rows(proj, COL_SLC, bp, sp), new_win[:, sp - min(WINDOW, sp):]))

        sh1, sc1, g1, sh2, sc2, g2 = mod_s
        proj, g_b = _in_proj(xs, norm_attn[l], sc1, sh1, cos_s, sin_s, w_main, w_gb, ts, ss)
        lpc = {'cmp_pos': cmp_pos[l], 'cmp_w1': cmp_w1[l], 'cmp_b1': cmp_b1[l],
               'cmp_w2': cmp_w2[l], 'cmp_b2': cmp_b2[l]}
        past = (gather_pages(cache_moba_kv[l], page_table),
                gather_pages(cache_cmp_kv[l], page_table),
                gather_pages(cache_slc_kv[l], page_table),
                state_win_kv[l], lpc)
        o_a, o_b, r_s = _sample_mixer(proj, g_b, past, bs, ss, past_len)
        xs, h2, logits = _out_proj(o_a, o_b, proj, xs, g1, norm_ffn[l], sc2, sh2,
                                   w_pa_b, w_pb_b, w_out_b, w_r, b_r, ts, ss)
        xs = _moe_block(xs, h2, logits, g2, norm_final, w_gate[l], w_up[l], w_down[l],
                        SAMPLE_MOE_BLOCK, ts, ss, last)
        rows_s.append(r_s)

    return (xp.reshape(bp, sp, d), xs.reshape(bs, ss, d),
            jnp.stack([r[0] for r in rows_p]), jnp.stack([r[0] for r in rows_s]),
            jnp.stack([r[1] for r in rows_p]), jnp.stack([r[1] for r in rows_s]),
            jnp.stack([r[2] for r in rows_p]), jnp.stack([r[2] for r in rows_s]),
            jnp.stack([r[3] for r in rows_p]), jnp.stack([r[3] for r in rows_s]))
```

```python
import functools

import jax
import jax.numpy as jnp
import numpy as np
from jax import lax
from jax.experimental import pallas as pl
from jax.experimental.pallas import tpu as pltpu

D_MODEL = 2048
DEPTH = 2
HEAD_DIM = 128
ROT_DIM = HEAD_DIM // 4
ROPE_THETA = 500000.0
NORM_EPS = 1e-6
MOBA_HEADS = 8
MOBA_KV_HEADS = 2
MOBA_GROUP = MOBA_HEADS // MOBA_KV_HEADS
MOBA_BLOCK = 256
MOBA_TOPK = 3
MOBA_QCHUNK = 16
NSA_HEADS = 8
NSA_KV_HEADS = 2
NSA_GROUP = NSA_HEADS // NSA_KV_HEADS
CMP_LEN = 32
CMP_STRIDE = 16
CMP_HIDDEN = 128
SLC_BLOCK = 64
SLC_TOPN = 16
WINDOW = 512
NSA_QCHUNK = 16
WIN_QBLOCK = 128
N_GROUPS = 4
EXPERTS_PER_GROUP = 8
N_EXPERTS = N_GROUPS * EXPERTS_PER_GROUP
EXPERT_TOPK = 2
D_EXPERT = 512
MOE_BLOCK = 128

NEG_INF = -1e30
BIG = 1e30
F32 = jnp.float32
BF16 = jnp.bfloat16

LANE = 128
VMEM_LIMIT_BYTES = 48 * 1024 * 1024


def _round_up(n, m):
    return (n + m - 1) // m * m


QKV_COLS = 4096
GB_COLS = 3 * NSA_HEADS
GATE_COLS = 2 * D_MODEL
PROJ_COLS = QKV_COLS + GATE_COLS
COL_QA, COL_KA, COL_VA, COL_QB = 0, 8, 10, 12
COL_CMP, COL_SLC, COL_WIN = 20, 24, 28


def _mod_spec(mod, tm, rows_per_mod):
    d = mod.shape[-1]
    if mod.shape[1] == 1:
        return pl.BlockSpec((1, 1, d), lambda i, *_: (i * tm // rows_per_mod, 0, 0))
    return pl.BlockSpec((1, tm, d), lambda i, *_: (i, 0, 0))


def _ada_kernel(c_ref, w_ref, b_ref, o_ref):
    c = jax.nn.silu(c_ref[...]).astype(BF16)
    o_ref[...] = jnp.dot(c, w_ref[0].astype(BF16), preferred_element_type=F32) + b_ref[0]


def _ada(c, w_ada, b_ada, l, tn=1024):
    m, k = c.shape
    depth, _, n = w_ada.shape
    return pl.pallas_call(
        _ada_kernel,
        grid=(n // tn,),
        in_specs=[pl.BlockSpec((m, k), lambda j: (0, 0)),
                  pl.BlockSpec((1, k, tn), lambda j: (l, 0, j)),
                  pl.BlockSpec((1, 1, tn), lambda j: (l, 0, j))],
        out_specs=pl.BlockSpec((m, tn), lambda j: (0, j)),
        out_shape=jax.ShapeDtypeStruct((m, n), F32),
        compiler_params=pltpu.CompilerParams(dimension_semantics=("arbitrary",),
                                             vmem_limit_bytes=VMEM_LIMIT_BYTES),
        name="ada_ln",
    )(c, w_ada, b_ada.reshape(depth, 1, n))


def _rope_tables(pos):
    half = ROT_DIM // 2
    inv_freq = ROPE_THETA ** (-jnp.arange(half, dtype=F32) / half)
    ang = pos.astype(F32)[:, None] * inv_freq[None, :]
    cos, sin = jnp.cos(ang), jnp.sin(ang)
    n = pos.shape[0]
    cos_t = jnp.concatenate([cos, cos, jnp.ones((n, HEAD_DIM - ROT_DIM), F32)], axis=1)
    sin_t = jnp.concatenate([-sin, sin, jnp.zeros((n, HEAD_DIM - ROT_DIM), F32)], axis=1)
    return cos_t, sin_t


def _rope_column_mask():
    m = np.zeros((1, PROJ_COLS), np.float32)
    for lo, hi in ((0, 1280), (1536, 2560), (3072, 3328), (3584, 3840)):
        m[0, lo:hi] = 1.0
    return jnp.asarray(m)


def _in_proj_kernel(x_ref, g_ref, sc_ref, sh_ref, cos_ref, sin_ref, rmask_ref, w_ref, wgb_ref,
                    o_ref, ogb_ref, hb_scr, *, tn):
    j = pl.program_id(1)

    @pl.when(j == 0)
    def _():
        x = x_ref[...]
        y = x * lax.rsqrt(jnp.mean(x * x, axis=-1, keepdims=True) + NORM_EPS) * g_ref[...]
        h = (y * (1.0 + sc_ref[0]) + sh_ref[0]).astype(BF16)
        hb_scr[...] = h
        ogb_ref[...] = jnp.dot(h, wgb_ref[...], preferred_element_type=F32)

    acc = jnp.dot(hb_scr[...], w_ref[...], preferred_element_type=F32)

    @pl.when(j < QKV_COLS // tn)
    def _():
        lane = lax.broadcasted_iota(jnp.int32, (acc.shape[0], HEAD_DIM), 1)
        cos_t, sin_t = cos_ref[...], sin_ref[...]
        for hh in range(tn // HEAD_DIM):
            blk = acc[:, hh * HEAD_DIM:(hh + 1) * HEAD_DIM]
            on = rmask_ref[:, hh * HEAD_DIM:(hh + 1) * HEAD_DIM] > 0.5
            partner = jnp.where(lane < ROT_DIM // 2,
                                pltpu.roll(blk, HEAD_DIM - ROT_DIM // 2, axis=1),
                                pltpu.roll(blk, ROT_DIM // 2, axis=1))
            o_ref[:, hh * HEAD_DIM:(hh + 1) * HEAD_DIM] = (
                blk * jnp.where(on, cos_t, 1.0) + partner * jnp.where(on, sin_t, 0.0))

    @pl.when(j >= QKV_COLS // tn)
    def _():
        o_ref[...] = jax.nn.sigmoid(acc)


def _in_proj(x, g, sc, sh, cos_t, sin_t, w_main, w_gb, tm, rows_per_mod, tn=1024):
    t, d = x.shape
    rmask = _rope_column_mask()
    return pl.pallas_call(
        functools.partial(_in_proj_kernel, tn=tn),
        grid=(t // tm, PROJ_COLS // tn),
        in_specs=[pl.BlockSpec((tm, d), lambda i, j: (i, 0)),
                  pl.BlockSpec((1, d), lambda i, j: (0, 0)),
                  _mod_spec(sc, tm, rows_per_mod), _mod_spec(sh, tm, rows_per_mod),
                  pl.BlockSpec((tm, HEAD_DIM), lambda i, j: (i, 0)),
                  pl.BlockSpec((tm, HEAD_DIM), lambda i, j: (i, 0)),
                  pl.BlockSpec((1, tn), lambda i, j: (0, j)),
                  pl.BlockSpec((d, tn), lambda i, j: (0, j)),
                  pl.BlockSpec((d, LANE), lambda i, j: (0, 0))],
        out_specs=[pl.BlockSpec((tm, tn), lambda i, j: (i, j)),
                   pl.BlockSpec((tm, LANE), lambda i, j: (i, 0))],
        out_shape=[jax.ShapeDtypeStruct((t, PROJ_COLS), F32),
                   jax.ShapeDtypeStruct((t, LANE), F32)],
        scratch_shapes=[pltpu.VMEM((tm, d), BF16)],
        compiler_params=pltpu.CompilerParams(dimension_semantics=("arbitrary", "arbitrary"),
                                             vmem_limit_bytes=VMEM_LIMIT_BYTES),
        name="in_proj",
    )(x, g.reshape(1, d), sc, sh, cos_t, sin_t, rmask, w_main, w_gb)


def _out_proj_kernel(oa_ref, ob_ref, gma_ref, gmb_ref, x_ref, g1_ref, gf_ref, sc_ref, sh_ref,
                     wpa_ref, wpb_ref, wout_ref, wr_ref, br_ref, xo_ref, h_ref, lg_ref):
    pa = jnp.dot(oa_ref[...].astype(BF16), wpa_ref[...], preferred_element_type=F32)
    pb = jnp.dot(ob_ref[...].astype(BF16), wpb_ref[...], preferred_element_type=F32)
    merged = (gma_ref[...] * pa + gmb_ref[...] * pb).astype(BF16)
    mix = jnp.dot(merged, wout_ref[...], preferred_element_type=F32)
    x = x_ref[...] + g1_ref[0] * mix
    xo_ref[...] = x
    y = x * lax.rsqrt(jnp.mean(x * x, axis=-1, keepdims=True) + NORM_EPS) * gf_ref[...]
    h = y * (1.0 + sc_ref[0]) + sh_ref[0]
    h_ref[...] = h
    lg_ref[...] = jnp.dot(h, wr_ref[...], precision=lax.Precision.HIGHEST,
                          preferred_element_type=F32) + br_ref[...]


def _out_proj(o_a, o_b, proj, x, g1, gf, sc, sh, w_pa, w_pb, w_out, w_r, b_r, tm, rows_per_mod):
    t, d = x.shape
    da = o_a.shape[1]
    const = lambda shape: pl.BlockSpec(shape, lambda i: (0,) * len(shape),
                                       pipeline_mode=pl.Buffered(1))
    gate_blk = QKV_COLS // d
    return pl.pallas_call(
        _out_proj_kernel,
        grid=(t // tm,),
        in_specs=[pl.BlockSpec((tm, da), lambda i: (i, 0)),
                  pl.BlockSpec((tm, da), lambda i: (i, 0)),
                  pl.BlockSpec((tm, d), lambda i: (i, gate_blk)),
                  pl.BlockSpec((tm, d), lambda i: (i, gate_blk + 1)),
                  pl.BlockSpec((tm, d), lambda i: (i, 0)),
                  _mod_spec(g1, tm, rows_per_mod),
                  const((1, d)),
                  _mod_spec(sc, tm, rows_per_mod), _mod_spec(sh, tm, rows_per_mod),
                  const((da, d)), const((da, d)), const((d, d)), const((d, LANE)), const((1, LANE))],
        out_specs=[pl.BlockSpec((tm, d), lambda i: (i, 0)),
                   pl.BlockSpec((tm, d), lambda i: (i, 0)),
                   pl.BlockSpec((tm, LANE), lambda i: (i, 0))],
        out_shape=[jax.ShapeDtypeStruct((t, d), F32), jax.ShapeDtypeStruct((t, d), F32),
                   jax.ShapeDtypeStruct((t, LANE), F32)],
        compiler_params=pltpu.CompilerParams(dimension_semantics=("arbitrary",),
                                             vmem_limit_bytes=VMEM_LIMIT_BYTES),
        name="out_proj",
    )(o_a, o_b, proj, proj, x, g1, gf.reshape(1, d), sc, sh, w_pa, w_pb, w_out, w_r, b_r)


def _moe_ffn_kernel(blk_e_ref, row_tok_ref, nused_ref, x_hbm, roww_ref, wg_ref, wu_ref, wd_ref,
                    o_ref, xbuf, sem, wg_bf, wu_bf, wd_bf, *, bm):
    i = pl.program_id(0)
    n_used = nused_ref[0]

    def issue(blk, slot):
        def one(r, c):
            tok = row_tok_ref[blk * bm + r]
            pltpu.make_async_copy(x_hbm.at[pl.ds(tok, 1)], xbuf.at[slot, pl.ds(r, 1)],
                                  sem.at[slot]).start()
            return c
        lax.fori_loop(0, bm, one, 0, unroll=8)

    def wait(slot):
        pltpu.make_async_copy(x_hbm.at[pl.ds(0, bm)], xbuf.at[slot], sem.at[slot]).wait()

    slot = i % 2

    @pl.when((i == 0) & (n_used > 0))
    def _():
        issue(0, 0)

    @pl.when(i < n_used)
    def _():
        wait(slot)

        @pl.when(i + 1 < n_used)
        def _():
            issue(i + 1, 1 - slot)

        prev = blk_e_ref[jnp.maximum(i - 1, 0)]

        @pl.when((i == 0) | (blk_e_ref[i] != prev))
        def _():
            wg_bf[...] = wg_ref[0, 0].astype(BF16)
            wu_bf[...] = wu_ref[0, 0].astype(BF16)
            wd_bf[...] = wd_ref[0, 0].astype(BF16)

        x = xbuf[slot].astype(BF16)
        hg = jnp.dot(x, wg_bf[...], preferred_element_type=F32)
        hu = jnp.dot(x, wu_bf[...], preferred_element_type=F32)
        act = (jax.nn.silu(hg) * hu).astype(BF16)
        y = jnp.dot(act, wd_bf[...], preferred_element_type=F32)
        o_ref[...] = y * roww_ref[...]

    @pl.when(i >= n_used)
    def _():
        o_ref[...] = jnp.zeros_like(o_ref)


def _moe_ffn(x, blk_expert, row_tok, n_used, row_w, w_gate, w_up, w_down, l, bm):
    n_rows = row_tok.shape[0]
    n_blk = n_rows // bm
    d = x.shape[1]
    de = w_gate.shape[3]
    grid_spec = pltpu.PrefetchScalarGridSpec(
        num_scalar_prefetch=3,
        grid=(n_blk,),
        in_specs=[pl.BlockSpec(memory_space=pl.ANY),
                  pl.BlockSpec((bm, 1), lambda i, be, rt, nu: (i, 0)),
                  pl.BlockSpec((1, 1, d, de), lambda i, be, rt, nu: (l, be[i], 0, 0)),
                  pl.BlockSpec((1, 1, d, de), lambda i, be, rt, nu: (l, be[i], 0, 0)),
                  pl.BlockSpec((1, 1, de, d), lambda i, be, rt, nu: (l, be[i], 0, 0))],
        out_specs=pl.BlockSpec((bm, d), lambda i, be, rt, nu: (i, 0)),
        scratch_shapes=[pltpu.VMEM((2, bm, d), F32),
                        pltpu.SemaphoreType.DMA((2,)),
                        pltpu.VMEM((d, de), BF16), pltpu.VMEM((d, de), BF16),
                        pltpu.VMEM((de, d), BF16)])
    return pl.pallas_call(
        functools.partial(_moe_ffn_kernel, bm=bm),
        grid_spec=grid_spec,
        out_shape=jax.ShapeDtypeStruct((n_rows, d), F32),
        compiler_params=pltpu.CompilerParams(dimension_semantics=("arbitrary",),
                                             vmem_limit_bytes=VMEM_LIMIT_BYTES),
        name="moe_ffn",
    )(blk_expert, row_tok, n_used, x, row_w, w_gate, w_up, w_down)


def _moe_combine_kernel(d0_ref, d1_ref, x_ref, g2_ref, nf_ref, yb_hbm, o_ref, ybuf, sem,
                        *, tm, final_norm):
    i = pl.program_id(0)
    n = pl.num_programs(0)

    def issue(blk, slot):
        def one(r, c):
            t = blk * tm + r
            pltpu.make_async_copy(yb_hbm.at[pl.ds(d0_ref[t], 1)], ybuf.at[slot, 0, pl.ds(r, 1)],
                                  sem.at[slot]).start()
            pltpu.make_async_copy(yb_hbm.at[pl.ds(d1_ref[t], 1)], ybuf.at[slot, 1, pl.ds(r, 1)],
                                  sem.at[slot]).start()
            return c
        lax.fori_loop(0, tm, one, 0, unroll=8)

    def wait(slot):
        pltpu.make_async_copy(yb_hbm.at[pl.ds(0, tm)], ybuf.at[slot, 0], sem.at[slot]).wait()
        pltpu.make_async_copy(yb_hbm.at[pl.ds(0, tm)], ybuf.at[slot, 1], sem.at[slot]).wait()

    slot = i % 2

    @pl.when(i == 0)
    def _():
        issue(0, 0)

    wait(slot)

    @pl.when(i + 1 < n)
    def _():
        issue(i + 1, 1 - slot)

    y = x_ref[...] + g2_ref[0] * (ybuf[slot, 0] + ybuf[slot, 1])
    if final_norm:
        y = y * lax.rsqrt(jnp.mean(y * y, axis=-1, keepdims=True) + NORM_EPS) * nf_ref[...]
    o_ref[...] = y


def _moe_combine(x, g2, norm_final, yb, d0, d1, tm, rows_per_mod, final_norm):
    t, d = x.shape
    grid_spec = pltpu.PrefetchScalarGridSpec(
        num_scalar_prefetch=2,
        grid=(t // tm,),
        in_specs=[pl.BlockSpec((tm, d), lambda i, a, b: (i, 0)),
                  _mod_spec(g2, tm, rows_per_mod),
                  pl.BlockSpec((1, d), lambda i, a, b: (0, 0)),
                  pl.BlockSpec(memory_space=pl.ANY)],
        out_specs=pl.BlockSpec((tm, d), lambda i, a, b: (i, 0)),
        scratch_shapes=[pltpu.VMEM((2, 2, tm, d), F32), pltpu.SemaphoreType.DMA((2,))])
    return pl.pallas_call(
        functools.partial(_moe_combine_kernel, tm=tm, final_norm=final_norm),
        grid_spec=grid_spec,
        out_shape=jax.ShapeDtypeStruct((t, d), F32),
        compiler_params=pltpu.CompilerParams(dimension_semantics=("arbitrary",),
                                             vmem_limit_bytes=VMEM_LIMIT_BYTES),
        name="moe_combine",
    )(d0, d1, x, g2, norm_final.reshape(1, d), yb)


def _route(logits, bm):
    t = logits.shape[0]
    a = t * EXPERT_TOPK
    g_prob = jax.nn.softmax(logits[:, :N_GROUPS], axis=-1)
    g_idx = jnp.argmax(g_prob, axis=-1, keepdims=True).astype(jnp.int32)
    g_w = jnp.max(g_prob, axis=-1, keepdims=True)
    e_logit = logits[:, N_GROUPS:N_GROUPS + N_EXPERTS].reshape(t, N_GROUPS, EXPERTS_PER_GROUP)
    e_logit = jnp.take_along_axis(e_logit, g_idx[:, :, None], axis=1)[:, 0]
    e_ids = jnp.arange(EXPERTS_PER_GROUP, dtype=jnp.int32)[None, :]
    i1 = jnp.argmax(e_logit, axis=-1, keepdims=True).astype(jnp.int32)
    rest = jnp.where(e_ids == i1, -jnp.inf, e_logit)
    i2 = jnp.argmax(rest, axis=-1, keepdims=True).astype(jnp.int32)
    e_val = jnp.concatenate([jnp.max(e_logit, axis=-1, keepdims=True),
                             jnp.max(rest, axis=-1, keepdims=True)], axis=-1)
    e_idx = jnp.concatenate([i1, i2], axis=-1)
    weights = (g_w * jax.nn.softmax(e_val, axis=-1)).reshape(a)
    flat_e = (g_idx * EXPERTS_PER_GROUP + e_idx).reshape(a)
    order = jnp.argsort(flat_e).astype(jnp.int32)
    e_sorted = flat_e[order]
    counts = jnp.zeros((N_EXPERTS,), jnp.int32).at[flat_e].add(1)
    padded = (counts + bm - 1) // bm * bm
    pad_end = jnp.cumsum(padded)
    pad_start = pad_end - padded
    start = jnp.cumsum(counts) - counts
    dest_sorted = pad_start[e_sorted] + jnp.arange(a, dtype=jnp.int32) - start[e_sorted]
    n_blk = -(-(a + N_EXPERTS * (bm - 1)) // bm)
    n_rows = n_blk * bm
    row_asg = jnp.zeros((n_rows,), jnp.int32).at[dest_sorted].set(order)
    row_valid = jnp.zeros((n_rows,), F32).at[dest_sorted].set(1.0)
    row_tok = row_asg // EXPERT_TOPK
    row_w = (weights[row_asg] * row_valid).reshape(n_rows, 1)
    blk_expert = jnp.minimum(jnp.searchsorted(pad_end, jnp.arange(n_blk) * bm, side='right'),
                             N_EXPERTS - 1).astype(jnp.int32)
    n_used = (pad_end[-1] // bm).astype(jnp.int32).reshape(1)
    dest = jnp.zeros((a,), jnp.int32).at[order].set(dest_sorted).reshape(t, EXPERT_TOPK)
    return blk_expert, row_tok, n_used, row_w, dest[:, 0], dest[:, 1]


ATT_SCALE = HEAD_DIM ** -0.5
_NT = (((1,), (1,)), ((), ()))


def _flash_update(qb, k, v, mask, carry):
    m, l, acc = carry
    s = lax.dot_general(qb, k.astype(BF16), _NT, preferred_element_type=F32) * ATT_SCALE
    s = jnp.where(mask, s, NEG_INF)
    m_new = jnp.maximum(m, jnp.max(s, axis=-1, keepdims=True))
    alpha = jnp.exp(m - m_new)
    p = jnp.exp(s - m_new)
    l = alpha * l + jnp.sum(p, axis=-1, keepdims=True)
    acc = alpha * acc + jnp.dot(p.astype(BF16), v.astype(BF16), preferred_element_type=F32)
    return m_new, l, acc


def _flash_init(tq):
    return (jnp.full((tq, 1), -jnp.inf, F32), jnp.zeros((tq, 1), F32),
            jnp.zeros((tq, HEAD_DIM), F32))


def _moba_prompt_kernel(q_ref, k_ref, v_ref, o_ref, kmean_scr, sel_scr, *, nb):
    i = pl.program_id(2)
    tq = MOBA_BLOCK

    @pl.when(i == 0)
    def _():
        kmean_scr[...] = jnp.zeros_like(kmean_scr)
        for j in range(nb):
            kmean_scr[j:j + 1, :] = jnp.mean(k_ref[0, j * tq:(j + 1) * tq, :], axis=0, keepdims=True)

    lane = lax.broadcasted_iota(jnp.int32, (tq, LANE), 1)
    row = lax.broadcasted_iota(jnp.int32, (tq, tq), 0)
    col = lax.broadcasted_iota(jnp.int32, (tq, tq), 1)
    own0 = pl.multiple_of(i * tq, tq)
    for g in range(MOBA_GROUP):
        qg = q_ref[0, :, g * HEAD_DIM:(g + 1) * HEAD_DIM]
        gate = lax.dot_general(qg, kmean_scr[...], _NT, precision=lax.Precision.HIGHEST,
                               preferred_element_type=F32)
        gate = jnp.where(lane < i, gate, NEG_INF)
        for j in range(nb):
            gj = gate[:, j:j + 1]
            better = ((gate > gj) | ((gate == gj) & (lane < j))) & (lane < nb)
            cnt = jnp.sum(jnp.where(better, 1.0, 0.0), axis=-1, keepdims=True)
            selj = jnp.where((cnt < MOBA_TOPK) & (j < i), 1.0, 0.0)
            sel_scr[g, j] = jnp.broadcast_to(selj, (tq, LANE))

        qb = qg.astype(BF16)
        carry = _flash_update(qb, k_ref[0, pl.ds(own0, tq), :], v_ref[0, pl.ds(own0, tq), :],
                              col <= row, _flash_init(tq))

        def body(j, carry, qb=qb, g=g):
            off = pl.multiple_of(j * tq, tq)
            selb = sel_scr[g, j]
            mask = jnp.concatenate([selb, selb], axis=1) > 0.5
            return _flash_update(qb, k_ref[0, pl.ds(off, tq), :], v_ref[0, pl.ds(off, tq), :],
                                 mask, carry)

        m, l, acc = lax.fori_loop(0, i, body, carry)
        o_ref[0, :, g * HEAD_DIM:(g + 1) * HEAD_DIM] = acc / l


def _moba_prompt(proj):
    b, s, _ = proj.shape
    nb = s // MOBA_BLOCK
    assert s % MOBA_BLOCK == 0 and nb <= LANE
    gw = MOBA_GROUP * HEAD_DIM
    return pl.pallas_call(
        functools.partial(_moba_prompt_kernel, nb=nb),
        grid=(b, MOBA_KV_HEADS, nb),
        in_specs=[pl.BlockSpec((1, MOBA_BLOCK, gw), lambda bi, h, i: (bi, i, COL_QA // MOBA_GROUP + h)),
                  pl.BlockSpec((1, s, HEAD_DIM), lambda bi, h, i: (bi, 0, COL_KA + h)),
                  pl.BlockSpec((1, s, HEAD_DIM), lambda bi, h, i: (bi, 0, COL_VA + h))],
        out_specs=pl.BlockSpec((1, MOBA_BLOCK, gw), lambda bi, h, i: (bi, i, h)),
        out_shape=jax.ShapeDtypeStruct((b, s, MOBA_HEADS * HEAD_DIM), F32),
        scratch_shapes=[pltpu.VMEM((LANE, HEAD_DIM), F32),
                        pltpu.VMEM((MOBA_GROUP, nb, MOBA_BLOCK, LANE), F32)],
        compiler_params=pltpu.CompilerParams(
            dimension_semantics=("arbitrary", "arbitrary", "arbitrary"),
            vmem_limit_bytes=VMEM_LIMIT_BYTES),
        name="moba_prompt",
    )(proj, proj, proj)


KV_COLS = 2 * NSA_KV_HEADS * HEAD_DIM


def _compress_kernel(x0_ref, x1_ref, x2_ref, x3_ref, pos_ref, w1_ref, b1_ref, w2_ref, b2_ref,
                     o_ref, *, nch):
    x_refs = (x0_ref, x1_ref, x2_ref, x3_ref)
    for kv in range(2):
        for h in range(NSA_KV_HEADS):
            c = kv * NSA_KV_HEADS + h
            first = jnp.zeros((nch, CMP_HIDDEN), F32)
            second = jnp.zeros((nch, CMP_HIDDEN), F32)
            for r in range(CMP_STRIDE):
                xr = x_refs[c][0, pl.ds(r, nch, stride=CMP_STRIDE), :]
                r2 = CMP_STRIDE + r
                first += jnp.dot((xr + pos_ref[kv, r:r + 1, :]).astype(BF16),
                                 w1_ref[kv, r * HEAD_DIM:(r + 1) * HEAD_DIM, :].astype(BF16),
                                 preferred_element_type=F32)
                second += jnp.dot((xr + pos_ref[kv, r2:r2 + 1, :]).astype(BF16),
                                  w1_ref[kv, r2 * HEAD_DIM:(r2 + 1) * HEAD_DIM, :].astype(BF16),
                                  preferred_element_type=F32)
            pre = first + pltpu.roll(second, nch - 1, axis=0) + b1_ref[kv:kv + 1, :]
            hdn = jax.nn.gelu(pre)
            o_ref[0, c] = (jnp.dot(hdn.astype(BF16), w2_ref[kv].astype(BF16),
                                   preferred_element_type=F32) + b2_ref[kv:kv + 1, :])


def _compress_prompt(proj, pos, w1, b1, w2, b2):
    b, s, _ = proj.shape
    nch = s // CMP_STRIDE
    full = lambda shape: pl.BlockSpec(shape, lambda bi: (0,) * len(shape))
    return pl.pallas_call(
        functools.partial(_compress_kernel, nch=nch),
        grid=(b,),
        in_specs=[pl.BlockSpec((1, s, HEAD_DIM), functools.partial(lambda bi, c: (bi, 0, COL_CMP + c), c=c))
                  for c in range(2 * NSA_KV_HEADS)] + [
                  full(pos.shape), full(w1.shape), full(b1.shape), full(w2.shape), full(b2.shape)],
        out_specs=pl.BlockSpec((1, 2 * NSA_KV_HEADS, nch, HEAD_DIM), lambda bi: (bi, 0, 0, 0)),
        out_shape=jax.ShapeDtypeStruct((b, 2 * NSA_KV_HEADS, nch, HEAD_DIM), F32),
        compiler_params=pltpu.CompilerParams(dimension_semantics=("arbitrary",),
                                             vmem_limit_bytes=VMEM_LIMIT_BYTES),
        name="nsa_compress",
    )(proj, proj, proj, proj, pos, w1, b1, w2, b2)


NSA_TQ = 256
NSA_KB = 256


def _nsa_prompt_kernel(q_ref, kc_ref, vc_ref, ks_ref, vs_ref, kw_ref, vw_ref, gb_ref, o_ref,
                       mask_scr, *, s_len):
    kvh = pl.program_id(1)
    i = pl.program_id(2)
    tq, kb = NSA_TQ, NSA_KB
    n_cmp = s_len // CMP_STRIDE - CMP_LEN // CMP_STRIDE + 1
    n_slc = s_len // SLC_BLOCK
    n_top = min(SLC_TOPN, n_slc)
    nkb = s_len // kb
    dh = HEAD_DIM

    lane = lax.broadcasted_iota(jnp.int32, (tq, LANE), 1)
    t_col = i * tq + lax.broadcasted_iota(jnp.int32, (tq, 1), 0)
    avail = (lane * CMP_STRIDE + (CMP_LEN - 1) <= t_col) & (lane < n_cmp)
    nn = lax.broadcasted_iota(jnp.int32, (LANE, LANE), 0)
    jj = lax.broadcasted_iota(jnp.int32, (LANE, LANE), 1)
    sel_map = jnp.where((nn * CMP_STRIDE < jj * SLC_BLOCK + SLC_BLOCK)
                        & (nn * CMP_STRIDE + CMP_LEN > jj * SLC_BLOCK)
                        & (nn < n_cmp) & (jj < n_slc), 1.0, 0.0)
    gates = jax.nn.sigmoid(gb_ref[0])

    def gate_col(branch, g):
        c = branch * NSA_HEADS + kvh * NSA_GROUP + g
        return jnp.sum(jnp.where(lane == c, gates, 0.0), axis=-1, keepdims=True)

    imp = jnp.zeros((tq, LANE), F32)
    kc = kc_ref[0, 0].astype(BF16)
    vc = vc_ref[0, 0].astype(BF16)
    for g in range(NSA_GROUP):
        qb = q_ref[0, :, g * dh:(g + 1) * dh].astype(BF16)
        s_c = lax.dot_general(qb, kc, _NT, preferred_element_type=F32) * ATT_SCALE
        s_c = jnp.where(avail, s_c, NEG_INF)
        e = jnp.where(avail, jnp.exp(s_c - jnp.max(s_c, axis=-1, keepdims=True)), 0.0)
        den = jnp.sum(e, axis=-1, keepdims=True)
        p_c = e * jnp.where(den > 0.0, 1.0 / den, 0.0)
        o_c = jnp.dot(p_c.astype(BF16), vc, preferred_element_type=F32)
        imp = imp + jnp.dot(p_c, sel_map, precision=lax.Precision.HIGHEST,
                            preferred_element_type=F32)
        o_ref[0, :, g * dh:(g + 1) * dh] = gate_col(0, g) * o_c

    n_rows = _round_up(n_slc, 8)
    imp_t = imp.T[:n_rows]
    jio = lax.broadcasted_iota(jnp.int32, (n_rows, tq), 0)
    cur = (i * tq + lax.broadcasted_iota(jnp.int32, (n_rows, tq), 1)) // SLC_BLOCK
    forced = (jio == 0) | (jio == cur) | (jio == cur - 1)
    imp_t = jnp.where(forced, BIG, imp_t)
    imp_t = jnp.where(jio > cur, NEG_INF, imp_t)
    imp_t = jnp.where(jio < n_slc, imp_t, -jnp.inf)
    cnt = jnp.zeros((n_rows, tq), F32)
    for r in range(n_slc):
        vr = imp_t[r:r + 1, :]
        cnt = cnt + jnp.where((vr > imp_t) | ((vr == imp_t) & (r < jio)), 1.0, 0.0)
    sel_t = jnp.where((cnt < n_top) & (jio < n_slc), 1.0, 0.0)
    if n_rows < LANE:
        sel_t = jnp.concatenate([sel_t, jnp.zeros((LANE - n_rows, tq), F32)], axis=0)
    sel = sel_t.T.astype(BF16)
    erow = lax.broadcasted_iota(jnp.int32, (LANE, kb), 0)
    ecol = lax.broadcasted_iota(jnp.int32, (LANE, kb), 1)
    for jb in range(nkb):
        expand = jnp.where((jb * kb + ecol) // SLC_BLOCK == erow, 1.0, 0.0).astype(BF16)
        mask_scr[jb] = jnp.dot(sel, expand, preferred_element_type=F32)

    rowk = lax.broadcasted_iota(jnp.int32, (tq, kb), 0)
    colk = lax.broadcasted_iota(jnp.int32, (tq, kb), 1)
    causal = colk <= rowk
    own0 = pl.multiple_of(i * kb, kb)
    for g in range(NSA_GROUP):
        qb = q_ref[0, :, g * dh:(g + 1) * dh].astype(BF16)

        carry = _flash_update(qb, ks_ref[0, pl.ds(own0, kb), :], vs_ref[0, pl.ds(own0, kb), :],
                              (mask_scr[i] > 0.5) & causal, _flash_init(tq))

        def slc_body(j, carry, qb=qb):
            off = pl.multiple_of(j * kb, kb)
            return _flash_update(qb, ks_ref[0, pl.ds(off, kb), :], vs_ref[0, pl.ds(off, kb), :],
                                 mask_scr[j] > 0.5, carry)

        _, l, acc = lax.fori_loop(0, i, slc_body, carry)
        o_s = acc / l

        carry = _flash_update(qb, kw_ref[0, pl.ds(own0, kb), :], vw_ref[0, pl.ds(own0, kb), :],
                              causal, _flash_init(tq))

        def win_body(j, carry, qb=qb):
            off = pl.multiple_of(j * kb, kb)
            mask = (j * kb + colk) > (i * tq + rowk - WINDOW)
            return _flash_update(qb, kw_ref[0, pl.ds(off, kb), :], vw_ref[0, pl.ds(off, kb), :],
                                 mask, carry)

        _, l, acc = lax.fori_loop(jnp.maximum(i - WINDOW // kb, 0), i, win_body, carry)
        o_w = acc / l
        o_ref[0, :, g * dh:(g + 1) * dh] += gate_col(1, g) * o_s + gate_col(2, g) * o_w


def _nsa_prompt(proj, kvc, g_b):
    b, s, _ = proj.shape
    assert s % NSA_TQ == 0 and NSA_TQ == NSA_KB and s // CMP_STRIDE == LANE
    gw = NSA_GROUP * HEAD_DIM
    nkh = NSA_KV_HEADS
    seq = lambda col: pl.BlockSpec((1, s, HEAD_DIM), lambda bi, h, i: (bi, 0, col + h))
    return pl.pallas_call(
        functools.partial(_nsa_prompt_kernel, s_len=s),
        grid=(b, nkh, s // NSA_TQ),
        in_specs=[pl.BlockSpec((1, NSA_TQ, gw), lambda bi, h, i: (bi, i, COL_QB // NSA_GROUP + h)),
                  pl.BlockSpec((1, 1, LANE, HEAD_DIM), lambda bi, h, i: (bi, h, 0, 0)),
                  pl.BlockSpec((1, 1, LANE, HEAD_DIM), lambda bi, h, i: (bi, nkh + h, 0, 0)),
                  seq(COL_SLC), seq(COL_SLC + nkh), seq(COL_WIN), seq(COL_WIN + nkh),
                  pl.BlockSpec((1, NSA_TQ, LANE), lambda bi, h, i: (bi, i, 0))],
        out_specs=pl.BlockSpec((1, NSA_TQ, gw), lambda bi, h, i: (bi, i, h)),
        out_shape=jax.ShapeDtypeStruct((b, s, NSA_HEADS * HEAD_DIM), F32),
        scratch_shapes=[pltpu.VMEM((s // NSA_KB, NSA_TQ, NSA_KB), F32)],
        compiler_params=pltpu.CompilerParams(
            dimension_semantics=("arbitrary", "arbitrary", "arbitrary"),
            vmem_limit_bytes=VMEM_LIMIT_BYTES),
        name="nsa_prompt",
    )(proj, kvc, kvc, proj, proj, proj, proj, g_b)


PAGE_SIZE = 128
PAGE_ROWS = PAGE_SIZE * 2 * NSA_KV_HEADS
PAGES_PER_STEP = 16
QROWS = 8
NQ = NSA_GROUP * QROWS


def _page_specs(n, l):
    def spec(k):
        return pl.BlockSpec((1, 1, PAGE_ROWS, HEAD_DIM), lambda b, j, pt: (l, pt[b, j * n + k], 0, 0))
    return [spec(k) for k in range(n)]


def _page_kv(ref, kv, h, rows=PAGE_SIZE):
    return ref[0, 0, pl.ds(kv * NSA_KV_HEADS + h, rows, stride=2 * NSA_KV_HEADS), :]


def _first_max_onehot(work, lane_f):
    mx = jnp.max(work, axis=-1, keepdims=True)
    first = jnp.min(jnp.where(work == mx, lane_f, 1e9), axis=-1, keepdims=True)
    return jnp.where(lane_f == first, 1.0, 0.0)


def _moba_sample_kernel(pt_ref, q_ref, knew_ref, vnew_ref, *rest, n_new, nb_past):
    pages = rest[:PAGES_PER_STEP]
    o_ref, m_scr, l_scr, acc_scr, kmean_scr = rest[PAGES_PER_STEP:]
    j = pl.program_id(1)
    nj = pl.num_programs(1)
    lane = lax.broadcasted_iota(jnp.int32, (NQ, LANE), 1)

    @pl.when(j == 0)
    def _():
        m_scr[...] = jnp.full_like(m_scr, NEG_INF)
        l_scr[...] = jnp.zeros_like(l_scr)
        kmean_scr[...] = jnp.zeros_like(kmean_scr)

    ppb = MOBA_BLOCK // PAGE_SIZE
    for h in range(MOBA_KV_HEADS):
        qb = q_ref[0, h].astype(BF16)
        for kb in range(PAGES_PER_STEP // ppb):
            blk = j * (PAGES_PER_STEP // ppb) + kb
            ks = [_page_kv(pages[kb * ppb + u], 0, h) for u in range(ppb)]
            vs = [_page_kv(pages[kb * ppb + u], 1, h) for u in range(ppb)]
            ss = [lax.dot_general(qb, k.astype(BF16), _NT, preferred_element_type=F32) * ATT_SCALE
                  for k in ks]
            m_b = functools.reduce(jnp.maximum, [jnp.max(s, axis=-1, keepdims=True) for s in ss])
            ps = [jnp.exp(s - m_b) for s in ss]
            l_b = sum(jnp.sum(p, axis=-1, keepdims=True) for p in ps)
            acc_b = sum(jnp.dot(p.astype(BF16), v.astype(BF16), preferred_element_type=F32)
                        for p, v in zip(ps, vs))
            kmean = sum(jnp.sum(k, axis=0, keepdims=True) for k in ks) * (1.0 / MOBA_BLOCK)
            m_scr[h] = jnp.where(lane == blk, m_b, m_scr[h])
            l_scr[h] = jnp.where(lane == blk, l_b, l_scr[h])
            acc_scr[h, blk] = acc_b
            kmean_scr[h, pl.ds(blk, 1), :] = kmean

    @pl.when(j == nj - 1)
    def _():
        lane_f = lane.astype(F32)
        trow = lax.broadcasted_iota(jnp.int32, (NQ, LANE), 0) & (QROWS - 1)
        for h in range(MOBA_KV_HEADS):
            q = q_ref[0, h]
            qb = q.astype(BF16)
            s_own = lax.dot_general(qb, knew_ref[0, h].astype(BF16), _NT,
                                    preferred_element_type=F32) * ATT_SCALE
            s_own = jnp.where((lane <= trow) & (lane < n_new), s_own, NEG_INF)
            m_own = jnp.max(s_own, axis=-1, keepdims=True)
            p_own = jnp.exp(s_own - m_own)
            l_own = jnp.sum(p_own, axis=-1, keepdims=True)
            acc_own = jnp.dot(p_own.astype(BF16), vnew_ref[0, h].astype(BF16),
                              preferred_element_type=F32)
            gate = lax.dot_general(q, kmean_scr[h], _NT, precision=lax.Precision.HIGHEST,
                                   preferred_element_type=F32)
            work = jnp.where(lane < nb_past, gate, -jnp.inf)
            sel = jnp.zeros((NQ, LANE), F32)
            for _ in range(min(MOBA_TOPK, nb_past)):
                pick = _first_max_onehot(work, lane_f)
                sel = jnp.maximum(sel, pick)
                work = jnp.where(pick > 0.5, -jnp.inf, work)
            chosen = sel > 0.5
            m_all = m_scr[h]
            m_tot = jnp.maximum(m_own, jnp.max(jnp.where(chosen, m_all, NEG_INF), axis=-1,
                                               keepdims=True))
            w = jnp.where(chosen, jnp.exp(m_all - m_tot), 0.0)
            w_own = jnp.exp(m_own - m_tot)
            l_tot = w_own * l_own + jnp.sum(w * l_scr[h], axis=-1, keepdims=True)

            def merge(blk, acc, w=w, h=h):
                wcol = jnp.sum(jnp.where(lane == blk, w, 0.0), axis=-1, keepdims=True)
                return acc + wcol * acc_scr[h, blk]

            acc = lax.fori_loop(0, nb_past, merge, w_own * acc_own)
            o_ref[0, h] = acc / l_tot


def _moba_sample(q, k_new, v_new, pool, page_table, l, n_new):
    b = q.shape[0]
    n_pages = page_table.shape[1]
    assert n_pages % PAGES_PER_STEP == 0 and (n_pages * PAGE_SIZE) % MOBA_BLOCK == 0
    nb_past = n_pages * PAGE_SIZE // MOBA_BLOCK
    assert nb_past <= LANE and n_new <= MOBA_BLOCK
    per_b = lambda shape: pl.BlockSpec(shape, lambda bi, j, pt: (bi,) + (0,) * (len(shape) - 1))
    grid_spec = pltpu.PrefetchScalarGridSpec(
        num_scalar_prefetch=1,
        grid=(b, n_pages // PAGES_PER_STEP),
        in_specs=[per_b((1, MOBA_KV_HEADS, NQ, HEAD_DIM)),
                  per_b((1, MOBA_KV_HEADS, LANE, HEAD_DIM)),
                  per_b((1, MOBA_KV_HEADS, LANE, HEAD_DIM))] + _page_specs(PAGES_PER_STEP, l),
        out_specs=per_b((1, MOBA_KV_HEADS, NQ, HEAD_DIM)),
        scratch_shapes=[pltpu.VMEM((MOBA_KV_HEADS, NQ, LANE), F32),
                        pltpu.VMEM((MOBA_KV_HEADS, NQ, LANE), F32),
                        pltpu.VMEM((MOBA_KV_HEADS, nb_past, NQ, HEAD_DIM), F32),
                        pltpu.VMEM((MOBA_KV_HEADS, LANE, HEAD_DIM), F32)])
    return pl.pallas_call(
        functools.partial(_moba_sample_kernel, n_new=n_new, nb_past=nb_past),
        grid_spec=grid_spec,
        out_shape=jax.ShapeDtypeStruct(q.shape, F32),
        compiler_params=pltpu.CompilerParams(dimension_semantics=("arbitrary", "arbitrary"),
                                             vmem_limit_bytes=VMEM_LIMIT_BYTES),
        name="moba_sample",
    )(page_table, q, k_new, v_new, *([pool] * PAGES_PER_STEP))


def _compress_sample_kernel(pt_ref, pos_ref, w1_ref, b1_ref, w2_ref, b2_ref, *rest):
    pages = rest[:PAGES_PER_STEP + 1]
    o_ref = rest[PAGES_PER_STEP + 1]
    cpp = PAGE_SIZE // CMP_STRIDE
    nch = PAGES_PER_STEP * cpp
    half = CMP_STRIDE * HEAD_DIM
    for kv in range(2):
        pos_first = jnp.concatenate([pos_ref[kv, r:r + 1, :] for r in range(CMP_STRIDE)], axis=1)
        pos_second = jnp.concatenate([pos_ref[kv, r:r + 1, :] for r in range(CMP_STRIDE, CMP_LEN)],
                                     axis=1)
        w_first = w1_ref[kv, :half, :].astype(BF16)
        w_second = w1_ref[kv, half:, :].astype(BF16)
        w2 = w2_ref[kv].astype(BF16)
        for h in range(NSA_KV_HEADS):
            c = kv * NSA_KV_HEADS + h
            x = jnp.concatenate(
                [jnp.concatenate([pg[0, 0, pl.ds(r * 2 * NSA_KV_HEADS + c, cpp,
                                                 stride=2 * NSA_KV_HEADS * CMP_STRIDE), :]
                                  for pg in pages], axis=0)
                 for r in range(CMP_STRIDE)], axis=1)
            first = jnp.dot((x[:nch] + pos_first).astype(BF16), w_first, preferred_element_type=F32)
            second = jnp.dot((x + pos_second).astype(BF16), w_second, preferred_element_type=F32)
            pre = first + pltpu.roll(second, nch + cpp - 1, axis=0)[:nch] + b1_ref[kv:kv + 1, :]
            hdn = jax.nn.gelu(pre)
            o_ref[0, c] = (jnp.dot(hdn.astype(BF16), w2, preferred_element_type=F32)
                           + b2_ref[kv:kv + 1, :])


def _compress_sample(pool, page_table, l, pos, w1, b1, w2, b2):
    b, n_pages = page_table.shape
    nch = PAGES_PER_STEP * PAGE_SIZE // CMP_STRIDE
    full = lambda shape: pl.BlockSpec(shape, lambda bi, j, pt: (0,) * len(shape))
    next_page = pl.BlockSpec(
        (1, 1, PAGE_ROWS, HEAD_DIM),
        lambda bi, j, pt: (l, pt[bi, jnp.minimum((j + 1) * PAGES_PER_STEP, n_pages - 1)], 0, 0))
    grid_spec = pltpu.PrefetchScalarGridSpec(
        num_scalar_prefetch=1,
        grid=(b, n_pages // PAGES_PER_STEP),
        in_specs=[full(pos.shape), full(w1.shape), full(b1.shape), full(w2.shape), full(b2.shape)]
        + _page_specs(PAGES_PER_STEP, l) + [next_page],
        out_specs=pl.BlockSpec((1, 2 * NSA_KV_HEADS, nch, HEAD_DIM), lambda bi, j, pt: (bi, 0, j, 0)))
    return pl.pallas_call(
        _compress_sample_kernel,
        grid_spec=grid_spec,
        out_shape=jax.ShapeDtypeStruct(
            (b, 2 * NSA_KV_HEADS, n_pages * PAGE_SIZE // CMP_STRIDE, HEAD_DIM), F32),
        compiler_params=pltpu.CompilerParams(dimension_semantics=("arbitrary", "arbitrary"),
                                             vmem_limit_bytes=VMEM_LIMIT_BYTES),
        name="nsa_compress_sample",
    )(page_table, pos, w1, b1, w2, b2, *([pool] * (PAGES_PER_STEP + 1)))


def _nsa_cmp_sample_kernel(q_ref, kc_ref, vc_ref, oc_ref, sel_ref, *, q0, n_cmp, n_slc, n_lanes):
    ncp = kc_ref.shape[2]
    col = lax.broadcasted_iota(jnp.int32, (NQ, ncp), 1)
    trow = lax.broadcasted_iota(jnp.int32, (NQ, ncp), 0) & (QROWS - 1)
    avail = (col * CMP_STRIDE + (CMP_LEN - 1) <= q0 + trow) & (col < n_cmp)
    nn = lax.broadcasted_iota(jnp.int32, (ncp, n_lanes), 0)
    jj = lax.broadcasted_iota(jnp.int32, (ncp, n_lanes), 1)
    sel_map = jnp.where((nn * CMP_STRIDE < jj * SLC_BLOCK + SLC_BLOCK)
                        & (nn * CMP_STRIDE + CMP_LEN > jj * SLC_BLOCK)
                        & (nn < n_cmp) & (jj < n_slc), 1.0, 0.0)
    lane = lax.broadcasted_iota(jnp.int32, (QROWS, n_lanes), 1)
    lane_f = lane.astype(F32)
    cur = (q0 + lax.broadcasted_iota(jnp.int32, (QROWS, n_lanes), 0)) // SLC_BLOCK
    for h in range(NSA_KV_HEADS):
        qb = q_ref[0, h].astype(BF16)
        s = lax.dot_general(qb, kc_ref[0, h].astype(BF16), _NT,
                            preferred_element_type=F32) * ATT_SCALE
        s = jnp.where(avail, s, NEG_INF)
        e = jnp.where(avail, jnp.exp(s - jnp.max(s, axis=-1, keepdims=True)), 0.0)
        den = jnp.sum(e, axis=-1, keepdims=True)
        p = e * jnp.where(den > 0.0, 1.0 / den, 0.0)
        oc_ref[0, h] = jnp.dot(p.astype(BF16), vc_ref[0, h].astype(BF16),
                               preferred_element_type=F32)
        pg = sum(p[g * QROWS:(g + 1) * QROWS] for g in range(NSA_GROUP))
        imp = jnp.dot(pg, sel_map, precision=lax.Precision.HIGHEST, preferred_element_type=F32)
        forced = (lane == 0) | (lane == cur) | (lane == cur - 1)
        imp = jnp.where(forced, BIG, imp)
        imp = jnp.where(lane > cur, NEG_INF, imp)
        work = jnp.where(lane < n_slc, imp, -jnp.inf)
        sel = jnp.zeros((QROWS, n_lanes), F32)
        for _ in range(min(SLC_TOPN, n_slc)):
            pick = _first_max_onehot(work, lane_f)
            sel = jnp.maximum(sel, pick)
            work = jnp.where(pick > 0.5, -jnp.inf, work)
        sel_ref[0, h] = sel


def _nsa_cmp_sample(q, kvc, q0, n_new):
    b = q.shape[0]
    ncp = kvc.shape[2]
    n_cmp = ncp - CMP_LEN // CMP_STRIDE + 1
    n_slc = -(-(q0 + n_new) // SLC_BLOCK)
    n_lanes = _round_up(n_slc, LANE)
    nkh = NSA_KV_HEADS
    return pl.pallas_call(
        functools.partial(_nsa_cmp_sample_kernel, q0=q0, n_cmp=n_cmp, n_slc=n_slc, n_lanes=n_lanes),
        grid=(b,),
        in_specs=[pl.BlockSpec((1, nkh, NQ, HEAD_DIM), lambda bi: (bi, 0, 0, 0)),
                  pl.BlockSpec((1, nkh, ncp, HEAD_DIM), lambda bi: (bi, 0, 0, 0)),
                  pl.BlockSpec((1, nkh, ncp, HEAD_DIM), lambda bi: (bi, 1, 0, 0))],
        out_specs=[pl.BlockSpec((1, nkh, NQ, HEAD_DIM), lambda bi: (bi, 0, 0, 0)),
                   pl.BlockSpec((1, nkh, QROWS, n_lanes), lambda bi: (bi, 0, 0, 0))],
        out_shape=[jax.ShapeDtypeStruct(q.shape, F32),
                   jax.ShapeDtypeStruct((b, nkh, QROWS, n_lanes), F32)],
        compiler_params=pltpu.CompilerParams(dimension_semantics=("arbitrary",),
                                             vmem_limit_bytes=VMEM_LIMIT_BYTES),
        name="nsa_cmp_sample",
    )(q, kvc, kvc)


def _nsa_slc_sample_kernel(pt_ref, q_ref, sel_ref, oc_ref, g_ref, ksn_ref, vsn_ref, kwn_ref,
                           vwn_ref, win_ref, *rest, n_new, n_win, cur_blk):
    pages = rest[:PAGES_PER_STEP]
    o_ref, m_scr, l_scr, acc_scr, ow_scr = rest[PAGES_PER_STEP:]
    j = pl.program_id(1)
    nj = pl.num_programs(1)
    n_lanes = sel_ref.shape[3]
    lane = lax.broadcasted_iota(jnp.int32, (NQ, LANE), 1)
    trow = lax.broadcasted_iota(jnp.int32, (NQ, LANE), 0) & (QROWS - 1)
    new_ok = (lane <= trow) & (lane < n_new)

    @pl.when(j == 0)
    def _():
        for h in range(NSA_KV_HEADS):
            qb = q_ref[0, h].astype(BF16)
            sel_h = jnp.concatenate([sel_ref[0, h]] * NSA_GROUP, axis=0)
            lane_s = lax.broadcasted_iota(jnp.int32, (NQ, n_lanes), 1)
            sel_cur = jnp.sum(jnp.where(lane_s == cur_blk, sel_h, 0.0), axis=-1,
                              keepdims=True) > 0.5
            s = lax.dot_general(qb, ksn_ref[0, h].astype(BF16), _NT,
                                preferred_element_type=F32) * ATT_SCALE
            s = jnp.where(new_ok & sel_cur, s, NEG_INF)
            m = jnp.max(s, axis=-1, keepdims=True)
            p = jnp.exp(s - m)
            m_scr[h] = m
            l_scr[h] = jnp.sum(p, axis=-1, keepdims=True)
            acc_scr[h] = jnp.dot(p.astype(BF16), vsn_ref[0, h].astype(BF16),
                                 preferred_element_type=F32)
            kw = win_ref[0, 0, pl.ds(h, n_win, stride=2 * NSA_KV_HEADS), :]
            vw = win_ref[0, 0, pl.ds(NSA_KV_HEADS + h, n_win, stride=2 * NSA_KV_HEADS), :]
            idx = lax.broadcasted_iota(jnp.int32, (NQ, n_win), 1)
            tr = lax.broadcasted_iota(jnp.int32, (NQ, n_win), 0) & (QROWS - 1)
            s_w = lax.dot_general(qb, kw.astype(BF16), _NT, preferred_element_type=F32) * ATT_SCALE
            s_w = jnp.where(idx + (WINDOW - n_win) > tr, s_w, NEG_INF)
            s_n = lax.dot_general(qb, kwn_ref[0, h].astype(BF16), _NT,
                                  preferred_element_type=F32) * ATT_SCALE
            s_n = jnp.where(new_ok, s_n, NEG_INF)
            mw = jnp.maximum(jnp.max(s_w, axis=-1, keepdims=True),
                             jnp.max(s_n, axis=-1, keepdims=True))
            p_w = jnp.exp(s_w - mw)
            p_n = jnp.exp(s_n - mw)
            lw = jnp.sum(p_w, axis=-1, keepdims=True) + jnp.sum(p_n, axis=-1, keepdims=True)
            ow_scr[h] = (jnp.dot(p_w.astype(BF16), vw.astype(BF16), preferred_element_type=F32)
                         + jnp.dot(p_n.astype(BF16), vwn_ref[0, h].astype(BF16),
                                   preferred_element_type=F32)) / lw

    step_keys = PAGES_PER_STEP * PAGE_SIZE
    erow = lax.broadcasted_iota(jnp.int32, (n_lanes, step_keys), 0)
    ecol = lax.broadcasted_iota(jnp.int32, (n_lanes, step_keys), 1)
    expand = jnp.where((j * step_keys + ecol) // SLC_BLOCK == erow, 1.0, 0.0).astype(BF16)
    for h in range(NSA_KV_HEADS):
        qb = q_ref[0, h].astype(BF16)
        sel_h = jnp.concatenate([sel_ref[0, h]] * NSA_GROUP, axis=0).astype(BF16)
        maskf = jnp.dot(sel_h, expand, preferred_element_type=F32)
        ss = []
        for k in range(PAGES_PER_STEP):
            s = lax.dot_general(qb, _page_kv(pages[k], 0, h).astype(BF16), _NT,
                                preferred_element_type=F32) * ATT_SCALE
            ss.append(jnp.where(maskf[:, k * PAGE_SIZE:(k + 1) * PAGE_SIZE] > 0.5, s, NEG_INF))
        m_old = m_scr[h]
        m_new = functools.reduce(jnp.maximum,
                                 [jnp.max(s, axis=-1, keepdims=True) for s in ss] + [m_old])
        alpha = jnp.exp(m_old - m_new)
        ps = [jnp.exp(s - m_new) for s in ss]
        l_scr[h] = alpha * l_scr[h] + sum(jnp.sum(p, axis=-1, keepdims=True) for p in ps)
        acc_scr[h] = alpha * acc_scr[h] + sum(
            jnp.dot(p.astype(BF16), _page_kv(pages[k], 1, h).astype(BF16),
                    preferred_element_type=F32) for k, p in enumerate(ps))
        m_scr[h] = m_new

    @pl.when(j == nj - 1)
    def _():
        for h in range(NSA_KV_HEADS):
            gates = jax.nn.sigmoid(g_ref[0, h])
            o_s = acc_scr[h] / l_scr[h]
            o_ref[0, h] = (gates[:, 0:1] * oc_ref[0, h] + gates[:, 1:2] * o_s
                           + gates[:, 2:3] * ow_scr[h])


def _nsa_slc_sample(q, sel, o_c, g3, ks_new, vs_new, kw_new, vw_new, win_state, pool, page_table,
                    l, n_new):
    b = q.shape[0]
    n_pages = page_table.shape[1]
    n_win = win_state.shape[2] // (2 * NSA_KV_HEADS)
    assert (n_pages * PAGE_SIZE) % SLC_BLOCK == 0 and n_new <= SLC_BLOCK and n_win <= WINDOW
    nkh = NSA_KV_HEADS
    per_b = lambda shape: pl.BlockSpec(shape, lambda bi, j, pt: (bi,) + (0,) * (len(shape) - 1))
    grid_spec = pltpu.PrefetchScalarGridSpec(
        num_scalar_prefetch=1,
        grid=(b, n_pages // PAGES_PER_STEP),
        in_specs=[per_b((1, nkh, NQ, HEAD_DIM)), per_b((1, nkh, QROWS, sel.shape[3])),
                  per_b((1, nkh, NQ, HEAD_DIM)), per_b((1, nkh, NQ, LANE)),
                  per_b((1, nkh, LANE, HEAD_DIM)), per_b((1, nkh, LANE, HEAD_DIM)),
                  per_b((1, nkh, LANE, HEAD_DIM)), per_b((1, nkh, LANE, HEAD_DIM)),
                  pl.BlockSpec((1, 1, win_state.shape[2], HEAD_DIM), lambda bi, j, pt: (l, bi, 0, 0))]
        + _page_specs(PAGES_PER_STEP, l),
        out_specs=per_b((1, nkh, NQ, HEAD_DIM)),
        scratch_shapes=[pltpu.VMEM((nkh, NQ, 1), F32), pltpu.VMEM((nkh, NQ, 1), F32),
                        pltpu.VMEM((nkh, NQ, HEAD_DIM), F32), pltpu.VMEM((nkh, NQ, HEAD_DIM), F32)])
    return pl.pallas_call(
        functools.partial(_nsa_slc_sample_kernel, n_new=n_new, n_win=n_win,
                          cur_blk=n_pages * PAGE_SIZE // SLC_BLOCK),
        grid_spec=grid_spec,
        out_shape=jax.ShapeDtypeStruct(q.shape, F32),
        compiler_params=pltpu.CompilerParams(dimension_semantics=("arbitrary", "arbitrary"),
                                             vmem_limit_bytes=VMEM_LIMIT_BYTES),
        name="nsa_slc_sample",
    )(page_table, q, sel, o_c, g3, ks_new, vs_new, kw_new, vw_new, win_state,
      *([pool] * PAGES_PER_STEP))


def in_proj_sizes():
    return (MOBA_HEADS * HEAD_DIM, MOBA_KV_HEADS * HEAD_DIM, MOBA_KV_HEADS * HEAD_DIM,
            NSA_HEADS * HEAD_DIM, 6 * NSA_KV_HEADS * HEAD_DIM, 3 * NSA_HEADS, D_MODEL, D_MODEL)


def rms_norm(x, g):
    xf = x.astype(F32)
    y = xf * lax.rsqrt(jnp.mean(xf * xf, axis=-1, keepdims=True) + NORM_EPS)
    return (y * g.astype(F32)).astype(x.dtype)


def apply_rope(x, pos):
    half = ROT_DIM // 2
    inv_freq = ROPE_THETA ** (-jnp.arange(half, dtype=F32) / half)
    ang = pos.astype(F32)[:, None] * inv_freq[None, :]
    ang = ang.reshape((ang.shape[0],) + (1,) * (x.ndim - 3) + (half,))
    cos, sin = jnp.cos(ang), jnp.sin(ang)
    xr = x[..., :ROT_DIM].astype(F32)
    x1, x2 = xr[..., :half], xr[..., half:]
    rot = jnp.concatenate([x1 * cos - x2 * sin, x2 * cos + x1 * sin], axis=-1)
    return jnp.concatenate([rot.astype(x.dtype), x[..., ROT_DIM:]], axis=-1)


def query_block(s, pref):
    return pref if s % pref == 0 else s


def gather_pages(pool, page_table):
    g = pool[page_table]
    return g.reshape((g.shape[0], g.shape[1] * g.shape[2]) + g.shape[3:])


def moba_attention(q, k_all, v_all, q0):
    B, S, KVH, G, dh = q.shape
    L = k_all.shape[1]
    nb = -(-L // MOBA_BLOCK)
    pad = ((0, 0), (0, nb * MOBA_BLOCK - L), (0, 0), (0, 0))
    kb = jnp.pad(k_all, pad).reshape(B, nb, MOBA_BLOCK, KVH, dh).transpose(0, 3, 1, 2, 4)
    vb = jnp.pad(v_all, pad).reshape(B, nb, MOBA_BLOCK, KVH, dh).transpose(0, 3, 1, 2, 4)
    k_mean = jnp.mean(kb, axis=3, dtype=F32)
    n_sel = min(MOBA_TOPK, nb)
    qc = query_block(S, MOBA_QCHUNK)
    nq = S // qc
    scale = HEAD_DIM ** -0.5
    bi = jnp.arange(B)[:, None, None, None, None]
    gi = jnp.arange(KVH)[None, None, :, None, None]
    blk_ids = jnp.arange(nb)
    in_blk = jnp.arange(MOBA_BLOCK)

    def one_chunk(args):
        qch, start = args
        t = start + jnp.arange(qc)
        own = t // MOBA_BLOCK
        gate = jnp.einsum('bqghd,bgnd->bqghn', qch.astype(F32), k_mean)
        gate = jnp.where((blk_ids[None, :] < own[:, None])[None, :, None, None, :], gate, NEG_INF)
        _, sel = lax.top_k(gate, n_sel)
        slot_ok = (jnp.arange(n_sel)[None, :] < own[:, None])[None, :, None, None, :, None]
        k_sel = kb[bi, gi, sel]
        v_sel = vb[bi, gi, sel]
        s_sel = jnp.einsum('bqghd,bqghskd->bqghsk', qch, k_sel, preferred_element_type=F32) * scale
        s_sel = jnp.where(slot_ok, s_sel, NEG_INF).reshape(B, qc, KVH, G, n_sel * MOBA_BLOCK)
        k_own = kb[:, :, own]
        v_own = vb[:, :, own]
        s_own = jnp.einsum('bqghd,bgqkd->bqghk', qch, k_own, preferred_element_type=F32) * scale
        causal = (own[:, None] * MOBA_BLOCK + in_blk[None, :]) <= t[:, None]
        s_own = jnp.where(causal[None, :, None, None, :], s_own, NEG_INF)
        p = jax.nn.softmax(jnp.concatenate([s_sel, s_own], axis=-1), axis=-1).astype(v_all.dtype)
        p_sel = p[..., :n_sel * MOBA_BLOCK].reshape(B, qc, KVH, G, n_sel, MOBA_BLOCK)
        p_own = p[..., n_sel * MOBA_BLOCK:]
        return (jnp.einsum('bqghsk,bqghskd->bqghd', p_sel, v_sel)
                + jnp.einsum('bqghk,bgqkd->bqghd', p_own, v_own))

    q_chunks = q.reshape(B, nq, qc, KVH, G, dh).transpose(1, 0, 2, 3, 4, 5)
    o = lax.map(one_chunk, (q_chunks, q0 + qc * jnp.arange(nq)))
    return o.transpose(1, 0, 2, 3, 4, 5).reshape(B, S, KVH * G * dh)


def compress_rows(x, w_pos, w1, b1, w2, b2):
    B, L, KVH, dh = x.shape
    r = CMP_LEN // CMP_STRIDE
    n_chunks = L // CMP_STRIDE
    n_cmp = n_chunks - r + 1
    ch = x[:, :n_chunks * CMP_STRIDE].reshape(B, n_chunks, CMP_STRIDE, KVH, dh)
    blocks = jnp.concatenate([ch[:, j:j + n_cmp] for j in range(r)], axis=2)
    blocks = (blocks + w_pos[None, None, :, None, :]).transpose(0, 1, 3, 2, 4)
    flat = blocks.reshape(B, n_cmp, KVH, CMP_LEN * dh)
    return jax.nn.gelu(flat @ w1 + b1) @ w2 + b2


def cmp_to_slc_map(n_cmp, n_slc):
    lo = np.arange(n_cmp) * CMP_STRIDE
    blo = np.arange(n_slc) * SLC_BLOCK
    m = (lo[:, None] < blo[None, :] + SLC_BLOCK) & (lo[:, None] + CMP_LEN > blo[None, :])
    return jnp.asarray(m.astype(np.float32))


def nsa_compressed_selected(q, k_cmp, v_cmp, k_slc, v_slc, q0):
    B, S, KVH, G, dh = q.shape
    L = k_slc.shape[1]
    n_cmp = k_cmp.shape[1]
    n_slc = -(-L // SLC_BLOCK)
    n_top = min(SLC_TOPN, n_slc)
    pad = ((0, 0), (0, n_slc * SLC_BLOCK - L), (0, 0), (0, 0))
    kb = jnp.pad(k_slc, pad).reshape(B, n_slc, SLC_BLOCK, KVH, dh).transpose(0, 3, 1, 2, 4)
    vb = jnp.pad(v_slc, pad).reshape(B, n_slc, SLC_BLOCK, KVH, dh).transpose(0, 3, 1, 2, 4)
    sel_map = cmp_to_slc_map(n_cmp, n_slc)
    cmp_last = jnp.arange(n_cmp) * CMP_STRIDE + CMP_LEN - 1
    blk_ids = jnp.arange(n_slc)
    in_blk = jnp.arange(SLC_BLOCK)
    qc = query_block(S, NSA_QCHUNK)
    nq = S // qc
    scale = HEAD_DIM ** -0.5
    bi = jnp.arange(B)[:, None, None, None]
    gi = jnp.arange(KVH)[None, None, :, None]

    def one_chunk(args):
        qch, start = args
        t = start + jnp.arange(qc)
        avail = (cmp_last[None, :] <= t[:, None])[None, :, None, None, :]
        s_c = jnp.einsum('bqghd,bngd->bqghn', qch, k_cmp, preferred_element_type=F32) * scale
        p_c = jnp.where(avail, jax.nn.softmax(jnp.where(avail, s_c, NEG_INF), axis=-1), 0.0)
        o_c = jnp.einsum('bqghn,bngd->bqghd', p_c.astype(v_cmp.dtype), v_cmp)
        imp = jnp.einsum('bqghn,nj->bqgj', p_c, sel_map)
        cur = (t // SLC_BLOCK)[:, None]
        forced = (blk_ids == 0) | (blk_ids == cur) | (blk_ids == cur - 1)
        imp = jnp.where(forced[None, :, None, :], BIG, imp)
        imp = jnp.where((blk_ids > cur)[None, :, None, :], NEG_INF, imp)
        _, sel = lax.top_k(imp, n_top)
        k_sel = kb[bi, gi, sel]
        v_sel = vb[bi, gi, sel]
        key_pos = sel[..., None] * SLC_BLOCK + in_blk
        ok = key_pos <= t[None, :, None, None, None]
        s_s = jnp.einsum('bqghd,bqgskd->bqghsk', qch, k_sel, preferred_element_type=F32) * scale
        s_s = jnp.where(ok[:, :, :, None], s_s, NEG_INF).reshape(B, qc, KVH, G, n_top * SLC_BLOCK)
        p_s = jax.nn.softmax(s_s, axis=-1).astype(v_slc.dtype).reshape(B, qc, KVH, G, n_top, SLC_BLOCK)
        o_s = jnp.einsum('bqghsk,bqgskd->bqghd', p_s, v_sel)
        return o_c, o_s

    q_chunks = q.reshape(B, nq, qc, KVH, G, dh).transpose(1, 0, 2, 3, 4, 5)
    o_c, o_s = lax.map(one_chunk, (q_chunks, q0 + qc * jnp.arange(nq)))
    o_c = o_c.transpose(1, 0, 2, 3, 4, 5).reshape(B, S, KVH, G, dh)
    o_s = o_s.transpose(1, 0, 2, 3, 4, 5).reshape(B, S, KVH, G, dh)
    return o_c, o_s


def window_attention(q, k_win, v_win, q0, k0):
    B, S, KVH, G, dh = q.shape
    qb = query_block(S, WIN_QBLOCK)
    nq = S // qb
    span = WINDOW + qb
    pad = ((0, 0), (WINDOW, 0), (0, 0), (0, 0))
    kp = jnp.pad(k_win, pad)
    vp = jnp.pad(v_win, pad)
    scale = HEAD_DIM ** -0.5

    def one_block(args):
        qblk, i = args
        s0 = q0 - k0 + i * qb
        kk = lax.dynamic_slice_in_dim(kp, s0, span, axis=1)
        vv = lax.dynamic_slice_in_dim(vp, s0, span, axis=1)
        t = q0 + i * qb + jnp.arange(qb)
        kpos = k0 - WINDOW + s0 + jnp.arange(span)
        ok = ((kpos[None, :] > t[:, None] - WINDOW) & (kpos[None, :] <= t[:, None])
              & (kpos[None, :] >= k0))
        s = jnp.einsum('bqghd,bkgd->bqghk', qblk, kk, preferred_element_type=F32) * scale
        s = jnp.where(ok[None, :, None, None, :], s, NEG_INF)
        p = jax.nn.softmax(s, axis=-1).astype(vv.dtype)
        return jnp.einsum('bqghk,bkgd->bqghd', p, vv)

    q_blocks = q.reshape(B, nq, qb, KVH, G, dh).transpose(1, 0, 2, 3, 4, 5)
    o = lax.map(one_block, (q_blocks, jnp.arange(nq)))
    return o.transpose(1, 0, 2, 3, 4, 5).reshape(B, S, KVH, G, dh)


def token_mixer(h, q0, past, lp):
    B, S, _ = h.shape
    cuts = [int(c) for c in np.cumsum(in_proj_sizes())[:-1]]
    q_a, k_a, v_a, q_b, kv_b, g_b, g_ma, g_mb = jnp.split(_mm3(h, lp['w_in']), cuts, axis=-1)
    pos = q0 + jnp.arange(S)
    q_a = apply_rope(q_a.reshape(B, S, MOBA_KV_HEADS, MOBA_GROUP, HEAD_DIM), pos)
    k_a = apply_rope(k_a.reshape(B, S, MOBA_KV_HEADS, HEAD_DIM), pos)
    v_a = v_a.reshape(B, S, MOBA_KV_HEADS, HEAD_DIM)
    q_b = apply_rope(q_b.reshape(B, S, NSA_KV_HEADS, NSA_GROUP, HEAD_DIM), pos)
    kv_b = kv_b.reshape(B, S, 3, 2, NSA_KV_HEADS, HEAD_DIM)
    new_moba = jnp.stack([k_a, v_a], axis=2)
    new_cmp = kv_b[:, :, 0]
    new_slc = jnp.stack([apply_rope(kv_b[:, :, 1, 0], pos), kv_b[:, :, 1, 1]], axis=2)
    new_win = jnp.stack([apply_rope(kv_b[:, :, 2, 0], pos), kv_b[:, :, 2, 1]], axis=2)
    if past is None:
        o_a = _moba_prompt(q_a.reshape(B, S, -1), new_moba.reshape(B, S, -1))
        kvc = _compress_prompt(new_cmp.reshape(B, S, -1), lp['cmp_pos'], lp['cmp_w1'], lp['cmp_b1'],
                               lp['cmp_w2'], lp['cmp_b2'])
        o_b = _nsa_prompt(q_b.reshape(B, S, -1), kvc, new_slc.reshape(B, S, -1),
                          new_win.reshape(B, S, -1),
                          jnp.pad(g_b, ((0, 0), (0, 0), (0, LANE - g_b.shape[-1]))))
        merged = (jax.nn.sigmoid(g_ma) * _mm3(o_a, lp['w_pa'])
                  + jax.nn.sigmoid(g_mb) * _mm3(o_b, lp['w_pb']))
        win_state = new_win[:, S - min(WINDOW, S):]
        return _mm3(merged, lp['w_out']), (new_moba, new_cmp, new_slc, win_state)
    else:
        moba_kv = jnp.concatenate([past[0], new_moba], axis=1)
        cmp_kv = jnp.concatenate([past[1], new_cmp], axis=1)
        slc_kv = jnp.concatenate([past[2], new_slc], axis=1)
        win_kv = jnp.concatenate([past[3], new_win], axis=1)
    k0 = q0 + S - win_kv.shape[1]
    win_state = win_kv[:, win_kv.shape[1] - min(WINDOW, win_kv.shape[1]):]

    o_a = moba_attention(q_a, moba_kv[:, :, 0], moba_kv[:, :, 1], q0)
    k_c = compress_rows(cmp_kv[:, :, 0], lp['cmp_pos'][0], lp['cmp_w1'][0], lp['cmp_b1'][0],
                        lp['cmp_w2'][0], lp['cmp_b2'][0])
    v_c = compress_rows(cmp_kv[:, :, 1], lp['cmp_pos'][1], lp['cmp_w1'][1], lp['cmp_b1'][1],
                        lp['cmp_w2'][1], lp['cmp_b2'][1])
    o_c, o_s = nsa_compressed_selected(q_b, k_c, v_c, slc_kv[:, :, 0], slc_kv[:, :, 1], q0)
    o_w = window_attention(q_b, win_kv[:, :, 0], win_kv[:, :, 1], q0, k0)
    gates = jax.nn.sigmoid(g_b.reshape(B, S, 3, NSA_KV_HEADS, NSA_GROUP, 1))
    o_b = (gates[:, :, 0] * o_c + gates[:, :, 1] * o_s + gates[:, :, 2] * o_w).reshape(B, S, NSA_HEADS * HEAD_DIM)
    merged = (jax.nn.sigmoid(g_ma) * _mm3(o_a, lp['w_pa'])
              + jax.nn.sigmoid(g_mb) * _mm3(o_b, lp['w_pb']))
    return _mm3(merged, lp['w_out']), (new_moba, new_cmp, new_slc, win_state)


def routed_experts(x, expert_ids, weights, w_gate, w_up, w_down):
    T, D = x.shape
    A = T * EXPERT_TOPK
    flat_e = expert_ids.reshape(A)
    order = jnp.argsort(flat_e)
    e_sorted = flat_e[order]
    tok_sorted = order // EXPERT_TOPK
    counts = jnp.zeros((N_EXPERTS,), jnp.int32).at[flat_e].add(1)
    padded = (counts + MOE_BLOCK - 1) // MOE_BLOCK * MOE_BLOCK
    pad_end = jnp.cumsum(padded)
    pad_start = pad_end - padded
    start = jnp.cumsum(counts) - counts
    dest = pad_start[e_sorted] + jnp.arange(A) - start[e_sorted]
    n_blk = -(-(A + N_EXPERTS * (MOE_BLOCK - 1)) // MOE_BLOCK)
    row_tok = jnp.zeros((n_blk * MOE_BLOCK,), jnp.int32).at[dest].set(tok_sorted)
    blk_expert = jnp.minimum(jnp.searchsorted(pad_end, jnp.arange(n_blk) * MOE_BLOCK, side='right'),
                             N_EXPERTS - 1)
    xb = x[row_tok].reshape(n_blk, MOE_BLOCK, D)

    def expert_block(args):
        xe, e = args
        return (jax.nn.silu(xe @ w_gate[e]) * (xe @ w_up[e])) @ w_down[e]

    yb = lax.map(expert_block, (xb, blk_expert)).reshape(n_blk * MOE_BLOCK, D)
    contrib = yb[dest] * weights.reshape(A)[order][:, None].astype(x.dtype)
    return jnp.zeros((T, D), x.dtype).at[tok_sorted].add(contrib)


def hierarchical_moe(h, lp):
    B, S, D = h.shape
    x = h.reshape(B * S, D)
    g_prob = jax.nn.softmax((x @ lp['w_rg']).astype(F32) + lp['b_rg'].astype(F32), axis=-1)
    g_w, g_idx = lax.top_k(g_prob, 1)
    e_logit = ((x @ lp['w_re']).astype(F32) + lp['b_re'].astype(F32)).reshape(-1, N_GROUPS, EXPERTS_PER_GROUP)
    e_logit = jnp.take_along_axis(e_logit, g_idx[:, :, None], axis=1)[:, 0]
    e_val, e_idx = lax.top_k(e_logit, EXPERT_TOPK)
    weights = g_w * jax.nn.softmax(e_val, axis=-1)
    expert_ids = g_idx * EXPERTS_PER_GROUP + e_idx
    y = routed_experts(x, expert_ids, weights, lp['w_gate'], lp['w_up'], lp['w_down'])
    return y.reshape(B, S, D)


def decoder_layer(x, c, q0, past, lp):
    ada = _mm(jax.nn.silu(c), lp['w_ada'], tn=2048) + lp['b_ada']
    sh1, sc1, g1, sh2, sc2, g2 = jnp.split(ada[:, None, :], 6, axis=-1)
    h = rms_norm(x, lp['norm_attn']) * (1 + sc1) + sh1
    mix, rows = token_mixer(h, q0, past, lp)
    x = x + g1 * mix
    h = rms_norm(x, lp['norm_ffn']) * (1 + sc2) + sh2
    x = x + g2 * hierarchical_moe(h, lp)
    return x, rows


def _unused_kernel(x_prompt, x_sample, cache_moba_kv, cache_cmp_kv, cache_slc_kv, state_win_kv,
           page_table, c_prompt, c_sample, w_in, w_pa, w_pb, w_out, cmp_pos, cmp_w1, cmp_b1,
           cmp_w2, cmp_b2, norm_attn, norm_ffn, norm_final, w_ada, b_ada, w_rg, b_rg, w_re,
           b_re, w_gate, w_up, w_down):
    past_len = page_table.shape[1] * cache_moba_kv.shape[2]
    y_p, y_s = x_prompt, x_sample
    rows_p, rows_s = [], []
    for l in range(DEPTH):
        lp = {'w_in': w_in[l], 'w_pa': w_pa[l], 'w_pb': w_pb[l], 'w_out': w_out[l],
              'cmp_pos': cmp_pos[l], 'cmp_w1': cmp_w1[l], 'cmp_b1': cmp_b1[l],
              'cmp_w2': cmp_w2[l], 'cmp_b2': cmp_b2[l], 'norm_attn': norm_attn[l],
              'norm_ffn': norm_ffn[l], 'w_ada': w_ada[l], 'b_ada': b_ada[l],
              'w_rg': w_rg[l], 'b_rg': b_rg[l], 'w_re': w_re[l], 'b_re': b_re[l],
              'w_gate': w_gate[l], 'w_up': w_up[l], 'w_down': w_down[l]}
        y_p, r_p = decoder_layer(y_p, c_prompt, 0, None, lp)
        past = (gather_pages(cache_moba_kv[l], page_table),
                gather_pages(cache_cmp_kv[l], page_table),
                gather_pages(cache_slc_kv[l], page_table),
                state_win_kv[l])
        y_s, r_s = decoder_layer(y_s, c_sample, past_len, past, lp)
        rows_p.append(r_p)
        rows_s.append(r_s)
    y_prompt = rms_norm(y_p, norm_final)
    y_sample = rms_norm(y_s, norm_final)
    return (y_prompt, y_sample,
            jnp.stack([r[0] for r in rows_p]), jnp.stack([r[0] for r in rows_s]),
            jnp.stack([r[1] for r in rows_p]), jnp.stack([r[1] for r in rows_s]),
            jnp.stack([r[2] for r in rows_p]), jnp.stack([r[2] for r in rows_s]),
            jnp.stack([r[3] for r in rows_p]), jnp.stack([r[3] for r in rows_s]))


PROMPT_TM_IN = 512
PROMPT_TM_OUT = 256
MOE_TM = 128
SAMPLE_MOE_BLOCK = 8


def _kv_rows(proj, col, b, s):
    lo = col * HEAD_DIM
    return proj[:, lo:lo + KV_COLS].reshape(b, s, 2, NSA_KV_HEADS, HEAD_DIM)


def _to_qrows(x, b, s):
    x = x.reshape(b, s, NSA_KV_HEADS, NSA_GROUP, HEAD_DIM).transpose(0, 2, 3, 1, 4)
    x = jnp.pad(x, ((0, 0), (0, 0), (0, 0), (0, QROWS - s), (0, 0)))
    return x.reshape(b, NSA_KV_HEADS, NQ, HEAD_DIM)


def _from_qrows(o, b, s):
    o = o.reshape(b, NSA_KV_HEADS, NSA_GROUP, QROWS, HEAD_DIM)[:, :, :, :s]
    return o.transpose(0, 3, 1, 2, 4).reshape(b * s, NSA_HEADS * HEAD_DIM)


def _new_rows(proj, col, b, s):
    lo = col * HEAD_DIM
    x = proj[:, lo:lo + NSA_KV_HEADS * HEAD_DIM].reshape(b, s, NSA_KV_HEADS, HEAD_DIM)
    return jnp.pad(x.transpose(0, 2, 1, 3), ((0, 0), (0, 0), (0, LANE - s), (0, 0)))


def _sample_mixer(proj, g_b, pools, win_state, page_table, cmp_w, l, b, s, q0):
    assert MOBA_KV_HEADS == NSA_KV_HEADS and MOBA_GROUP == NSA_GROUP and s <= QROWS
    assert q0 % CMP_STRIDE == 0 and s < CMP_STRIDE and q0 % MOBA_BLOCK == 0
    pool_moba, pool_cmp, pool_slc = pools
    nkh = NSA_KV_HEADS
    q_a = _to_qrows(proj[:, :MOBA_HEADS * HEAD_DIM], b, s)
    q_b = _to_qrows(proj[:, COL_QB * HEAD_DIM:COL_CMP * HEAD_DIM], b, s)
    o_a = _moba_sample(q_a, _new_rows(proj, COL_KA, b, s), _new_rows(proj, COL_VA, b, s),
                       pool_moba, page_table, l, s)
    kvc = _compress_sample(pool_cmp, page_table, l, *cmp_w)
    o_c, sel = _nsa_cmp_sample(q_b, kvc, q0, s)
    g3 = g_b[:, :GB_COLS].reshape(b, s, 3, nkh, NSA_GROUP).transpose(0, 3, 4, 1, 2)
    g3 = jnp.pad(g3, ((0, 0), (0, 0), (0, 0), (0, QROWS - s), (0, LANE - 3)))
    o_b = _nsa_slc_sample(q_b, sel, o_c, g3.reshape(b, nkh, NQ, LANE),
                          _new_rows(proj, COL_SLC, b, s), _new_rows(proj, COL_SLC + nkh, b, s),
                          _new_rows(proj, COL_WIN, b, s), _new_rows(proj, COL_WIN + nkh, b, s),
                          win_state, pool_slc, page_table, l, s)
    return _from_qrows(o_a, b, s), _from_qrows(o_b, b, s)


def _moe_block(x, h, logits, g2, norm_final, w_gate, w_up, w_down, l, bm, tm, rows_per_mod,
               final_norm):
    blk_expert, row_tok, n_used, row_w, d0, d1 = _route(logits, bm)
    yb = _moe_ffn(h, blk_expert, row_tok, n_used, row_w, w_gate, w_up, w_down, l, bm)
    return _moe_combine(x, g2, norm_final, yb, d0, d1, tm, rows_per_mod, final_norm)


def kernel(x_prompt, x_sample, cache_moba_kv, cache_cmp_kv, cache_slc_kv, state_win_kv,
           page_table, c_prompt, c_sample, w_in, w_pa, w_pb, w_out, cmp_pos, cmp_w1, cmp_b1,
           cmp_w2, cmp_b2, norm_attn, norm_ffn, norm_final, w_ada, b_ada, w_rg, b_rg, w_re,
           b_re, w_gate, w_up, w_down):
    bp, sp, d = x_prompt.shape
    bs, ss, _ = x_sample.shape
    tp, ts = bp * sp, bs * ss
    past_len = page_table.shape[1] * cache_moba_kv.shape[2]
    xp = x_prompt.reshape(tp, d)
    xs = x_sample.reshape(ts, d)
    cos_p, sin_p = _rope_tables(jnp.tile(jnp.arange(sp), bp))
    cos_s, sin_s = _rope_tables(jnp.tile(past_len + jnp.arange(ss), bs))
    n_c = _round_up(bp + bs, 8)
    c_all = jnp.pad(jnp.concatenate([c_prompt, c_sample], axis=0), ((0, n_c - bp - bs), (0, 0)))
    n_pool = cache_moba_kv.shape[1]
    pools = tuple(c.reshape(DEPTH, n_pool, PAGE_ROWS, HEAD_DIM)
                  for c in (cache_moba_kv, cache_cmp_kv, cache_slc_kv))
    n_win = state_win_kv.shape[2]
    win_view = state_win_kv.reshape(DEPTH, bs, n_win * 2 * NSA_KV_HEADS, HEAD_DIM)
    rows_p, rows_s = [], []
    for l in range(DEPTH):
        last = l == DEPTH - 1
        ada = _ada(c_all, w_ada, b_ada, l).reshape(n_c, 6, d)
        mod_p = [ada[:bp, k].reshape(bp, 1, d) for k in range(6)]
        mod_s = [jnp.repeat(ada[bp:bp + bs, k], ss, axis=0).reshape(1, ts, d) for k in range(6)]
        w_main = jnp.concatenate([w_in[l][:, :QKV_COLS], w_in[l][:, QKV_COLS + GB_COLS:]],
                                 axis=1).astype(BF16)
        w_gb = jnp.pad(w_in[l][:, QKV_COLS:QKV_COLS + GB_COLS],
                       ((0, 0), (0, LANE - GB_COLS))).astype(BF16)
        w_pa_b, w_pb_b, w_out_b = w_pa[l].astype(BF16), w_pb[l].astype(BF16), w_out[l].astype(BF16)
        w_r = jnp.pad(jnp.concatenate([w_rg[l], w_re[l]], axis=1),
                      ((0, 0), (0, LANE - N_GROUPS - N_EXPERTS)))
        b_r = jnp.pad(jnp.concatenate([b_rg[l], b_re[l]]),
                      (0, LANE - N_GROUPS - N_EXPERTS)).reshape(1, LANE)

        sh1, sc1, g1, sh2, sc2, g2 = mod_p
        proj, g_b = _in_proj(xp, norm_attn[l], sc1, sh1, cos_p, sin_p, w_main, w_gb,
                             PROMPT_TM_IN, sp)
        proj3 = proj.reshape(bp, sp, PROJ_COLS)
        o_a = _moba_prompt(proj3)
        kvc = _compress_prompt(proj3, cmp_pos[l], cmp_w1[l], cmp_b1[l], cmp_w2[l], cmp_b2[l])
        o_b = _nsa_prompt(proj3, kvc, g_b.reshape(bp, sp, LANE))
        xp, h2, logits = _out_proj(o_a.reshape(tp, -1), o_b.reshape(tp, -1), proj, xp, g1,
                                   norm_ffn[l], sc2, sh2, w_pa_b, w_pb_b, w_out_b, w_r, b_r,
                                   PROMPT_TM_OUT, sp)
        xp = _moe_block(xp, h2, logits, g2, norm_final, w_gate, w_up, w_down, l,
                        MOE_BLOCK, MOE_TM, sp, last)
        new_win = _kv_rows(proj, COL_WIN, bp, sp)
        rows_p.append((_kv_rows(proj, COL_KA, bp, sp), _kv_rows(proj, COL_CMP, bp, sp),
                       _kv_rows(proj, COL_SLC, bp, sp), new_win[:, sp - min(WINDOW, sp):]))

        sh1, sc1, g1, sh2, sc2, g2 = mod_s
        proj, g_b = _in_proj(xs, norm_attn[l], sc1, sh1, cos_s, sin_s, w_main, w_gb, ts, ss)
        cmp_w = (cmp_pos[l], cmp_w1[l], cmp_b1[l], cmp_w2[l], cmp_b2[l])
        o_a, o_b = _sample_mixer(proj, g_b, pools, win_view, page_table, cmp_w, l, bs, ss, past_len)
        xs, h2, logits = _out_proj(o_a, o_b, proj, xs, g1, norm_ffn[l], sc2, sh2,
                                   w_pa_b, w_pb_b, w_out_b, w_r, b_r, ts, ss)
        xs = _moe_block(xs, h2, logits, g2, norm_final, w_gate, w_up, w_down, l,
                        SAMPLE_MOE_BLOCK, ts, ss, last)
        win_rows = jnp.concatenate([state_win_kv[l], _kv_rows(proj, COL_WIN, bs, ss)], axis=1)
        rows_s.append((_kv_rows(proj, COL_KA, bs, ss), _kv_rows(proj, COL_CMP, bs, ss),
                       _kv_rows(proj, COL_SLC, bs, ss),
                       win_rows[:, win_rows.shape[1] - min(WINDOW, win_rows.shape[1]):]))

    return (xp.reshape(bp, sp, d), xs.reshape(bs, ss, d),
            jnp.stack([r[0] for r in rows_p]), jnp.stack([r[0] for r in rows_s]),
            jnp.stack([r[1] for r in rows_p]), jnp.stack([r[1] for r in rows_s]),
            jnp.stack([r[2] for r in rows_p]), jnp.stack([r[2] for r in rows_s]),
            jnp.stack([r[3] for r in rows_p]), jnp.stack([r[3] for r in rows_s]))
```

```python
import functools

import jax
import jax.numpy as jnp
import numpy as np
from jax import lax
from jax.experimental import pallas as pl
from jax.experimental.pallas import tpu as pltpu

D_MODEL = 2048
DEPTH = 2
HEAD_DIM = 128
ROT_DIM = HEAD_DIM // 4
ROPE_THETA = 500000.0
NORM_EPS = 1e-6
MOBA_HEADS = 8
MOBA_KV_HEADS = 2
MOBA_GROUP = MOBA_HEADS // MOBA_KV_HEADS
MOBA_BLOCK = 256
MOBA_TOPK = 3
MOBA_QCHUNK = 16
NSA_HEADS = 8
NSA_KV_HEADS = 2
NSA_GROUP = NSA_HEADS // NSA_KV_HEADS
CMP_LEN = 32
CMP_STRIDE = 16
CMP_HIDDEN = 128
SLC_BLOCK = 64
SLC_TOPN = 16
WINDOW = 512
NSA_QCHUNK = 16
WIN_QBLOCK = 128
N_GROUPS = 4
EXPERTS_PER_GROUP = 8
N_EXPERTS = N_GROUPS * EXPERTS_PER_GROUP
EXPERT_TOPK = 2
D_EXPERT = 512
MOE_BLOCK = 128

NEG_INF = -1e30
BIG = 1e30
F32 = jnp.float32
BF16 = jnp.bfloat16

LANE = 128
VMEM_LIMIT_BYTES = 48 * 1024 * 1024


def _round_up(n, m):
    return (n + m - 1) // m * m


QKV_COLS = 4096
GB_COLS = 3 * NSA_HEADS
GATE_COLS = 2 * D_MODEL
PROJ_COLS = QKV_COLS + GATE_COLS
COL_QA, COL_KA, COL_VA, COL_QB = 0, 8, 10, 12
COL_CMP, COL_SLC, COL_WIN = 20, 24, 28


def _mod_spec(mod, tm, rows_per_mod):
    d = mod.shape[-1]
    if mod.shape[1] == 1:
        return pl.BlockSpec((1, 1, d), lambda i, *_: (i * tm // rows_per_mod, 0, 0))
    return pl.BlockSpec((1, tm, d), lambda i, *_: (i, 0, 0))


def _ada_kernel(c_ref, w_ref, b_ref, o_ref):
    c = jax.nn.silu(c_ref[...]).astype(BF16)
    o_ref[...] = jnp.dot(c, w_ref[0].astype(BF16), preferred_element_type=F32) + b_ref[0]


def _ada(c, w_ada, b_ada, l, tn=1024):
    m, k = c.shape
    depth, _, n = w_ada.shape
    return pl.pallas_call(
        _ada_kernel,
        grid=(n // tn,),
        in_specs=[pl.BlockSpec((m, k), lambda j: (0, 0)),
                  pl.BlockSpec((1, k, tn), lambda j: (l, 0, j)),
                  pl.BlockSpec((1, 1, tn), lambda j: (l, 0, j))],
        out_specs=pl.BlockSpec((m, tn), lambda j: (0, j)),
        out_shape=jax.ShapeDtypeStruct((m, n), F32),
        compiler_params=pltpu.CompilerParams(dimension_semantics=("arbitrary",),
                                             vmem_limit_bytes=VMEM_LIMIT_BYTES),
        name="ada_ln",
    )(c, w_ada, b_ada.reshape(depth, 1, n))


def _rope_tables(pos):
    half = ROT_DIM // 2
    inv_freq = ROPE_THETA ** (-jnp.arange(half, dtype=F32) / half)
    ang = pos.astype(F32)[:, None] * inv_freq[None, :]
    cos, sin = jnp.cos(ang), jnp.sin(ang)
    n = pos.shape[0]
    cos_t = jnp.concatenate([cos, cos, jnp.ones((n, HEAD_DIM - ROT_DIM), F32)], axis=1)
    sin_t = jnp.concatenate([-sin, sin, jnp.zeros((n, HEAD_DIM - ROT_DIM), F32)], axis=1)
    return cos_t, sin_t


def _rope_column_mask():
    m = np.zeros((1, PROJ_COLS), np.float32)
    for lo, hi in ((0, 1280), (1536, 2560), (3072, 3328), (3584, 3840)):
        m[0, lo:hi] = 1.0
    return jnp.asarray(m)


def _in_proj_kernel(x_ref, g_ref, sc_ref, sh_ref, cos_ref, sin_ref, rmask_ref, w_ref, wgb_ref,
                    o_ref, ogb_ref, hb_scr, *, tn):
    j = pl.program_id(1)

    @pl.when(j == 0)
    def _():
        x = x_ref[...]
        y = x * lax.rsqrt(jnp.mean(x * x, axis=-1, keepdims=True) + NORM_EPS) * g_ref[...]
        h = (y * (1.0 + sc_ref[0]) + sh_ref[0]).astype(BF16)
        hb_scr[...] = h
        ogb_ref[...] = jnp.dot(h, wgb_ref[...], preferred_element_type=F32)

    acc = jnp.dot(hb_scr[...], w_ref[...], preferred_element_type=F32)

    @pl.when(j < QKV_COLS // tn)
    def _():
        lane = lax.broadcasted_iota(jnp.int32, (acc.shape[0], HEAD_DIM), 1)
        cos_t, sin_t = cos_ref[...], sin_ref[...]
        for hh in range(tn // HEAD_DIM):
            blk = acc[:, hh * HEAD_DIM:(hh + 1) * HEAD_DIM]
            on = rmask_ref[:, hh * HEAD_DIM:(hh + 1) * HEAD_DIM] > 0.5
            partner = jnp.where(lane < ROT_DIM // 2,
                                pltpu.roll(blk, HEAD_DIM - ROT_DIM // 2, axis=1),
                                pltpu.roll(blk, ROT_DIM // 2, axis=1))
            o_ref[:, hh * HEAD_DIM:(hh + 1) * HEAD_DIM] = (
                blk * jnp.where(on, cos_t, 1.0) + partner * jnp.where(on, sin_t, 0.0))

    @pl.when(j >= QKV_COLS // tn)
    def _():
        o_ref[...] = jax.nn.sigmoid(acc)


def _in_proj(x, g, sc, sh, cos_t, sin_t, w_main, w_gb, tm, rows_per_mod, tn=1024):
    t, d = x.shape
    rmask = _rope_column_mask()
    return pl.pallas_call(
        functools.partial(_in_proj_kernel, tn=tn),
        grid=(t // tm, PROJ_COLS // tn),
        in_specs=[pl.BlockSpec((tm, d), lambda i, j: (i, 0)),
                  pl.BlockSpec((1, d), lambda i, j: (0, 0)),
                  _mod_spec(sc, tm, rows_per_mod), _mod_spec(sh, tm, rows_per_mod),
                  pl.BlockSpec((tm, HEAD_DIM), lambda i, j: (i, 0)),
                  pl.BlockSpec((tm, HEAD_DIM), lambda i, j: (i, 0)),
                  pl.BlockSpec((1, tn), lambda i, j: (0, j)),
                  pl.BlockSpec((d, tn), lambda i, j: (0, j)),
                  pl.BlockSpec((d, LANE), lambda i, j: (0, 0))],
        out_specs=[pl.BlockSpec((tm, tn), lambda i, j: (i, j)),
                   pl.BlockSpec((tm, LANE), lambda i, j: (i, 0))],
        out_shape=[jax.ShapeDtypeStruct((t, PROJ_COLS), F32),
                   jax.ShapeDtypeStruct((t, LANE), F32)],
        scratch_shapes=[pltpu.VMEM((tm, d), BF16)],
        compiler_params=pltpu.CompilerParams(dimension_semantics=("arbitrary", "arbitrary"),
                                             vmem_limit_bytes=VMEM_LIMIT_BYTES),
        name="in_proj",
    )(x, g.reshape(1, d), sc, sh, cos_t, sin_t, rmask, w_main, w_gb)


def _out_proj_kernel(oa_ref, ob_ref, gma_ref, gmb_ref, x_ref, g1_ref, gf_ref, sc_ref, sh_ref,
                     wpa_ref, wpb_ref, wout_ref, wr_ref, br_ref, xo_ref, h_ref, lg_ref):
    pa = jnp.dot(oa_ref[...].astype(BF16), wpa_ref[...], preferred_element_type=F32)
    pb = jnp.dot(ob_ref[...].astype(BF16), wpb_ref[...], preferred_element_type=F32)
    merged = (gma_ref[...] * pa + gmb_ref[...] * pb).astype(BF16)
    mix = jnp.dot(merged, wout_ref[...], preferred_element_type=F32)
    x = x_ref[...] + g1_ref[0] * mix
    xo_ref[...] = x
    y = x * lax.rsqrt(jnp.mean(x * x, axis=-1, keepdims=True) + NORM_EPS) * gf_ref[...]
    h = y * (1.0 + sc_ref[0]) + sh_ref[0]
    h_ref[...] = h
    lg_ref[...] = jnp.dot(h, wr_ref[...], precision=lax.Precision.HIGHEST,
                          preferred_element_type=F32) + br_ref[...]


def _out_proj(o_a, o_b, proj, x, g1, gf, sc, sh, w_pa, w_pb, w_out, w_r, b_r, tm, rows_per_mod):
    t, d = x.shape
    da = o_a.shape[1]
    const = lambda shape: pl.BlockSpec(shape, lambda i: (0,) * len(shape),
                                       pipeline_mode=pl.Buffered(1))
    gate_blk = QKV_COLS // d
    return pl.pallas_call(
        _out_proj_kernel,
        grid=(t // tm,),
        in_specs=[pl.BlockSpec((tm, da), lambda i: (i, 0)),
                  pl.BlockSpec((tm, da), lambda i: (i, 0)),
                  pl.BlockSpec((tm, d), lambda i: (i, gate_blk)),
                  pl.BlockSpec((tm, d), lambda i: (i, gate_blk + 1)),
                  pl.BlockSpec((tm, d), lambda i: (i, 0)),
                  _mod_spec(g1, tm, rows_per_mod),
                  const((1, d)),
                  _mod_spec(sc, tm, rows_per_mod), _mod_spec(sh, tm, rows_per_mod),
                  const((da, d)), const((da, d)), const((d, d)), const((d, LANE)), const((1, LANE))],
        out_specs=[pl.BlockSpec((tm, d), lambda i: (i, 0)),
                   pl.BlockSpec((tm, d), lambda i: (i, 0)),
                   pl.BlockSpec((tm, LANE), lambda i: (i, 0))],
        out_shape=[jax.ShapeDtypeStruct((t, d), F32), jax.ShapeDtypeStruct((t, d), F32),
                   jax.ShapeDtypeStruct((t, LANE), F32)],
        compiler_params=pltpu.CompilerParams(dimension_semantics=("arbitrary",),
                                             vmem_limit_bytes=VMEM_LIMIT_BYTES),
        name="out_proj",
    )(o_a, o_b, proj, proj, x, g1, gf.reshape(1, d), sc, sh, w_pa, w_pb, w_out, w_r, b_r)


def _moe_ffn_kernel(blk_e_ref, row_tok_ref, nused_ref, x_hbm, roww_ref, wg_ref, wu_ref, wd_ref,
                    o_ref, xbuf, sem, wg_bf, wu_bf, wd_bf, *, bm):
    i = pl.program_id(0)
    n_used = nused_ref[0]

    def issue(blk, slot):
        def one(r, c):
            tok = row_tok_ref[blk * bm + r]
            pltpu.make_async_copy(x_hbm.at[pl.ds(tok, 1)], xbuf.at[slot, pl.ds(r, 1)],
                                  sem.at[slot]).start()
            return c
        lax.fori_loop(0, bm, one, 0, unroll=8)

    def wait(slot):
        pltpu.make_async_copy(x_hbm.at[pl.ds(0, bm)], xbuf.at[slot], sem.at[slot]).wait()

    slot = i % 2

    @pl.when((i == 0) & (n_used > 0))
    def _():
        issue(0, 0)

    @pl.when(i < n_used)
    def _():
        wait(slot)

        @pl.when(i + 1 < n_used)
        def _():
            issue(i + 1, 1 - slot)

        prev = blk_e_ref[jnp.maximum(i - 1, 0)]

        @pl.when((i == 0) | (blk_e_ref[i] != prev))
        def _():
            wg_bf[...] = wg_ref[0, 0].astype(BF16)
            wu_bf[...] = wu_ref[0, 0].astype(BF16)
            wd_bf[...] = wd_ref[0, 0].astype(BF16)

        x = xbuf[slot].astype(BF16)
        hg = jnp.dot(x, wg_bf[...], preferred_element_type=F32)
        hu = jnp.dot(x, wu_bf[...], preferred_element_type=F32)
        act = (jax.nn.silu(hg) * hu).astype(BF16)
        y = jnp.dot(act, wd_bf[...], preferred_element_type=F32)
        o_ref[...] = y * roww_ref[...]

    @pl.when(i >= n_used)
    def _():
        o_ref[...] = jnp.zeros_like(o_ref)


def _moe_ffn(x, blk_expert, row_tok, n_used, row_w, w_gate, w_up, w_down, l, bm):
    n_rows = row_tok.shape[0]
    n_blk = n_rows // bm
    d = x.shape[1]
    de = w_gate.shape[3]
    grid_spec = pltpu.PrefetchScalarGridSpec(
        num_scalar_prefetch=3,
        grid=(n_blk,),
        in_specs=[pl.BlockSpec(memory_space=pl.ANY),
                  pl.BlockSpec((bm, 1), lambda i, be, rt, nu: (i, 0)),
                  pl.BlockSpec((1, 1, d, de), lambda i, be, rt, nu: (l, be[i], 0, 0)),
                  pl.BlockSpec((1, 1, d, de), lambda i, be, rt, nu: (l, be[i], 0, 0)),
                  pl.BlockSpec((1, 1, de, d), lambda i, be, rt, nu: (l, be[i], 0, 0))],
        out_specs=pl.BlockSpec((bm, d), lambda i, be, rt, nu: (i, 0)),
        scratch_shapes=[pltpu.VMEM((2, bm, d), F32),
                        pltpu.SemaphoreType.DMA((2,)),
                        pltpu.VMEM((d, de), BF16), pltpu.VMEM((d, de), BF16),
                        pltpu.VMEM((de, d), BF16)])
    return pl.pallas_call(
        functools.partial(_moe_ffn_kernel, bm=bm),
        grid_spec=grid_spec,
        out_shape=jax.ShapeDtypeStruct((n_rows, d), F32),
        compiler_params=pltpu.CompilerParams(dimension_semantics=("arbitrary",),
                                             vmem_limit_bytes=VMEM_LIMIT_BYTES),
        name="moe_ffn",
    )(blk_expert, row_tok, n_used, x, row_w, w_gate, w_up, w_down)


def _moe_combine_kernel(d0_ref, d1_ref, x_ref, g2_ref, nf_ref, yb_hbm, o_ref, ybuf, sem,
                        *, tm, final_norm):
    i = pl.program_id(0)
    n = pl.num_programs(0)

    def issue(blk, slot):
        def one(r, c):
            t = blk * tm + r
            pltpu.make_async_copy(yb_hbm.at[pl.ds(d0_ref[t], 1)], ybuf.at[slot, 0, pl.ds(r, 1)],
                                  sem.at[slot]).start()
            pltpu.make_async_copy(yb_hbm.at[pl.ds(d1_ref[t], 1)], ybuf.at[slot, 1, pl.ds(r, 1)],
                                  sem.at[slot]).start()
            return c
        lax.fori_loop(0, tm, one, 0, unroll=8)

    def wait(slot):
        pltpu.make_async_copy(yb_hbm.at[pl.ds(0, tm)], ybuf.at[slot, 0], sem.at[slot]).wait()
        pltpu.make_async_copy(yb_hbm.at[pl.ds(0, tm)], ybuf.at[slot, 1], sem.at[slot]).wait()

    slot = i % 2

    @pl.when(i == 0)
    def _():
        issue(0, 0)

    wait(slot)

    @pl.when(i + 1 < n)
    def _():
        issue(i + 1, 1 - slot)

    y = x_ref[...] + g2_ref[0] * (ybuf[slot, 0] + ybuf[slot, 1])
    if final_norm:
        y = y * lax.rsqrt(jnp.mean(y * y, axis=-1, keepdims=True) + NORM_EPS) * nf_ref[...]
    o_ref[...] = y


def _moe_combine(x, g2, norm_final, yb, d0, d1, tm, rows_per_mod, final_norm):
    t, d = x.shape
    grid_spec = pltpu.PrefetchScalarGridSpec(
        num_scalar_prefetch=2,
        grid=(t // tm,),
        in_specs=[pl.BlockSpec((tm, d), lambda i, a, b: (i, 0)),
                  _mod_spec(g2, tm, rows_per_mod),
                  pl.BlockSpec((1, d), lambda i, a, b: (0, 0)),
                  pl.BlockSpec(memory_space=pl.ANY)],
        out_specs=pl.BlockSpec((tm, d), lambda i, a, b: (i, 0)),
        scratch_shapes=[pltpu.VMEM((2, 2, tm, d), F32), pltpu.SemaphoreType.DMA((2,))])
    return pl.pallas_call(
        functools.partial(_moe_combine_kernel, tm=tm, final_norm=final_norm),
        grid_spec=grid_spec,
        out_shape=jax.ShapeDtypeStruct((t, d), F32),
        compiler_params=pltpu.CompilerParams(dimension_semantics=("arbitrary",),
                                             vmem_limit_bytes=VMEM_LIMIT_BYTES),
        name="moe_combine",
    )(d0, d1, x, g2, norm_final.reshape(1, d), yb)


def _route(logits, bm):
    t = logits.shape[0]
    a = t * EXPERT_TOPK
    g_prob = jax.nn.softmax(logits[:, :N_GROUPS], axis=-1)
    g_idx = jnp.argmax(g_prob, axis=-1, keepdims=True).astype(jnp.int32)
    g_w = jnp.max(g_prob, axis=-1, keepdims=True)
    e_logit = logits[:, N_GROUPS:N_GROUPS + N_EXPERTS].reshape(t, N_GROUPS, EXPERTS_PER_GROUP)
    e_logit = jnp.take_along_axis(e_logit, g_idx[:, :, None], axis=1)[:, 0]
    e_ids = jnp.arange(EXPERTS_PER_GROUP, dtype=jnp.int32)[None, :]
    i1 = jnp.argmax(e_logit, axis=-1, keepdims=True).astype(jnp.int32)
    rest = jnp.where(e_ids == i1, -jnp.inf, e_logit)
    i2 = jnp.argmax(rest, axis=-1, keepdims=True).astype(jnp.int32)
    e_val = jnp.concatenate([jnp.max(e_logit, axis=-1, keepdims=True),
                             jnp.max(rest, axis=-1, keepdims=True)], axis=-1)
    e_idx = jnp.concatenate([i1, i2], axis=-1)
    weights = (g_w * jax.nn.softmax(e_val, axis=-1)).reshape(a)
    flat_e = (g_idx * EXPERTS_PER_GROUP + e_idx).reshape(a)
    order = jnp.argsort(flat_e).astype(jnp.int32)
    e_sorted = flat_e[order]
    expert_ids = jnp.arange(N_EXPERTS, dtype=jnp.int32)
    counts = jnp.sum((flat_e[:, None] == expert_ids[None, :]).astype(jnp.int32), axis=0)
    padded = (counts + bm - 1) // bm * bm
    pad_end = jnp.cumsum(padded)
    pad_start = pad_end - padded
    start = jnp.cumsum(counts) - counts
    dest_sorted = pad_start[e_sorted] + jnp.arange(a, dtype=jnp.int32) - start[e_sorted]
    n_blk = -(-(a + N_EXPERTS * (bm - 1)) // bm)
    n_rows = n_blk * bm
    row_asg = jnp.full((n_rows,), -1, jnp.int32).at[dest_sorted].set(order)
    row_valid = row_asg >= 0
    row_asg = jnp.maximum(row_asg, 0)
    row_tok = row_asg // EXPERT_TOPK
    row_w = jnp.where(row_valid, weights[row_asg], 0.0).reshape(n_rows, 1)
    blk_start = jnp.arange(n_blk, dtype=jnp.int32) * bm
    blk_expert = jnp.minimum(jnp.sum((pad_end[None, :] <= blk_start[:, None]).astype(jnp.int32), axis=1),
                             N_EXPERTS - 1).astype(jnp.int32)
    n_used = (pad_end[-1] // bm).astype(jnp.int32).reshape(1)
    dest = jnp.zeros((a,), jnp.int32).at[order].set(dest_sorted).reshape(t, EXPERT_TOPK)
    return blk_expert, row_tok, n_used, row_w, dest[:, 0], dest[:, 1]


ATT_SCALE = HEAD_DIM ** -0.5
_NT = (((1,), (1,)), ((), ()))


def _flash_update(qb, k, v, mask, carry):
    m, l, acc = carry
    s = lax.dot_general(qb, k.astype(BF16), _NT, preferred_element_type=F32) * ATT_SCALE
    s = jnp.where(mask, s, NEG_INF)
    m_new = jnp.maximum(m, jnp.max(s, axis=-1, keepdims=True))
    alpha = jnp.exp(m - m_new)
    p = jnp.exp(s - m_new)
    l = alpha * l + jnp.sum(p, axis=-1, keepdims=True)
    acc = alpha * acc + jnp.dot(p.astype(BF16), v.astype(BF16), preferred_element_type=F32)
    return m_new, l, acc


def _flash_init(tq):
    return (jnp.full((tq, 1), -jnp.inf, F32), jnp.zeros((tq, 1), F32),
            jnp.zeros((tq, HEAD_DIM), F32))


def _moba_prompt_kernel(q_ref, k_ref, v_ref, o_ref, kmean_scr, sel_scr, *, nb):
    i = pl.program_id(2)
    tq = MOBA_BLOCK

    @pl.when(i == 0)
    def _():
        kmean_scr[...] = jnp.zeros_like(kmean_scr)
        for j in range(nb):
            kmean_scr[j:j + 1, :] = jnp.mean(k_ref[0, j * tq:(j + 1) * tq, :], axis=0, keepdims=True)

    lane = lax.broadcasted_iota(jnp.int32, (tq, LANE), 1)
    row = lax.broadcasted_iota(jnp.int32, (tq, tq), 0)
    col = lax.broadcasted_iota(jnp.int32, (tq, tq), 1)
    own0 = pl.multiple_of(i * tq, tq)
    for g in range(MOBA_GROUP):
        qg = q_ref[0, :, g * HEAD_DIM:(g + 1) * HEAD_DIM]
        gate = lax.dot_general(qg, kmean_scr[...], _NT, precision=lax.Precision.HIGHEST,
                               preferred_element_type=F32)
        gate = jnp.where(lane < i, gate, NEG_INF)
        for j in range(nb):
            gj = gate[:, j:j + 1]
            better = ((gate > gj) | ((gate == gj) & (lane < j))) & (lane < nb)
            cnt = jnp.sum(jnp.where(better, 1.0, 0.0), axis=-1, keepdims=True)
            selj = jnp.where((cnt < MOBA_TOPK) & (j < i), 1.0, 0.0)
            sel_scr[g, j] = jnp.broadcast_to(selj, (tq, LANE))

        qb = qg.astype(BF16)
        carry = _flash_update(qb, k_ref[0, pl.ds(own0, tq), :], v_ref[0, pl.ds(own0, tq), :],
                              col <= row, _flash_init(tq))

        def body(j, carry, qb=qb, g=g):
            off = pl.multiple_of(j * tq, tq)
            selb = sel_scr[g, j]
            mask = jnp.concatenate([selb, selb], axis=1) > 0.5
            return _flash_update(qb, k_ref[0, pl.ds(off, tq), :], v_ref[0, pl.ds(off, tq), :],
                                 mask, carry)

        m, l, acc = lax.fori_loop(0, i, body, carry)
        o_ref[0, :, g * HEAD_DIM:(g + 1) * HEAD_DIM] = acc / l


def _moba_prompt(proj):
    b, s, _ = proj.shape
    nb = s // MOBA_BLOCK
    assert s % MOBA_BLOCK == 0 and nb <= LANE
    gw = MOBA_GROUP * HEAD_DIM
    return pl.pallas_call(
        functools.partial(_moba_prompt_kernel, nb=nb),
        grid=(b, MOBA_KV_HEADS, nb),
        in_specs=[pl.BlockSpec((1, MOBA_BLOCK, gw), lambda bi, h, i: (bi, i, COL_QA // MOBA_GROUP + h)),
                  pl.BlockSpec((1, s, HEAD_DIM), lambda bi, h, i: (bi, 0, COL_KA + h)),
                  pl.BlockSpec((1, s, HEAD_DIM), lambda bi, h, i: (bi, 0, COL_VA + h))],
        out_specs=pl.BlockSpec((1, MOBA_BLOCK, gw), lambda bi, h, i: (bi, i, h)),
        out_shape=jax.ShapeDtypeStruct((b, s, MOBA_HEADS * HEAD_DIM), F32),
        scratch_shapes=[pltpu.VMEM((LANE, HEAD_DIM), F32),
                        pltpu.VMEM((MOBA_GROUP, nb, MOBA_BLOCK, LANE), F32)],
        compiler_params=pltpu.CompilerParams(
            dimension_semantics=("arbitrary", "arbitrary", "arbitrary"),
            vmem_limit_bytes=VMEM_LIMIT_BYTES),
        name="moba_prompt",
    )(proj, proj, proj)


KV_COLS = 2 * NSA_KV_HEADS * HEAD_DIM


def _compress_kernel(x0_ref, x1_ref, x2_ref, x3_ref, pos_ref, w1_ref, b1_ref, w2_ref, b2_ref,
                     o_ref, *, nch):
    x_refs = (x0_ref, x1_ref, x2_ref, x3_ref)
    for kv in range(2):
        for h in range(NSA_KV_HEADS):
            c = kv * NSA_KV_HEADS + h
            first = jnp.zeros((nch, CMP_HIDDEN), F32)
            second = jnp.zeros((nch, CMP_HIDDEN), F32)
            for r in range(CMP_STRIDE):
                xr = x_refs[c][0, pl.ds(r, nch, stride=CMP_STRIDE), :]
                r2 = CMP_STRIDE + r
                first += jnp.dot((xr + pos_ref[kv, r:r + 1, :]).astype(BF16),
                                 w1_ref[kv, r * HEAD_DIM:(r + 1) * HEAD_DIM, :].astype(BF16),
                                 preferred_element_type=F32)
                second += jnp.dot((xr + pos_ref[kv, r2:r2 + 1, :]).astype(BF16),
                                  w1_ref[kv, r2 * HEAD_DIM:(r2 + 1) * HEAD_DIM, :].astype(BF16),
                                  preferred_element_type=F32)
            pre = first + pltpu.roll(second, nch - 1, axis=0) + b1_ref[kv:kv + 1, :]
            hdn = jax.nn.gelu(pre)
            o_ref[0, c] = (jnp.dot(hdn.astype(BF16), w2_ref[kv].astype(BF16),
                                   preferred_element_type=F32) + b2_ref[kv:kv + 1, :])


def _compress_prompt(proj, pos, w1, b1, w2, b2):
    b, s, _ = proj.shape
    nch = s // CMP_STRIDE
    full = lambda shape: pl.BlockSpec(shape, lambda bi: (0,) * len(shape))
    return pl.pallas_call(
        functools.partial(_compress_kernel, nch=nch),
        grid=(b,),
        in_specs=[pl.BlockSpec((1, s, HEAD_DIM), functools.partial(lambda bi, c: (bi, 0, COL_CMP + c), c=c))
                  for c in range(2 * NSA_KV_HEADS)] + [
                  full(pos.shape), full(w1.shape), full(b1.shape), full(w2.shape), full(b2.shape)],
        out_specs=pl.BlockSpec((1, 2 * NSA_KV_HEADS, nch, HEAD_DIM), lambda bi: (bi, 0, 0, 0)),
        out_shape=jax.ShapeDtypeStruct((b, 2 * NSA_KV_HEADS, nch, HEAD_DIM), F32),
        compiler_params=pltpu.CompilerParams(dimension_semantics=("arbitrary",),
                                             vmem_limit_bytes=VMEM_LIMIT_BYTES),
        name="nsa_compress",
    )(proj, proj, proj, proj, pos, w1, b1, w2, b2)


NSA_TQ = 256
NSA_KB = 256


def _nsa_prompt_kernel(q_ref, kc_ref, vc_ref, ks_ref, vs_ref, kw_ref, vw_ref, gb_ref, o_ref,
                       mask_scr, *, s_len):
    kvh = pl.program_id(1)
    i = pl.program_id(2)
    tq, kb = NSA_TQ, NSA_KB
    n_cmp = s_len // CMP_STRIDE - CMP_LEN // CMP_STRIDE + 1
    n_slc = s_len // SLC_BLOCK
    n_top = min(SLC_TOPN, n_slc)
    nkb = s_len // kb
    dh = HEAD_DIM

    lane = lax.broadcasted_iota(jnp.int32, (tq, LANE), 1)
    t_col = i * tq + lax.broadcasted_iota(jnp.int32, (tq, 1), 0)
    avail = (lane * CMP_STRIDE + (CMP_LEN - 1) <= t_col) & (lane < n_cmp)
    nn = lax.broadcasted_iota(jnp.int32, (LANE, LANE), 0)
    jj = lax.broadcasted_iota(jnp.int32, (LANE, LANE), 1)
    sel_map = jnp.where((nn * CMP_STRIDE < jj * SLC_BLOCK + SLC_BLOCK)
                        & (nn * CMP_STRIDE + CMP_LEN > jj * SLC_BLOCK)
                        & (nn < n_cmp) & (jj < n_slc), 1.0, 0.0)
    gates = jax.nn.sigmoid(gb_ref[0])

    def gate_col(branch, g):
        c = branch * NSA_HEADS + kvh * NSA_GROUP + g
        return jnp.sum(jnp.where(lane == c, gates, 0.0), axis=-1, keepdims=True)

    imp = jnp.zeros((tq, LANE), F32)
    kc = kc_ref[0, 0].astype(BF16)
    vc = vc_ref[0, 0].astype(BF16)
    for g in range(NSA_GROUP):
        qb = q_ref[0, :, g * dh:(g + 1) * dh].astype(BF16)
        s_c = lax.dot_general(qb, kc, _NT, preferred_element_type=F32) * ATT_SCALE
        s_c = jnp.where(avail, s_c, NEG_INF)
        e = jnp.where(avail, jnp.exp(s_c - jnp.max(s_c, axis=-1, keepdims=True)), 0.0)
        den = jnp.sum(e, axis=-1, keepdims=True)
        p_c = e * jnp.where(den > 0.0, 1.0 / den, 0.0)
        o_c = jnp.dot(p_c.astype(BF16), vc, preferred_element_type=F32)
        imp = imp + jnp.dot(p_c, sel_map, precision=lax.Precision.HIGHEST,
                            preferred_element_type=F32)
        o_ref[0, :, g * dh:(g + 1) * dh] = gate_col(0, g) * o_c

    n_rows = _round_up(n_slc, 8)
    imp_t = imp.T[:n_rows]
    jio = lax.broadcasted_iota(jnp.int32, (n_rows, tq), 0)
    cur = (i * tq + lax.broadcasted_iota(jnp.int32, (n_rows, tq), 1)) // SLC_BLOCK
    forced = (jio == 0) | (jio == cur) | (jio == cur - 1)
    imp_t = jnp.where(forced, BIG, imp_t)
    imp_t = jnp.where(jio > cur, NEG_INF, imp_t)
    imp_t = jnp.where(jio < n_slc, imp_t, -jnp.inf)
    cnt = jnp.zeros((n_rows, tq), F32)
    for r in range(n_slc):
        vr = imp_t[r:r + 1, :]
        cnt = cnt + jnp.where((vr > imp_t) | ((vr == imp_t) & (r < jio)), 1.0, 0.0)
    sel_t = jnp.where((cnt < n_top) & (jio < n_slc), 1.0, 0.0)
    if n_rows < LANE:
        sel_t = jnp.concatenate([sel_t, jnp.zeros((LANE - n_rows, tq), F32)], axis=0)
    sel = sel_t.T.astype(BF16)
    erow = lax.broadcasted_iota(jnp.int32, (LANE, kb), 0)
    ecol = lax.broadcasted_iota(jnp.int32, (LANE, kb), 1)
    for jb in range(nkb):
        expand = jnp.where((jb * kb + ecol) // SLC_BLOCK == erow, 1.0, 0.0).astype(BF16)
        mask_scr[jb] = jnp.dot(sel, expand, preferred_element_type=F32)

    rowk = lax.broadcasted_iota(jnp.int32, (tq, kb), 0)
    colk = lax.broadcasted_iota(jnp.int32, (tq, kb), 1)
    causal = colk <= rowk
    own0 = pl.multiple_of(i * kb, kb)
    for g in range(NSA_GROUP):
        qb = q_ref[0, :, g * dh:(g + 1) * dh].astype(BF16)

        carry = _flash_update(qb, ks_ref[0, pl.ds(own0, kb), :], vs_ref[0, pl.ds(own0, kb), :],
                              (mask_scr[i] > 0.5) & causal, _flash_init(tq))

        def slc_body(j, carry, qb=qb):
            off = pl.multiple_of(j * kb, kb)
            return _flash_update(qb, ks_ref[0, pl.ds(off, kb), :], vs_ref[0, pl.ds(off, kb), :],
                                 mask_scr[j] > 0.5, carry)

        _, l, acc = lax.fori_loop(0, i, slc_body, carry)
        o_s = acc / l

        carry = _flash_update(qb, kw_ref[0, pl.ds(own0, kb), :], vw_ref[0, pl.ds(own0, kb), :],
                              causal, _flash_init(tq))

        def win_body(j, carry, qb=qb):
            off = pl.multiple_of(j * kb, kb)
            mask = (j * kb + colk) > (i * tq + rowk - WINDOW)
            return _flash_update(qb, kw_ref[0, pl.ds(off, kb), :], vw_ref[0, pl.ds(off, kb), :],
                                 mask, carry)

        _, l, acc = lax.fori_loop(jnp.maximum(i - WINDOW // kb, 0), i, win_body, carry)
        o_w = acc / l
        o_ref[0, :, g * dh:(g + 1) * dh] += gate_col(1, g) * o_s + gate_col(2, g) * o_w


def _nsa_prompt(proj, kvc, g_b):
    b, s, _ = proj.shape
    assert s % NSA_TQ == 0 and NSA_TQ == NSA_KB and s // CMP_STRIDE == LANE
    gw = NSA_GROUP * HEAD_DIM
    nkh = NSA_KV_HEADS
    seq = lambda col: pl.BlockSpec((1, s, HEAD_DIM), lambda bi, h, i: (bi, 0, col + h))
    return pl.pallas_call(
        functools.partial(_nsa_prompt_kernel, s_len=s),
        grid=(b, nkh, s // NSA_TQ),
        in_specs=[pl.BlockSpec((1, NSA_TQ, gw), lambda bi, h, i: (bi, i, COL_QB // NSA_GROUP + h)),
                  pl.BlockSpec((1, 1, LANE, HEAD_DIM), lambda bi, h, i: (bi, h, 0, 0)),
                  pl.BlockSpec((1, 1, LANE, HEAD_DIM), lambda bi, h, i: (bi, nkh + h, 0, 0)),
                  seq(COL_SLC), seq(COL_SLC + nkh), seq(COL_WIN), seq(COL_WIN + nkh),
                  pl.BlockSpec((1, NSA_TQ, LANE), lambda bi, h, i: (bi, i, 0))],
        out_specs=pl.BlockSpec((1, NSA_TQ, gw), lambda bi, h, i: (bi, i, h)),
        out_shape=jax.ShapeDtypeStruct((b, s, NSA_HEADS * HEAD_DIM), F32),
        scratch_shapes=[pltpu.VMEM((s // NSA_KB, NSA_TQ, NSA_KB), F32)],
        compiler_params=pltpu.CompilerParams(
            dimension_semantics=("arbitrary", "arbitrary", "arbitrary"),
            vmem_limit_bytes=VMEM_LIMIT_BYTES),
        name="nsa_prompt",
    )(proj, kvc, kvc, proj, proj, proj, proj, g_b)


def _flash_update_t(qb, kb, vt, mask, carry):
    m, l, acc = carry
    s = lax.dot_general(kb, qb, _NT, preferred_element_type=F32) * ATT_SCALE
    s = jnp.where(mask, s, NEG_INF)
    m_new = jnp.maximum(m, jnp.max(s, axis=0, keepdims=True))
    alpha = jnp.exp(m - m_new)
    p = jnp.exp(s - m_new)
    l = alpha * l + jnp.sum(p, axis=0, keepdims=True)
    acc = alpha * acc + jnp.dot(vt, p.astype(BF16), preferred_element_type=F32)
    return m_new, l, acc


def _flash_init_t(tq):
    return (jnp.full((1, tq), -jnp.inf, F32), jnp.zeros((1, tq), F32),
            jnp.zeros((HEAD_DIM, tq), F32))


def _stage_kv(k_ref, v_ref, kb_scr, vt_scr, nblk, blk):
    for j in range(nblk):
        kb_scr[j] = k_ref[0, j * blk:(j + 1) * blk, :].astype(BF16)
        vt_scr[j] = v_ref[0, j * blk:(j + 1) * blk, :].T.astype(BF16)


def _moba_prompt_t_kernel(q_ref, k_ref, v_ref, o_ref, kmean_scr, sel_scr, kb_scr, vt_scr, *, nb):
    i = pl.program_id(2)
    tq = MOBA_BLOCK
    nbp = kmean_scr.shape[0]

    @pl.when(i == 0)
    def _():
        kmean_scr[...] = jnp.zeros_like(kmean_scr)
        for j in range(nb):
            kmean_scr[j:j + 1, :] = jnp.mean(k_ref[0, j * tq:(j + 1) * tq, :], axis=0, keepdims=True)
        _stage_kv(k_ref, v_ref, kb_scr, vt_scr, nb, tq)

    wq = MOBA_GROUP * tq
    blk_row = lax.broadcasted_iota(jnp.int32, (nbp, tq), 0)
    qbs = []
    for g in range(MOBA_GROUP):
        qg = q_ref[0, :, g * HEAD_DIM:(g + 1) * HEAD_DIM]
        gate = lax.dot_general(kmean_scr[...], qg, _NT, precision=lax.Precision.HIGHEST,
                               preferred_element_type=F32)
        gate = jnp.where(blk_row < i, gate, NEG_INF)
        cnt = jnp.zeros((nbp, tq), F32)
        for j in range(nb):
            gj = gate[j:j + 1, :]
            cnt = cnt + jnp.where((gj > gate) | ((gj == gate) & (j < blk_row)), 1.0, 0.0)
        sel_scr[:, g * tq:(g + 1) * tq] = jnp.where((cnt < MOBA_TOPK) & (blk_row < i), 1.0, 0.0)
        qbs.append(qg.astype(BF16))
    qb = jnp.concatenate(qbs, axis=0)

    key_row = lax.broadcasted_iota(jnp.int32, (tq, wq), 0)
    q_idx = lax.broadcasted_iota(jnp.int32, (tq, wq), 1) & (tq - 1)
    carry = _flash_update_t(qb, kb_scr[i], vt_scr[i], key_row <= q_idx, _flash_init_t(wq))

    def body(j, carry):
        chosen = jnp.broadcast_to(sel_scr[pl.ds(j, 1), :], (tq, wq)) > 0.5
        return _flash_update_t(qb, kb_scr[j], vt_scr[j], chosen, carry)

    _, l, acc = lax.fori_loop(0, i, body, carry)
    o_t = acc / l
    for g in range(MOBA_GROUP):
        o_ref[0, :, g * HEAD_DIM:(g + 1) * HEAD_DIM] = o_t[:, g * tq:(g + 1) * tq].T


def _moba_prompt_t(proj):
    b, s, _ = proj.shape
    nb = s // MOBA_BLOCK
    assert s % MOBA_BLOCK == 0
    nbp = _round_up(nb, 8)
    gw = MOBA_GROUP * HEAD_DIM
    return pl.pallas_call(
        functools.partial(_moba_prompt_t_kernel, nb=nb),
        grid=(b, MOBA_KV_HEADS, nb),
        in_specs=[pl.BlockSpec((1, MOBA_BLOCK, gw), lambda bi, h, i: (bi, i, COL_QA // MOBA_GROUP + h)),
                  pl.BlockSpec((1, s, HEAD_DIM), lambda bi, h, i: (bi, 0, COL_KA + h)),
                  pl.BlockSpec((1, s, HEAD_DIM), lambda bi, h, i: (bi, 0, COL_VA + h))],
        out_specs=pl.BlockSpec((1, MOBA_BLOCK, gw), lambda bi, h, i: (bi, i, h)),
        out_shape=jax.ShapeDtypeStruct((b, s, MOBA_HEADS * HEAD_DIM), F32),
        scratch_shapes=[pltpu.VMEM((nbp, HEAD_DIM), F32),
                        pltpu.VMEM((nbp, MOBA_GROUP * MOBA_BLOCK), F32),
                        pltpu.VMEM((nb, MOBA_BLOCK, HEAD_DIM), BF16),
                        pltpu.VMEM((nb, HEAD_DIM, MOBA_BLOCK), BF16)],
        compiler_params=pltpu.CompilerParams(
            dimension_semantics=("arbitrary", "arbitrary", "arbitrary"),
            vmem_limit_bytes=VMEM_LIMIT_BYTES),
        name="moba_prompt",
    )(proj, proj, proj)


def _nsa_prompt_t_kernel(q_ref, kc_ref, vc_ref, ks_ref, vs_ref, kw_ref, vw_ref, gb_ref, o_ref,
                         ksb_scr, vst_scr, kwb_scr, vwt_scr, kcb_scr, vct_scr, gt_scr, sel_scr,
                         o_scr, *, s_len):
    kvh = pl.program_id(1)
    i = pl.program_id(2)
    tq, kb = NSA_TQ, NSA_KB
    n_cmp = s_len // CMP_STRIDE - CMP_LEN // CMP_STRIDE + 1
    n_slc = s_len // SLC_BLOCK
    n_top = min(SLC_TOPN, n_slc)
    nkb = s_len // kb
    n_rows = sel_scr.shape[0]
    per_kb = kb // SLC_BLOCK
    dh = HEAD_DIM

    @pl.when(i == 0)
    def _():
        _stage_kv(ks_ref, vs_ref, ksb_scr, vst_scr, nkb, kb)
        _stage_kv(kw_ref, vw_ref, kwb_scr, vwt_scr, nkb, kb)
        kcb_scr[...] = kc_ref[0, 0].astype(BF16)
        vct_scr[...] = vc_ref[0, 0].T.astype(BF16)

    gt_scr[...] = jax.nn.sigmoid(gb_ref[0]).T

    def gate_row(branch, g):
        return gt_scr[pl.ds(branch * NSA_HEADS + kvh * NSA_GROUP + g, 1), :]

    n_sub = lax.broadcasted_iota(jnp.int32, (LANE, tq), 0)
    t_lane = i * tq + lax.broadcasted_iota(jnp.int32, (LANE, tq), 1)
    avail = (n_sub * CMP_STRIDE + (CMP_LEN - 1) <= t_lane) & (n_sub < n_cmp)
    jj = lax.broadcasted_iota(jnp.int32, (LANE, LANE), 0)
    nn = lax.broadcasted_iota(jnp.int32, (LANE, LANE), 1)
    sel_map_t = jnp.where((nn * CMP_STRIDE < jj * SLC_BLOCK + SLC_BLOCK)
                          & (nn * CMP_STRIDE + CMP_LEN > jj * SLC_BLOCK)
                          & (nn < n_cmp) & (jj < n_slc), 1.0, 0.0)
    imp = jnp.zeros((LANE, tq), F32)
    for g in range(NSA_GROUP):
        qb = q_ref[0, :, g * dh:(g + 1) * dh].astype(BF16)
        s_c = lax.dot_general(kcb_scr[...], qb, _NT, preferred_element_type=F32) * ATT_SCALE
        s_c = jnp.where(avail, s_c, NEG_INF)
        e = jnp.where(avail, jnp.exp(s_c - jnp.max(s_c, axis=0, keepdims=True)), 0.0)
        den = jnp.sum(e, axis=0, keepdims=True)
        p_c = e * jnp.where(den > 0.0, 1.0 / den, 0.0)
        o_c = jnp.dot(vct_scr[...], p_c.astype(BF16), preferred_element_type=F32)
        imp = imp + jnp.dot(sel_map_t, p_c, precision=lax.Precision.HIGHEST,
                            preferred_element_type=F32)
        o_scr[:, g * tq:(g + 1) * tq] = gate_row(0, g) * o_c

    imp = imp[:n_rows]
    jio = lax.broadcasted_iota(jnp.int32, (n_rows, tq), 0)
    cur = (i * tq + lax.broadcasted_iota(jnp.int32, (n_rows, tq), 1)) // SLC_BLOCK
    forced = (jio == 0) | (jio == cur) | (jio == cur - 1)
    imp = jnp.where(forced, BIG, imp)
    imp = jnp.where(jio > cur, NEG_INF, imp)
    imp = jnp.where(jio < n_slc, imp, -jnp.inf)
    cnt = jnp.zeros((n_rows, tq), F32)
    for r in range(n_slc):
        vr = imp[r:r + 1, :]
        cnt = cnt + jnp.where((vr > imp) | ((vr == imp) & (r < jio)), 1.0, 0.0)
    sel_scr[...] = jnp.where((cnt < n_top) & (jio < n_slc), 1.0, 0.0)

    wq = NSA_GROUP * tq

    def slc_mask(jb):
        rows = [jnp.broadcast_to(sel_scr[pl.ds(jb * per_kb + a, 1), :], (SLC_BLOCK, tq))
                for a in range(per_kb)]
        one = jnp.concatenate(rows, axis=0)
        return jnp.concatenate([one] * NSA_GROUP, axis=1) > 0.5

    def gate_wide(branch):
        return jnp.concatenate([jnp.broadcast_to(gate_row(branch, g), (dh, tq))
                                for g in range(NSA_GROUP)], axis=1)

    qb = jnp.concatenate([q_ref[0, :, g * dh:(g + 1) * dh].astype(BF16)
                          for g in range(NSA_GROUP)], axis=0)
    key_row = lax.broadcasted_iota(jnp.int32, (kb, wq), 0)
    q_idx = lax.broadcasted_iota(jnp.int32, (kb, wq), 1) & (tq - 1)
    causal = key_row <= q_idx

    carry = _flash_update_t(qb, ksb_scr[i], vst_scr[i], slc_mask(i) & causal, _flash_init_t(wq))

    def slc_body(j, carry):
        return _flash_update_t(qb, ksb_scr[j], vst_scr[j], slc_mask(j), carry)

    _, l, acc = lax.fori_loop(0, i, slc_body, carry)
    o_scr[...] += gate_wide(1) * (acc / l)

    carry = _flash_update_t(qb, kwb_scr[i], vwt_scr[i], causal, _flash_init_t(wq))

    def win_body(j, carry):
        mask = (j * kb + key_row) > (i * tq + q_idx - WINDOW)
        return _flash_update_t(qb, kwb_scr[j], vwt_scr[j], mask, carry)

    _, l, acc = lax.fori_loop(jnp.maximum(i - WINDOW // kb, 0), i, win_body, carry)
    o_t = o_scr[...] + gate_wide(2) * (acc / l)
    for g in range(NSA_GROUP):
        o_ref[0, :, g * dh:(g + 1) * dh] = o_t[:, g * tq:(g + 1) * tq].T


def _nsa_prompt_t(proj, kvc, g_b):
    b, s, _ = proj.shape
    assert s % NSA_TQ == 0 and NSA_TQ == NSA_KB and s // CMP_STRIDE == LANE
    assert NSA_KB % SLC_BLOCK == 0
    gw = NSA_GROUP * HEAD_DIM
    nkh = NSA_KV_HEADS
    nkb = s // NSA_KB
    n_rows = _round_up(s // SLC_BLOCK, 8)
    seq = lambda col: pl.BlockSpec((1, s, HEAD_DIM), lambda bi, h, i: (bi, 0, col + h))
    kv_scr = [pltpu.VMEM((nkb, NSA_KB, HEAD_DIM), BF16), pltpu.VMEM((nkb, HEAD_DIM, NSA_KB), BF16)]
    return pl.pallas_call(
        functools.partial(_nsa_prompt_t_kernel, s_len=s),
        grid=(b, nkh, s // NSA_TQ),
        in_specs=[pl.BlockSpec((1, NSA_TQ, gw), lambda bi, h, i: (bi, i, COL_QB // NSA_GROUP + h)),
                  pl.BlockSpec((1, 1, LANE, HEAD_DIM), lambda bi, h, i: (bi, h, 0, 0)),
                  pl.BlockSpec((1, 1, LANE, HEAD_DIM), lambda bi, h, i: (bi, nkh + h, 0, 0)),
                  seq(COL_SLC), seq(COL_SLC + nkh), seq(COL_WIN), seq(COL_WIN + nkh),
                  pl.BlockSpec((1, NSA_TQ, LANE), lambda bi, h, i: (bi, i, 0))],
        out_specs=pl.BlockSpec((1, NSA_TQ, gw), lambda bi, h, i: (bi, i, h)),
        out_shape=jax.ShapeDtypeStruct((b, s, NSA_HEADS * HEAD_DIM), F32),
        scratch_shapes=kv_scr + kv_scr + [
            pltpu.VMEM((LANE, HEAD_DIM), BF16), pltpu.VMEM((HEAD_DIM, LANE), BF16),
            pltpu.VMEM((LANE, NSA_TQ), F32), pltpu.VMEM((n_rows, NSA_TQ), F32),
            pltpu.VMEM((HEAD_DIM, NSA_GROUP * NSA_TQ), F32)],
        compiler_params=pltpu.CompilerParams(
            dimension_semantics=("arbitrary", "arbitrary", "arbitrary"),
            vmem_limit_bytes=VMEM_LIMIT_BYTES),
        name="nsa_prompt",
    )(proj, kvc, kvc, proj, proj, proj, proj, g_b)


PAGE_SIZE = 128
PAGE_ROWS = PAGE_SIZE * 2 * NSA_KV_HEADS
PAGES_PER_STEP = 16
QROWS = 8
NQ = NSA_GROUP * QROWS


def _page_specs(n, l):
    def spec(k):
        return pl.BlockSpec((1, 1, PAGE_ROWS, HEAD_DIM), lambda b, j, pt: (l, pt[b, j * n + k], 0, 0))
    return [spec(k) for k in range(n)]


def _page_kv(ref, kv, h, rows=PAGE_SIZE):
    return ref[0, 0, pl.ds(kv * NSA_KV_HEADS + h, rows, stride=2 * NSA_KV_HEADS), :]


def _first_max_onehot(work, lane_f):
    mx = jnp.max(work, axis=-1, keepdims=True)
    first = jnp.min(jnp.where(work == mx, lane_f, 1e9), axis=-1, keepdims=True)
    return jnp.where(lane_f == first, 1.0, 0.0)


def _moba_sample_kernel(pt_ref, q_ref, knew_ref, vnew_ref, *rest, n_new, nb_past):
    pages = rest[:PAGES_PER_STEP]
    o_ref, m_scr, l_scr, acc_scr, kmean_scr = rest[PAGES_PER_STEP:]
    j = pl.program_id(1)
    nj = pl.num_programs(1)
    lane = lax.broadcasted_iota(jnp.int32, (NQ, LANE), 1)

    @pl.when(j == 0)
    def _():
        m_scr[...] = jnp.full_like(m_scr, NEG_INF)
        l_scr[...] = jnp.zeros_like(l_scr)
        kmean_scr[...] = jnp.zeros_like(kmean_scr)

    bps = PAGES_PER_STEP * PAGE_SIZE // MOBA_BLOCK
    for h in range(MOBA_KV_HEADS):
        qb = q_ref[0, h].astype(BF16)
        k_all = jnp.concatenate([_page_kv(pg, 0, h) for pg in pages], axis=0)
        v_all = jnp.concatenate([_page_kv(pg, 1, h) for pg in pages], axis=0).astype(BF16)
        s = lax.dot_general(qb, k_all.astype(BF16), _NT, preferred_element_type=F32) * ATT_SCALE
        blocks = [slice(u * MOBA_BLOCK, (u + 1) * MOBA_BLOCK) for u in range(bps)]
        m_bs = [jnp.max(s[:, sl], axis=-1, keepdims=True) for sl in blocks]
        p = jnp.exp(s - jnp.concatenate([jnp.broadcast_to(m_b, (NQ, MOBA_BLOCK)) for m_b in m_bs],
                                        axis=1))
        m_new, l_new = m_scr[h], l_scr[h]
        for u, sl in enumerate(blocks):
            blk = j * bps + u
            m_new = jnp.where(lane == blk, m_bs[u], m_new)
            l_new = jnp.where(lane == blk, jnp.sum(p[:, sl], axis=-1, keepdims=True), l_new)
            acc_scr[h, blk] = jnp.dot(p[:, sl].astype(BF16), v_all[sl], preferred_element_type=F32)
            kmean_scr[h, pl.ds(blk, 1), :] = (jnp.sum(k_all[sl], axis=0, keepdims=True)
                                             * (1.0 / MOBA_BLOCK))
        m_scr[h] = m_new
        l_scr[h] = l_new

    @pl.when(j == nj - 1)
    def _():
        lane_f = lane.astype(F32)
        trow = lax.broadcasted_iota(jnp.int32, (NQ, LANE), 0) & (QROWS - 1)
        for h in range(MOBA_KV_HEADS):
            q = q_ref[0, h]
            qb = q.astype(BF16)
            s_own = lax.dot_general(qb, knew_ref[0, h].astype(BF16), _NT,
                                    preferred_element_type=F32) * ATT_SCALE
            s_own = jnp.where((lane <= trow) & (lane < n_new), s_own, NEG_INF)
            m_own = jnp.max(s_own, axis=-1, keepdims=True)
            p_own = jnp.exp(s_own - m_own)
            l_own = jnp.sum(p_own, axis=-1, keepdims=True)
            acc_own = jnp.dot(p_own.astype(BF16), vnew_ref[0, h].astype(BF16),
                              preferred_element_type=F32)
            gate = lax.dot_general(q, kmean_scr[h], _NT, precision=lax.Precision.HIGHEST,
                                   preferred_element_type=F32)
            work = jnp.where(lane < nb_past, gate, -jnp.inf)
            sel = jnp.zeros((NQ, LANE), F32)
            for _ in range(min(MOBA_TOPK, nb_past)):
                pick = _first_max_onehot(work, lane_f)
                sel = jnp.maximum(sel, pick)
                work = jnp.where(pick > 0.5, -jnp.inf, work)
            chosen = sel > 0.5
            m_all = m_scr[h]
            m_tot = jnp.maximum(m_own, jnp.max(jnp.where(chosen, m_all, NEG_INF), axis=-1,
                                               keepdims=True))
            w = jnp.where(chosen, jnp.exp(m_all - m_tot), 0.0)
            w_own = jnp.exp(m_own - m_tot)
            l_tot = w_own * l_own + jnp.sum(w * l_scr[h], axis=-1, keepdims=True)

            def merge(blk, acc, w=w, h=h):
                wcol = jnp.sum(jnp.where(lane == blk, w, 0.0), axis=-1, keepdims=True)
                return acc + wcol * acc_scr[h, blk]

            acc = lax.fori_loop(0, nb_past, merge, w_own * acc_own)
            o_ref[0, h] = acc / l_tot


def _moba_sample(q, k_new, v_new, pool, page_table, l, n_new):
    b = q.shape[0]
    n_pages = page_table.shape[1]
    assert n_pages % PAGES_PER_STEP == 0 and (n_pages * PAGE_SIZE) % MOBA_BLOCK == 0
    nb_past = n_pages * PAGE_SIZE // MOBA_BLOCK
    assert nb_past <= LANE and n_new <= MOBA_BLOCK
    per_b = lambda shape: pl.BlockSpec(shape, lambda bi, j, pt: (bi,) + (0,) * (len(shape) - 1))
    grid_spec = pltpu.PrefetchScalarGridSpec(
        num_scalar_prefetch=1,
        grid=(b, n_pages // PAGES_PER_STEP),
        in_specs=[per_b((1, MOBA_KV_HEADS, NQ, HEAD_DIM)),
                  per_b((1, MOBA_KV_HEADS, LANE, HEAD_DIM)),
                  per_b((1, MOBA_KV_HEADS, LANE, HEAD_DIM))] + _page_specs(PAGES_PER_STEP, l),
        out_specs=per_b((1, MOBA_KV_HEADS, NQ, HEAD_DIM)),
        scratch_shapes=[pltpu.VMEM((MOBA_KV_HEADS, NQ, LANE), F32),
                        pltpu.VMEM((MOBA_KV_HEADS, NQ, LANE), F32),
                        pltpu.VMEM((MOBA_KV_HEADS, nb_past, NQ, HEAD_DIM), F32),
                        pltpu.VMEM((MOBA_KV_HEADS, LANE, HEAD_DIM), F32)])
    return pl.pallas_call(
        functools.partial(_moba_sample_kernel, n_new=n_new, nb_past=nb_past),
        grid_spec=grid_spec,
        out_shape=jax.ShapeDtypeStruct(q.shape, F32),
        compiler_params=pltpu.CompilerParams(dimension_semantics=("arbitrary", "arbitrary"),
                                             vmem_limit_bytes=VMEM_LIMIT_BYTES),
        name="moba_sample",
    )(page_table, q, k_new, v_new, *([pool] * PAGES_PER_STEP))


def _compress_sample_kernel(pt_ref, pos_ref, w1_ref, b1_ref, w2_ref, b2_ref, *rest):
    pages = rest[:PAGES_PER_STEP + 1]
    o_ref = rest[PAGES_PER_STEP + 1]
    cpp = PAGE_SIZE // CMP_STRIDE
    nch = PAGES_PER_STEP * cpp
    half = CMP_STRIDE * HEAD_DIM
    for kv in range(2):
        pos_first = jnp.concatenate([pos_ref[kv, r:r + 1, :] for r in range(CMP_STRIDE)], axis=1)
        pos_second = jnp.concatenate([pos_ref[kv, r:r + 1, :] for r in range(CMP_STRIDE, CMP_LEN)],
                                     axis=1)
        w_first = w1_ref[kv, :half, :].astype(BF16)
        w_second = w1_ref[kv, half:, :].astype(BF16)
        w2 = w2_ref[kv].astype(BF16)
        for h in range(NSA_KV_HEADS):
            c = kv * NSA_KV_HEADS + h
            x = jnp.concatenate(
                [jnp.concatenate([pg[0, 0, pl.ds(r * 2 * NSA_KV_HEADS + c, cpp,
                                                 stride=2 * NSA_KV_HEADS * CMP_STRIDE), :]
                                  for pg in pages], axis=0)
                 for r in range(CMP_STRIDE)], axis=1)
            first = jnp.dot((x[:nch] + pos_first).astype(BF16), w_first, preferred_element_type=F32)
            second = jnp.dot((x + pos_second).astype(BF16), w_second, preferred_element_type=F32)
            pre = first + pltpu.roll(second, nch + cpp - 1, axis=0)[:nch] + b1_ref[kv:kv + 1, :]
            hdn = jax.nn.gelu(pre)
            o_ref[0, c] = (jnp.dot(hdn.astype(BF16), w2, preferred_element_type=F32)
                           + b2_ref[kv:kv + 1, :])


def _compress_sample(pool, page_table, l, pos, w1, b1, w2, b2):
    b, n_pages = page_table.shape
    nch = PAGES_PER_STEP * PAGE_SIZE // CMP_STRIDE
    full = lambda shape: pl.BlockSpec(shape, lambda bi, j, pt: (0,) * len(shape))
    next_page = pl.BlockSpec(
        (1, 1, PAGE_ROWS, HEAD_DIM),
        lambda bi, j, pt: (l, pt[bi, jnp.minimum((j + 1) * PAGES_PER_STEP, n_pages - 1)], 0, 0))
    grid_spec = pltpu.PrefetchScalarGridSpec(
        num_scalar_prefetch=1,
        grid=(b, n_pages // PAGES_PER_STEP),
        in_specs=[full(pos.shape), full(w1.shape), full(b1.shape), full(w2.shape), full(b2.shape)]
        + _page_specs(PAGES_PER_STEP, l) + [next_page],
        out_specs=pl.BlockSpec((1, 2 * NSA_KV_HEADS, nch, HEAD_DIM), lambda bi, j, pt: (bi, 0, j, 0)))
    return pl.pallas_call(
        _compress_sample_kernel,
        grid_spec=grid_spec,
        out_shape=jax.ShapeDtypeStruct(
            (b, 2 * NSA_KV_HEADS, n_pages * PAGE_SIZE // CMP_STRIDE, HEAD_DIM), F32),
        compiler_params=pltpu.CompilerParams(dimension_semantics=("arbitrary", "arbitrary"),
                                             vmem_limit_bytes=VMEM_LIMIT_BYTES),
        name="nsa_compress_sample",
    )(page_table, pos, w1, b1, w2, b2, *([pool] * (PAGES_PER_STEP + 1)))


def _nsa_cmp_sample_kernel(q_ref, kc_ref, vc_ref, oc_ref, sel_ref, *, q0, n_cmp, n_slc, n_lanes):
    ncp = kc_ref.shape[2]
    col = lax.broadcasted_iota(jnp.int32, (NQ, ncp), 1)
    trow = lax.broadcasted_iota(jnp.int32, (NQ, ncp), 0) & (QROWS - 1)
    avail = (col * CMP_STRIDE + (CMP_LEN - 1) <= q0 + trow) & (col < n_cmp)
    nn = lax.broadcasted_iota(jnp.int32, (ncp, n_lanes), 0)
    jj = lax.broadcasted_iota(jnp.int32, (ncp, n_lanes), 1)
    sel_map = jnp.where((nn * CMP_STRIDE < jj * SLC_BLOCK + SLC_BLOCK)
                        & (nn * CMP_STRIDE + CMP_LEN > jj * SLC_BLOCK)
                        & (nn < n_cmp) & (jj < n_slc), 1.0, 0.0)
    lane = lax.broadcasted_iota(jnp.int32, (QROWS, n_lanes), 1)
    lane_f = lane.astype(F32)
    cur = (q0 + lax.broadcasted_iota(jnp.int32, (QROWS, n_lanes), 0)) // SLC_BLOCK
    for h in range(NSA_KV_HEADS):
        qb = q_ref[0, h].astype(BF16)
        s = lax.dot_general(qb, kc_ref[0, h].astype(BF16), _NT,
                            preferred_element_type=F32) * ATT_SCALE
        s = jnp.where(avail, s, NEG_INF)
        e = jnp.where(avail, jnp.exp(s - jnp.max(s, axis=-1, keepdims=True)), 0.0)
        den = jnp.sum(e, axis=-1, keepdims=True)
        p = e * jnp.where(den > 0.0, 1.0 / den, 0.0)
        oc_ref[0, h] = jnp.dot(p.astype(BF16), vc_ref[0, h].astype(BF16),
                               preferred_element_type=F32)
        pg = sum(p[g * QROWS:(g + 1) * QROWS] for g in range(NSA_GROUP))
        imp = jnp.dot(pg, sel_map, precision=lax.Precision.HIGHEST, preferred_element_type=F32)
        forced = (lane == 0) | (lane == cur) | (lane == cur - 1)
        imp = jnp.where(forced, BIG, imp)
        imp = jnp.where(lane > cur, NEG_INF, imp)
        work = jnp.where(lane < n_slc, imp, -jnp.inf)
        sel = jnp.zeros((QROWS, n_lanes), F32)
        for _ in range(min(SLC_TOPN, n_slc)):
            pick = _first_max_onehot(work, lane_f)
            sel = jnp.maximum(sel, pick)
            work = jnp.where(pick > 0.5, -jnp.inf, work)
        sel_ref[0, h] = sel


def _nsa_cmp_sample(q, kvc, q0, n_new):
    b = q.shape[0]
    ncp = kvc.shape[2]
    n_cmp = ncp - CMP_LEN // CMP_STRIDE + 1
    n_slc = -(-(q0 + n_new) // SLC_BLOCK)
    n_lanes = _round_up(n_slc, LANE)
    nkh = NSA_KV_HEADS
    return pl.pallas_call(
        functools.partial(_nsa_cmp_sample_kernel, q0=q0, n_cmp=n_cmp, n_slc=n_slc, n_lanes=n_lanes),
        grid=(b,),
        in_specs=[pl.BlockSpec((1, nkh, NQ, HEAD_DIM), lambda bi: (bi, 0, 0, 0)),
                  pl.BlockSpec((1, nkh, ncp, HEAD_DIM), lambda bi: (bi, 0, 0, 0)),
                  pl.BlockSpec((1, nkh, ncp, HEAD_DIM), lambda bi: (bi, 1, 0, 0))],
        out_specs=[pl.BlockSpec((1, nkh, NQ, HEAD_DIM), lambda bi: (bi, 0, 0, 0)),
                   pl.BlockSpec((1, nkh, QROWS, n_lanes), lambda bi: (bi, 0, 0, 0))],
        out_shape=[jax.ShapeDtypeStruct(q.shape, F32),
                   jax.ShapeDtypeStruct((b, nkh, QROWS, n_lanes), F32)],
        compiler_params=pltpu.CompilerParams(dimension_semantics=("arbitrary",),
                                             vmem_limit_bytes=VMEM_LIMIT_BYTES),
        name="nsa_cmp_sample",
    )(q, kvc, kvc)


def _nsa_slc_sample_kernel(pt_ref, q_ref, sel_ref, oc_ref, g_ref, ksn_ref, vsn_ref, kwn_ref,
                           vwn_ref, win_ref, *rest, n_new, n_win, cur_blk):
    pages = rest[:PAGES_PER_STEP]
    o_ref, m_scr, l_scr, acc_scr, ow_scr = rest[PAGES_PER_STEP:]
    j = pl.program_id(1)
    nj = pl.num_programs(1)
    n_lanes = sel_ref.shape[3]
    lane = lax.broadcasted_iota(jnp.int32, (NQ, LANE), 1)
    trow = lax.broadcasted_iota(jnp.int32, (NQ, LANE), 0) & (QROWS - 1)
    new_ok = (lane <= trow) & (lane < n_new)

    @pl.when(j == 0)
    def _():
        for h in range(NSA_KV_HEADS):
            qb = q_ref[0, h].astype(BF16)
            sel_h = jnp.concatenate([sel_ref[0, h]] * NSA_GROUP, axis=0)
            lane_s = lax.broadcasted_iota(jnp.int32, (NQ, n_lanes), 1)
            sel_cur = jnp.sum(jnp.where(lane_s == cur_blk, sel_h, 0.0), axis=-1,
                              keepdims=True) > 0.5
            s = lax.dot_general(qb, ksn_ref[0, h].astype(BF16), _NT,
                                preferred_element_type=F32) * ATT_SCALE
            s = jnp.where(new_ok & sel_cur, s, NEG_INF)
            m = jnp.max(s, axis=-1, keepdims=True)
            p = jnp.exp(s - m)
            m_scr[h] = m
            l_scr[h] = jnp.sum(p, axis=-1, keepdims=True)
            acc_scr[h] = jnp.dot(p.astype(BF16), vsn_ref[0, h].astype(BF16),
                                 preferred_element_type=F32)
            kw = win_ref[0, 0, pl.ds(h, n_win, stride=2 * NSA_KV_HEADS), :]
            vw = win_ref[0, 0, pl.ds(NSA_KV_HEADS + h, n_win, stride=2 * NSA_KV_HEADS), :]
            idx = lax.broadcasted_iota(jnp.int32, (NQ, n_win), 1)
            tr = lax.broadcasted_iota(jnp.int32, (NQ, n_win), 0) & (QROWS - 1)
            s_w = lax.dot_general(qb, kw.astype(BF16), _NT, preferred_element_type=F32) * ATT_SCALE
            s_w = jnp.where(idx + (WINDOW - n_win) > tr, s_w, NEG_INF)
            s_n = lax.dot_general(qb, kwn_ref[0, h].astype(BF16), _NT,
                                  preferred_element_type=F32) * ATT_SCALE
            s_n = jnp.where(new_ok, s_n, NEG_INF)
            mw = jnp.maximum(jnp.max(s_w, axis=-1, keepdims=True),
                             jnp.max(s_n, axis=-1, keepdims=True))
            p_w = jnp.exp(s_w - mw)
            p_n = jnp.exp(s_n - mw)
            lw = jnp.sum(p_w, axis=-1, keepdims=True) + jnp.sum(p_n, axis=-1, keepdims=True)
            ow_scr[h] = (jnp.dot(p_w.astype(BF16), vw.astype(BF16), preferred_element_type=F32)
                         + jnp.dot(p_n.astype(BF16), vwn_ref[0, h].astype(BF16),
                                   preferred_element_type=F32)) / lw

    step_keys = PAGES_PER_STEP * PAGE_SIZE
    erow = lax.broadcasted_iota(jnp.int32, (n_lanes, step_keys), 0)
    ecol = lax.broadcasted_iota(jnp.int32, (n_lanes, step_keys), 1)
    expand = jnp.where((j * step_keys + ecol) // SLC_BLOCK == erow, 1.0, 0.0).astype(BF16)
    for h in range(NSA_KV_HEADS):
        qb = q_ref[0, h].astype(BF16)
        sel_h = jnp.concatenate([sel_ref[0, h]] * NSA_GROUP, axis=0).astype(BF16)
        maskf = jnp.dot(sel_h, expand, preferred_element_type=F32)
        ss = []
        for k in range(PAGES_PER_STEP):
            s = lax.dot_general(qb, _page_kv(pages[k], 0, h).astype(BF16), _NT,
                                preferred_element_type=F32) * ATT_SCALE
            ss.append(jnp.where(maskf[:, k * PAGE_SIZE:(k + 1) * PAGE_SIZE] > 0.5, s, NEG_INF))
        m_old = m_scr[h]
        m_new = functools.reduce(jnp.maximum,
                                 [jnp.max(s, axis=-1, keepdims=True) for s in ss] + [m_old])
        alpha = jnp.exp(m_old - m_new)
        ps = [jnp.exp(s - m_new) for s in ss]
        l_scr[h] = alpha * l_scr[h] + sum(jnp.sum(p, axis=-1, keepdims=True) for p in ps)
        acc_scr[h] = alpha * acc_scr[h] + sum(
            jnp.dot(p.astype(BF16), _page_kv(pages[k], 1, h).astype(BF16),
                    preferred_element_type=F32) for k, p in enumerate(ps))
        m_scr[h] = m_new

    @pl.when(j == nj - 1)
    def _():
        for h in range(NSA_KV_HEADS):
            gates = jax.nn.sigmoid(g_ref[0, h])
            o_s = acc_scr[h] / l_scr[h]
            o_ref[0, h] = (gates[:, 0:1] * oc_ref[0, h] + gates[:, 1:2] * o_s
                           + gates[:, 2:3] * ow_scr[h])


def _nsa_slc_sample(q, sel, o_c, g3, ks_new, vs_new, kw_new, vw_new, win_state, pool, page_table,
                    l, n_new):
    b = q.shape[0]
    n_pages = page_table.shape[1]
    n_win = win_state.shape[2] // (2 * NSA_KV_HEADS)
    assert (n_pages * PAGE_SIZE) % SLC_BLOCK == 0 and n_new <= SLC_BLOCK and n_win <= WINDOW
    nkh = NSA_KV_HEADS
    per_b = lambda shape: pl.BlockSpec(shape, lambda bi, j, pt: (bi,) + (0,) * (len(shape) - 1))
    grid_spec = pltpu.PrefetchScalarGridSpec(
        num_scalar_prefetch=1,
        grid=(b, n_pages // PAGES_PER_STEP),
        in_specs=[per_b((1, nkh, NQ, HEAD_DIM)), per_b((1, nkh, QROWS, sel.shape[3])),
                  per_b((1, nkh, NQ, HEAD_DIM)), per_b((1, nkh, NQ, LANE)),
                  per_b((1, nkh, LANE, HEAD_DIM)), per_b((1, nkh, LANE, HEAD_DIM)),
                  per_b((1, nkh, LANE, HEAD_DIM)), per_b((1, nkh, LANE, HEAD_DIM)),
                  pl.BlockSpec((1, 1, win_state.shape[2], HEAD_DIM), lambda bi, j, pt: (l, bi, 0, 0))]
        + _page_specs(PAGES_PER_STEP, l),
        out_specs=per_b((1, nkh, NQ, HEAD_DIM)),
        scratch_shapes=[pltpu.VMEM((nkh, NQ, 1), F32), pltpu.VMEM((nkh, NQ, 1), F32),
                        pltpu.VMEM((nkh, NQ, HEAD_DIM), F32), pltpu.VMEM((nkh, NQ, HEAD_DIM), F32)])
    return pl.pallas_call(
        functools.partial(_nsa_slc_sample_kernel, n_new=n_new, n_win=n_win,
                          cur_blk=n_pages * PAGE_SIZE // SLC_BLOCK),
        grid_spec=grid_spec,
        out_shape=jax.ShapeDtypeStruct(q.shape, F32),
        compiler_params=pltpu.CompilerParams(dimension_semantics=("arbitrary", "arbitrary"),
                                             vmem_limit_bytes=VMEM_LIMIT_BYTES),
        name="nsa_slc_sample",
    )(page_table, q, sel, o_c, g3, ks_new, vs_new, kw_new, vw_new, win_state,
      *([pool] * PAGES_PER_STEP))


def in_proj_sizes():
    return (MOBA_HEADS * HEAD_DIM, MOBA_KV_HEADS * HEAD_DIM, MOBA_KV_HEADS * HEAD_DIM,
            NSA_HEADS * HEAD_DIM, 6 * NSA_KV_HEADS * HEAD_DIM, 3 * NSA_HEADS, D_MODEL, D_MODEL)


def rms_norm(x, g):
    xf = x.astype(F32)
    y = xf * lax.rsqrt(jnp.mean(xf * xf, axis=-1, keepdims=True) + NORM_EPS)
    return (y * g.astype(F32)).astype(x.dtype)


def apply_rope(x, pos):
    half = ROT_DIM // 2
    inv_freq = ROPE_THETA ** (-jnp.arange(half, dtype=F32) / half)
    ang = pos.astype(F32)[:, None] * inv_freq[None, :]
    ang = ang.reshape((ang.shape[0],) + (1,) * (x.ndim - 3) + (half,))
    cos, sin = jnp.cos(ang), jnp.sin(ang)
    xr = x[..., :ROT_DIM].astype(F32)
    x1, x2 = xr[..., :half], xr[..., half:]
    rot = jnp.concatenate([x1 * cos - x2 * sin, x2 * cos + x1 * sin], axis=-1)
    return jnp.concatenate([rot.astype(x.dtype), x[..., ROT_DIM:]], axis=-1)


def query_block(s, pref):
    return pref if s % pref == 0 else s


def gather_pages(pool, page_table):
    g = pool[page_table]
    return g.reshape((g.shape[0], g.shape[1] * g.shape[2]) + g.shape[3:])


def moba_attention(q, k_all, v_all, q0):
    B, S, KVH, G, dh = q.shape
    L = k_all.shape[1]
    nb = -(-L // MOBA_BLOCK)
    pad = ((0, 0), (0, nb * MOBA_BLOCK - L), (0, 0), (0, 0))
    kb = jnp.pad(k_all, pad).reshape(B, nb, MOBA_BLOCK, KVH, dh).transpose(0, 3, 1, 2, 4)
    vb = jnp.pad(v_all, pad).reshape(B, nb, MOBA_BLOCK, KVH, dh).transpose(0, 3, 1, 2, 4)
    k_mean = jnp.mean(kb, axis=3, dtype=F32)
    n_sel = min(MOBA_TOPK, nb)
    qc = query_block(S, MOBA_QCHUNK)
    nq = S // qc
    scale = HEAD_DIM ** -0.5
    bi = jnp.arange(B)[:, None, None, None, None]
    gi = jnp.arange(KVH)[None, None, :, None, None]
    blk_ids = jnp.arange(nb)
    in_blk = jnp.arange(MOBA_BLOCK)

    def one_chunk(args):
        qch, start = args
        t = start + jnp.arange(qc)
        own = t // MOBA_BLOCK
        gate = jnp.einsum('bqghd,bgnd->bqghn', qch.astype(F32), k_mean)
        gate = jnp.where((blk_ids[None, :] < own[:, None])[None, :, None, None, :], gate, NEG_INF)
        _, sel = lax.top_k(gate, n_sel)
        slot_ok = (jnp.arange(n_sel)[None, :] < own[:, None])[None, :, None, None, :, None]
        k_sel = kb[bi, gi, sel]
        v_sel = vb[bi, gi, sel]
        s_sel = jnp.einsum('bqghd,bqghskd->bqghsk', qch, k_sel, preferred_element_type=F32) * scale
        s_sel = jnp.where(slot_ok, s_sel, NEG_INF).reshape(B, qc, KVH, G, n_sel * MOBA_BLOCK)
        k_own = kb[:, :, own]
        v_own = vb[:, :, own]
        s_own = jnp.einsum('bqghd,bgqkd->bqghk', qch, k_own, preferred_element_type=F32) * scale
        causal = (own[:, None] * MOBA_BLOCK + in_blk[None, :]) <= t[:, None]
        s_own = jnp.where(causal[None, :, None, None, :], s_own, NEG_INF)
        p = jax.nn.softmax(jnp.concatenate([s_sel, s_own], axis=-1), axis=-1).astype(v_all.dtype)
        p_sel = p[..., :n_sel * MOBA_BLOCK].reshape(B, qc, KVH, G, n_sel, MOBA_BLOCK)
        p_own = p[..., n_sel * MOBA_BLOCK:]
        return (jnp.einsum('bqghsk,bqghskd->bqghd', p_sel, v_sel)
                + jnp.einsum('bqghk,bgqkd->bqghd', p_own, v_own))

    q_chunks = q.reshape(B, nq, qc, KVH, G, dh).transpose(1, 0, 2, 3, 4, 5)
    o = lax.map(one_chunk, (q_chunks, q0 + qc * jnp.arange(nq)))
    return o.transpose(1, 0, 2, 3, 4, 5).reshape(B, S, KVH * G * dh)


def compress_rows(x, w_pos, w1, b1, w2, b2):
    B, L, KVH, dh = x.shape
    r = CMP_LEN // CMP_STRIDE
    n_chunks = L // CMP_STRIDE
    n_cmp = n_chunks - r + 1
    ch = x[:, :n_chunks * CMP_STRIDE].reshape(B, n_chunks, CMP_STRIDE, KVH, dh)
    blocks = jnp.concatenate([ch[:, j:j + n_cmp] for j in range(r)], axis=2)
    blocks = (blocks + w_pos[None, None, :, None, :]).transpose(0, 1, 3, 2, 4)
    flat = blocks.reshape(B, n_cmp, KVH, CMP_LEN * dh)
    return jax.nn.gelu(flat @ w1 + b1) @ w2 + b2


def cmp_to_slc_map(n_cmp, n_slc):
    lo = np.arange(n_cmp) * CMP_STRIDE
    blo = np.arange(n_slc) * SLC_BLOCK
    m = (lo[:, None] < blo[None, :] + SLC_BLOCK) & (lo[:, None] + CMP_LEN > blo[None, :])
    return jnp.asarray(m.astype(np.float32))


def nsa_compressed_selected(q, k_cmp, v_cmp, k_slc, v_slc, q0):
    B, S, KVH, G, dh = q.shape
    L = k_slc.shape[1]
    n_cmp = k_cmp.shape[1]
    n_slc = -(-L // SLC_BLOCK)
    n_top = min(SLC_TOPN, n_slc)
    pad = ((0, 0), (0, n_slc * SLC_BLOCK - L), (0, 0), (0, 0))
    kb = jnp.pad(k_slc, pad).reshape(B, n_slc, SLC_BLOCK, KVH, dh).transpose(0, 3, 1, 2, 4)
    vb = jnp.pad(v_slc, pad).reshape(B, n_slc, SLC_BLOCK, KVH, dh).transpose(0, 3, 1, 2, 4)
    sel_map = cmp_to_slc_map(n_cmp, n_slc)
    cmp_last = jnp.arange(n_cmp) * CMP_STRIDE + CMP_LEN - 1
    blk_ids = jnp.arange(n_slc)
    in_blk = jnp.arange(SLC_BLOCK)
    qc = query_block(S, NSA_QCHUNK)
    nq = S // qc
    scale = HEAD_DIM ** -0.5
    bi = jnp.arange(B)[:, None, None, None]
    gi = jnp.arange(KVH)[None, None, :, None]

    def one_chunk(args):
        qch, start = args
        t = start + jnp.arange(qc)
        avail = (cmp_last[None, :] <= t[:, None])[None, :, None, None, :]
        s_c = jnp.einsum('bqghd,bngd->bqghn', qch, k_cmp, preferred_element_type=F32) * scale
        p_c = jnp.where(avail, jax.nn.softmax(jnp.where(avail, s_c, NEG_INF), axis=-1), 0.0)
        o_c = jnp.einsum('bqghn,bngd->bqghd', p_c.astype(v_cmp.dtype), v_cmp)
        imp = jnp.einsum('bqghn,nj->bqgj', p_c, sel_map)
        cur = (t // SLC_BLOCK)[:, None]
        forced = (blk_ids == 0) | (blk_ids == cur) | (blk_ids == cur - 1)
        imp = jnp.where(forced[None, :, None, :], BIG, imp)
        imp = jnp.where((blk_ids > cur)[None, :, None, :], NEG_INF, imp)
        _, sel = lax.top_k(imp, n_top)
        k_sel = kb[bi, gi, sel]
        v_sel = vb[bi, gi, sel]
        key_pos = sel[..., None] * SLC_BLOCK + in_blk
        ok = key_pos <= t[None, :, None, None, None]
        s_s = jnp.einsum('bqghd,bqgskd->bqghsk', qch, k_sel, preferred_element_type=F32) * scale
        s_s = jnp.where(ok[:, :, :, None], s_s, NEG_INF).reshape(B, qc, KVH, G, n_top * SLC_BLOCK)
        p_s = jax.nn.softmax(s_s, axis=-1).astype(v_slc.dtype).reshape(B, qc, KVH, G, n_top, SLC_BLOCK)
        o_s = jnp.einsum('bqghsk,bqgskd->bqghd', p_s, v_sel)
        return o_c, o_s

    q_chunks = q.reshape(B, nq, qc, KVH, G, dh).transpose(1, 0, 2, 3, 4, 5)
    o_c, o_s = lax.map(one_chunk, (q_chunks, q0 + qc * jnp.arange(nq)))
    o_c = o_c.transpose(1, 0, 2, 3, 4, 5).reshape(B, S, KVH, G, dh)
    o_s = o_s.transpose(1, 0, 2, 3, 4, 5).reshape(B, S, KVH, G, dh)
    return o_c, o_s


def window_attention(q, k_win, v_win, q0, k0):
    B, S, KVH, G, dh = q.shape
    qb = query_block(S, WIN_QBLOCK)
    nq = S // qb
    span = WINDOW + qb
    pad = ((0, 0), (WINDOW, 0), (0, 0), (0, 0))
    kp = jnp.pad(k_win, pad)
    vp = jnp.pad(v_win, pad)
    scale = HEAD_DIM ** -0.5

    def one_block(args):
        qblk, i = args
        s0 = q0 - k0 + i * qb
        kk = lax.dynamic_slice_in_dim(kp, s0, span, axis=1)
        vv = lax.dynamic_slice_in_dim(vp, s0, span, axis=1)
        t = q0 + i * qb + jnp.arange(qb)
        kpos = k0 - WINDOW + s0 + jnp.arange(span)
        ok = ((kpos[None, :] > t[:, None] - WINDOW) & (kpos[None, :] <= t[:, None])
              & (kpos[None, :] >= k0))
        s = jnp.einsum('bqghd,bkgd->bqghk', qblk, kk, preferred_element_type=F32) * scale
        s = jnp.where(ok[None, :, None, None, :], s, NEG_INF)
        p = jax.nn.softmax(s, axis=-1).astype(vv.dtype)
        return jnp.einsum('bqghk,bkgd->bqghd', p, vv)

    q_blocks = q.reshape(B, nq, qb, KVH, G, dh).transpose(1, 0, 2, 3, 4, 5)
    o = lax.map(one_block, (q_blocks, jnp.arange(nq)))
    return o.transpose(1, 0, 2, 3, 4, 5).reshape(B, S, KVH, G, dh)


def token_mixer(h, q0, past, lp):
    B, S, _ = h.shape
    cuts = [int(c) for c in np.cumsum(in_proj_sizes())[:-1]]
    q_a, k_a, v_a, q_b, kv_b, g_b, g_ma, g_mb = jnp.split(_mm3(h, lp['w_in']), cuts, axis=-1)
    pos = q0 + jnp.arange(S)
    q_a = apply_rope(q_a.reshape(B, S, MOBA_KV_HEADS, MOBA_GROUP, HEAD_DIM), pos)
    k_a = apply_rope(k_a.reshape(B, S, MOBA_KV_HEADS, HEAD_DIM), pos)
    v_a = v_a.reshape(B, S, MOBA_KV_HEADS, HEAD_DIM)
    q_b = apply_rope(q_b.reshape(B, S, NSA_KV_HEADS, NSA_GROUP, HEAD_DIM), pos)
    kv_b = kv_b.reshape(B, S, 3, 2, NSA_KV_HEADS, HEAD_DIM)
    new_moba = jnp.stack([k_a, v_a], axis=2)
    new_cmp = kv_b[:, :, 0]
    new_slc = jnp.stack([apply_rope(kv_b[:, :, 1, 0], pos), kv_b[:, :, 1, 1]], axis=2)
    new_win = jnp.stack([apply_rope(kv_b[:, :, 2, 0], pos), kv_b[:, :, 2, 1]], axis=2)
    if past is None:
        o_a = _moba_prompt(q_a.reshape(B, S, -1), new_moba.reshape(B, S, -1))
        kvc = _compress_prompt(new_cmp.reshape(B, S, -1), lp['cmp_pos'], lp['cmp_w1'], lp['cmp_b1'],
                               lp['cmp_w2'], lp['cmp_b2'])
        o_b = _nsa_prompt(q_b.reshape(B, S, -1), kvc, new_slc.reshape(B, S, -1),
                          new_win.reshape(B, S, -1),
                          jnp.pad(g_b, ((0, 0), (0, 0), (0, LANE - g_b.shape[-1]))))
        merged = (jax.nn.sigmoid(g_ma) * _mm3(o_a, lp['w_pa'])
                  + jax.nn.sigmoid(g_mb) * _mm3(o_b, lp['w_pb']))
        win_state = new_win[:, S - min(WINDOW, S):]
        return _mm3(merged, lp['w_out']), (new_moba, new_cmp, new_slc, win_state)
    else:
        moba_kv = jnp.concatenate([past[0], new_moba], axis=1)
        cmp_kv = jnp.concatenate([past[1], new_cmp], axis=1)
        slc_kv = jnp.concatenate([past[2], new_slc], axis=1)
        win_kv = jnp.concatenate([past[3], new_win], axis=1)
    k0 = q0 + S - win_kv.shape[1]
    win_state = win_kv[:, win_kv.shape[1] - min(WINDOW, win_kv.shape[1]):]

    o_a = moba_attention(q_a, moba_kv[:, :, 0], moba_kv[:, :, 1], q0)
    k_c = compress_rows(cmp_kv[:, :, 0], lp['cmp_pos'][0], lp['cmp_w1'][0], lp['cmp_b1'][0],
                        lp['cmp_w2'][0], lp['cmp_b2'][0])
    v_c = compress_rows(cmp_kv[:, :, 1], lp['cmp_pos'][1], lp['cmp_w1'][1], lp['cmp_b1'][1],
                        lp['cmp_w2'][1], lp['cmp_b2'][1])
    o_c, o_s = nsa_compressed_selected(q_b, k_c, v_c, slc_kv[:, :, 0], slc_kv[:, :, 1], q0)
    o_w = window_attention(q_b, win_kv[:, :, 0], win_kv[:, :, 1], q0, k0)
    gates = jax.nn.sigmoid(g_b.reshape(B, S, 3, NSA_KV_HEADS, NSA_GROUP, 1))
    o_b = (gates[:, :, 0] * o_c + gates[:, :, 1] * o_s + gates[:, :, 2] * o_w).reshape(B, S, NSA_HEADS * HEAD_DIM)
    merged = (jax.nn.sigmoid(g_ma) * _mm3(o_a, lp['w_pa'])
              + jax.nn.sigmoid(g_mb) * _mm3(o_b, lp['w_pb']))
    return _mm3(merged, lp['w_out']), (new_moba, new_cmp, new_slc, win_state)


def routed_experts(x, expert_ids, weights, w_gate, w_up, w_down):
    T, D = x.shape
    A = T * EXPERT_TOPK
    flat_e = expert_ids.reshape(A)
    order = jnp.argsort(flat_e)
    e_sorted = flat_e[order]
    tok_sorted = order // EXPERT_TOPK
    counts = jnp.zeros((N_EXPERTS,), jnp.int32).at[flat_e].add(1)
    padded = (counts + MOE_BLOCK - 1) // MOE_BLOCK * MOE_BLOCK
    pad_end = jnp.cumsum(padded)
    pad_start = pad_end - padded
    start = jnp.cumsum(counts) - counts
    dest = pad_start[e_sorted] + jnp.arange(A) - start[e_sorted]
    n_blk = -(-(A + N_EXPERTS * (MOE_BLOCK - 1)) // MOE_BLOCK)
    row_tok = jnp.zeros((n_blk * MOE_BLOCK,), jnp.int32).at[dest].set(tok_sorted)
    blk_expert = jnp.minimum(jnp.searchsorted(pad_end, jnp.arange(n_blk) * MOE_BLOCK, side='right'),
                             N_EXPERTS - 1)
    xb = x[row_tok].reshape(n_blk, MOE_BLOCK, D)

    def expert_block(args):
        xe, e = args
        return (jax.nn.silu(xe @ w_gate[e]) * (xe @ w_up[e])) @ w_down[e]

    yb = lax.map(expert_block, (xb, blk_expert)).reshape(n_blk * MOE_BLOCK, D)
    contrib = yb[dest] * weights.reshape(A)[order][:, None].astype(x.dtype)
    return jnp.zeros((T, D), x.dtype).at[tok_sorted].add(contrib)


def hierarchical_moe(h, lp):
    B, S, D = h.shape
    x = h.reshape(B * S, D)
    g_prob = jax.nn.softmax((x @ lp['w_rg']).astype(F32) + lp['b_rg'].astype(F32), axis=-1)
    g_w, g_idx = lax.top_k(g_prob, 1)
    e_logit = ((x @ lp['w_re']).astype(F32) + lp['b_re'].astype(F32)).reshape(-1, N_GROUPS, EXPERTS_PER_GROUP)
    e_logit = jnp.take_along_axis(e_logit, g_idx[:, :, None], axis=1)[:, 0]
    e_val, e_idx = lax.top_k(e_logit, EXPERT_TOPK)
    weights = g_w * jax.nn.softmax(e_val, axis=-1)
    expert_ids = g_idx * EXPERTS_PER_GROUP + e_idx
    y = routed_experts(x, expert_ids, weights, lp['w_gate'], lp['w_up'], lp['w_down'])
    return y.reshape(B, S, D)


def decoder_layer(x, c, q0, past, lp):
    ada = _mm(jax.nn.silu(c), lp['w_ada'], tn=2048) + lp['b_ada']
    sh1, sc1, g1, sh2, sc2, g2 = jnp.split(ada[:, None, :], 6, axis=-1)
    h = rms_norm(x, lp['norm_attn']) * (1 + sc1) + sh1
    mix, rows = token_mixer(h, q0, past, lp)
    x = x + g1 * mix
    h = rms_norm(x, lp['norm_ffn']) * (1 + sc2) + sh2
    x = x + g2 * hierarchical_moe(h, lp)
    return x, rows


def _unused_kernel(x_prompt, x_sample, cache_moba_kv, cache_cmp_kv, cache_slc_kv, state_win_kv,
           page_table, c_prompt, c_sample, w_in, w_pa, w_pb, w_out, cmp_pos, cmp_w1, cmp_b1,
           cmp_w2, cmp_b2, norm_attn, norm_ffn, norm_final, w_ada, b_ada, w_rg, b_rg, w_re,
           b_re, w_gate, w_up, w_down):
    past_len = page_table.shape[1] * cache_moba_kv.shape[2]
    y_p, y_s = x_prompt, x_sample
    rows_p, rows_s = [], []
    for l in range(DEPTH):
        lp = {'w_in': w_in[l], 'w_pa': w_pa[l], 'w_pb': w_pb[l], 'w_out': w_out[l],
              'cmp_pos': cmp_pos[l], 'cmp_w1': cmp_w1[l], 'cmp_b1': cmp_b1[l],
              'cmp_w2': cmp_w2[l], 'cmp_b2': cmp_b2[l], 'norm_attn': norm_attn[l],
              'norm_ffn': norm_ffn[l], 'w_ada': w_ada[l], 'b_ada': b_ada[l],
              'w_rg': w_rg[l], 'b_rg': b_rg[l], 'w_re': w_re[l], 'b_re': b_re[l],
              'w_gate': w_gate[l], 'w_up': w_up[l], 'w_down': w_down[l]}
        y_p, r_p = decoder_layer(y_p, c_prompt, 0, None, lp)
        past = (gather_pages(cache_moba_kv[l], page_table),
                gather_pages(cache_cmp_kv[l], page_table),
                gather_pages(cache_slc_kv[l], page_table),
                state_win_kv[l])
        y_s, r_s = decoder_layer(y_s, c_sample, past_len, past, lp)
        rows_p.append(r_p)
        rows_s.append(r_s)
    y_prompt = rms_norm(y_p, norm_final)
    y_sample = rms_norm(y_s, norm_final)
    return (y_prompt, y_sample,
            jnp.stack([r[0] for r in rows_p]), jnp.stack([r[0] for r in rows_s]),
            jnp.stack([r[1] for r in rows_p]), jnp.stack([r[1] for r in rows_s]),
            jnp.stack([r[2] for r in rows_p]), jnp.stack([r[2] for r in rows_s]),
            jnp.stack([r[3] for r in rows_p]), jnp.stack([r[3] for r in rows_s]))


PROMPT_TM_IN = 512
PROMPT_TM_OUT = 256
MOE_TM = 128
SAMPLE_MOE_BLOCK = 8


def _kv_rows(proj, col, b, s):
    lo = col * HEAD_DIM
    return proj[:, lo:lo + KV_COLS].reshape(b, s, 2, NSA_KV_HEADS, HEAD_DIM)


def _to_qrows(x, b, s):
    x = x.reshape(b, s, NSA_KV_HEADS, NSA_GROUP, HEAD_DIM).transpose(0, 2, 3, 1, 4)
    x = jnp.pad(x, ((0, 0), (0, 0), (0, 0), (0, QROWS - s), (0, 0)))
    return x.reshape(b, NSA_KV_HEADS, NQ, HEAD_DIM)


def _from_qrows(o, b, s):
    o = o.reshape(b, NSA_KV_HEADS, NSA_GROUP, QROWS, HEAD_DIM)[:, :, :, :s]
    return o.transpose(0, 3, 1, 2, 4).reshape(b * s, NSA_HEADS * HEAD_DIM)


def _new_rows(proj, col, b, s):
    lo = col * HEAD_DIM
    x = proj[:, lo:lo + NSA_KV_HEADS * HEAD_DIM].reshape(b, s, NSA_KV_HEADS, HEAD_DIM)
    return jnp.pad(x.transpose(0, 2, 1, 3), ((0, 0), (0, 0), (0, LANE - s), (0, 0)))


def _sample_mixer(proj, g_b, pools, win_state, page_table, cmp_w, l, b, s, q0):
    assert MOBA_KV_HEADS == NSA_KV_HEADS and MOBA_GROUP == NSA_GROUP and s <= QROWS
    assert q0 % CMP_STRIDE == 0 and s < CMP_STRIDE and q0 % MOBA_BLOCK == 0
    pool_moba, pool_cmp, pool_slc = pools
    nkh = NSA_KV_HEADS
    q_a = _to_qrows(proj[:, :MOBA_HEADS * HEAD_DIM], b, s)
    q_b = _to_qrows(proj[:, COL_QB * HEAD_DIM:COL_CMP * HEAD_DIM], b, s)
    o_a = _moba_sample(q_a, _new_rows(proj, COL_KA, b, s), _new_rows(proj, COL_VA, b, s),
                       pool_moba, page_table, l, s)
    kvc = _compress_sample(pool_cmp, page_table, l, *cmp_w)
    o_c, sel = _nsa_cmp_sample(q_b, kvc, q0, s)
    g3 = g_b[:, :GB_COLS].reshape(b, s, 3, nkh, NSA_GROUP).transpose(0, 3, 4, 1, 2)
    g3 = jnp.pad(g3, ((0, 0), (0, 0), (0, 0), (0, QROWS - s), (0, LANE - 3)))
    o_b = _nsa_slc_sample(q_b, sel, o_c, g3.reshape(b, nkh, NQ, LANE),
                          _new_rows(proj, COL_SLC, b, s), _new_rows(proj, COL_SLC + nkh, b, s),
                          _new_rows(proj, COL_WIN, b, s), _new_rows(proj, COL_WIN + nkh, b, s),
                          win_state, pool_slc, page_table, l, s)
    return _from_qrows(o_a, b, s), _from_qrows(o_b, b, s)


def _moe_block(x, h, logits, g2, norm_final, w_gate, w_up, w_down, l, bm, tm, rows_per_mod,
               final_norm):
    blk_expert, row_tok, n_used, row_w, d0, d1 = _route(logits, bm)
    yb = _moe_ffn(h, blk_expert, row_tok, n_used, row_w, w_gate, w_up, w_down, l, bm)
    return _moe_combine(x, g2, norm_final, yb, d0, d1, tm, rows_per_mod, final_norm)


def kernel(x_prompt, x_sample, cache_moba_kv, cache_cmp_kv, cache_slc_kv, state_win_kv,
           page_table, c_prompt, c_sample, w_in, w_pa, w_pb, w_out, cmp_pos, cmp_w1, cmp_b1,
           cmp_w2, cmp_b2, norm_attn, norm_ffn, norm_final, w_ada, b_ada, w_rg, b_rg, w_re,
           b_re, w_gate, w_up, w_down):
    bp, sp, d = x_prompt.shape
    bs, ss, _ = x_sample.shape
    tp, ts = bp * sp, bs * ss
    past_len = page_table.shape[1] * cache_moba_kv.shape[2]
    xp = x_prompt.reshape(tp, d)
    xs = x_sample.reshape(ts, d)
    cos_p, sin_p = _rope_tables(jnp.tile(jnp.arange(sp), bp))
    cos_s, sin_s = _rope_tables(jnp.tile(past_len + jnp.arange(ss), bs))
    n_c = _round_up(bp + bs, 8)
    c_all = jnp.pad(jnp.concatenate([c_prompt, c_sample], axis=0), ((0, n_c - bp - bs), (0, 0)))
    n_pool = cache_moba_kv.shape[1]
    pools = tuple(c.reshape(DEPTH, n_pool, PAGE_ROWS, HEAD_DIM)
                  for c in (cache_moba_kv, cache_cmp_kv, cache_slc_kv))
    n_win = state_win_kv.shape[2]
    win_view = state_win_kv.reshape(DEPTH, bs, n_win * 2 * NSA_KV_HEADS, HEAD_DIM)
    rows_p, rows_s = [], []
    for l in range(DEPTH):
        last = l == DEPTH - 1
        ada = _ada(c_all, w_ada, b_ada, l).reshape(n_c, 6, d)
        mod_p = [ada[:bp, k].reshape(bp, 1, d) for k in range(6)]
        mod_s = [jnp.repeat(ada[bp:bp + bs, k], ss, axis=0).reshape(1, ts, d) for k in range(6)]
        w_main = jnp.concatenate([w_in[l][:, :QKV_COLS], w_in[l][:, QKV_COLS + GB_COLS:]],
                                 axis=1).astype(BF16)
        w_gb = jnp.pad(w_in[l][:, QKV_COLS:QKV_COLS + GB_COLS],
                       ((0, 0), (0, LANE - GB_COLS))).astype(BF16)
        w_pa_b, w_pb_b, w_out_b = w_pa[l].astype(BF16), w_pb[l].astype(BF16), w_out[l].astype(BF16)
        w_r = jnp.pad(jnp.concatenate([w_rg[l], w_re[l]], axis=1),
                      ((0, 0), (0, LANE - N_GROUPS - N_EXPERTS)))
        b_r = jnp.pad(jnp.concatenate([b_rg[l], b_re[l]]),
                      (0, LANE - N_GROUPS - N_EXPERTS)).reshape(1, LANE)

        sh1, sc1, g1, sh2, sc2, g2 = mod_p
        proj, g_b = _in_proj(xp, norm_attn[l], sc1, sh1, cos_p, sin_p, w_main, w_gb,
                             PROMPT_TM_IN, sp)
        proj3 = proj.reshape(bp, sp, PROJ_COLS)
        o_a = _moba_prompt_t(proj3)
        kvc = _compress_prompt(proj3, cmp_pos[l], cmp_w1[l], cmp_b1[l], cmp_w2[l], cmp_b2[l])
        o_b = _nsa_prompt_t(proj3, kvc, g_b.reshape(bp, sp, LANE))
        xp, h2, logits = _out_proj(o_a.reshape(tp, -1), o_b.reshape(tp, -1), proj, xp, g1,
                                   norm_ffn[l], sc2, sh2, w_pa_b, w_pb_b, w_out_b, w_r, b_r,
                                   PROMPT_TM_OUT, sp)
        xp = _moe_block(xp, h2, logits, g2, norm_final, w_gate, w_up, w_down, l,
                        MOE_BLOCK, MOE_TM, sp, last)
        new_win = _kv_rows(proj, COL_WIN, bp, sp)
        rows_p.append((_kv_rows(proj, COL_KA, bp, sp), _kv_rows(proj, COL_CMP, bp, sp),
                       _kv_rows(proj, COL_SLC, bp, sp), new_win[:, sp - min(WINDOW, sp):]))

        sh1, sc1, g1, sh2, sc2, g2 = mod_s
        proj, g_b = _in_proj(xs, norm_attn[l], sc1, sh1, cos_s, sin_s, w_main, w_gb, ts, ss)
        cmp_w = (cmp_pos[l], cmp_w1[l], cmp_b1[l], cmp_w2[l], cmp_b2[l])
        o_a, o_b = _sample_mixer(proj, g_b, pools, win_view, page_table, cmp_w, l, bs, ss, past_len)
        xs, h2, logits = _out_proj(o_a, o_b, proj, xs, g1, norm_ffn[l], sc2, sh2,
                                   w_pa_b, w_pb_b, w_out_b, w_r, b_r, ts, ss)
        xs = _moe_block(xs, h2, logits, g2, norm_final, w_gate, w_up, w_down, l,
                        SAMPLE_MOE_BLOCK, ts, ss, last)
        win_rows = jnp.concatenate([state_win_kv[l], _kv_rows(proj, COL_WIN, bs, ss)], axis=1)
        rows_s.append((_kv_rows(proj, COL_KA, bs, ss), _kv_rows(proj, COL_CMP, bs, ss),
                       _kv_rows(proj, COL_SLC, bs, ss),
                       win_rows[:, win_rows.shape[1] - min(WINDOW, win_rows.shape[1]):]))

    return (xp.reshape(bp, sp, d), xs.reshape(bs, ss, d),
            jnp.stack([r[0] for r in rows_p]), jnp.stack([r[0] for r in rows_s]),
            jnp.stack([r[1] for r in rows_p]), jnp.stack([r[1] for r in rows_s]),
            jnp.stack([r[2] for r in rows_p]), jnp.stack([r[2] for r in rows_s]),
            jnp.stack([r[3] for r in rows_p]), jnp.stack([r[3] for r in rows_s]))
```

```python
import functools

import jax
import jax.numpy as jnp
import numpy as np
from jax import lax
from jax.experimental import pallas as pl
from jax.experimental.pallas import tpu as pltpu

D_MODEL = 2048
DEPTH = 2
HEAD_DIM = 128
ROT_DIM = HEAD_DIM // 4
ROPE_THETA = 500000.0
NORM_EPS = 1e-6
MOBA_HEADS = 8
MOBA_KV_HEADS = 2
MOBA_GROUP = MOBA_HEADS // MOBA_KV_HEADS
MOBA_BLOCK = 256
MOBA_TOPK = 3
MOBA_QCHUNK = 16
NSA_HEADS = 8
NSA_KV_HEADS = 2
NSA_GROUP = NSA_HEADS // NSA_KV_HEADS
CMP_LEN = 32
CMP_STRIDE = 16
CMP_HIDDEN = 128
SLC_BLOCK = 64
SLC_TOPN = 16
WINDOW = 512
NSA_QCHUNK = 16
WIN_QBLOCK = 128
N_GROUPS = 4
EXPERTS_PER_GROUP = 8
N_EXPERTS = N_GROUPS * EXPERTS_PER_GROUP
EXPERT_TOPK = 2
D_EXPERT = 512
MOE_BLOCK = 128

NEG_INF = -1e30
BIG = 1e30
F32 = jnp.float32
BF16 = jnp.bfloat16

LANE = 128
VMEM_LIMIT_BYTES = 48 * 1024 * 1024


def _round_up(n, m):
    return (n + m - 1) // m * m


QKV_COLS = 4096
GB_COLS = 3 * NSA_HEADS
GATE_COLS = 2 * D_MODEL
PROJ_COLS = QKV_COLS + GATE_COLS
COL_QA, COL_KA, COL_VA, COL_QB = 0, 8, 10, 12
COL_CMP, COL_SLC, COL_WIN = 20, 24, 28


def _mod_spec(mod, tm, rows_per_mod):
    d = mod.shape[-1]
    if mod.shape[1] == 1:
        return pl.BlockSpec((1, 1, d), lambda i, *_: (i * tm // rows_per_mod, 0, 0))
    return pl.BlockSpec((1, tm, d), lambda i, *_: (i, 0, 0))


def _ada_kernel(c_ref, w_ref, b_ref, o_ref):
    c = jax.nn.silu(c_ref[...]).astype(BF16)
    o_ref[...] = jnp.dot(c, w_ref[0].astype(BF16), preferred_element_type=F32) + b_ref[0]


def _ada(c, w_ada, b_ada, l, tn=1024):
    m, k = c.shape
    depth, _, n = w_ada.shape
    return pl.pallas_call(
        _ada_kernel,
        grid=(n // tn,),
        in_specs=[pl.BlockSpec((m, k), lambda j: (0, 0)),
                  pl.BlockSpec((1, k, tn), lambda j: (l, 0, j)),
                  pl.BlockSpec((1, 1, tn), lambda j: (l, 0, j))],
        out_specs=pl.BlockSpec((m, tn), lambda j: (0, j)),
        out_shape=jax.ShapeDtypeStruct((m, n), F32),
        compiler_params=pltpu.CompilerParams(dimension_semantics=("arbitrary",),
                                             vmem_limit_bytes=VMEM_LIMIT_BYTES),
        name="ada_ln",
    )(c, w_ada, b_ada.reshape(depth, 1, n))


def _rope_tables(pos):
    half = ROT_DIM // 2
    inv_freq = ROPE_THETA ** (-jnp.arange(half, dtype=F32) / half)
    ang = pos.astype(F32)[:, None] * inv_freq[None, :]
    cos, sin = jnp.cos(ang), jnp.sin(ang)
    n = pos.shape[0]
    cos_t = jnp.concatenate([cos, cos, jnp.ones((n, HEAD_DIM - ROT_DIM), F32)], axis=1)
    sin_t = jnp.concatenate([-sin, sin, jnp.zeros((n, HEAD_DIM - ROT_DIM), F32)], axis=1)
    return cos_t, sin_t


def _rope_column_mask():
    m = np.zeros((1, PROJ_COLS), np.float32)
    for lo, hi in ((0, 1280), (1536, 2560), (3072, 3328), (3584, 3840)):
        m[0, lo:hi] = 1.0
    return jnp.asarray(m)


def _in_proj_kernel(x_ref, g_ref, sc_ref, sh_ref, cos_ref, sin_ref, rmask_ref, w_ref, wgb_ref,
                    o_ref, ogb_ref, hb_scr, *, tn):
    j = pl.program_id(1)

    @pl.when(j == 0)
    def _():
        x = x_ref[...]
        y = x * lax.rsqrt(jnp.mean(x * x, axis=-1, keepdims=True) + NORM_EPS) * g_ref[...]
        h = (y * (1.0 + sc_ref[0]) + sh_ref[0]).astype(BF16)
        hb_scr[...] = h
        ogb_ref[...] = jnp.dot(h, wgb_ref[...], preferred_element_type=F32)

    acc = jnp.dot(hb_scr[...], w_ref[...], preferred_element_type=F32)

    @pl.when(j < QKV_COLS // tn)
    def _():
        lane = lax.broadcasted_iota(jnp.int32, (acc.shape[0], HEAD_DIM), 1)
        cos_t, sin_t = cos_ref[...], sin_ref[...]
        for hh in range(tn // HEAD_DIM):
            blk = acc[:, hh * HEAD_DIM:(hh + 1) * HEAD_DIM]
            on = rmask_ref[:, hh * HEAD_DIM:(hh + 1) * HEAD_DIM] > 0.5
            partner = jnp.where(lane < ROT_DIM // 2,
                                pltpu.roll(blk, HEAD_DIM - ROT_DIM // 2, axis=1),
                                pltpu.roll(blk, ROT_DIM // 2, axis=1))
            o_ref[:, hh * HEAD_DIM:(hh + 1) * HEAD_DIM] = (
                blk * jnp.where(on, cos_t, 1.0) + partner * jnp.where(on, sin_t, 0.0))

    @pl.when(j >= QKV_COLS // tn)
    def _():
        o_ref[...] = jax.nn.sigmoid(acc)


def _in_proj(x, g, sc, sh, cos_t, sin_t, w_main, w_gb, tm, rows_per_mod, tn=1024):
    t, d = x.shape
    rmask = _rope_column_mask()
    return pl.pallas_call(
        functools.partial(_in_proj_kernel, tn=tn),
        grid=(t // tm, PROJ_COLS // tn),
        in_specs=[pl.BlockSpec((tm, d), lambda i, j: (i, 0)),
                  pl.BlockSpec((1, d), lambda i, j: (0, 0)),
                  _mod_spec(sc, tm, rows_per_mod), _mod_spec(sh, tm, rows_per_mod),
                  pl.BlockSpec((tm, HEAD_DIM), lambda i, j: (i, 0)),
                  pl.BlockSpec((tm, HEAD_DIM), lambda i, j: (i, 0)),
                  pl.BlockSpec((1, tn), lambda i, j: (0, j)),
                  pl.BlockSpec((d, tn), lambda i, j: (0, j)),
                  pl.BlockSpec((d, LANE), lambda i, j: (0, 0))],
        out_specs=[pl.BlockSpec((tm, tn), lambda i, j: (i, j)),
                   pl.BlockSpec((tm, LANE), lambda i, j: (i, 0))],
        out_shape=[jax.ShapeDtypeStruct((t, PROJ_COLS), F32),
                   jax.ShapeDtypeStruct((t, LANE), F32)],
        scratch_shapes=[pltpu.VMEM((tm, d), BF16)],
        compiler_params=pltpu.CompilerParams(dimension_semantics=("arbitrary", "arbitrary"),
                                             vmem_limit_bytes=VMEM_LIMIT_BYTES),
        name="in_proj",
    )(x, g.reshape(1, d), sc, sh, cos_t, sin_t, rmask, w_main, w_gb)


def _out_proj_kernel(oa_ref, ob_ref, gma_ref, gmb_ref, x_ref, g1_ref, gf_ref, sc_ref, sh_ref,
                     wpa_ref, wpb_ref, wout_ref, wr_ref, br_ref, xo_ref, h_ref, lg_ref):
    pa = jnp.dot(oa_ref[...].astype(BF16), wpa_ref[...], preferred_element_type=F32)
    pb = jnp.dot(ob_ref[...].astype(BF16), wpb_ref[...], preferred_element_type=F32)
    merged = (gma_ref[...] * pa + gmb_ref[...] * pb).astype(BF16)
    mix = jnp.dot(merged, wout_ref[...], preferred_element_type=F32)
    x = x_ref[...] + g1_ref[0] * mix
    xo_ref[...] = x
    y = x * lax.rsqrt(jnp.mean(x * x, axis=-1, keepdims=True) + NORM_EPS) * gf_ref[...]
    h = y * (1.0 + sc_ref[0]) + sh_ref[0]
    h_ref[...] = h
    lg_ref[...] = jnp.dot(h, wr_ref[...], precision=lax.Precision.HIGHEST,
                          preferred_element_type=F32) + br_ref[...]


def _out_proj(o_a, o_b, proj, x, g1, gf, sc, sh, w_pa, w_pb, w_out, w_r, b_r, tm, rows_per_mod):
    t, d = x.shape
    da = o_a.shape[1]
    const = lambda shape: pl.BlockSpec(shape, lambda i: (0,) * len(shape),
                                       pipeline_mode=pl.Buffered(1))
    gate_blk = QKV_COLS // d
    return pl.pallas_call(
        _out_proj_kernel,
        grid=(t // tm,),
        in_specs=[pl.BlockSpec((tm, da), lambda i: (i, 0)),
                  pl.BlockSpec((tm, da), lambda i: (i, 0)),
                  pl.BlockSpec((tm, d), lambda i: (i, gate_blk)),
                  pl.BlockSpec((tm, d), lambda i: (i, gate_blk + 1)),
                  pl.BlockSpec((tm, d), lambda i: (i, 0)),
                  _mod_spec(g1, tm, rows_per_mod),
                  const((1, d)),
                  _mod_spec(sc, tm, rows_per_mod), _mod_spec(sh, tm, rows_per_mod),
                  const((da, d)), const((da, d)), const((d, d)), const((d, LANE)), const((1, LANE))],
        out_specs=[pl.BlockSpec((tm, d), lambda i: (i, 0)),
                   pl.BlockSpec((tm, d), lambda i: (i, 0)),
                   pl.BlockSpec((tm, LANE), lambda i: (i, 0))],
        out_shape=[jax.ShapeDtypeStruct((t, d), F32), jax.ShapeDtypeStruct((t, d), F32),
                   jax.ShapeDtypeStruct((t, LANE), F32)],
        compiler_params=pltpu.CompilerParams(dimension_semantics=("arbitrary",),
                                             vmem_limit_bytes=VMEM_LIMIT_BYTES),
        name="out_proj",
    )(o_a, o_b, proj, proj, x, g1, gf.reshape(1, d), sc, sh, w_pa, w_pb, w_out, w_r, b_r)


def _moe_ffn_kernel(blk_e_ref, row_tok_ref, nused_ref, x_hbm, roww_ref, wg_ref, wu_ref, wd_ref,
                    o_ref, xbuf, sem, wg_bf, wu_bf, wd_bf, *, bm):
    i = pl.program_id(0)
    n_used = nused_ref[0]

    def issue(blk, slot):
        def one(r, c):
            tok = row_tok_ref[blk * bm + r]
            pltpu.make_async_copy(x_hbm.at[pl.ds(tok, 1)], xbuf.at[slot, pl.ds(r, 1)],
                                  sem.at[slot]).start()
            return c
        lax.fori_loop(0, bm, one, 0, unroll=8)

    def wait(slot):
        pltpu.make_async_copy(x_hbm.at[pl.ds(0, bm)], xbuf.at[slot], sem.at[slot]).wait()

    slot = i % 2

    @pl.when((i == 0) & (n_used > 0))
    def _():
        issue(0, 0)

    @pl.when(i < n_used)
    def _():
        wait(slot)

        @pl.when(i + 1 < n_used)
        def _():
            issue(i + 1, 1 - slot)

        prev = blk_e_ref[jnp.maximum(i - 1, 0)]

        @pl.when((i == 0) | (blk_e_ref[i] != prev))
        def _():
            wg_bf[...] = wg_ref[0, 0].astype(BF16)
            wu_bf[...] = wu_ref[0, 0].astype(BF16)
            wd_bf[...] = wd_ref[0, 0].astype(BF16)

        x = xbuf[slot].astype(BF16)
        hg = jnp.dot(x, wg_bf[...], preferred_element_type=F32)
        hu = jnp.dot(x, wu_bf[...], preferred_element_type=F32)
        act = (jax.nn.silu(hg) * hu).astype(BF16)
        y = jnp.dot(act, wd_bf[...], preferred_element_type=F32)
        o_ref[...] = y * roww_ref[...]

    @pl.when(i >= n_used)
    def _():
        o_ref[...] = jnp.zeros_like(o_ref)


def _moe_ffn(x, blk_expert, row_tok, n_used, row_w, w_gate, w_up, w_down, l, bm):
    n_rows = row_tok.shape[0]
    n_blk = n_rows // bm
    d = x.shape[1]
    de = w_gate.shape[3]
    grid_spec = pltpu.PrefetchScalarGridSpec(
        num_scalar_prefetch=3,
        grid=(n_blk,),
        in_specs=[pl.BlockSpec(memory_space=pl.ANY),
                  pl.BlockSpec((bm, 1), lambda i, be, rt, nu: (i, 0)),
                  pl.BlockSpec((1, 1, d, de), lambda i, be, rt, nu: (l, be[i], 0, 0)),
                  pl.BlockSpec((1, 1, d, de), lambda i, be, rt, nu: (l, be[i], 0, 0)),
                  pl.BlockSpec((1, 1, de, d), lambda i, be, rt, nu: (l, be[i], 0, 0))],
        out_specs=pl.BlockSpec((bm, d), lambda i, be, rt, nu: (i, 0)),
        scratch_shapes=[pltpu.VMEM((2, bm, d), F32),
                        pltpu.SemaphoreType.DMA((2,)),
                        pltpu.VMEM((d, de), BF16), pltpu.VMEM((d, de), BF16),
                        pltpu.VMEM((de, d), BF16)])
    return pl.pallas_call(
        functools.partial(_moe_ffn_kernel, bm=bm),
        grid_spec=grid_spec,
        out_shape=jax.ShapeDtypeStruct((n_rows, d), F32),
        compiler_params=pltpu.CompilerParams(dimension_semantics=("arbitrary",),
                                             vmem_limit_bytes=VMEM_LIMIT_BYTES),
        name="moe_ffn",
    )(blk_expert, row_tok, n_used, x, row_w, w_gate, w_up, w_down)


def _moe_combine_kernel(d0_ref, d1_ref, x_ref, g2_ref, nf_ref, yb_hbm, o_ref, ybuf, sem,
                        *, tm, final_norm):
    i = pl.program_id(0)
    n = pl.num_programs(0)

    def issue(blk, slot):
        def one(r, c):
            t = blk * tm + r
            pltpu.make_async_copy(yb_hbm.at[pl.ds(d0_ref[t], 1)], ybuf.at[slot, 0, pl.ds(r, 1)],
                                  sem.at[slot]).start()
            pltpu.make_async_copy(yb_hbm.at[pl.ds(d1_ref[t], 1)], ybuf.at[slot, 1, pl.ds(r, 1)],
                                  sem.at[slot]).start()
            return c
        lax.fori_loop(0, tm, one, 0, unroll=8)

    def wait(slot):
        pltpu.make_async_copy(yb_hbm.at[pl.ds(0, tm)], ybuf.at[slot, 0], sem.at[slot]).wait()
        pltpu.make_async_copy(yb_hbm.at[pl.ds(0, tm)], ybuf.at[slot, 1], sem.at[slot]).wait()

    slot = i % 2

    @pl.when(i == 0)
    def _():
        issue(0, 0)

    wait(slot)

    @pl.when(i + 1 < n)
    def _():
        issue(i + 1, 1 - slot)

    y = x_ref[...] + g2_ref[0] * (ybuf[slot, 0] + ybuf[slot, 1])
    if final_norm:
        y = y * lax.rsqrt(jnp.mean(y * y, axis=-1, keepdims=True) + NORM_EPS) * nf_ref[...]
    o_ref[...] = y


def _moe_combine(x, g2, norm_final, yb, d0, d1, tm, rows_per_mod, final_norm):
    t, d = x.shape
    grid_spec = pltpu.PrefetchScalarGridSpec(
        num_scalar_prefetch=2,
        grid=(t // tm,),
        in_specs=[pl.BlockSpec((tm, d), lambda i, a, b: (i, 0)),
                  _mod_spec(g2, tm, rows_per_mod),
                  pl.BlockSpec((1, d), lambda i, a, b: (0, 0)),
                  pl.BlockSpec(memory_space=pl.ANY)],
        out_specs=pl.BlockSpec((tm, d), lambda i, a, b: (i, 0)),
        scratch_shapes=[pltpu.VMEM((2, 2, tm, d), F32), pltpu.SemaphoreType.DMA((2,))])
    return pl.pallas_call(
        functools.partial(_moe_combine_kernel, tm=tm, final_norm=final_norm),
        grid_spec=grid_spec,
        out_shape=jax.ShapeDtypeStruct((t, d), F32),
        compiler_params=pltpu.CompilerParams(dimension_semantics=("arbitrary",),
                                             vmem_limit_bytes=VMEM_LIMIT_BYTES),
        name="moe_combine",
    )(d0, d1, x, g2, norm_final.reshape(1, d), yb)


def _route(logits, bm):
    t = logits.shape[0]
    a = t * EXPERT_TOPK
    g_prob = jax.nn.softmax(logits[:, :N_GROUPS], axis=-1)
    g_idx = jnp.argmax(g_prob, axis=-1, keepdims=True).astype(jnp.int32)
    g_w = jnp.max(g_prob, axis=-1, keepdims=True)
    e_logit = logits[:, N_GROUPS:N_GROUPS + N_EXPERTS].reshape(t, N_GROUPS, EXPERTS_PER_GROUP)
    e_logit = jnp.take_along_axis(e_logit, g_idx[:, :, None], axis=1)[:, 0]
    e_ids = jnp.arange(EXPERTS_PER_GROUP, dtype=jnp.int32)[None, :]
    i1 = jnp.argmax(e_logit, axis=-1, keepdims=True).astype(jnp.int32)
    rest = jnp.where(e_ids == i1, -jnp.inf, e_logit)
    i2 = jnp.argmax(rest, axis=-1, keepdims=True).astype(jnp.int32)
    e_val = jnp.concatenate([jnp.max(e_logit, axis=-1, keepdims=True),
                             jnp.max(rest, axis=-1, keepdims=True)], axis=-1)
    e_idx = jnp.concatenate([i1, i2], axis=-1)
    weights = (g_w * jax.nn.softmax(e_val, axis=-1)).reshape(a)
    flat_e = (g_idx * EXPERTS_PER_GROUP + e_idx).reshape(a)
    order = jnp.argsort(flat_e).astype(jnp.int32)
    e_sorted = flat_e[order]
    expert_ids = jnp.arange(N_EXPERTS, dtype=jnp.int32)
    counts = jnp.sum((flat_e[:, None] == expert_ids[None, :]).astype(jnp.int32), axis=0)
    padded = (counts + bm - 1) // bm * bm
    pad_end = jnp.cumsum(padded)
    pad_start = pad_end - padded
    start = jnp.cumsum(counts) - counts
    dest_sorted = pad_start[e_sorted] + jnp.arange(a, dtype=jnp.int32) - start[e_sorted]
    n_blk = -(-(a + N_EXPERTS * (bm - 1)) // bm)
    n_rows = n_blk * bm
    row_asg = jnp.full((n_rows,), -1, jnp.int32).at[dest_sorted].set(order)
    row_valid = row_asg >= 0
    row_asg = jnp.maximum(row_asg, 0)
    row_tok = row_asg // EXPERT_TOPK
    row_w = jnp.where(row_valid, weights[row_asg], 0.0).reshape(n_rows, 1)
    blk_start = jnp.arange(n_blk, dtype=jnp.int32) * bm
    blk_expert = jnp.minimum(jnp.sum((pad_end[None, :] <= blk_start[:, None]).astype(jnp.int32), axis=1),
                             N_EXPERTS - 1).astype(jnp.int32)
    n_used = (pad_end[-1] // bm).astype(jnp.int32).reshape(1)
    dest = jnp.zeros((a,), jnp.int32).at[order].set(dest_sorted).reshape(t, EXPERT_TOPK)
    return blk_expert, row_tok, n_used, row_w, dest[:, 0], dest[:, 1]


ATT_SCALE = HEAD_DIM ** -0.5
_NT = (((1,), (1,)), ((), ()))


KV_COLS = 2 * NSA_KV_HEADS * HEAD_DIM


def _compress_kernel(x0_ref, x1_ref, x2_ref, x3_ref, pos_ref, w1_ref, b1_ref, w2_ref, b2_ref,
                     o_ref, *, nch):
    x_refs = (x0_ref, x1_ref, x2_ref, x3_ref)
    for kv in range(2):
        for h in range(NSA_KV_HEADS):
            c = kv * NSA_KV_HEADS + h
            first = jnp.zeros((nch, CMP_HIDDEN), F32)
            second = jnp.zeros((nch, CMP_HIDDEN), F32)
            for r in range(CMP_STRIDE):
                xr = x_refs[c][0, pl.ds(r, nch, stride=CMP_STRIDE), :]
                r2 = CMP_STRIDE + r
                first += jnp.dot((xr + pos_ref[kv, r:r + 1, :]).astype(BF16),
                                 w1_ref[kv, r * HEAD_DIM:(r + 1) * HEAD_DIM, :].astype(BF16),
                                 preferred_element_type=F32)
                second += jnp.dot((xr + pos_ref[kv, r2:r2 + 1, :]).astype(BF16),
                                  w1_ref[kv, r2 * HEAD_DIM:(r2 + 1) * HEAD_DIM, :].astype(BF16),
                                  preferred_element_type=F32)
            pre = first + pltpu.roll(second, nch - 1, axis=0) + b1_ref[kv:kv + 1, :]
            hdn = jax.nn.gelu(pre)
            o_ref[0, c] = (jnp.dot(hdn.astype(BF16), w2_ref[kv].astype(BF16),
                                   preferred_element_type=F32) + b2_ref[kv:kv + 1, :])


def _compress_prompt(proj, pos, w1, b1, w2, b2):
    b, s, _ = proj.shape
    nch = s // CMP_STRIDE
    full = lambda shape: pl.BlockSpec(shape, lambda bi: (0,) * len(shape))
    return pl.pallas_call(
        functools.partial(_compress_kernel, nch=nch),
        grid=(b,),
        in_specs=[pl.BlockSpec((1, s, HEAD_DIM), functools.partial(lambda bi, c: (bi, 0, COL_CMP + c), c=c))
                  for c in range(2 * NSA_KV_HEADS)] + [
                  full(pos.shape), full(w1.shape), full(b1.shape), full(w2.shape), full(b2.shape)],
        out_specs=pl.BlockSpec((1, 2 * NSA_KV_HEADS, nch, HEAD_DIM), lambda bi: (bi, 0, 0, 0)),
        out_shape=jax.ShapeDtypeStruct((b, 2 * NSA_KV_HEADS, nch, HEAD_DIM), F32),
        compiler_params=pltpu.CompilerParams(dimension_semantics=("arbitrary",),
                                             vmem_limit_bytes=VMEM_LIMIT_BYTES),
        name="nsa_compress",
    )(proj, proj, proj, proj, pos, w1, b1, w2, b2)


NSA_TQ = 256
NSA_KB = 256


def _flash_update_t(qb, kb, vt, mask, carry):
    m, l, acc = carry
    s = lax.dot_general(kb, qb, _NT, preferred_element_type=F32) * ATT_SCALE
    s = jnp.where(mask, s, NEG_INF)
    m_new = jnp.maximum(m, jnp.max(s, axis=0, keepdims=True))
    alpha = jnp.exp(m - m_new)
    p = jnp.exp(s - m_new)
    l = alpha * l + jnp.sum(p, axis=0, keepdims=True)
    acc = alpha * acc + jnp.dot(vt, p.astype(BF16), preferred_element_type=F32)
    return m_new, l, acc


def _flash_init_t(tq):
    return (jnp.full((1, tq), -jnp.inf, F32), jnp.zeros((1, tq), F32),
            jnp.zeros((HEAD_DIM, tq), F32))


def _stage_kv(k_ref, v_ref, kb_scr, vt_scr, nblk, blk):
    for j in range(nblk):
        kb_scr[j] = k_ref[0, j * blk:(j + 1) * blk, :].astype(BF16)
        vt_scr[j] = v_ref[0, j * blk:(j + 1) * blk, :].T.astype(BF16)


def _moba_prompt_t_kernel(q_ref, k_ref, v_ref, o_ref, kmean_scr, sel_scr, kb_scr, vt_scr, *, nb):
    i = pl.program_id(2)
    tq = MOBA_BLOCK
    nbp = kmean_scr.shape[0]

    @pl.when(i == 0)
    def _():
        kmean_scr[...] = jnp.zeros_like(kmean_scr)
        for j in range(nb):
            kmean_scr[j:j + 1, :] = jnp.mean(k_ref[0, j * tq:(j + 1) * tq, :], axis=0, keepdims=True)
        _stage_kv(k_ref, v_ref, kb_scr, vt_scr, nb, tq)

    wq = MOBA_GROUP * tq
    blk_row = lax.broadcasted_iota(jnp.int32, (nbp, tq), 0)
    qbs = []
    for g in range(MOBA_GROUP):
        qg = q_ref[0, :, g * HEAD_DIM:(g + 1) * HEAD_DIM]
        gate = lax.dot_general(kmean_scr[...], qg, _NT, precision=lax.Precision.HIGHEST,
                               preferred_element_type=F32)
        gate = jnp.where(blk_row < i, gate, NEG_INF)
        cnt = jnp.zeros((nbp, tq), F32)
        for j in range(nb):
            gj = gate[j:j + 1, :]
            cnt = cnt + jnp.where((gj > gate) | ((gj == gate) & (j < blk_row)), 1.0, 0.0)
        sel_scr[:, g * tq:(g + 1) * tq] = jnp.where((cnt < MOBA_TOPK) & (blk_row < i), 1.0, 0.0)
        qbs.append(qg.astype(BF16))
    qb = jnp.concatenate(qbs, axis=0)

    key_row = lax.broadcasted_iota(jnp.int32, (tq, wq), 0)
    q_idx = lax.broadcasted_iota(jnp.int32, (tq, wq), 1) & (tq - 1)
    carry = _flash_update_t(qb, kb_scr[i], vt_scr[i], key_row <= q_idx, _flash_init_t(wq))

    def body(j, carry):
        chosen = jnp.broadcast_to(sel_scr[pl.ds(j, 1), :], (tq, wq)) > 0.5
        return _flash_update_t(qb, kb_scr[j], vt_scr[j], chosen, carry)

    _, l, acc = lax.fori_loop(0, i, body, carry)
    o_t = acc / l
    for g in range(MOBA_GROUP):
        o_ref[0, :, g * HEAD_DIM:(g + 1) * HEAD_DIM] = o_t[:, g * tq:(g + 1) * tq].T


def _moba_prompt_t(proj):
    b, s, _ = proj.shape
    nb = s // MOBA_BLOCK
    assert s % MOBA_BLOCK == 0
    nbp = _round_up(nb, 8)
    gw = MOBA_GROUP * HEAD_DIM
    return pl.pallas_call(
        functools.partial(_moba_prompt_t_kernel, nb=nb),
        grid=(b, MOBA_KV_HEADS, nb),
        in_specs=[pl.BlockSpec((1, MOBA_BLOCK, gw), lambda bi, h, i: (bi, i, COL_QA // MOBA_GROUP + h)),
                  pl.BlockSpec((1, s, HEAD_DIM), lambda bi, h, i: (bi, 0, COL_KA + h)),
                  pl.BlockSpec((1, s, HEAD_DIM), lambda bi, h, i: (bi, 0, COL_VA + h))],
        out_specs=pl.BlockSpec((1, MOBA_BLOCK, gw), lambda bi, h, i: (bi, i, h)),
        out_shape=jax.ShapeDtypeStruct((b, s, MOBA_HEADS * HEAD_DIM), F32),
        scratch_shapes=[pltpu.VMEM((nbp, HEAD_DIM), F32),
                        pltpu.VMEM((nbp, MOBA_GROUP * MOBA_BLOCK), F32),
                        pltpu.VMEM((nb, MOBA_BLOCK, HEAD_DIM), BF16),
                        pltpu.VMEM((nb, HEAD_DIM, MOBA_BLOCK), BF16)],
        compiler_params=pltpu.CompilerParams(
            dimension_semantics=("arbitrary", "arbitrary", "arbitrary"),
            vmem_limit_bytes=VMEM_LIMIT_BYTES),
        name="moba_prompt",
    )(proj, proj, proj)


def _nsa_prompt_t_kernel(q_ref, kc_ref, vc_ref, ks_ref, vs_ref, kw_ref, vw_ref, gb_ref, o_ref,
                         ksb_scr, vst_scr, kwb_scr, vwt_scr, kcb_scr, vct_scr, gt_scr, sel_scr,
                         o_scr, *, s_len):
    kvh = pl.program_id(1)
    i = pl.program_id(2)
    tq, kb = NSA_TQ, NSA_KB
    n_cmp = s_len // CMP_STRIDE - CMP_LEN // CMP_STRIDE + 1
    n_slc = s_len // SLC_BLOCK
    n_top = min(SLC_TOPN, n_slc)
    nkb = s_len // kb
    n_rows = sel_scr.shape[0]
    per_kb = kb // SLC_BLOCK
    dh = HEAD_DIM

    @pl.when(i == 0)
    def _():
        _stage_kv(ks_ref, vs_ref, ksb_scr, vst_scr, nkb, kb)
        _stage_kv(kw_ref, vw_ref, kwb_scr, vwt_scr, nkb, kb)
        kcb_scr[...] = kc_ref[0, 0].astype(BF16)
        vct_scr[...] = vc_ref[0, 0].T.astype(BF16)

    gt_scr[...] = jax.nn.sigmoid(gb_ref[0]).T

    def gate_row(branch, g):
        return gt_scr[pl.ds(branch * NSA_HEADS + kvh * NSA_GROUP + g, 1), :]

    n_sub = lax.broadcasted_iota(jnp.int32, (LANE, tq), 0)
    t_lane = i * tq + lax.broadcasted_iota(jnp.int32, (LANE, tq), 1)
    avail = (n_sub * CMP_STRIDE + (CMP_LEN - 1) <= t_lane) & (n_sub < n_cmp)
    jj = lax.broadcasted_iota(jnp.int32, (LANE, LANE), 0)
    nn = lax.broadcasted_iota(jnp.int32, (LANE, LANE), 1)
    sel_map_t = jnp.where((nn * CMP_STRIDE < jj * SLC_BLOCK + SLC_BLOCK)
                          & (nn * CMP_STRIDE + CMP_LEN > jj * SLC_BLOCK)
                          & (nn < n_cmp) & (jj < n_slc), 1.0, 0.0)
    wq = NSA_GROUP * tq

    def gate_wide(branch):
        return jnp.concatenate([jnp.broadcast_to(gate_row(branch, g), (dh, tq))
                                for g in range(NSA_GROUP)], axis=1)

    qb = jnp.concatenate([q_ref[0, :, g * dh:(g + 1) * dh].astype(BF16)
                          for g in range(NSA_GROUP)], axis=0)
    avail_w = jnp.concatenate([avail] * NSA_GROUP, axis=1)
    s_c = lax.dot_general(kcb_scr[...], qb, _NT, preferred_element_type=F32) * ATT_SCALE
    s_c = jnp.where(avail_w, s_c, NEG_INF)
    e = jnp.where(avail_w, jnp.exp(s_c - jnp.max(s_c, axis=0, keepdims=True)), 0.0)
    den = jnp.sum(e, axis=0, keepdims=True)
    p_c = e * jnp.where(den > 0.0, 1.0 / den, 0.0)
    o_scr[...] = gate_wide(0) * jnp.dot(vct_scr[...], p_c.astype(BF16), preferred_element_type=F32)
    p_sum = sum(p_c[:, g * tq:(g + 1) * tq] for g in range(NSA_GROUP))
    imp = jnp.dot(sel_map_t, p_sum, precision=lax.Precision.HIGHEST, preferred_element_type=F32)

    imp = imp[:n_rows]
    jio = lax.broadcasted_iota(jnp.int32, (n_rows, tq), 0)
    cur = (i * tq + lax.broadcasted_iota(jnp.int32, (n_rows, tq), 1)) // SLC_BLOCK
    forced = (jio == 0) | (jio == cur) | (jio == cur - 1)
    imp = jnp.where(forced, BIG, imp)
    imp = jnp.where(jio > cur, NEG_INF, imp)
    imp = jnp.where(jio < n_slc, imp, -jnp.inf)
    cnt = jnp.zeros((n_rows, tq), F32)
    for r in range(n_slc):
        vr = imp[r:r + 1, :]
        cnt = cnt + jnp.where((vr > imp) | ((vr == imp) & (r < jio)), 1.0, 0.0)
    sel_scr[...] = jnp.where((cnt < n_top) & (jio < n_slc), 1.0, 0.0)

    def slc_mask(jb):
        rows = [jnp.broadcast_to(sel_scr[pl.ds(jb * per_kb + a, 1), :], (SLC_BLOCK, tq))
                for a in range(per_kb)]
        one = jnp.concatenate(rows, axis=0)
        return jnp.concatenate([one] * NSA_GROUP, axis=1) > 0.5

    key_row = lax.broadcasted_iota(jnp.int32, (kb, wq), 0)
    q_idx = lax.broadcasted_iota(jnp.int32, (kb, wq), 1) & (tq - 1)
    causal = key_row <= q_idx

    carry = _flash_update_t(qb, ksb_scr[i], vst_scr[i], slc_mask(i) & causal, _flash_init_t(wq))

    def slc_body(j, carry):
        return _flash_update_t(qb, ksb_scr[j], vst_scr[j], slc_mask(j), carry)

    _, l, acc = lax.fori_loop(0, i, slc_body, carry)
    o_scr[...] += gate_wide(1) * (acc / l)

    carry = _flash_update_t(qb, kwb_scr[i], vwt_scr[i], causal, _flash_init_t(wq))

    def win_body(j, carry):
        mask = (j * kb + key_row) > (i * tq + q_idx - WINDOW)
        return _flash_update_t(qb, kwb_scr[j], vwt_scr[j], mask, carry)

    _, l, acc = lax.fori_loop(jnp.maximum(i - WINDOW // kb, 0), i, win_body, carry)
    o_t = o_scr[...] + gate_wide(2) * (acc / l)
    for g in range(NSA_GROUP):
        o_ref[0, :, g * dh:(g + 1) * dh] = o_t[:, g * tq:(g + 1) * tq].T


def _nsa_prompt_t(proj, kvc, g_b):
    b, s, _ = proj.shape
    assert s % NSA_TQ == 0 and NSA_TQ == NSA_KB and s // CMP_STRIDE == LANE
    assert NSA_KB % SLC_BLOCK == 0
    gw = NSA_GROUP * HEAD_DIM
    nkh = NSA_KV_HEADS
    nkb = s // NSA_KB
    n_rows = _round_up(s // SLC_BLOCK, 8)
    seq = lambda col: pl.BlockSpec((1, s, HEAD_DIM), lambda bi, h, i: (bi, 0, col + h))
    kv_scr = [pltpu.VMEM((nkb, NSA_KB, HEAD_DIM), BF16), pltpu.VMEM((nkb, HEAD_DIM, NSA_KB), BF16)]
    return pl.pallas_call(
        functools.partial(_nsa_prompt_t_kernel, s_len=s),
        grid=(b, nkh, s // NSA_TQ),
        in_specs=[pl.BlockSpec((1, NSA_TQ, gw), lambda bi, h, i: (bi, i, COL_QB // NSA_GROUP + h)),
                  pl.BlockSpec((1, 1, LANE, HEAD_DIM), lambda bi, h, i: (bi, h, 0, 0)),
                  pl.BlockSpec((1, 1, LANE, HEAD_DIM), lambda bi, h, i: (bi, nkh + h, 0, 0)),
                  seq(COL_SLC), seq(COL_SLC + nkh), seq(COL_WIN), seq(COL_WIN + nkh),
                  pl.BlockSpec((1, NSA_TQ, LANE), lambda bi, h, i: (bi, i, 0))],
        out_specs=pl.BlockSpec((1, NSA_TQ, gw), lambda bi, h, i: (bi, i, h)),
        out_shape=jax.ShapeDtypeStruct((b, s, NSA_HEADS * HEAD_DIM), F32),
        scratch_shapes=kv_scr + kv_scr + [
            pltpu.VMEM((LANE, HEAD_DIM), BF16), pltpu.VMEM((HEAD_DIM, LANE), BF16),
            pltpu.VMEM((LANE, NSA_TQ), F32), pltpu.VMEM((n_rows, NSA_TQ), F32),
            pltpu.VMEM((HEAD_DIM, NSA_GROUP * NSA_TQ), F32)],
        compiler_params=pltpu.CompilerParams(
            dimension_semantics=("arbitrary", "arbitrary", "arbitrary"),
            vmem_limit_bytes=VMEM_LIMIT_BYTES),
        name="nsa_prompt",
    )(proj, kvc, kvc, proj, proj, proj, proj, g_b)


PAGE_SIZE = 128
PAGE_ROWS = PAGE_SIZE * 2 * NSA_KV_HEADS
PAGES_PER_STEP = 16
QROWS = 8
NQ = NSA_GROUP * QROWS


def _page_specs(n, l):
    def spec(k):
        return pl.BlockSpec((1, 1, PAGE_ROWS, HEAD_DIM), lambda b, j, pt: (l, pt[b, j * n + k], 0, 0))
    return [spec(k) for k in range(n)]


def _page_kv(ref, kv, h, rows=PAGE_SIZE):
    return ref[0, 0, pl.ds(kv * NSA_KV_HEADS + h, rows, stride=2 * NSA_KV_HEADS), :]


def _first_max_onehot(work, lane_f):
    mx = jnp.max(work, axis=-1, keepdims=True)
    first = jnp.min(jnp.where(work == mx, lane_f, 1e9), axis=-1, keepdims=True)
    return jnp.where(lane_f == first, 1.0, 0.0)


def _moba_sample_kernel(pt_ref, q_ref, knew_ref, vnew_ref, *rest, n_new, nb_past):
    pages = rest[:PAGES_PER_STEP]
    o_ref, m_scr, l_scr, acc_scr, kmean_scr = rest[PAGES_PER_STEP:]
    j = pl.program_id(1)
    nj = pl.num_programs(1)
    lane = lax.broadcasted_iota(jnp.int32, (NQ, LANE), 1)

    @pl.when(j == 0)
    def _():
        m_scr[...] = jnp.full_like(m_scr, NEG_INF)
        l_scr[...] = jnp.zeros_like(l_scr)
        kmean_scr[...] = jnp.zeros_like(kmean_scr)

    bps = PAGES_PER_STEP * PAGE_SIZE // MOBA_BLOCK
    for h in range(MOBA_KV_HEADS):
        qb = q_ref[0, h].astype(BF16)
        k_all = jnp.concatenate([_page_kv(pg, 0, h) for pg in pages], axis=0)
        v_all = jnp.concatenate([_page_kv(pg, 1, h) for pg in pages], axis=0).astype(BF16)
        s = lax.dot_general(qb, k_all.astype(BF16), _NT, preferred_element_type=F32) * ATT_SCALE
        blocks = [slice(u * MOBA_BLOCK, (u + 1) * MOBA_BLOCK) for u in range(bps)]
        m_bs = [jnp.max(s[:, sl], axis=-1, keepdims=True) for sl in blocks]
        p = jnp.exp(s - jnp.concatenate([jnp.broadcast_to(m_b, (NQ, MOBA_BLOCK)) for m_b in m_bs],
                                        axis=1))
        m_new, l_new = m_scr[h], l_scr[h]
        for u, sl in enumerate(blocks):
            blk = j * bps + u
            m_new = jnp.where(lane == blk, m_bs[u], m_new)
            l_new = jnp.where(lane == blk, jnp.sum(p[:, sl], axis=-1, keepdims=True), l_new)
            acc_scr[h, blk] = jnp.dot(p[:, sl].astype(BF16), v_all[sl], preferred_element_type=F32)
            kmean_scr[h, pl.ds(blk, 1), :] = (jnp.sum(k_all[sl], axis=0, keepdims=True)
                                             * (1.0 / MOBA_BLOCK))
        m_scr[h] = m_new
        l_scr[h] = l_new

    @pl.when(j == nj - 1)
    def _():
        lane_f = lane.astype(F32)
        trow = lax.broadcasted_iota(jnp.int32, (NQ, LANE), 0) & (QROWS - 1)
        for h in range(MOBA_KV_HEADS):
            q = q_ref[0, h]
            qb = q.astype(BF16)
            s_own = lax.dot_general(qb, knew_ref[0, h].astype(BF16), _NT,
                                    preferred_element_type=F32) * ATT_SCALE
            s_own = jnp.where((lane <= trow) & (lane < n_new), s_own, NEG_INF)
            m_own = jnp.max(s_own, axis=-1, keepdims=True)
            p_own = jnp.exp(s_own - m_own)
            l_own = jnp.sum(p_own, axis=-1, keepdims=True)
            acc_own = jnp.dot(p_own.astype(BF16), vnew_ref[0, h].astype(BF16),
                              preferred_element_type=F32)
            gate = lax.dot_general(q, kmean_scr[h], _NT, precision=lax.Precision.HIGHEST,
                                   preferred_element_type=F32)
            work = jnp.where(lane < nb_past, gate, -jnp.inf)
            sel = jnp.zeros((NQ, LANE), F32)
            for _ in range(min(MOBA_TOPK, nb_past)):
                pick = _first_max_onehot(work, lane_f)
                sel = jnp.maximum(sel, pick)
                work = jnp.where(pick > 0.5, -jnp.inf, work)
            chosen = sel > 0.5
            m_all = m_scr[h]
            m_tot = jnp.maximum(m_own, jnp.max(jnp.where(chosen, m_all, NEG_INF), axis=-1,
                                               keepdims=True))
            w = jnp.where(chosen, jnp.exp(m_all - m_tot), 0.0)
            w_own = jnp.exp(m_own - m_tot)
            l_tot = w_own * l_own + jnp.sum(w * l_scr[h], axis=-1, keepdims=True)

            def merge(blk, acc, w=w, h=h):
                wcol = jnp.sum(jnp.where(lane == blk, w, 0.0), axis=-1, keepdims=True)
                return acc + wcol * acc_scr[h, blk]

            acc = lax.fori_loop(0, nb_past, merge, w_own * acc_own)
            o_ref[0, h] = acc / l_tot


def _moba_sample(q, k_new, v_new, pool, page_table, l, n_new):
    b = q.shape[0]
    n_pages = page_table.shape[1]
    assert n_pages % PAGES_PER_STEP == 0 and (n_pages * PAGE_SIZE) % MOBA_BLOCK == 0
    nb_past = n_pages * PAGE_SIZE // MOBA_BLOCK
    assert nb_past <= LANE and n_new <= MOBA_BLOCK
    per_b = lambda shape: pl.BlockSpec(shape, lambda bi, j, pt: (bi,) + (0,) * (len(shape) - 1))
    grid_spec = pltpu.PrefetchScalarGridSpec(
        num_scalar_prefetch=1,
        grid=(b, n_pages // PAGES_PER_STEP),
        in_specs=[per_b((1, MOBA_KV_HEADS, NQ, HEAD_DIM)),
                  per_b((1, MOBA_KV_HEADS, LANE, HEAD_DIM)),
                  per_b((1, MOBA_KV_HEADS, LANE, HEAD_DIM))] + _page_specs(PAGES_PER_STEP, l),
        out_specs=per_b((1, MOBA_KV_HEADS, NQ, HEAD_DIM)),
        scratch_shapes=[pltpu.VMEM((MOBA_KV_HEADS, NQ, LANE), F32),
                        pltpu.VMEM((MOBA_KV_HEADS, NQ, LANE), F32),
                        pltpu.VMEM((MOBA_KV_HEADS, nb_past, NQ, HEAD_DIM), F32),
                        pltpu.VMEM((MOBA_KV_HEADS, LANE, HEAD_DIM), F32)])
    return pl.pallas_call(
        functools.partial(_moba_sample_kernel, n_new=n_new, nb_past=nb_past),
        grid_spec=grid_spec,
        out_shape=jax.ShapeDtypeStruct(q.shape, F32),
        compiler_params=pltpu.CompilerParams(dimension_semantics=("arbitrary", "arbitrary"),
                                             vmem_limit_bytes=VMEM_LIMIT_BYTES),
        name="moba_sample",
    )(page_table, q, k_new, v_new, *([pool] * PAGES_PER_STEP))


def _compress_sample_kernel(pt_ref, pos_ref, w1_ref, b1_ref, w2_ref, b2_ref, *rest):
    pages = rest[:PAGES_PER_STEP + 1]
    o_ref = rest[PAGES_PER_STEP + 1]
    cpp = PAGE_SIZE // CMP_STRIDE
    nch = PAGES_PER_STEP * cpp
    half = CMP_STRIDE * HEAD_DIM
    n_col = 2 * NSA_KV_HEADS
    ys = [jnp.swapaxes(pg[0, 0].reshape(cpp, CMP_STRIDE * n_col, HEAD_DIM), 0, 1) for pg in pages]
    for kv in range(2):
        pos_first = jnp.concatenate([pos_ref[kv, r:r + 1, :] for r in range(CMP_STRIDE)], axis=1)
        pos_second = jnp.concatenate([pos_ref[kv, r:r + 1, :] for r in range(CMP_STRIDE, CMP_LEN)],
                                     axis=1)
        w_first = w1_ref[kv, :half, :].astype(BF16)
        w_second = w1_ref[kv, half:, :].astype(BF16)
        w2 = w2_ref[kv].astype(BF16)
        for h in range(NSA_KV_HEADS):
            c = kv * NSA_KV_HEADS + h
            x = jnp.concatenate(
                [jnp.concatenate([y[r * n_col + c] for y in ys], axis=0)
                 for r in range(CMP_STRIDE)], axis=1)
            first = jnp.dot((x[:nch] + pos_first).astype(BF16), w_first, preferred_element_type=F32)
            second = jnp.dot((x + pos_second).astype(BF16), w_second, preferred_element_type=F32)
            pre = first + pltpu.roll(second, nch + cpp - 1, axis=0)[:nch] + b1_ref[kv:kv + 1, :]
            hdn = jax.nn.gelu(pre)
            o_ref[0, c] = (jnp.dot(hdn.astype(BF16), w2, preferred_element_type=F32)
                           + b2_ref[kv:kv + 1, :])


def _compress_sample(pool, page_table, l, pos, w1, b1, w2, b2):
    b, n_pages = page_table.shape
    nch = PAGES_PER_STEP * PAGE_SIZE // CMP_STRIDE
    full = lambda shape: pl.BlockSpec(shape, lambda bi, j, pt: (0,) * len(shape))
    next_page = pl.BlockSpec(
        (1, 1, PAGE_ROWS, HEAD_DIM),
        lambda bi, j, pt: (l, pt[bi, jnp.minimum((j + 1) * PAGES_PER_STEP, n_pages - 1)], 0, 0))
    grid_spec = pltpu.PrefetchScalarGridSpec(
        num_scalar_prefetch=1,
        grid=(b, n_pages // PAGES_PER_STEP),
        in_specs=[full(pos.shape), full(w1.shape), full(b1.shape), full(w2.shape), full(b2.shape)]
        + _page_specs(PAGES_PER_STEP, l) + [next_page],
        out_specs=pl.BlockSpec((1, 2 * NSA_KV_HEADS, nch, HEAD_DIM), lambda bi, j, pt: (bi, 0, j, 0)))
    return pl.pallas_call(
        _compress_sample_kernel,
        grid_spec=grid_spec,
        out_shape=jax.ShapeDtypeStruct(
            (b, 2 * NSA_KV_HEADS, n_pages * PAGE_SIZE // CMP_STRIDE, HEAD_DIM), F32),
        compiler_params=pltpu.CompilerParams(dimension_semantics=("arbitrary", "arbitrary"),
                                             vmem_limit_bytes=VMEM_LIMIT_BYTES),
        name="nsa_compress_sample",
    )(page_table, pos, w1, b1, w2, b2, *([pool] * (PAGES_PER_STEP + 1)))


def _nsa_cmp_sample_kernel(q_ref, kc_ref, vc_ref, oc_ref, sel_ref, *, q0, n_cmp, n_slc, n_lanes):
    ncp = kc_ref.shape[2]
    col = lax.broadcasted_iota(jnp.int32, (NQ, ncp), 1)
    trow = lax.broadcasted_iota(jnp.int32, (NQ, ncp), 0) & (QROWS - 1)
    avail = (col * CMP_STRIDE + (CMP_LEN - 1) <= q0 + trow) & (col < n_cmp)
    nn = lax.broadcasted_iota(jnp.int32, (ncp, n_lanes), 0)
    jj = lax.broadcasted_iota(jnp.int32, (ncp, n_lanes), 1)
    sel_map = jnp.where((nn * CMP_STRIDE < jj * SLC_BLOCK + SLC_BLOCK)
                        & (nn * CMP_STRIDE + CMP_LEN > jj * SLC_BLOCK)
                        & (nn < n_cmp) & (jj < n_slc), 1.0, 0.0)
    lane = lax.broadcasted_iota(jnp.int32, (QROWS, n_lanes), 1)
    lane_f = lane.astype(F32)
    cur = (q0 + lax.broadcasted_iota(jnp.int32, (QROWS, n_lanes), 0)) // SLC_BLOCK
    for h in range(NSA_KV_HEADS):
        qb = q_ref[0, h].astype(BF16)
        s = lax.dot_general(qb, kc_ref[0, h].astype(BF16), _NT,
                            preferred_element_type=F32) * ATT_SCALE
        s = jnp.where(avail, s, NEG_INF)
        e = jnp.where(avail, jnp.exp(s - jnp.max(s, axis=-1, keepdims=True)), 0.0)
        den = jnp.sum(e, axis=-1, keepdims=True)
        p = e * jnp.where(den > 0.0, 1.0 / den, 0.0)
        oc_ref[0, h] = jnp.dot(p.astype(BF16), vc_ref[0, h].astype(BF16),
                               preferred_element_type=F32)
        pg = sum(p[g * QROWS:(g + 1) * QROWS] for g in range(NSA_GROUP))
        imp = jnp.dot(pg, sel_map, precision=lax.Precision.HIGHEST, preferred_element_type=F32)
        forced = (lane == 0) | (lane == cur) | (lane == cur - 1)
        imp = jnp.where(forced, BIG, imp)
        imp = jnp.where(lane > cur, NEG_INF, imp)
        work = jnp.where(lane < n_slc, imp, -jnp.inf)
        sel = jnp.zeros((QROWS, n_lanes), F32)
        for _ in range(min(SLC_TOPN, n_slc)):
            pick = _first_max_onehot(work, lane_f)
            sel = jnp.maximum(sel, pick)
            work = jnp.where(pick > 0.5, -jnp.inf, work)
        sel_ref[0, h] = sel


def _nsa_cmp_sample(q, kvc, q0, n_new):
    b = q.shape[0]
    ncp = kvc.shape[2]
    n_cmp = ncp - CMP_LEN // CMP_STRIDE + 1
    n_slc = -(-(q0 + n_new) // SLC_BLOCK)
    n_lanes = _round_up(n_slc, LANE)
    nkh = NSA_KV_HEADS
    return pl.pallas_call(
        functools.partial(_nsa_cmp_sample_kernel, q0=q0, n_cmp=n_cmp, n_slc=n_slc, n_lanes=n_lanes),
        grid=(b,),
        in_specs=[pl.BlockSpec((1, nkh, NQ, HEAD_DIM), lambda bi: (bi, 0, 0, 0)),
                  pl.BlockSpec((1, nkh, ncp, HEAD_DIM), lambda bi: (bi, 0, 0, 0)),
                  pl.BlockSpec((1, nkh, ncp, HEAD_DIM), lambda bi: (bi, 1, 0, 0))],
        out_specs=[pl.BlockSpec((1, nkh, NQ, HEAD_DIM), lambda bi: (bi, 0, 0, 0)),
                   pl.BlockSpec((1, nkh, QROWS, n_lanes), lambda bi: (bi, 0, 0, 0))],
        out_shape=[jax.ShapeDtypeStruct(q.shape, F32),
                   jax.ShapeDtypeStruct((b, nkh, QROWS, n_lanes), F32)],
        compiler_params=pltpu.CompilerParams(dimension_semantics=("arbitrary",),
                                             vmem_limit_bytes=VMEM_LIMIT_BYTES),
        name="nsa_cmp_sample",
    )(q, kvc, kvc)


def _nsa_slc_sample_kernel(pt_ref, q_ref, sel_ref, oc_ref, g_ref, ksn_ref, vsn_ref, kwn_ref,
                           vwn_ref, win_ref, *rest, n_new, n_win, cur_blk):
    pages = rest[:PAGES_PER_STEP]
    o_ref, m_scr, l_scr, acc_scr, ow_scr = rest[PAGES_PER_STEP:]
    j = pl.program_id(1)
    nj = pl.num_programs(1)
    n_lanes = sel_ref.shape[3]
    lane = lax.broadcasted_iota(jnp.int32, (NQ, LANE), 1)
    trow = lax.broadcasted_iota(jnp.int32, (NQ, LANE), 0) & (QROWS - 1)
    new_ok = (lane <= trow) & (lane < n_new)

    @pl.when(j == 0)
    def _():
        for h in range(NSA_KV_HEADS):
            qb = q_ref[0, h].astype(BF16)
            sel_h = jnp.concatenate([sel_ref[0, h]] * NSA_GROUP, axis=0)
            lane_s = lax.broadcasted_iota(jnp.int32, (NQ, n_lanes), 1)
            sel_cur = jnp.sum(jnp.where(lane_s == cur_blk, sel_h, 0.0), axis=-1,
                              keepdims=True) > 0.5
            s = lax.dot_general(qb, ksn_ref[0, h].astype(BF16), _NT,
                                preferred_element_type=F32) * ATT_SCALE
            s = jnp.where(new_ok & sel_cur, s, NEG_INF)
            m = jnp.max(s, axis=-1, keepdims=True)
            p = jnp.exp(s - m)
            m_scr[h] = m
            l_scr[h] = jnp.sum(p, axis=-1, keepdims=True)
            acc_scr[h] = jnp.dot(p.astype(BF16), vsn_ref[0, h].astype(BF16),
                                 preferred_element_type=F32)
            kw = win_ref[0, 0, pl.ds(h, n_win, stride=2 * NSA_KV_HEADS), :]
            vw = win_ref[0, 0, pl.ds(NSA_KV_HEADS + h, n_win, stride=2 * NSA_KV_HEADS), :]
            idx = lax.broadcasted_iota(jnp.int32, (NQ, n_win), 1)
            tr = lax.broadcasted_iota(jnp.int32, (NQ, n_win), 0) & (QROWS - 1)
            s_w = lax.dot_general(qb, kw.astype(BF16), _NT, preferred_element_type=F32) * ATT_SCALE
            s_w = jnp.where(idx + (WINDOW - n_win) > tr, s_w, NEG_INF)
            s_n = lax.dot_general(qb, kwn_ref[0, h].astype(BF16), _NT,
                                  preferred_element_type=F32) * ATT_SCALE
            s_n = jnp.where(new_ok, s_n, NEG_INF)
            mw = jnp.maximum(jnp.max(s_w, axis=-1, keepdims=True),
                             jnp.max(s_n, axis=-1, keepdims=True))
            p_w = jnp.exp(s_w - mw)
            p_n = jnp.exp(s_n - mw)
            lw = jnp.sum(p_w, axis=-1, keepdims=True) + jnp.sum(p_n, axis=-1, keepdims=True)
            ow_scr[h] = (jnp.dot(p_w.astype(BF16), vw.astype(BF16), preferred_element_type=F32)
                         + jnp.dot(p_n.astype(BF16), vwn_ref[0, h].astype(BF16),
                                   preferred_element_type=F32)) / lw

    step_keys = PAGES_PER_STEP * PAGE_SIZE
    erow = lax.broadcasted_iota(jnp.int32, (n_lanes, step_keys), 0)
    ecol = lax.broadcasted_iota(jnp.int32, (n_lanes, step_keys), 1)
    expand = jnp.where((j * step_keys + ecol) // SLC_BLOCK == erow, 1.0, 0.0).astype(BF16)
    for h in range(NSA_KV_HEADS):
        qb = q_ref[0, h].astype(BF16)
        sel_h = jnp.concatenate([sel_ref[0, h]] * NSA_GROUP, axis=0).astype(BF16)
        maskf = jnp.dot(sel_h, expand, preferred_element_type=F32)
        ss = []
        for k in range(PAGES_PER_STEP):
            s = lax.dot_general(qb, _page_kv(pages[k], 0, h).astype(BF16), _NT,
                                preferred_element_type=F32) * ATT_SCALE
            ss.append(jnp.where(maskf[:, k * PAGE_SIZE:(k + 1) * PAGE_SIZE] > 0.5, s, NEG_INF))
        m_old = m_scr[h]
        m_new = functools.reduce(jnp.maximum,
                                 [jnp.max(s, axis=-1, keepdims=True) for s in ss] + [m_old])
        alpha = jnp.exp(m_old - m_new)
        ps = [jnp.exp(s - m_new) for s in ss]
        l_scr[h] = alpha * l_scr[h] + sum(jnp.sum(p, axis=-1, keepdims=True) for p in ps)
        acc_scr[h] = alpha * acc_scr[h] + sum(
            jnp.dot(p.astype(BF16), _page_kv(pages[k], 1, h).astype(BF16),
                    preferred_element_type=F32) for k, p in enumerate(ps))
        m_scr[h] = m_new

    @pl.when(j == nj - 1)
    def _():
        for h in range(NSA_KV_HEADS):
            gates = jax.nn.sigmoid(g_ref[0, h])
            o_s = acc_scr[h] / l_scr[h]
            o_ref[0, h] = (gates[:, 0:1] * oc_ref[0, h] + gates[:, 1:2] * o_s
                           + gates[:, 2:3] * ow_scr[h])


def _nsa_slc_sample(q, sel, o_c, g3, ks_new, vs_new, kw_new, vw_new, win_state, pool, page_table,
                    l, n_new):
    b = q.shape[0]
    n_pages = page_table.shape[1]
    n_win = win_state.shape[2] // (2 * NSA_KV_HEADS)
    assert (n_pages * PAGE_SIZE) % SLC_BLOCK == 0 and n_new <= SLC_BLOCK and n_win <= WINDOW
    nkh = NSA_KV_HEADS
    per_b = lambda shape: pl.BlockSpec(shape, lambda bi, j, pt: (bi,) + (0,) * (len(shape) - 1))
    grid_spec = pltpu.PrefetchScalarGridSpec(
        num_scalar_prefetch=1,
        grid=(b, n_pages // PAGES_PER_STEP),
        in_specs=[per_b((1, nkh, NQ, HEAD_DIM)), per_b((1, nkh, QROWS, sel.shape[3])),
                  per_b((1, nkh, NQ, HEAD_DIM)), per_b((1, nkh, NQ, LANE)),
                  per_b((1, nkh, LANE, HEAD_DIM)), per_b((1, nkh, LANE, HEAD_DIM)),
                  per_b((1, nkh, LANE, HEAD_DIM)), per_b((1, nkh, LANE, HEAD_DIM)),
                  pl.BlockSpec((1, 1, win_state.shape[2], HEAD_DIM), lambda bi, j, pt: (l, bi, 0, 0))]
        + _page_specs(PAGES_PER_STEP, l),
        out_specs=per_b((1, nkh, NQ, HEAD_DIM)),
        scratch_shapes=[pltpu.VMEM((nkh, NQ, 1), F32), pltpu.VMEM((nkh, NQ, 1), F32),
                        pltpu.VMEM((nkh, NQ, HEAD_DIM), F32), pltpu.VMEM((nkh, NQ, HEAD_DIM), F32)])
    return pl.pallas_call(
        functools.partial(_nsa_slc_sample_kernel, n_new=n_new, n_win=n_win,
                          cur_blk=n_pages * PAGE_SIZE // SLC_BLOCK),
        grid_spec=grid_spec,
        out_shape=jax.ShapeDtypeStruct(q.shape, F32),
        compiler_params=pltpu.CompilerParams(dimension_semantics=("arbitrary", "arbitrary"),
                                             vmem_limit_bytes=VMEM_LIMIT_BYTES),
        name="nsa_slc_sample",
    )(page_table, q, sel, o_c, g3, ks_new, vs_new, kw_new, vw_new, win_state,
      *([pool] * PAGES_PER_STEP))


PROMPT_TM_IN = 512
PROMPT_TM_OUT = 256
MOE_TM = 128
SAMPLE_MOE_BLOCK = 8


def _kv_rows(proj, col, b, s):
    lo = col * HEAD_DIM
    return proj[:, lo:lo + KV_COLS].reshape(b, s, 2, NSA_KV_HEADS, HEAD_DIM)


def _to_qrows(x, b, s):
    x = x.reshape(b, s, NSA_KV_HEADS, NSA_GROUP, HEAD_DIM).transpose(0, 2, 3, 1, 4)
    x = jnp.pad(x, ((0, 0), (0, 0), (0, 0), (0, QROWS - s), (0, 0)))
    return x.reshape(b, NSA_KV_HEADS, NQ, HEAD_DIM)


def _from_qrows(o, b, s):
    o = o.reshape(b, NSA_KV_HEADS, NSA_GROUP, QROWS, HEAD_DIM)[:, :, :, :s]
    return o.transpose(0, 3, 1, 2, 4).reshape(b * s, NSA_HEADS * HEAD_DIM)


def _new_rows(proj, col, b, s):
    lo = col * HEAD_DIM
    x = proj[:, lo:lo + NSA_KV_HEADS * HEAD_DIM].reshape(b, s, NSA_KV_HEADS, HEAD_DIM)
    return jnp.pad(x.transpose(0, 2, 1, 3), ((0, 0), (0, 0), (0, LANE - s), (0, 0)))


def _sample_mixer(proj, g_b, pools, win_state, page_table, cmp_w, l, b, s, q0):
    assert MOBA_KV_HEADS == NSA_KV_HEADS and MOBA_GROUP == NSA_GROUP and s <= QROWS
    assert q0 % CMP_STRIDE == 0 and s < CMP_STRIDE and q0 % MOBA_BLOCK == 0
    pool_moba, pool_cmp, pool_slc = pools
    nkh = NSA_KV_HEADS
    q_a = _to_qrows(proj[:, :MOBA_HEADS * HEAD_DIM], b, s)
    q_b = _to_qrows(proj[:, COL_QB * HEAD_DIM:COL_CMP * HEAD_DIM], b, s)
    o_a = _moba_sample(q_a, _new_rows(proj, COL_KA, b, s), _new_rows(proj, COL_VA, b, s),
                       pool_moba, page_table, l, s)
    kvc = _compress_sample(pool_cmp, page_table, l, *cmp_w)
    o_c, sel = _nsa_cmp_sample(q_b, kvc, q0, s)
    g3 = g_b[:, :GB_COLS].reshape(b, s, 3, nkh, NSA_GROUP).transpose(0, 3, 4, 1, 2)
    g3 = jnp.pad(g3, ((0, 0), (0, 0), (0, 0), (0, QROWS - s), (0, LANE - 3)))
    o_b = _nsa_slc_sample(q_b, sel, o_c, g3.reshape(b, nkh, NQ, LANE),
                          _new_rows(proj, COL_SLC, b, s), _new_rows(proj, COL_SLC + nkh, b, s),
                          _new_rows(proj, COL_WIN, b, s), _new_rows(proj, COL_WIN + nkh, b, s),
                          win_state, pool_slc, page_table, l, s)
    return _from_qrows(o_a, b, s), _from_qrows(o_b, b, s)


def _moe_block(x, h, logits, g2, norm_final, w_gate, w_up, w_down, l, bm, tm, rows_per_mod,
               final_norm):
    blk_expert, row_tok, n_used, row_w, d0, d1 = _route(logits, bm)
    yb = _moe_ffn(h, blk_expert, row_tok, n_used, row_w, w_gate, w_up, w_down, l, bm)
    return _moe_combine(x, g2, norm_final, yb, d0, d1, tm, rows_per_mod, final_norm)


def kernel(x_prompt, x_sample, cache_moba_kv, cache_cmp_kv, cache_slc_kv, state_win_kv,
           page_table, c_prompt, c_sample, w_in, w_pa, w_pb, w_out, cmp_pos, cmp_w1, cmp_b1,
           cmp_w2, cmp_b2, norm_attn, norm_ffn, norm_final, w_ada, b_ada, w_rg, b_rg, w_re,
           b_re, w_gate, w_up, w_down):
    bp, sp, d = x_prompt.shape
    bs, ss, _ = x_sample.shape
    tp, ts = bp * sp, bs * ss
    past_len = page_table.shape[1] * cache_moba_kv.shape[2]
    xp = x_prompt.reshape(tp, d)
    xs = x_sample.reshape(ts, d)
    cos_p, sin_p = _rope_tables(jnp.tile(jnp.arange(sp), bp))
    cos_s, sin_s = _rope_tables(jnp.tile(past_len + jnp.arange(ss), bs))
    n_c = _round_up(bp + bs, 8)
    c_all = jnp.pad(jnp.concatenate([c_prompt, c_sample], axis=0), ((0, n_c - bp - bs), (0, 0)))
    n_pool = cache_moba_kv.shape[1]
    pools = tuple(c.reshape(DEPTH, n_pool, PAGE_ROWS, HEAD_DIM)
                  for c in (cache_moba_kv, cache_cmp_kv, cache_slc_kv))
    n_win = state_win_kv.shape[2]
    win_view = state_win_kv.reshape(DEPTH, bs, n_win * 2 * NSA_KV_HEADS, HEAD_DIM)
    rows_p, rows_s = [], []
    for l in range(DEPTH):
        last = l == DEPTH - 1
        ada = _ada(c_all, w_ada, b_ada, l).reshape(n_c, 6, d)
        mod_p = [ada[:bp, k].reshape(bp, 1, d) for k in range(6)]
        mod_s = [jnp.repeat(ada[bp:bp + bs, k], ss, axis=0).reshape(1, ts, d) for k in range(6)]
        w_main = jnp.concatenate([w_in[l][:, :QKV_COLS], w_in[l][:, QKV_COLS + GB_COLS:]],
                                 axis=1).astype(BF16)
        w_gb = jnp.pad(w_in[l][:, QKV_COLS:QKV_COLS + GB_COLS],
                       ((0, 0), (0, LANE - GB_COLS))).astype(BF16)
        w_pa_b, w_pb_b, w_out_b = w_pa[l].astype(BF16), w_pb[l].astype(BF16), w_out[l].astype(BF16)
        w_r = jnp.pad(jnp.concatenate([w_rg[l], w_re[l]], axis=1),
                      ((0, 0), (0, LANE - N_GROUPS - N_EXPERTS)))
        b_r = jnp.pad(jnp.concatenate([b_rg[l], b_re[l]]),
                      (0, LANE - N_GROUPS - N_EXPERTS)).reshape(1, LANE)

        sh1, sc1, g1, sh2, sc2, g2 = mod_p
        proj, g_b = _in_proj(xp, norm_attn[l], sc1, sh1, cos_p, sin_p, w_main, w_gb,
                             PROMPT_TM_IN, sp)
        proj3 = proj.reshape(bp, sp, PROJ_COLS)
        o_a = _moba_prompt_t(proj3)
        kvc = _compress_prompt(proj3, cmp_pos[l], cmp_w1[l], cmp_b1[l], cmp_w2[l], cmp_b2[l])
        o_b = _nsa_prompt_t(proj3, kvc, g_b.reshape(bp, sp, LANE))
        xp, h2, logits = _out_proj(o_a.reshape(tp, -1), o_b.reshape(tp, -1), proj, xp, g1,
                                   norm_ffn[l], sc2, sh2, w_pa_b, w_pb_b, w_out_b, w_r, b_r,
                                   PROMPT_TM_OUT, sp)
        xp = _moe_block(xp, h2, logits, g2, norm_final, w_gate, w_up, w_down, l,
                        MOE_BLOCK, MOE_TM, sp, last)
        new_win = _kv_rows(proj, COL_WIN, bp, sp)
        rows_p.append((_kv_rows(proj, COL_KA, bp, sp), _kv_rows(proj, COL_CMP, bp, sp),
                       _kv_rows(proj, COL_SLC, bp, sp), new_win[:, sp - min(WINDOW, sp):]))

        sh1, sc1, g1, sh2, sc2, g2 = mod_s
        proj, g_b = _in_proj(xs, norm_attn[l], sc1, sh1, cos_s, sin_s, w_main, w_gb, ts, ss)
        cmp_w = (cmp_pos[l], cmp_w1[l], cmp_b1[l], cmp_w2[l], cmp_b2[l])
        o_a, o_b = _sample_mixer(proj, g_b, pools, win_view, page_table, cmp_w, l, bs, ss, past_len)
        xs, h2, logits = _out_proj(o_a, o_b, proj, xs, g1, norm_ffn[l], sc2, sh2,
                                   w_pa_b, w_pb_b, w_out_b, w_r, b_r, ts, ss)
        xs = _moe_block(xs, h2, logits, g2, norm_final, w_gate, w_up, w_down, l,
                        SAMPLE_MOE_BLOCK, ts, ss, last)
        win_rows = jnp.concatenate([state_win_kv[l], _kv_rows(proj, COL_WIN, bs, ss)], axis=1)
        rows_s.append((_kv_rows(proj, COL_KA, bs, ss), _kv_rows(proj, COL_CMP, bs, ss),
                       _kv_rows(proj, COL_SLC, bs, ss),
                       win_rows[:, win_rows.shape[1] - min(WINDOW, win_rows.shape[1]):]))

    return (xp.reshape(bp, sp, d), xs.reshape(bs, ss, d),
            jnp.stack([r[0] for r in rows_p]), jnp.stack([r[0] for r in rows_s]),
            jnp.stack([r[1] for r in rows_p]), jnp.stack([r[1] for r in rows_s]),
            jnp.stack([r[2] for r in rows_p]), jnp.stack([r[2] for r in rows_s]),
            jnp.stack([r[3] for r in rows_p]), jnp.stack([r[3] for r in rows_s]))
```

```python
import functools

import jax
import jax.numpy as jnp
import numpy as np
from jax import lax
from jax.experimental import pallas as pl
from jax.experimental.pallas import tpu as pltpu

D_MODEL = 2048
DEPTH = 2
HEAD_DIM = 128
ROT_DIM = HEAD_DIM // 4
ROPE_THETA = 500000.0
NORM_EPS = 1e-6
MOBA_HEADS = 8
MOBA_KV_HEADS = 2
MOBA_GROUP = MOBA_HEADS // MOBA_KV_HEADS
MOBA_BLOCK = 256
MOBA_TOPK = 3
MOBA_QCHUNK = 16
NSA_HEADS = 8
NSA_KV_HEADS = 2
NSA_GROUP = NSA_HEADS // NSA_KV_HEADS
CMP_LEN = 32
CMP_STRIDE = 16
CMP_HIDDEN = 128
SLC_BLOCK = 64
SLC_TOPN = 16
WINDOW = 512
NSA_QCHUNK = 16
WIN_QBLOCK = 128
N_GROUPS = 4
EXPERTS_PER_GROUP = 8
N_EXPERTS = N_GROUPS * EXPERTS_PER_GROUP
EXPERT_TOPK = 2
D_EXPERT = 512
MOE_BLOCK = 128

NEG_INF = -1e30
BIG = 1e30
F32 = jnp.float32
BF16 = jnp.bfloat16

LANE = 128
VMEM_LIMIT_BYTES = 48 * 1024 * 1024


def _round_up(n, m):
    return (n + m - 1) // m * m


QKV_COLS = 4096
GB_COLS = 3 * NSA_HEADS
GATE_COLS = 2 * D_MODEL
PROJ_COLS = QKV_COLS + GATE_COLS
COL_QA, COL_KA, COL_VA, COL_QB = 0, 8, 10, 12
COL_CMP, COL_SLC, COL_WIN = 20, 24, 28


def _mod_spec(mod, tm, rows_per_mod):
    d = mod.shape[-1]
    if mod.shape[1] == 1:
        return pl.BlockSpec((1, 1, d), lambda i, *_: (i * tm // rows_per_mod, 0, 0))
    return pl.BlockSpec((1, tm, d), lambda i, *_: (i, 0, 0))


def _ada_kernel(c_ref, w_ref, b_ref, o_ref):
    c = jax.nn.silu(c_ref[...]).astype(BF16)
    o_ref[...] = jnp.dot(c, w_ref[0].astype(BF16), preferred_element_type=F32) + b_ref[0]


def _ada(c, w_ada, b_ada, l, tn=1024):
    m, k = c.shape
    depth, _, n = w_ada.shape
    return pl.pallas_call(
        _ada_kernel,
        grid=(n // tn,),
        in_specs=[pl.BlockSpec((m, k), lambda j: (0, 0)),
                  pl.BlockSpec((1, k, tn), lambda j: (l, 0, j)),
                  pl.BlockSpec((1, 1, tn), lambda j: (l, 0, j))],
        out_specs=pl.BlockSpec((m, tn), lambda j: (0, j)),
        out_shape=jax.ShapeDtypeStruct((m, n), F32),
        compiler_params=pltpu.CompilerParams(dimension_semantics=("arbitrary",),
                                             vmem_limit_bytes=VMEM_LIMIT_BYTES),
        name="ada_ln",
    )(c, w_ada, b_ada.reshape(depth, 1, n))


def _rope_tables(pos):
    half = ROT_DIM // 2
    inv_freq = ROPE_THETA ** (-jnp.arange(half, dtype=F32) / half)
    ang = pos.astype(F32)[:, None] * inv_freq[None, :]
    cos, sin = jnp.cos(ang), jnp.sin(ang)
    n = pos.shape[0]
    cos_t = jnp.concatenate([cos, cos, jnp.ones((n, HEAD_DIM - ROT_DIM), F32)], axis=1)
    sin_t = jnp.concatenate([-sin, sin, jnp.zeros((n, HEAD_DIM - ROT_DIM), F32)], axis=1)
    return cos_t, sin_t


def _rope_column_mask():
    m = np.zeros((1, QKV_COLS), np.float32)
    for lo, hi in ((0, 1280), (1536, 2560), (3072, 3328), (3584, 3840)):
        m[0, lo:hi] = 1.0
    return jnp.asarray(m)


def _norm_modulate(x_ref, g_ref, sc_ref, sh_ref):
    x = x_ref[...]
    y = x * lax.rsqrt(jnp.mean(x * x, axis=-1, keepdims=True) + NORM_EPS) * g_ref[...]
    return (y * (1.0 + sc_ref[0]) + sh_ref[0]).astype(BF16)


def _in_proj_qkv_kernel(x_ref, g_ref, sc_ref, sh_ref, cos_ref, sin_ref, rmask_ref, w_ref, wgb_ref,
                        o_ref, ogb_ref, hb_scr, *, tn):
    @pl.when(pl.program_id(1) == 0)
    def _():
        h = _norm_modulate(x_ref, g_ref, sc_ref, sh_ref)
        hb_scr[...] = h
        ogb_ref[...] = jnp.dot(h, wgb_ref[...], preferred_element_type=F32)

    acc = jnp.dot(hb_scr[...], w_ref[...], preferred_element_type=F32)
    lane = lax.broadcasted_iota(jnp.int32, (acc.shape[0], HEAD_DIM), 1)
    cos_t, sin_t = cos_ref[...], sin_ref[...]
    for hh in range(tn // HEAD_DIM):
        blk = acc[:, hh * HEAD_DIM:(hh + 1) * HEAD_DIM]
        on = rmask_ref[:, hh * HEAD_DIM:(hh + 1) * HEAD_DIM] > 0.5
        partner = jnp.where(lane < ROT_DIM // 2,
                            pltpu.roll(blk, HEAD_DIM - ROT_DIM // 2, axis=1),
                            pltpu.roll(blk, ROT_DIM // 2, axis=1))
        o_ref[:, hh * HEAD_DIM:(hh + 1) * HEAD_DIM] = (
            blk * jnp.where(on, cos_t, 1.0) + partner * jnp.where(on, sin_t, 0.0))


def _in_proj_gate_kernel(x_ref, g_ref, sc_ref, sh_ref, w_ref, o_ref, hb_scr):
    @pl.when(pl.program_id(1) == 0)
    def _():
        hb_scr[...] = _norm_modulate(x_ref, g_ref, sc_ref, sh_ref)

    o_ref[...] = jax.nn.sigmoid(jnp.dot(hb_scr[...], w_ref[...], preferred_element_type=F32))


def _in_proj(x, g, sc, sh, cos_t, sin_t, w_main, w_gb, tm, rows_per_mod, tn=1024):
    t, d = x.shape
    rmask = _rope_column_mask()
    x_specs = [pl.BlockSpec((tm, d), lambda i, j: (i, 0)),
               pl.BlockSpec((1, d), lambda i, j: (0, 0)),
               _mod_spec(sc, tm, rows_per_mod), _mod_spec(sh, tm, rows_per_mod)]
    params = pltpu.CompilerParams(dimension_semantics=("arbitrary", "arbitrary"),
                                  vmem_limit_bytes=VMEM_LIMIT_BYTES)
    qkv, g_b = pl.pallas_call(
        functools.partial(_in_proj_qkv_kernel, tn=tn),
        grid=(t // tm, QKV_COLS // tn),
        in_specs=x_specs + [pl.BlockSpec((tm, HEAD_DIM), lambda i, j: (i, 0)),
                            pl.BlockSpec((tm, HEAD_DIM), lambda i, j: (i, 0)),
                            pl.BlockSpec((1, tn), lambda i, j: (0, j)),
                            pl.BlockSpec((d, tn), lambda i, j: (0, j)),
                            pl.BlockSpec((d, LANE), lambda i, j: (0, 0))],
        out_specs=[pl.BlockSpec((tm, tn), lambda i, j: (i, j)),
                   pl.BlockSpec((tm, LANE), lambda i, j: (i, 0))],
        out_shape=[jax.ShapeDtypeStruct((t, QKV_COLS), F32),
                   jax.ShapeDtypeStruct((t, LANE), F32)],
        scratch_shapes=[pltpu.VMEM((tm, d), BF16)],
        compiler_params=params,
        name="in_proj_qkv",
    )(x, g.reshape(1, d), sc, sh, cos_t, sin_t, rmask, w_main, w_gb)
    gates = pl.pallas_call(
        _in_proj_gate_kernel,
        grid=(t // tm, GATE_COLS // tn),
        in_specs=x_specs + [pl.BlockSpec((d, tn), lambda i, j: (0, QKV_COLS // tn + j))],
        out_specs=pl.BlockSpec((tm, tn), lambda i, j: (i, j)),
        out_shape=jax.ShapeDtypeStruct((t, GATE_COLS), F32),
        scratch_shapes=[pltpu.VMEM((tm, d), BF16)],
        compiler_params=params,
        name="in_proj_gate",
    )(x, g.reshape(1, d), sc, sh, w_main)
    return qkv, gates, g_b


def _out_proj_kernel(oa_ref, ob_ref, gma_ref, gmb_ref, x_ref, g1_ref, gf_ref, sc_ref, sh_ref,
                     wpa_ref, wpb_ref, wout_ref, wr_ref, br_ref, xo_ref, h_ref, lg_ref):
    pa = jnp.dot(oa_ref[...].astype(BF16), wpa_ref[...], preferred_element_type=F32)
    pb = jnp.dot(ob_ref[...].astype(BF16), wpb_ref[...], preferred_element_type=F32)
    merged = (gma_ref[...] * pa + gmb_ref[...] * pb).astype(BF16)
    mix = jnp.dot(merged, wout_ref[...], preferred_element_type=F32)
    x = x_ref[...] + g1_ref[0] * mix
    xo_ref[...] = x
    y = x * lax.rsqrt(jnp.mean(x * x, axis=-1, keepdims=True) + NORM_EPS) * gf_ref[...]
    h = y * (1.0 + sc_ref[0]) + sh_ref[0]
    h_ref[...] = h
    lg_ref[...] = jnp.dot(h, wr_ref[...], precision=lax.Precision.HIGHEST,
                          preferred_element_type=F32) + br_ref[...]


def _out_proj(o_a, o_b, gates, x, g1, gf, sc, sh, w_pa, w_pb, w_out, w_r, b_r, tm, rows_per_mod):
    t, d = x.shape
    da = o_a.shape[1]
    const = lambda shape: pl.BlockSpec(shape, lambda i: (0,) * len(shape),
                                       pipeline_mode=pl.Buffered(1))
    gate_blk = 0
    return pl.pallas_call(
        _out_proj_kernel,
        grid=(t // tm,),
        in_specs=[pl.BlockSpec((tm, da), lambda i: (i, 0)),
                  pl.BlockSpec((tm, da), lambda i: (i, 0)),
                  pl.BlockSpec((tm, d), lambda i: (i, gate_blk)),
                  pl.BlockSpec((tm, d), lambda i: (i, gate_blk + 1)),
                  pl.BlockSpec((tm, d), lambda i: (i, 0)),
                  _mod_spec(g1, tm, rows_per_mod),
                  const((1, d)),
                  _mod_spec(sc, tm, rows_per_mod), _mod_spec(sh, tm, rows_per_mod),
                  const((da, d)), const((da, d)), const((d, d)), const((d, LANE)), const((1, LANE))],
        out_specs=[pl.BlockSpec((tm, d), lambda i: (i, 0)),
                   pl.BlockSpec((tm, d), lambda i: (i, 0)),
                   pl.BlockSpec((tm, LANE), lambda i: (i, 0))],
        out_shape=[jax.ShapeDtypeStruct((t, d), F32), jax.ShapeDtypeStruct((t, d), F32),
                   jax.ShapeDtypeStruct((t, LANE), F32)],
        compiler_params=pltpu.CompilerParams(dimension_semantics=("arbitrary",),
                                             vmem_limit_bytes=VMEM_LIMIT_BYTES),
        name="out_proj",
    )(o_a, o_b, gates, gates, x, g1, gf.reshape(1, d), sc, sh, w_pa, w_pb, w_out, w_r, b_r)


def _moe_ffn_kernel(blk_e_ref, row_tok_ref, nused_ref, x_hbm, roww_ref, wg_ref, wu_ref, wd_ref,
                    o_ref, xbuf, sem, wg_bf, wu_bf, wd_bf, *, bm):
    i = pl.program_id(0)
    n_used = nused_ref[0]

    def issue(blk, slot):
        for r in range(bm):
            tok = row_tok_ref[blk * bm + r]
            pltpu.make_async_copy(x_hbm.at[pl.ds(tok, 1)], xbuf.at[slot, pl.ds(r, 1)],
                                  sem.at[slot]).start()

    def wait(slot):
        pltpu.make_async_copy(x_hbm.at[pl.ds(0, bm)], xbuf.at[slot], sem.at[slot]).wait()

    slot = i % 2

    @pl.when((i == 0) & (n_used > 0))
    def _():
        issue(0, 0)

    @pl.when(i < n_used)
    def _():
        wait(slot)
        prev = blk_e_ref[jnp.maximum(i - 1, 0)]

        @pl.when((i == 0) | (blk_e_ref[i] != prev))
        def _():
            wg_bf[...] = wg_ref[0, 0].astype(BF16)
            wu_bf[...] = wu_ref[0, 0].astype(BF16)
            wd_bf[...] = wd_ref[0, 0].astype(BF16)

        issue(jnp.minimum(i + 1, n_used - 1), 1 - slot)
        x = xbuf[slot].astype(BF16)
        hg = jnp.dot(x, wg_bf[...], preferred_element_type=F32)
        hu = jnp.dot(x, wu_bf[...], preferred_element_type=F32)
        act = (jax.nn.silu(hg) * hu).astype(BF16)
        y = jnp.dot(act, wd_bf[...], preferred_element_type=F32)
        o_ref[...] = y * roww_ref[...]

        @pl.when(i + 1 == n_used)
        def _():
            wait(1 - slot)

    @pl.when(i >= n_used)
    def _():
        o_ref[...] = jnp.zeros_like(o_ref)


def _moe_ffn(x, blk_expert, row_tok, n_used, row_w, w_gate, w_up, w_down, l, bm):
    n_rows = row_tok.shape[0]
    n_blk = n_rows // bm
    d = x.shape[1]
    de = w_gate.shape[3]
    grid_spec = pltpu.PrefetchScalarGridSpec(
        num_scalar_prefetch=3,
        grid=(n_blk,),
        in_specs=[pl.BlockSpec(memory_space=pl.ANY),
                  pl.BlockSpec((bm, 1), lambda i, be, rt, nu: (i, 0)),
                  pl.BlockSpec((1, 1, d, de), lambda i, be, rt, nu: (l, be[i], 0, 0)),
                  pl.BlockSpec((1, 1, d, de), lambda i, be, rt, nu: (l, be[i], 0, 0)),
                  pl.BlockSpec((1, 1, de, d), lambda i, be, rt, nu: (l, be[i], 0, 0))],
        out_specs=pl.BlockSpec((bm, d), lambda i, be, rt, nu: (i, 0)),
        scratch_shapes=[pltpu.VMEM((2, bm, d), F32),
                        pltpu.SemaphoreType.DMA((2,)),
                        pltpu.VMEM((d, de), BF16), pltpu.VMEM((d, de), BF16),
                        pltpu.VMEM((de, d), BF16)])
    return pl.pallas_call(
        functools.partial(_moe_ffn_kernel, bm=bm),
        grid_spec=grid_spec,
        out_shape=jax.ShapeDtypeStruct((n_rows, d), F32),
        compiler_params=pltpu.CompilerParams(dimension_semantics=("arbitrary",),
                                             vmem_limit_bytes=VMEM_LIMIT_BYTES),
        name="moe_ffn",
    )(blk_expert, row_tok, n_used, x, row_w, w_gate, w_up, w_down)


def _moe_combine_kernel(d0_ref, d1_ref, x_ref, g2_ref, nf_ref, yb_hbm, o_ref, ybuf, sem,
                        *, tm, final_norm):
    i = pl.program_id(0)
    n = pl.num_programs(0)

    def issue(blk, slot):
        for r in range(tm):
            t = blk * tm + r
            pltpu.make_async_copy(yb_hbm.at[pl.ds(d0_ref[t], 1)], ybuf.at[slot, 0, pl.ds(r, 1)],
                                  sem.at[slot]).start()
            pltpu.make_async_copy(yb_hbm.at[pl.ds(d1_ref[t], 1)], ybuf.at[slot, 1, pl.ds(r, 1)],
                                  sem.at[slot]).start()

    def wait(slot):
        pltpu.make_async_copy(yb_hbm.at[pl.ds(0, tm)], ybuf.at[slot, 0], sem.at[slot]).wait()
        pltpu.make_async_copy(yb_hbm.at[pl.ds(0, tm)], ybuf.at[slot, 1], sem.at[slot]).wait()

    slot = i % 2

    @pl.when(i == 0)
    def _():
        issue(0, 0)

    wait(slot)
    issue(jnp.minimum(i + 1, n - 1), 1 - slot)
    y = x_ref[...] + g2_ref[0] * (ybuf[slot, 0] + ybuf[slot, 1])
    if final_norm:
        y = y * lax.rsqrt(jnp.mean(y * y, axis=-1, keepdims=True) + NORM_EPS) * nf_ref[...]
    o_ref[...] = y

    @pl.when(i + 1 == n)
    def _():
        wait(1 - slot)


def _moe_combine(x, g2, norm_final, yb, d0, d1, tm, rows_per_mod, final_norm):
    t, d = x.shape
    grid_spec = pltpu.PrefetchScalarGridSpec(
        num_scalar_prefetch=2,
        grid=(t // tm,),
        in_specs=[pl.BlockSpec((tm, d), lambda i, a, b: (i, 0)),
                  _mod_spec(g2, tm, rows_per_mod),
                  pl.BlockSpec((1, d), lambda i, a, b: (0, 0)),
                  pl.BlockSpec(memory_space=pl.ANY)],
        out_specs=pl.BlockSpec((tm, d), lambda i, a, b: (i, 0)),
        scratch_shapes=[pltpu.VMEM((2, 2, tm, d), F32), pltpu.SemaphoreType.DMA((2,))])
    return pl.pallas_call(
        functools.partial(_moe_combine_kernel, tm=tm, final_norm=final_norm),
        grid_spec=grid_spec,
        out_shape=jax.ShapeDtypeStruct((t, d), F32),
        compiler_params=pltpu.CompilerParams(dimension_semantics=("arbitrary",),
                                             vmem_limit_bytes=VMEM_LIMIT_BYTES),
        name="moe_combine",
    )(d0, d1, x, g2, norm_final.reshape(1, d), yb)


def _route(logits, bm):
    t = logits.shape[0]
    a = t * EXPERT_TOPK
    g_prob = jax.nn.softmax(logits[:, :N_GROUPS], axis=-1)
    g_idx = jnp.argmax(g_prob, axis=-1, keepdims=True).astype(jnp.int32)
    g_w = jnp.max(g_prob, axis=-1, keepdims=True)
    e_logit = logits[:, N_GROUPS:N_GROUPS + N_EXPERTS].reshape(t, N_GROUPS, EXPERTS_PER_GROUP)
    e_logit = jnp.take_along_axis(e_logit, g_idx[:, :, None], axis=1)[:, 0]
    e_ids = jnp.arange(EXPERTS_PER_GROUP, dtype=jnp.int32)[None, :]
    i1 = jnp.argmax(e_logit, axis=-1, keepdims=True).astype(jnp.int32)
    rest = jnp.where(e_ids == i1, -jnp.inf, e_logit)
    i2 = jnp.argmax(rest, axis=-1, keepdims=True).astype(jnp.int32)
    e_val = jnp.concatenate([jnp.max(e_logit, axis=-1, keepdims=True),
                             jnp.max(rest, axis=-1, keepdims=True)], axis=-1)
    e_idx = jnp.concatenate([i1, i2], axis=-1)
    weights = (g_w * jax.nn.softmax(e_val, axis=-1)).reshape(a)
    flat_e = (g_idx * EXPERTS_PER_GROUP + e_idx).reshape(a)
    order = jnp.argsort(flat_e).astype(jnp.int32)
    e_sorted = flat_e[order]
    expert_ids = jnp.arange(N_EXPERTS, dtype=jnp.int32)
    counts = jnp.sum((flat_e[:, None] == expert_ids[None, :]).astype(jnp.int32), axis=0)
    padded = (counts + bm - 1) // bm * bm
    pad_end = jnp.cumsum(padded)
    pad_start = pad_end - padded
    start = jnp.cumsum(counts) - counts
    dest_sorted = pad_start[e_sorted] + jnp.arange(a, dtype=jnp.int32) - start[e_sorted]
    n_blk = -(-(a + N_EXPERTS * (bm - 1)) // bm)
    n_rows = n_blk * bm
    row_asg = jnp.full((n_rows,), -1, jnp.int32).at[dest_sorted].set(order)
    row_valid = row_asg >= 0
    row_asg = jnp.maximum(row_asg, 0)
    row_tok = row_asg // EXPERT_TOPK
    row_w = jnp.where(row_valid, weights[row_asg], 0.0).reshape(n_rows, 1)
    blk_start = jnp.arange(n_blk, dtype=jnp.int32) * bm
    blk_expert = jnp.minimum(jnp.sum((pad_end[None, :] <= blk_start[:, None]).astype(jnp.int32), axis=1),
                             N_EXPERTS - 1).astype(jnp.int32)
    n_used = (pad_end[-1] // bm).astype(jnp.int32).reshape(1)
    dest = jnp.zeros((a,), jnp.int32).at[order].set(dest_sorted).reshape(t, EXPERT_TOPK)
    return blk_expert, row_tok, n_used, row_w, dest[:, 0], dest[:, 1]


ATT_SCALE = HEAD_DIM ** -0.5
_NT = (((1,), (1,)), ((), ()))


KV_COLS = 2 * NSA_KV_HEADS * HEAD_DIM


def _compress_kernel(x0_ref, x1_ref, x2_ref, x3_ref, pos_ref, w1_ref, b1_ref, w2_ref, b2_ref,
                     o_ref, *, nch):
    x_refs = (x0_ref, x1_ref, x2_ref, x3_ref)
    for kv in range(2):
        for h in range(NSA_KV_HEADS):
            c = kv * NSA_KV_HEADS + h
            first = jnp.zeros((nch, CMP_HIDDEN), F32)
            second = jnp.zeros((nch, CMP_HIDDEN), F32)
            for r in range(CMP_STRIDE):
                xr = x_refs[c][0, pl.ds(r, nch, stride=CMP_STRIDE), :]
                r2 = CMP_STRIDE + r
                first += jnp.dot((xr + pos_ref[kv, r:r + 1, :]).astype(BF16),
                                 w1_ref[kv, r * HEAD_DIM:(r + 1) * HEAD_DIM, :].astype(BF16),
                                 preferred_element_type=F32)
                second += jnp.dot((xr + pos_ref[kv, r2:r2 + 1, :]).astype(BF16),
                                  w1_ref[kv, r2 * HEAD_DIM:(r2 + 1) * HEAD_DIM, :].astype(BF16),
                                  preferred_element_type=F32)
            pre = first + pltpu.roll(second, nch - 1, axis=0) + b1_ref[kv:kv + 1, :]
            hdn = jax.nn.gelu(pre)
            o_ref[0, c] = (jnp.dot(hdn.astype(BF16), w2_ref[kv].astype(BF16),
                                   preferred_element_type=F32) + b2_ref[kv:kv + 1, :])


def _compress_prompt(proj, pos, w1, b1, w2, b2):
    b, s, _ = proj.shape
    nch = s // CMP_STRIDE
    full = lambda shape: pl.BlockSpec(shape, lambda bi: (0,) * len(shape))
    return pl.pallas_call(
        functools.partial(_compress_kernel, nch=nch),
        grid=(b,),
        in_specs=[pl.BlockSpec((1, s, HEAD_DIM), functools.partial(lambda bi, c: (bi, 0, COL_CMP + c), c=c))
                  for c in range(2 * NSA_KV_HEADS)] + [
                  full(pos.shape), full(w1.shape), full(b1.shape), full(w2.shape), full(b2.shape)],
        out_specs=pl.BlockSpec((1, 2 * NSA_KV_HEADS, nch, HEAD_DIM), lambda bi: (bi, 0, 0, 0)),
        out_shape=jax.ShapeDtypeStruct((b, 2 * NSA_KV_HEADS, nch, HEAD_DIM), F32),
        compiler_params=pltpu.CompilerParams(dimension_semantics=("arbitrary",),
                                             vmem_limit_bytes=VMEM_LIMIT_BYTES),
        name="nsa_compress",
    )(proj, proj, proj, proj, pos, w1, b1, w2, b2)


NSA_TQ = 256
NSA_KB = 256


def _flash_update_t(qb, kb, vt, mask, carry):
    m, l, acc = carry
    s = lax.dot_general(kb, qb, _NT, preferred_element_type=F32) * ATT_SCALE
    s = jnp.where(mask, s, NEG_INF)
    m_new = jnp.maximum(m, jnp.max(s, axis=0, keepdims=True))
    alpha = jnp.exp(m - m_new)
    p = jnp.exp(s - m_new)
    l = alpha * l + jnp.sum(p, axis=0, keepdims=True)
    acc = alpha * acc + jnp.dot(vt, p.astype(BF16), preferred_element_type=F32)
    return m_new, l, acc


def _flash_init_t(tq):
    return (jnp.full((1, tq), -jnp.inf, F32), jnp.zeros((1, tq), F32),
            jnp.zeros((HEAD_DIM, tq), F32))


def _stage_kv(k_ref, v_ref, kb_scr, vt_scr, nblk, blk):
    for j in range(nblk):
        kb_scr[j] = k_ref[0, j * blk:(j + 1) * blk, :].astype(BF16)
        vt_scr[j] = v_ref[0, j * blk:(j + 1) * blk, :].T.astype(BF16)


def _moba_prompt_t_kernel(q_ref, k_ref, v_ref, o_ref, kmean_scr, sel_scr, kb_scr, vt_scr, *, nb):
    i = pl.program_id(2)
    tq = MOBA_BLOCK
    nbp = kmean_scr.shape[0]

    @pl.when(i == 0)
    def _():
        kmean_scr[...] = jnp.zeros_like(kmean_scr)
        for j in range(nb):
            kmean_scr[j:j + 1, :] = jnp.mean(k_ref[0, j * tq:(j + 1) * tq, :], axis=0, keepdims=True)
        _stage_kv(k_ref, v_ref, kb_scr, vt_scr, nb, tq)

    wq = MOBA_GROUP * tq
    blk_row = lax.broadcasted_iota(jnp.int32, (nbp, tq), 0)
    qbs = []
    for g in range(MOBA_GROUP):
        qg = q_ref[0, :, g * HEAD_DIM:(g + 1) * HEAD_DIM]
        gate = lax.dot_general(kmean_scr[...], qg, _NT, precision=lax.Precision.HIGHEST,
                               preferred_element_type=F32)
        gate = jnp.where(blk_row < i, gate, NEG_INF)
        cnt = jnp.zeros((nbp, tq), F32)
        for j in range(nb):
            gj = gate[j:j + 1, :]
            cnt = cnt + jnp.where((gj > gate) | ((gj == gate) & (j < blk_row)), 1.0, 0.0)
        sel_scr[:, g * tq:(g + 1) * tq] = jnp.where((cnt < MOBA_TOPK) & (blk_row < i), 1.0, 0.0)
        qbs.append(qg.astype(BF16))
    qb = jnp.concatenate(qbs, axis=0)

    key_row = lax.broadcasted_iota(jnp.int32, (tq, wq), 0)
    q_idx = lax.broadcasted_iota(jnp.int32, (tq, wq), 1) & (tq - 1)
    carry = _flash_update_t(qb, kb_scr[i], vt_scr[i], key_row <= q_idx, _flash_init_t(wq))

    def body(j, carry):
        chosen = jnp.broadcast_to(sel_scr[pl.ds(j, 1), :], (tq, wq)) > 0.5
        return _flash_update_t(qb, kb_scr[j], vt_scr[j], chosen, carry)

    _, l, acc = lax.fori_loop(0, i, body, carry)
    o_t = acc / l
    for g in range(MOBA_GROUP):
        o_ref[0, :, g * HEAD_DIM:(g + 1) * HEAD_DIM] = o_t[:, g * tq:(g + 1) * tq].T


def _moba_prompt_t(proj):
    b, s, _ = proj.shape
    nb = s // MOBA_BLOCK
    assert s % MOBA_BLOCK == 0
    nbp = _round_up(nb, 8)
    gw = MOBA_GROUP * HEAD_DIM
    return pl.pallas_call(
        functools.partial(_moba_prompt_t_kernel, nb=nb),
        grid=(b, MOBA_KV_HEADS, nb),
        in_specs=[pl.BlockSpec((1, MOBA_BLOCK, gw), lambda bi, h, i: (bi, i, COL_QA // MOBA_GROUP + h)),
                  pl.BlockSpec((1, s, HEAD_DIM), lambda bi, h, i: (bi, 0, COL_KA + h)),
                  pl.BlockSpec((1, s, HEAD_DIM), lambda bi, h, i: (bi, 0, COL_VA + h))],
        out_specs=pl.BlockSpec((1, MOBA_BLOCK, gw), lambda bi, h, i: (bi, i, h)),
        out_shape=jax.ShapeDtypeStruct((b, s, MOBA_HEADS * HEAD_DIM), F32),
        scratch_shapes=[pltpu.VMEM((nbp, HEAD_DIM), F32),
                        pltpu.VMEM((nbp, MOBA_GROUP * MOBA_BLOCK), F32),
                        pltpu.VMEM((nb, MOBA_BLOCK, HEAD_DIM), BF16),
                        pltpu.VMEM((nb, HEAD_DIM, MOBA_BLOCK), BF16)],
        compiler_params=pltpu.CompilerParams(
            dimension_semantics=("arbitrary", "arbitrary", "arbitrary"),
            vmem_limit_bytes=VMEM_LIMIT_BYTES),
        name="moba_prompt",
    )(proj, proj, proj)


def _nsa_prompt_t_kernel(q_ref, kc_ref, vc_ref, ks_ref, vs_ref, kw_ref, vw_ref, gb_ref, o_ref,
                         ksb_scr, vst_scr, kwb_scr, vwt_scr, kcb_scr, vct_scr, gt_scr, sel_scr,
                         o_scr, *, s_len):
    kvh = pl.program_id(1)
    i = pl.program_id(2)
    tq, kb = NSA_TQ, NSA_KB
    n_cmp = s_len // CMP_STRIDE - CMP_LEN // CMP_STRIDE + 1
    n_slc = s_len // SLC_BLOCK
    n_top = min(SLC_TOPN, n_slc)
    nkb = s_len // kb
    n_rows = sel_scr.shape[0]
    per_kb = kb // SLC_BLOCK
    dh = HEAD_DIM

    @pl.when(i == 0)
    def _():
        _stage_kv(ks_ref, vs_ref, ksb_scr, vst_scr, nkb, kb)
        _stage_kv(kw_ref, vw_ref, kwb_scr, vwt_scr, nkb, kb)
        kcb_scr[...] = kc_ref[0, 0].astype(BF16)
        vct_scr[...] = vc_ref[0, 0].T.astype(BF16)

    gt_scr[...] = jax.nn.sigmoid(gb_ref[0]).T

    def gate_row(branch, g):
        return gt_scr[pl.ds(branch * NSA_HEADS + kvh * NSA_GROUP + g, 1), :]

    n_sub = lax.broadcasted_iota(jnp.int32, (LANE, tq), 0)
    t_lane = i * tq + lax.broadcasted_iota(jnp.int32, (LANE, tq), 1)
    avail = (n_sub * CMP_STRIDE + (CMP_LEN - 1) <= t_lane) & (n_sub < n_cmp)
    jj = lax.broadcasted_iota(jnp.int32, (LANE, LANE), 0)
    nn = lax.broadcasted_iota(jnp.int32, (LANE, LANE), 1)
    sel_map_t = jnp.where((nn * CMP_STRIDE < jj * SLC_BLOCK + SLC_BLOCK)
                          & (nn * CMP_STRIDE + CMP_LEN > jj * SLC_BLOCK)
                          & (nn < n_cmp) & (jj < n_slc), 1.0, 0.0)
    wq = NSA_GROUP * tq

    def gate_wide(branch):
        return jnp.concatenate([jnp.broadcast_to(gate_row(branch, g), (dh, tq))
                                for g in range(NSA_GROUP)], axis=1)

    qb = jnp.concatenate([q_ref[0, :, g * dh:(g + 1) * dh].astype(BF16)
                          for g in range(NSA_GROUP)], axis=0)
    avail_w = jnp.concatenate([avail] * NSA_GROUP, axis=1)
    s_c = lax.dot_general(kcb_scr[...], qb, _NT, preferred_element_type=F32) * ATT_SCALE
    s_c = jnp.where(avail_w, s_c, NEG_INF)
    e = jnp.where(avail_w, jnp.exp(s_c - jnp.max(s_c, axis=0, keepdims=True)), 0.0)
    den = jnp.sum(e, axis=0, keepdims=True)
    p_c = e * jnp.where(den > 0.0, 1.0 / den, 0.0)
    o_scr[...] = gate_wide(0) * jnp.dot(vct_scr[...], p_c.astype(BF16), preferred_element_type=F32)
    p_sum = sum(p_c[:, g * tq:(g + 1) * tq] for g in range(NSA_GROUP))
    imp = jnp.dot(sel_map_t, p_sum, precision=lax.Precision.HIGHEST, preferred_element_type=F32)

    imp = imp[:n_rows]
    jio = lax.broadcasted_iota(jnp.int32, (n_rows, tq), 0)
    cur = (i * tq + lax.broadcasted_iota(jnp.int32, (n_rows, tq), 1)) // SLC_BLOCK
    forced = (jio == 0) | (jio == cur) | (jio == cur - 1)
    imp = jnp.where(forced, BIG, imp)
    imp = jnp.where(jio > cur, NEG_INF, imp)
    imp = jnp.where(jio < n_slc, imp, -jnp.inf)
    cnt = jnp.zeros((n_rows, tq), F32)
    for r in range(n_slc):
        vr = imp[r:r + 1, :]
        cnt = cnt + jnp.where((vr > imp) | ((vr == imp) & (r < jio)), 1.0, 0.0)
    sel_scr[...] = jnp.where((cnt < n_top) & (jio < n_slc), 1.0, 0.0)

    def slc_mask(jb):
        rows = [jnp.broadcast_to(sel_scr[pl.ds(jb * per_kb + a, 1), :], (SLC_BLOCK, tq))
                for a in range(per_kb)]
        one = jnp.concatenate(rows, axis=0)
        return jnp.concatenate([one] * NSA_GROUP, axis=1) > 0.5

    key_row = lax.broadcasted_iota(jnp.int32, (kb, wq), 0)
    q_idx = lax.broadcasted_iota(jnp.int32, (kb, wq), 1) & (tq - 1)
    causal = key_row <= q_idx

    carry = _flash_update_t(qb, ksb_scr[i], vst_scr[i], slc_mask(i) & causal, _flash_init_t(wq))

    def slc_body(j, carry):
        return _flash_update_t(qb, ksb_scr[j], vst_scr[j], slc_mask(j), carry)

    _, l, acc = lax.fori_loop(0, i, slc_body, carry)
    o_scr[...] += gate_wide(1) * (acc / l)

    carry = _flash_update_t(qb, kwb_scr[i], vwt_scr[i], causal, _flash_init_t(wq))

    def win_body(j, carry):
        mask = (j * kb + key_row) > (i * tq + q_idx - WINDOW)
        return _flash_update_t(qb, kwb_scr[j], vwt_scr[j], mask, carry)

    _, l, acc = lax.fori_loop(jnp.maximum(i - WINDOW // kb, 0), i, win_body, carry)
    o_t = o_scr[...] + gate_wide(2) * (acc / l)
    for g in range(NSA_GROUP):
        o_ref[0, :, g * dh:(g + 1) * dh] = o_t[:, g * tq:(g + 1) * tq].T


def _nsa_prompt_t(proj, kvc, g_b):
    b, s, _ = proj.shape
    assert s % NSA_TQ == 0 and NSA_TQ == NSA_KB and s // CMP_STRIDE == LANE
    assert NSA_KB % SLC_BLOCK == 0
    gw = NSA_GROUP * HEAD_DIM
    nkh = NSA_KV_HEADS
    nkb = s // NSA_KB
    n_rows = _round_up(s // SLC_BLOCK, 8)
    seq = lambda col: pl.BlockSpec((1, s, HEAD_DIM), lambda bi, h, i: (bi, 0, col + h))
    kv_scr = [pltpu.VMEM((nkb, NSA_KB, HEAD_DIM), BF16), pltpu.VMEM((nkb, HEAD_DIM, NSA_KB), BF16)]
    return pl.pallas_call(
        functools.partial(_nsa_prompt_t_kernel, s_len=s),
        grid=(b, nkh, s // NSA_TQ),
        in_specs=[pl.BlockSpec((1, NSA_TQ, gw), lambda bi, h, i: (bi, i, COL_QB // NSA_GROUP + h)),
                  pl.BlockSpec((1, 1, LANE, HEAD_DIM), lambda bi, h, i: (bi, h, 0, 0)),
                  pl.BlockSpec((1, 1, LANE, HEAD_DIM), lambda bi, h, i: (bi, nkh + h, 0, 0)),
                  seq(COL_SLC), seq(COL_SLC + nkh), seq(COL_WIN), seq(COL_WIN + nkh),
                  pl.BlockSpec((1, NSA_TQ, LANE), lambda bi, h, i: (bi, i, 0))],
        out_specs=pl.BlockSpec((1, NSA_TQ, gw), lambda bi, h, i: (bi, i, h)),
        out_shape=jax.ShapeDtypeStruct((b, s, NSA_HEADS * HEAD_DIM), F32),
        scratch_shapes=kv_scr + kv_scr + [
            pltpu.VMEM((LANE, HEAD_DIM), BF16), pltpu.VMEM((HEAD_DIM, LANE), BF16),
            pltpu.VMEM((LANE, NSA_TQ), F32), pltpu.VMEM((n_rows, NSA_TQ), F32),
            pltpu.VMEM((HEAD_DIM, NSA_GROUP * NSA_TQ), F32)],
        compiler_params=pltpu.CompilerParams(
            dimension_semantics=("arbitrary", "arbitrary", "arbitrary"),
            vmem_limit_bytes=VMEM_LIMIT_BYTES),
        name="nsa_prompt",
    )(proj, kvc, kvc, proj, proj, proj, proj, g_b)


PAGE_SIZE = 128
PAGE_ROWS = PAGE_SIZE * 2 * NSA_KV_HEADS
PAGES_PER_STEP = 16
QROWS = 8
NQ = NSA_GROUP * QROWS


def _page_specs(n, l):
    def spec(k):
        return pl.BlockSpec((1, 1, PAGE_ROWS, HEAD_DIM), lambda b, j, pt: (l, pt[b, j * n + k], 0, 0))
    return [spec(k) for k in range(n)]


def _page_kv(ref, kv, h, rows=PAGE_SIZE):
    return ref[0, 0, pl.ds(kv * NSA_KV_HEADS + h, rows, stride=2 * NSA_KV_HEADS), :]


def _first_max_onehot(work, lane_f):
    mx = jnp.max(work, axis=-1, keepdims=True)
    first = jnp.min(jnp.where(work == mx, lane_f, 1e9), axis=-1, keepdims=True)
    return jnp.where(lane_f == first, 1.0, 0.0)


def _moba_sample_kernel(pt_ref, q_ref, knew_ref, vnew_ref, *rest, n_new, nb_past):
    pages = rest[:PAGES_PER_STEP]
    o_ref, m_scr, l_scr, acc_scr, kmean_scr = rest[PAGES_PER_STEP:]
    j = pl.program_id(1)
    nj = pl.num_programs(1)
    lane = lax.broadcasted_iota(jnp.int32, (NQ, LANE), 1)

    @pl.when(j == 0)
    def _():
        m_scr[...] = jnp.full_like(m_scr, NEG_INF)
        l_scr[...] = jnp.zeros_like(l_scr)
        kmean_scr[...] = jnp.zeros_like(kmean_scr)

    bps = PAGES_PER_STEP * PAGE_SIZE // MOBA_BLOCK
    for h in range(MOBA_KV_HEADS):
        qb = q_ref[0, h].astype(BF16)
        k_all = jnp.concatenate([_page_kv(pg, 0, h) for pg in pages], axis=0)
        v_all = jnp.concatenate([_page_kv(pg, 1, h) for pg in pages], axis=0).astype(BF16)
        s = lax.dot_general(qb, k_all.astype(BF16), _NT, preferred_element_type=F32) * ATT_SCALE
        blocks = [slice(u * MOBA_BLOCK, (u + 1) * MOBA_BLOCK) for u in range(bps)]
        m_bs = [jnp.max(s[:, sl], axis=-1, keepdims=True) for sl in blocks]
        p = jnp.exp(s - jnp.concatenate([jnp.broadcast_to(m_b, (NQ, MOBA_BLOCK)) for m_b in m_bs],
                                        axis=1))
        m_new, l_new = m_scr[h], l_scr[h]
        for u, sl in enumerate(blocks):
            blk = j * bps + u
            m_new = jnp.where(lane == blk, m_bs[u], m_new)
            l_new = jnp.where(lane == blk, jnp.sum(p[:, sl], axis=-1, keepdims=True), l_new)
            acc_scr[h, blk] = jnp.dot(p[:, sl].astype(BF16), v_all[sl], preferred_element_type=F32)
            kmean_scr[h, pl.ds(blk, 1), :] = (jnp.sum(k_all[sl], axis=0, keepdims=True)
                                             * (1.0 / MOBA_BLOCK))
        m_scr[h] = m_new
        l_scr[h] = l_new

    @pl.when(j == nj - 1)
    def _():
        lane_f = lane.astype(F32)
        trow = lax.broadcasted_iota(jnp.int32, (NQ, LANE), 0) & (QROWS - 1)
        for h in range(MOBA_KV_HEADS):
            q = q_ref[0, h]
            qb = q.astype(BF16)
            s_own = lax.dot_general(qb, knew_ref[0, h].astype(BF16), _NT,
                                    preferred_element_type=F32) * ATT_SCALE
            s_own = jnp.where((lane <= trow) & (lane < n_new), s_own, NEG_INF)
            m_own = jnp.max(s_own, axis=-1, keepdims=True)
            p_own = jnp.exp(s_own - m_own)
            l_own = jnp.sum(p_own, axis=-1, keepdims=True)
            acc_own = jnp.dot(p_own.astype(BF16), vnew_ref[0, h].astype(BF16),
                              preferred_element_type=F32)
            gate = lax.dot_general(q, kmean_scr[h], _NT, precision=lax.Precision.HIGHEST,
                                   preferred_element_type=F32)
            work = jnp.where(lane < nb_past, gate, -jnp.inf)
            sel = jnp.zeros((NQ, LANE), F32)
            for _ in range(min(MOBA_TOPK, nb_past)):
                pick = _first_max_onehot(work, lane_f)
                sel = jnp.maximum(sel, pick)
                work = jnp.where(pick > 0.5, -jnp.inf, work)
            chosen = sel > 0.5
            m_all = m_scr[h]
            m_tot = jnp.maximum(m_own, jnp.max(jnp.where(chosen, m_all, NEG_INF), axis=-1,
                                               keepdims=True))
            w = jnp.where(chosen, jnp.exp(m_all - m_tot), 0.0)
            w_own = jnp.exp(m_own - m_tot)
            l_tot = w_own * l_own + jnp.sum(w * l_scr[h], axis=-1, keepdims=True)

            def merge(blk, acc, w=w, h=h):
                wcol = jnp.sum(jnp.where(lane == blk, w, 0.0), axis=-1, keepdims=True)
                return acc + wcol * acc_scr[h, blk]

            acc = lax.fori_loop(0, nb_past, merge, w_own * acc_own)
            o_ref[0, h] = acc / l_tot


def _moba_sample(q, k_new, v_new, pool, page_table, l, n_new):
    b = q.shape[0]
    n_pages = page_table.shape[1]
    assert n_pages % PAGES_PER_STEP == 0 and (n_pages * PAGE_SIZE) % MOBA_BLOCK == 0
    nb_past = n_pages * PAGE_SIZE // MOBA_BLOCK
    assert nb_past <= LANE and n_new <= MOBA_BLOCK
    per_b = lambda shape: pl.BlockSpec(shape, lambda bi, j, pt: (bi,) + (0,) * (len(shape) - 1))
    grid_spec = pltpu.PrefetchScalarGridSpec(
        num_scalar_prefetch=1,
        grid=(b, n_pages // PAGES_PER_STEP),
        in_specs=[per_b((1, MOBA_KV_HEADS, NQ, HEAD_DIM)),
                  per_b((1, MOBA_KV_HEADS, LANE, HEAD_DIM)),
                  per_b((1, MOBA_KV_HEADS, LANE, HEAD_DIM))] + _page_specs(PAGES_PER_STEP, l),
        out_specs=per_b((1, MOBA_KV_HEADS, NQ, HEAD_DIM)),
        scratch_shapes=[pltpu.VMEM((MOBA_KV_HEADS, NQ, LANE), F32),
                        pltpu.VMEM((MOBA_KV_HEADS, NQ, LANE), F32),
                        pltpu.VMEM((MOBA_KV_HEADS, nb_past, NQ, HEAD_DIM), F32),
                        pltpu.VMEM((MOBA_KV_HEADS, LANE, HEAD_DIM), F32)])
    return pl.pallas_call(
        functools.partial(_moba_sample_kernel, n_new=n_new, nb_past=nb_past),
        grid_spec=grid_spec,
        out_shape=jax.ShapeDtypeStruct(q.shape, F32),
        compiler_params=pltpu.CompilerParams(dimension_semantics=("arbitrary", "arbitrary"),
                                             vmem_limit_bytes=VMEM_LIMIT_BYTES),
        name="moba_sample",
    )(page_table, q, k_new, v_new, *([pool] * PAGES_PER_STEP))


def _compress_sample_kernel(pt_ref, pos_ref, w1_ref, b1_ref, w2_ref, b2_ref, *rest):
    pages = rest[:PAGES_PER_STEP + 1]
    o_ref = rest[PAGES_PER_STEP + 1]
    cpp = PAGE_SIZE // CMP_STRIDE
    nch = PAGES_PER_STEP * cpp
    half = CMP_STRIDE * HEAD_DIM
    n_col = 2 * NSA_KV_HEADS
    ys = [jnp.swapaxes(pg[0, 0].reshape(cpp, CMP_STRIDE * n_col, HEAD_DIM), 0, 1) for pg in pages]
    for kv in range(2):
        pos_first = jnp.concatenate([pos_ref[kv, r:r + 1, :] for r in range(CMP_STRIDE)], axis=1)
        pos_second = jnp.concatenate([pos_ref[kv, r:r + 1, :] for r in range(CMP_STRIDE, CMP_LEN)],
                                     axis=1)
        w_first = w1_ref[kv, :half, :].astype(BF16)
        w_second = w1_ref[kv, half:, :].astype(BF16)
        w2 = w2_ref[kv].astype(BF16)
        for h in range(NSA_KV_HEADS):
            c = kv * NSA_KV_HEADS + h
            x = jnp.concatenate(
                [jnp.concatenate([y[r * n_col + c] for y in ys], axis=0)
                 for r in range(CMP_STRIDE)], axis=1)
            first = jnp.dot((x[:nch] + pos_first).astype(BF16), w_first, preferred_element_type=F32)
            second = jnp.dot((x + pos_second).astype(BF16), w_second, preferred_element_type=F32)
            pre = first + pltpu.roll(second, nch + cpp - 1, axis=0)[:nch] + b1_ref[kv:kv + 1, :]
            hdn = jax.nn.gelu(pre)
            o_ref[0, c] = (jnp.dot(hdn.astype(BF16), w2, preferred_element_type=F32)
                           + b2_ref[kv:kv + 1, :])


def _compress_sample(pool, page_table, l, pos, w1, b1, w2, b2):
    b, n_pages = page_table.shape
    nch = PAGES_PER_STEP * PAGE_SIZE // CMP_STRIDE
    full = lambda shape: pl.BlockSpec(shape, lambda bi, j, pt: (0,) * len(shape))
    next_page = pl.BlockSpec(
        (1, 1, PAGE_ROWS, HEAD_DIM),
        lambda bi, j, pt: (l, pt[bi, jnp.minimum((j + 1) * PAGES_PER_STEP, n_pages - 1)], 0, 0))
    grid_spec = pltpu.PrefetchScalarGridSpec(
        num_scalar_prefetch=1,
        grid=(b, n_pages // PAGES_PER_STEP),
        in_specs=[full(pos.shape), full(w1.shape), full(b1.shape), full(w2.shape), full(b2.shape)]
        + _page_specs(PAGES_PER_STEP, l) + [next_page],
        out_specs=pl.BlockSpec((1, 2 * NSA_KV_HEADS, nch, HEAD_DIM), lambda bi, j, pt: (bi, 0, j, 0)))
    return pl.pallas_call(
        _compress_sample_kernel,
        grid_spec=grid_spec,
        out_shape=jax.ShapeDtypeStruct(
            (b, 2 * NSA_KV_HEADS, n_pages * PAGE_SIZE // CMP_STRIDE, HEAD_DIM), F32),
        compiler_params=pltpu.CompilerParams(dimension_semantics=("arbitrary", "arbitrary"),
                                             vmem_limit_bytes=VMEM_LIMIT_BYTES),
        name="nsa_compress_sample",
    )(page_table, pos, w1, b1, w2, b2, *([pool] * (PAGES_PER_STEP + 1)))


def _nsa_cmp_sample_kernel(q_ref, kc_ref, vc_ref, oc_ref, sel_ref, *, q0, n_cmp, n_slc, n_lanes):
    ncp = kc_ref.shape[2]
    col = lax.broadcasted_iota(jnp.int32, (NQ, ncp), 1)
    trow = lax.broadcasted_iota(jnp.int32, (NQ, ncp), 0) & (QROWS - 1)
    avail = (col * CMP_STRIDE + (CMP_LEN - 1) <= q0 + trow) & (col < n_cmp)
    nn = lax.broadcasted_iota(jnp.int32, (ncp, n_lanes), 0)
    jj = lax.broadcasted_iota(jnp.int32, (ncp, n_lanes), 1)
    sel_map = jnp.where((nn * CMP_STRIDE < jj * SLC_BLOCK + SLC_BLOCK)
                        & (nn * CMP_STRIDE + CMP_LEN > jj * SLC_BLOCK)
                        & (nn < n_cmp) & (jj < n_slc), 1.0, 0.0)
    lane = lax.broadcasted_iota(jnp.int32, (QROWS, n_lanes), 1)
    lane_f = lane.astype(F32)
    cur = (q0 + lax.broadcasted_iota(jnp.int32, (QROWS, n_lanes), 0)) // SLC_BLOCK
    for h in range(NSA_KV_HEADS):
        qb = q_ref[0, h].astype(BF16)
        s = lax.dot_general(qb, kc_ref[0, h].astype(BF16), _NT,
                            preferred_element_type=F32) * ATT_SCALE
        s = jnp.where(avail, s, NEG_INF)
        e = jnp.where(avail, jnp.exp(s - jnp.max(s, axis=-1, keepdims=True)), 0.0)
        den = jnp.sum(e, axis=-1, keepdims=True)
        p = e * jnp.where(den > 0.0, 1.0 / den, 0.0)
        oc_ref[0, h] = jnp.dot(p.astype(BF16), vc_ref[0, h].astype(BF16),
                               preferred_element_type=F32)
        pg = sum(p[g * QROWS:(g + 1) * QROWS] for g in range(NSA_GROUP))
        imp = jnp.dot(pg, sel_map, precision=lax.Precision.HIGHEST, preferred_element_type=F32)
        forced = (lane == 0) | (lane == cur) | (lane == cur - 1)
        imp = jnp.where(forced, BIG, imp)
        imp = jnp.where(lane > cur, NEG_INF, imp)
        work = jnp.where(lane < n_slc, imp, -jnp.inf)
        sel = jnp.zeros((QROWS, n_lanes), F32)
        for _ in range(min(SLC_TOPN, n_slc)):
            pick = _first_max_onehot(work, lane_f)
            sel = jnp.maximum(sel, pick)
            work = jnp.where(pick > 0.5, -jnp.inf, work)
        sel_ref[0, h] = sel


def _nsa_cmp_sample(q, kvc, q0, n_new):
    b = q.shape[0]
    ncp = kvc.shape[2]
    n_cmp = ncp - CMP_LEN // CMP_STRIDE + 1
    n_slc = -(-(q0 + n_new) // SLC_BLOCK)
    n_lanes = _round_up(n_slc, LANE)
    nkh = NSA_KV_HEADS
    return pl.pallas_call(
        functools.partial(_nsa_cmp_sample_kernel, q0=q0, n_cmp=n_cmp, n_slc=n_slc, n_lanes=n_lanes),
        grid=(b,),
        in_specs=[pl.BlockSpec((1, nkh, NQ, HEAD_DIM), lambda bi: (bi, 0, 0, 0)),
                  pl.BlockSpec((1, nkh, ncp, HEAD_DIM), lambda bi: (bi, 0, 0, 0)),
                  pl.BlockSpec((1, nkh, ncp, HEAD_DIM), lambda bi: (bi, 1, 0, 0))],
        out_specs=[pl.BlockSpec((1, nkh, NQ, HEAD_DIM), lambda bi: (bi, 0, 0, 0)),
                   pl.BlockSpec((1, nkh, QROWS, n_lanes), lambda bi: (bi, 0, 0, 0))],
        out_shape=[jax.ShapeDtypeStruct(q.shape, F32),
                   jax.ShapeDtypeStruct((b, nkh, QROWS, n_lanes), F32)],
        compiler_params=pltpu.CompilerParams(dimension_semantics=("arbitrary",),
                                             vmem_limit_bytes=VMEM_LIMIT_BYTES),
        name="nsa_cmp_sample",
    )(q, kvc, kvc)


def _nsa_slc_sample_kernel(pt_ref, q_ref, sel_ref, oc_ref, g_ref, ksn_ref, vsn_ref, kwn_ref,
                           vwn_ref, win_ref, *rest, n_new, n_win, cur_blk):
    pages = rest[:PAGES_PER_STEP]
    o_ref, m_scr, l_scr, acc_scr, ow_scr = rest[PAGES_PER_STEP:]
    j = pl.program_id(1)
    nj = pl.num_programs(1)
    n_lanes = sel_ref.shape[3]
    lane = lax.broadcasted_iota(jnp.int32, (NQ, LANE), 1)
    trow = lax.broadcasted_iota(jnp.int32, (NQ, LANE), 0) & (QROWS - 1)
    new_ok = (lane <= trow) & (lane < n_new)

    @pl.when(j == 0)
    def _():
        for h in range(NSA_KV_HEADS):
            qb = q_ref[0, h].astype(BF16)
            sel_h = jnp.concatenate([sel_ref[0, h]] * NSA_GROUP, axis=0)
            lane_s = lax.broadcasted_iota(jnp.int32, (NQ, n_lanes), 1)
            sel_cur = jnp.sum(jnp.where(lane_s == cur_blk, sel_h, 0.0), axis=-1,
                              keepdims=True) > 0.5
            s = lax.dot_general(qb, ksn_ref[0, h].astype(BF16), _NT,
                                preferred_element_type=F32) * ATT_SCALE
            s = jnp.where(new_ok & sel_cur, s, NEG_INF)
            m = jnp.max(s, axis=-1, keepdims=True)
            p = jnp.exp(s - m)
            m_scr[h] = m
            l_scr[h] = jnp.sum(p, axis=-1, keepdims=True)
            acc_scr[h] = jnp.dot(p.astype(BF16), vsn_ref[0, h].astype(BF16),
                                 preferred_element_type=F32)
            kw = win_ref[0, 0, pl.ds(h, n_win, stride=2 * NSA_KV_HEADS), :]
            vw = win_ref[0, 0, pl.ds(NSA_KV_HEADS + h, n_win, stride=2 * NSA_KV_HEADS), :]
            idx = lax.broadcasted_iota(jnp.int32, (NQ, n_win), 1)
            tr = lax.broadcasted_iota(jnp.int32, (NQ, n_win), 0) & (QROWS - 1)
            s_w = lax.dot_general(qb, kw.astype(BF16), _NT, preferred_element_type=F32) * ATT_SCALE
            s_w = jnp.where(idx + (WINDOW - n_win) > tr, s_w, NEG_INF)
            s_n = lax.dot_general(qb, kwn_ref[0, h].astype(BF16), _NT,
                                  preferred_element_type=F32) * ATT_SCALE
            s_n = jnp.where(new_ok, s_n, NEG_INF)
            mw = jnp.maximum(jnp.max(s_w, axis=-1, keepdims=True),
                             jnp.max(s_n, axis=-1, keepdims=True))
            p_w = jnp.exp(s_w - mw)
            p_n = jnp.exp(s_n - mw)
            lw = jnp.sum(p_w, axis=-1, keepdims=True) + jnp.sum(p_n, axis=-1, keepdims=True)
            ow_scr[h] = (jnp.dot(p_w.astype(BF16), vw.astype(BF16), preferred_element_type=F32)
                         + jnp.dot(p_n.astype(BF16), vwn_ref[0, h].astype(BF16),
                                   preferred_element_type=F32)) / lw

    step_keys = PAGES_PER_STEP * PAGE_SIZE
    erow = lax.broadcasted_iota(jnp.int32, (n_lanes, step_keys), 0)
    ecol = lax.broadcasted_iota(jnp.int32, (n_lanes, step_keys), 1)
    expand = jnp.where((j * step_keys + ecol) // SLC_BLOCK == erow, 1.0, 0.0).astype(BF16)
    for h in range(NSA_KV_HEADS):
        qb = q_ref[0, h].astype(BF16)
        sel_h = jnp.concatenate([sel_ref[0, h]] * NSA_GROUP, axis=0).astype(BF16)
        maskf = jnp.dot(sel_h, expand, preferred_element_type=F32)
        ss = []
        for k in range(PAGES_PER_STEP):
            s = lax.dot_general(qb, _page_kv(pages[k], 0, h).astype(BF16), _NT,
                                preferred_element_type=F32) * ATT_SCALE
            ss.append(jnp.where(maskf[:, k * PAGE_SIZE:(k + 1) * PAGE_SIZE] > 0.5, s, NEG_INF))
        m_old = m_scr[h]
        m_new = functools.reduce(jnp.maximum,
                                 [jnp.max(s, axis=-1, keepdims=True) for s in ss] + [m_old])
        alpha = jnp.exp(m_old - m_new)
        ps = [jnp.exp(s - m_new) for s in ss]
        l_scr[h] = alpha * l_scr[h] + sum(jnp.sum(p, axis=-1, keepdims=True) for p in ps)
        acc_scr[h] = alpha * acc_scr[h] + sum(
            jnp.dot(p.astype(BF16), _page_kv(pages[k], 1, h).astype(BF16),
                    preferred_element_type=F32) for k, p in enumerate(ps))
        m_scr[h] = m_new

    @pl.when(j == nj - 1)
    def _():
        for h in range(NSA_KV_HEADS):
            gates = jax.nn.sigmoid(g_ref[0, h])
            o_s = acc_scr[h] / l_scr[h]
            o_ref[0, h] = (gates[:, 0:1] * oc_ref[0, h] + gates[:, 1:2] * o_s
                           + gates[:, 2:3] * ow_scr[h])


def _nsa_slc_sample(q, sel, o_c, g3, ks_new, vs_new, kw_new, vw_new, win_state, pool, page_table,
                    l, n_new):
    b = q.shape[0]
    n_pages = page_table.shape[1]
    n_win = win_state.shape[2] // (2 * NSA_KV_HEADS)
    assert (n_pages * PAGE_SIZE) % SLC_BLOCK == 0 and n_new <= SLC_BLOCK and n_win <= WINDOW
    nkh = NSA_KV_HEADS
    per_b = lambda shape: pl.BlockSpec(shape, lambda bi, j, pt: (bi,) + (0,) * (len(shape) - 1))
    grid_spec = pltpu.PrefetchScalarGridSpec(
        num_scalar_prefetch=1,
        grid=(b, n_pages // PAGES_PER_STEP),
        in_specs=[per_b((1, nkh, NQ, HEAD_DIM)), per_b((1, nkh, QROWS, sel.shape[3])),
                  per_b((1, nkh, NQ, HEAD_DIM)), per_b((1, nkh, NQ, LANE)),
                  per_b((1, nkh, LANE, HEAD_DIM)), per_b((1, nkh, LANE, HEAD_DIM)),
                  per_b((1, nkh, LANE, HEAD_DIM)), per_b((1, nkh, LANE, HEAD_DIM)),
                  pl.BlockSpec((1, 1, win_state.shape[2], HEAD_DIM), lambda bi, j, pt: (l, bi, 0, 0))]
        + _page_specs(PAGES_PER_STEP, l),
        out_specs=per_b((1, nkh, NQ, HEAD_DIM)),
        scratch_shapes=[pltpu.VMEM((nkh, NQ, 1), F32), pltpu.VMEM((nkh, NQ, 1), F32),
                        pltpu.VMEM((nkh, NQ, HEAD_DIM), F32), pltpu.VMEM((nkh, NQ, HEAD_DIM), F32)])
    return pl.pallas_call(
        functools.partial(_nsa_slc_sample_kernel, n_new=n_new, n_win=n_win,
                          cur_blk=n_pages * PAGE_SIZE // SLC_BLOCK),
        grid_spec=grid_spec,
        out_shape=jax.ShapeDtypeStruct(q.shape, F32),
        compiler_params=pltpu.CompilerParams(dimension_semantics=("arbitrary", "arbitrary"),
                                             vmem_limit_bytes=VMEM_LIMIT_BYTES),
        name="nsa_slc_sample",
    )(page_table, q, sel, o_c, g3, ks_new, vs_new, kw_new, vw_new, win_state,
      *([pool] * PAGES_PER_STEP))


PROMPT_TM_IN = 512
PROMPT_TM_OUT = 256
MOE_TM = 128
PROMPT_MOE_BLOCK = 256
SAMPLE_MOE_BLOCK = 8


def _kv_rows(proj, col, b, s):
    lo = col * HEAD_DIM
    return proj[:, lo:lo + KV_COLS].reshape(b, s, 2, NSA_KV_HEADS, HEAD_DIM)


def _to_qrows(x, b, s):
    x = x.reshape(b, s, NSA_KV_HEADS, NSA_GROUP, HEAD_DIM).transpose(0, 2, 3, 1, 4)
    x = jnp.pad(x, ((0, 0), (0, 0), (0, 0), (0, QROWS - s), (0, 0)))
    return x.reshape(b, NSA_KV_HEADS, NQ, HEAD_DIM)


def _from_qrows(o, b, s):
    o = o.reshape(b, NSA_KV_HEADS, NSA_GROUP, QROWS, HEAD_DIM)[:, :, :, :s]
    return o.transpose(0, 3, 1, 2, 4).reshape(b * s, NSA_HEADS * HEAD_DIM)


def _new_rows(proj, col, b, s):
    lo = col * HEAD_DIM
    x = proj[:, lo:lo + NSA_KV_HEADS * HEAD_DIM].reshape(b, s, NSA_KV_HEADS, HEAD_DIM)
    return jnp.pad(x.transpose(0, 2, 1, 3), ((0, 0), (0, 0), (0, LANE - s), (0, 0)))


def _sample_mixer(proj, g_b, pools, win_state, page_table, cmp_w, l, b, s, q0):
    assert MOBA_KV_HEADS == NSA_KV_HEADS and MOBA_GROUP == NSA_GROUP and s <= QROWS
    assert q0 % CMP_STRIDE == 0 and s < CMP_STRIDE and q0 % MOBA_BLOCK == 0
    pool_moba, pool_cmp, pool_slc = pools
    nkh = NSA_KV_HEADS
    q_a = _to_qrows(proj[:, :MOBA_HEADS * HEAD_DIM], b, s)
    q_b = _to_qrows(proj[:, COL_QB * HEAD_DIM:COL_CMP * HEAD_DIM], b, s)
    o_a = _moba_sample(q_a, _new_rows(proj, COL_KA, b, s), _new_rows(proj, COL_VA, b, s),
                       pool_moba, page_table, l, s)
    kvc = _compress_sample(pool_cmp, page_table, l, *cmp_w)
    o_c, sel = _nsa_cmp_sample(q_b, kvc, q0, s)
    g3 = g_b[:, :GB_COLS].reshape(b, s, 3, nkh, NSA_GROUP).transpose(0, 3, 4, 1, 2)
    g3 = jnp.pad(g3, ((0, 0), (0, 0), (0, 0), (0, QROWS - s), (0, LANE - 3)))
    o_b = _nsa_slc_sample(q_b, sel, o_c, g3.reshape(b, nkh, NQ, LANE),
                          _new_rows(proj, COL_SLC, b, s), _new_rows(proj, COL_SLC + nkh, b, s),
                          _new_rows(proj, COL_WIN, b, s), _new_rows(proj, COL_WIN + nkh, b, s),
                          win_state, pool_slc, page_table, l, s)
    return _from_qrows(o_a, b, s), _from_qrows(o_b, b, s)


def _moe_block(x, h, logits, g2, norm_final, w_gate, w_up, w_down, l, bm, tm, rows_per_mod,
               final_norm):
    blk_expert, row_tok, n_used, row_w, d0, d1 = _route(logits, bm)
    yb = _moe_ffn(h, blk_expert, row_tok, n_used, row_w, w_gate, w_up, w_down, l, bm)
    return _moe_combine(x, g2, norm_final, yb, d0, d1, tm, rows_per_mod, final_norm)


def kernel(x_prompt, x_sample, cache_moba_kv, cache_cmp_kv, cache_slc_kv, state_win_kv,
           page_table, c_prompt, c_sample, w_in, w_pa, w_pb, w_out, cmp_pos, cmp_w1, cmp_b1,
           cmp_w2, cmp_b2, norm_attn, norm_ffn, norm_final, w_ada, b_ada, w_rg, b_rg, w_re,
           b_re, w_gate, w_up, w_down):
    bp, sp, d = x_prompt.shape
    bs, ss, _ = x_sample.shape
    tp, ts = bp * sp, bs * ss
    past_len = page_table.shape[1] * cache_moba_kv.shape[2]
    xp = x_prompt.reshape(tp, d)
    xs = x_sample.reshape(ts, d)
    cos_p, sin_p = _rope_tables(jnp.tile(jnp.arange(sp), bp))
    cos_s, sin_s = _rope_tables(jnp.tile(past_len + jnp.arange(ss), bs))
    n_c = _round_up(bp + bs, 8)
    c_all = jnp.pad(jnp.concatenate([c_prompt, c_sample], axis=0), ((0, n_c - bp - bs), (0, 0)))
    n_pool = cache_moba_kv.shape[1]
    pools = tuple(c.reshape(DEPTH, n_pool, PAGE_ROWS, HEAD_DIM)
                  for c in (cache_moba_kv, cache_cmp_kv, cache_slc_kv))
    n_win = state_win_kv.shape[2]
    win_view = state_win_kv.reshape(DEPTH, bs, n_win * 2 * NSA_KV_HEADS, HEAD_DIM)
    rows_p, rows_s = [], []
    for l in range(DEPTH):
        last = l == DEPTH - 1
        ada = _ada(c_all, w_ada, b_ada, l).reshape(n_c, 6, d)
        mod_p = [ada[:bp, k].reshape(bp, 1, d) for k in range(6)]
        mod_s = [jnp.repeat(ada[bp:bp + bs, k], ss, axis=0).reshape(1, ts, d) for k in range(6)]
        w_main = jnp.concatenate([w_in[l][:, :QKV_COLS], w_in[l][:, QKV_COLS + GB_COLS:]],
                                 axis=1).astype(BF16)
        w_gb = jnp.pad(w_in[l][:, QKV_COLS:QKV_COLS + GB_COLS],
                       ((0, 0), (0, LANE - GB_COLS))).astype(BF16)
        w_pa_b, w_pb_b, w_out_b = w_pa[l].astype(BF16), w_pb[l].astype(BF16), w_out[l].astype(BF16)
        w_r = jnp.pad(jnp.concatenate([w_rg[l], w_re[l]], axis=1),
                      ((0, 0), (0, LANE - N_GROUPS - N_EXPERTS)))
        b_r = jnp.pad(jnp.concatenate([b_rg[l], b_re[l]]),
                      (0, LANE - N_GROUPS - N_EXPERTS)).reshape(1, LANE)

        sh1, sc1, g1, sh2, sc2, g2 = mod_p
        proj, gates, g_b = _in_proj(xp, norm_attn[l], sc1, sh1, cos_p, sin_p, w_main, w_gb,
                                    PROMPT_TM_IN, sp)
        proj3 = proj.reshape(bp, sp, QKV_COLS)
        o_a = _moba_prompt_t(proj3)
        kvc = _compress_prompt(proj3, cmp_pos[l], cmp_w1[l], cmp_b1[l], cmp_w2[l], cmp_b2[l])
        o_b = _nsa_prompt_t(proj3, kvc, g_b.reshape(bp, sp, LANE))
        xp, h2, logits = _out_proj(o_a.reshape(tp, -1), o_b.reshape(tp, -1), gates, xp, g1,
                                   norm_ffn[l], sc2, sh2, w_pa_b, w_pb_b, w_out_b, w_r, b_r,
                                   PROMPT_TM_OUT, sp)
        xp = _moe_block(xp, h2, logits, g2, norm_final, w_gate, w_up, w_down, l,
                        PROMPT_MOE_BLOCK, MOE_TM, sp, last)
        new_win = _kv_rows(proj, COL_WIN, bp, sp)
        rows_p.append((_kv_rows(proj, COL_KA, bp, sp), _kv_rows(proj, COL_CMP, bp, sp),
                       _kv_rows(proj, COL_SLC, bp, sp), new_win[:, sp - min(WINDOW, sp):]))

        sh1, sc1, g1, sh2, sc2, g2 = mod_s
        proj, gates, g_b = _in_proj(xs, norm_attn[l], sc1, sh1, cos_s, sin_s, w_main, w_gb, ts, ss)
        cmp_w = (cmp_pos[l], cmp_w1[l], cmp_b1[l], cmp_w2[l], cmp_b2[l])
        o_a, o_b = _sample_mixer(proj, g_b, pools, win_view, page_table, cmp_w, l, bs, ss, past_len)
        xs, h2, logits = _out_proj(o_a, o_b, gates, xs, g1, norm_ffn[l], sc2, sh2,
                                   w_pa_b, w_pb_b, w_out_b, w_r, b_r, ts, ss)
        xs = _moe_block(xs, h2, logits, g2, norm_final, w_gate, w_up, w_down, l,
                        SAMPLE_MOE_BLOCK, ts, ss, last)
        win_rows = jnp.concatenate([state_win_kv[l], _kv_rows(proj, COL_WIN, bs, ss)], axis=1)
        rows_s.append((_kv_rows(proj, COL_KA, bs, ss), _kv_rows(proj, COL_CMP, bs, ss),
                       _kv_rows(proj, COL_SLC, bs, ss),
                       win_rows[:, win_rows.shape[1] - min(WINDOW, win_rows.shape[1]):]))

    return (xp.reshape(bp, sp, d), xs.reshape(bs, ss, d),
            jnp.stack([r[0] for r in rows_p]), jnp.stack([r[0] for r in rows_s]),
            jnp.stack([r[1] for r in rows_p]), jnp.stack([r[1] for r in rows_s]),
            jnp.stack([r[2] for r in rows_p]), jnp.stack([r[2] for r in rows_s]),
            jnp.stack([r[3] for r in rows_p]), jnp.stack([r[3] for r in rows_s]))
```

```python
import functools

import jax
import jax.numpy as jnp
import numpy as np
from jax import lax
from jax.experimental import pallas as pl
from jax.experimental.pallas import tpu as pltpu

D_MODEL = 2048
DEPTH = 2
HEAD_DIM = 128
ROT_DIM = HEAD_DIM // 4
ROPE_THETA = 500000.0
NORM_EPS = 1e-6
MOBA_HEADS = 8
MOBA_KV_HEADS = 2
MOBA_GROUP = MOBA_HEADS // MOBA_KV_HEADS
MOBA_BLOCK = 256
MOBA_TOPK = 3
MOBA_QCHUNK = 16
NSA_HEADS = 8
NSA_KV_HEADS = 2
NSA_GROUP = NSA_HEADS // NSA_KV_HEADS
CMP_LEN = 32
CMP_STRIDE = 16
CMP_HIDDEN = 128
SLC_BLOCK = 64
SLC_TOPN = 16
WINDOW = 512
NSA_QCHUNK = 16
WIN_QBLOCK = 128
N_GROUPS = 4
EXPERTS_PER_GROUP = 8
N_EXPERTS = N_GROUPS * EXPERTS_PER_GROUP
EXPERT_TOPK = 2
D_EXPERT = 512
MOE_BLOCK = 128

NEG_INF = -1e30
BIG = 1e30
F32 = jnp.float32
BF16 = jnp.bfloat16

LANE = 128
VMEM_LIMIT_BYTES = 48 * 1024 * 1024


def _round_up(n, m):
    return (n + m - 1) // m * m


QKV_COLS = 4096
GB_COLS = 3 * NSA_HEADS
GATE_COLS = 2 * D_MODEL
PROJ_COLS = QKV_COLS + GATE_COLS
COL_QA, COL_KA, COL_VA, COL_QB = 0, 8, 10, 12
COL_CMP, COL_SLC, COL_WIN = 20, 24, 28


def _mod_spec(mod, tm, rows_per_mod):
    d = mod.shape[-1]
    if mod.shape[1] == 1:
        return pl.BlockSpec((1, 1, d), lambda i, *_: (i * tm // rows_per_mod, 0, 0))
    return pl.BlockSpec((1, tm, d), lambda i, *_: (i, 0, 0))


def _ada_kernel(c_ref, w_ref, b_ref, o_ref):
    c = jax.nn.silu(c_ref[...]).astype(BF16)
    o_ref[...] = jnp.dot(c, w_ref[0].astype(BF16), preferred_element_type=F32) + b_ref[0]


def _ada(c, w_ada, b_ada, l, tn=1024):
    m, k = c.shape
    depth, _, n = w_ada.shape
    return pl.pallas_call(
        _ada_kernel,
        grid=(n // tn,),
        in_specs=[pl.BlockSpec((m, k), lambda j: (0, 0)),
                  pl.BlockSpec((1, k, tn), lambda j: (l, 0, j)),
                  pl.BlockSpec((1, 1, tn), lambda j: (l, 0, j))],
        out_specs=pl.BlockSpec((m, tn), lambda j: (0, j)),
        out_shape=jax.ShapeDtypeStruct((m, n), F32),
        compiler_params=pltpu.CompilerParams(dimension_semantics=("arbitrary",),
                                             vmem_limit_bytes=VMEM_LIMIT_BYTES),
        name="ada_ln",
    )(c, w_ada, b_ada.reshape(depth, 1, n))


def _rope_tables(pos):
    half = ROT_DIM // 2
    inv_freq = ROPE_THETA ** (-jnp.arange(half, dtype=F32) / half)
    ang = pos.astype(F32)[:, None] * inv_freq[None, :]
    cos, sin = jnp.cos(ang), jnp.sin(ang)
    n = pos.shape[0]
    cos_t = jnp.concatenate([cos, cos, jnp.ones((n, HEAD_DIM - ROT_DIM), F32)], axis=1)
    sin_t = jnp.concatenate([-sin, sin, jnp.zeros((n, HEAD_DIM - ROT_DIM), F32)], axis=1)
    return cos_t, sin_t


def _rope_column_mask():
    m = np.zeros((1, QKV_COLS), np.float32)
    for lo, hi in ((0, 1280), (1536, 2560), (3072, 3328), (3584, 3840)):
        m[0, lo:hi] = 1.0
    return jnp.asarray(m)


def _norm_modulate(x_ref, g_ref, sc_ref, sh_ref):
    x = x_ref[...]
    y = x * lax.rsqrt(jnp.mean(x * x, axis=-1, keepdims=True) + NORM_EPS) * g_ref[...]
    return (y * (1.0 + sc_ref[0]) + sh_ref[0]).astype(BF16)


def _in_proj_qkv_kernel(x_ref, g_ref, sc_ref, sh_ref, cos_ref, sin_ref, rmask_ref, w_ref, wgb_ref,
                        o_ref, ogb_ref, hb_scr, *, tn):
    @pl.when(pl.program_id(1) == 0)
    def _():
        h = _norm_modulate(x_ref, g_ref, sc_ref, sh_ref)
        hb_scr[...] = h
        ogb_ref[...] = jnp.dot(h, wgb_ref[...], preferred_element_type=F32)

    acc = jnp.dot(hb_scr[...], w_ref[...], preferred_element_type=F32)
    lane = lax.broadcasted_iota(jnp.int32, (acc.shape[0], HEAD_DIM), 1)
    cos_t, sin_t = cos_ref[...], sin_ref[...]
    for hh in range(tn // HEAD_DIM):
        blk = acc[:, hh * HEAD_DIM:(hh + 1) * HEAD_DIM]
        on = rmask_ref[:, hh * HEAD_DIM:(hh + 1) * HEAD_DIM] > 0.5
        partner = jnp.where(lane < ROT_DIM // 2,
                            pltpu.roll(blk, HEAD_DIM - ROT_DIM // 2, axis=1),
                            pltpu.roll(blk, ROT_DIM // 2, axis=1))
        o_ref[:, hh * HEAD_DIM:(hh + 1) * HEAD_DIM] = (
            blk * jnp.where(on, cos_t, 1.0) + partner * jnp.where(on, sin_t, 0.0))


def _in_proj_gate_kernel(x_ref, g_ref, sc_ref, sh_ref, w_ref, o_ref, hb_scr):
    @pl.when(pl.program_id(1) == 0)
    def _():
        hb_scr[...] = _norm_modulate(x_ref, g_ref, sc_ref, sh_ref)

    o_ref[...] = jax.nn.sigmoid(jnp.dot(hb_scr[...], w_ref[...], preferred_element_type=F32))


def _in_proj(x, g, sc, sh, cos_t, sin_t, w_main, w_gb, tm, rows_per_mod, tn=1024):
    t, d = x.shape
    rmask = _rope_column_mask()
    x_specs = [pl.BlockSpec((tm, d), lambda i, j: (i, 0)),
               pl.BlockSpec((1, d), lambda i, j: (0, 0)),
               _mod_spec(sc, tm, rows_per_mod), _mod_spec(sh, tm, rows_per_mod)]
    params = pltpu.CompilerParams(dimension_semantics=("arbitrary", "arbitrary"),
                                  vmem_limit_bytes=VMEM_LIMIT_BYTES)
    qkv, g_b = pl.pallas_call(
        functools.partial(_in_proj_qkv_kernel, tn=tn),
        grid=(t // tm, QKV_COLS // tn),
        in_specs=x_specs + [pl.BlockSpec((tm, HEAD_DIM), lambda i, j: (i, 0)),
                            pl.BlockSpec((tm, HEAD_DIM), lambda i, j: (i, 0)),
                            pl.BlockSpec((1, tn), lambda i, j: (0, j)),
                            pl.BlockSpec((d, tn), lambda i, j: (0, j)),
                            pl.BlockSpec((d, LANE), lambda i, j: (0, 0))],
        out_specs=[pl.BlockSpec((tm, tn), lambda i, j: (i, j)),
                   pl.BlockSpec((tm, LANE), lambda i, j: (i, 0))],
        out_shape=[jax.ShapeDtypeStruct((t, QKV_COLS), F32),
                   jax.ShapeDtypeStruct((t, LANE), F32)],
        scratch_shapes=[pltpu.VMEM((tm, d), BF16)],
        compiler_params=params,
        name="in_proj_qkv",
    )(x, g.reshape(1, d), sc, sh, cos_t, sin_t, rmask, w_main, w_gb)
    gates = pl.pallas_call(
        _in_proj_gate_kernel,
        grid=(t // tm, GATE_COLS // tn),
        in_specs=x_specs + [pl.BlockSpec((d, tn), lambda i, j: (0, QKV_COLS // tn + j))],
        out_specs=pl.BlockSpec((tm, tn), lambda i, j: (i, j)),
        out_shape=jax.ShapeDtypeStruct((t, GATE_COLS), F32),
        scratch_shapes=[pltpu.VMEM((tm, d), BF16)],
        compiler_params=params,
        name="in_proj_gate",
    )(x, g.reshape(1, d), sc, sh, w_main)
    return qkv, gates, g_b


def _out_proj_kernel(oa_ref, ob_ref, gma_ref, gmb_ref, x_ref, g1_ref, gf_ref, sc_ref, sh_ref,
                     wpa_ref, wpb_ref, wout_ref, wr_ref, br_ref, xo_ref, h_ref, lg_ref):
    pa = jnp.dot(oa_ref[...].astype(BF16), wpa_ref[...], preferred_element_type=F32)
    pb = jnp.dot(ob_ref[...].astype(BF16), wpb_ref[...], preferred_element_type=F32)
    merged = (gma_ref[...] * pa + gmb_ref[...] * pb).astype(BF16)
    mix = jnp.dot(merged, wout_ref[...], preferred_element_type=F32)
    x = x_ref[...] + g1_ref[0] * mix
    xo_ref[...] = x
    y = x * lax.rsqrt(jnp.mean(x * x, axis=-1, keepdims=True) + NORM_EPS) * gf_ref[...]
    h = y * (1.0 + sc_ref[0]) + sh_ref[0]
    h_ref[...] = h
    lg_ref[...] = jnp.dot(h, wr_ref[...], precision=lax.Precision.HIGHEST,
                          preferred_element_type=F32) + br_ref[...]


def _out_proj(o_a, o_b, gates, x, g1, gf, sc, sh, w_pa, w_pb, w_out, w_r, b_r, tm, rows_per_mod):
    t, d = x.shape
    da = o_a.shape[1]
    const = lambda shape: pl.BlockSpec(shape, lambda i: (0,) * len(shape),
                                       pipeline_mode=pl.Buffered(1))
    gate_blk = 0
    return pl.pallas_call(
        _out_proj_kernel,
        grid=(t // tm,),
        in_specs=[pl.BlockSpec((tm, da), lambda i: (i, 0)),
                  pl.BlockSpec((tm, da), lambda i: (i, 0)),
                  pl.BlockSpec((tm, d), lambda i: (i, gate_blk)),
                  pl.BlockSpec((tm, d), lambda i: (i, gate_blk + 1)),
                  pl.BlockSpec((tm, d), lambda i: (i, 0)),
                  _mod_spec(g1, tm, rows_per_mod),
                  const((1, d)),
                  _mod_spec(sc, tm, rows_per_mod), _mod_spec(sh, tm, rows_per_mod),
                  const((da, d)), const((da, d)), const((d, d)), const((d, LANE)), const((1, LANE))],
        out_specs=[pl.BlockSpec((tm, d), lambda i: (i, 0)),
                   pl.BlockSpec((tm, d), lambda i: (i, 0)),
                   pl.BlockSpec((tm, LANE), lambda i: (i, 0))],
        out_shape=[jax.ShapeDtypeStruct((t, d), F32), jax.ShapeDtypeStruct((t, d), F32),
                   jax.ShapeDtypeStruct((t, LANE), F32)],
        compiler_params=pltpu.CompilerParams(dimension_semantics=("arbitrary",),
                                             vmem_limit_bytes=VMEM_LIMIT_BYTES),
        name="out_proj",
    )(o_a, o_b, gates, gates, x, g1, gf.reshape(1, d), sc, sh, w_pa, w_pb, w_out, w_r, b_r)


def _moe_ffn_kernel(blk_e_ref, seg_ref, order_ref, nused_ref, x_hbm, wg_ref, wu_ref, wd_ref,
                    o_ref, xbuf, sem, wg_bf, wu_bf, wd_bf, *, bm):
    i = pl.program_id(0)
    n_used = nused_ref[0]
    last = order_ref.shape[0] - 1

    def issue(blk, slot):
        base = seg_ref[blk]
        for r in range(bm):
            tok = order_ref[jnp.minimum(base + r, last)] // EXPERT_TOPK
            pltpu.make_async_copy(x_hbm.at[pl.ds(tok, 1)], xbuf.at[slot, pl.ds(r, 1)],
                                  sem.at[slot]).start()

    def wait(slot):
        pltpu.make_async_copy(x_hbm.at[pl.ds(0, bm)], xbuf.at[slot], sem.at[slot]).wait()

    slot = i % 2

    @pl.when((i == 0) & (n_used > 0))
    def _():
        issue(0, 0)

    @pl.when(i < n_used)
    def _():
        wait(slot)
        prev = blk_e_ref[jnp.maximum(i - 1, 0)]

        @pl.when((i == 0) | (blk_e_ref[i] != prev))
        def _():
            wg_bf[...] = wg_ref[0, 0].astype(BF16)
            wu_bf[...] = wu_ref[0, 0].astype(BF16)
            wd_bf[...] = wd_ref[0, 0].astype(BF16)

        issue(jnp.minimum(i + 1, n_used - 1), 1 - slot)
        x = xbuf[slot].astype(BF16)
        hg = jnp.dot(x, wg_bf[...], preferred_element_type=F32)
        hu = jnp.dot(x, wu_bf[...], preferred_element_type=F32)
        act = (jax.nn.silu(hg) * hu).astype(BF16)
        o_ref[...] = jnp.dot(act, wd_bf[...], preferred_element_type=F32)

        @pl.when(i + 1 == n_used)
        def _():
            wait(1 - slot)

    @pl.when(i >= n_used)
    def _():
        o_ref[...] = jnp.zeros_like(o_ref)


def _moe_ffn(x, blk_expert, seg_start, order, n_used, w_gate, w_up, w_down, l, bm):
    n_blk = blk_expert.shape[0]
    n_rows = n_blk * bm
    d = x.shape[1]
    de = w_gate.shape[3]
    grid_spec = pltpu.PrefetchScalarGridSpec(
        num_scalar_prefetch=4,
        grid=(n_blk,),
        in_specs=[pl.BlockSpec(memory_space=pl.ANY),
                  pl.BlockSpec((1, 1, d, de), lambda i, be, sg, od, nu: (l, be[i], 0, 0)),
                  pl.BlockSpec((1, 1, d, de), lambda i, be, sg, od, nu: (l, be[i], 0, 0)),
                  pl.BlockSpec((1, 1, de, d), lambda i, be, sg, od, nu: (l, be[i], 0, 0))],
        out_specs=pl.BlockSpec((bm, d), lambda i, be, sg, od, nu: (i, 0)),
        scratch_shapes=[pltpu.VMEM((2, bm, d), F32),
                        pltpu.SemaphoreType.DMA((2,)),
                        pltpu.VMEM((d, de), BF16), pltpu.VMEM((d, de), BF16),
                        pltpu.VMEM((de, d), BF16)])
    return pl.pallas_call(
        functools.partial(_moe_ffn_kernel, bm=bm),
        grid_spec=grid_spec,
        out_shape=jax.ShapeDtypeStruct((n_rows, d), F32),
        compiler_params=pltpu.CompilerParams(dimension_semantics=("arbitrary",),
                                             vmem_limit_bytes=VMEM_LIMIT_BYTES),
        name="moe_ffn",
    )(blk_expert, seg_start, order, n_used, x, w_gate, w_up, w_down)


def _moe_combine_kernel(d0_ref, d1_ref, x_ref, g2_ref, w_ref, nf_ref, yb_hbm, o_ref, ybuf, sem,
                        *, tm, final_norm):
    i = pl.program_id(0)
    n = pl.num_programs(0)

    def issue(blk, slot):
        for r in range(tm):
            t = blk * tm + r
            pltpu.make_async_copy(yb_hbm.at[pl.ds(d0_ref[t], 1)], ybuf.at[slot, 0, pl.ds(r, 1)],
                                  sem.at[slot]).start()
            pltpu.make_async_copy(yb_hbm.at[pl.ds(d1_ref[t], 1)], ybuf.at[slot, 1, pl.ds(r, 1)],
                                  sem.at[slot]).start()

    def wait(slot):
        pltpu.make_async_copy(yb_hbm.at[pl.ds(0, tm)], ybuf.at[slot, 0], sem.at[slot]).wait()
        pltpu.make_async_copy(yb_hbm.at[pl.ds(0, tm)], ybuf.at[slot, 1], sem.at[slot]).wait()

    slot = i % 2

    @pl.when(i == 0)
    def _():
        issue(0, 0)

    wait(slot)
    issue(jnp.minimum(i + 1, n - 1), 1 - slot)
    w = w_ref[...]
    y = x_ref[...] + g2_ref[0] * (w[:, 0:1] * ybuf[slot, 0] + w[:, 1:2] * ybuf[slot, 1])
    if final_norm:
        y = y * lax.rsqrt(jnp.mean(y * y, axis=-1, keepdims=True) + NORM_EPS) * nf_ref[...]
    o_ref[...] = y

    @pl.when(i + 1 == n)
    def _():
        wait(1 - slot)


def _moe_combine(x, g2, w_tok, norm_final, yb, d0, d1, tm, rows_per_mod, final_norm):
    t, d = x.shape
    grid_spec = pltpu.PrefetchScalarGridSpec(
        num_scalar_prefetch=2,
        grid=(t // tm,),
        in_specs=[pl.BlockSpec((tm, d), lambda i, a, b: (i, 0)),
                  _mod_spec(g2, tm, rows_per_mod),
                  pl.BlockSpec((tm, LANE), lambda i, a, b: (i, 0)),
                  pl.BlockSpec((1, d), lambda i, a, b: (0, 0)),
                  pl.BlockSpec(memory_space=pl.ANY)],
        out_specs=pl.BlockSpec((tm, d), lambda i, a, b: (i, 0)),
        scratch_shapes=[pltpu.VMEM((2, 2, tm, d), F32), pltpu.SemaphoreType.DMA((2,))])
    return pl.pallas_call(
        functools.partial(_moe_combine_kernel, tm=tm, final_norm=final_norm),
        grid_spec=grid_spec,
        out_shape=jax.ShapeDtypeStruct((t, d), F32),
        compiler_params=pltpu.CompilerParams(dimension_semantics=("arbitrary",),
                                             vmem_limit_bytes=VMEM_LIMIT_BYTES),
        name="moe_combine",
    )(d0, d1, x, g2, w_tok, norm_final.reshape(1, d), yb)


def _route(logits, bm):
    t = logits.shape[0]
    a = t * EXPERT_TOPK
    g_prob = jax.nn.softmax(logits[:, :N_GROUPS], axis=-1)
    g_idx = jnp.argmax(g_prob, axis=-1, keepdims=True).astype(jnp.int32)
    g_w = jnp.max(g_prob, axis=-1, keepdims=True)
    e_logit = logits[:, N_GROUPS:N_GROUPS + N_EXPERTS].reshape(t, N_GROUPS, EXPERTS_PER_GROUP)
    e_logit = jnp.take_along_axis(e_logit, g_idx[:, :, None], axis=1)[:, 0]
    e_ids = jnp.arange(EXPERTS_PER_GROUP, dtype=jnp.int32)[None, :]
    i1 = jnp.argmax(e_logit, axis=-1, keepdims=True).astype(jnp.int32)
    rest = jnp.where(e_ids == i1, -jnp.inf, e_logit)
    i2 = jnp.argmax(rest, axis=-1, keepdims=True).astype(jnp.int32)
    e_val = jnp.concatenate([jnp.max(e_logit, axis=-1, keepdims=True),
                             jnp.max(rest, axis=-1, keepdims=True)], axis=-1)
    e_idx = jnp.concatenate([i1, i2], axis=-1)
    weights = (g_w * jax.nn.softmax(e_val, axis=-1)).reshape(a)
    flat_e = (g_idx * EXPERTS_PER_GROUP + e_idx).reshape(a)
    order = jnp.argsort(flat_e).astype(jnp.int32)
    e_sorted = flat_e[order]
    expert_ids = jnp.arange(N_EXPERTS, dtype=jnp.int32)
    counts = jnp.sum((flat_e[:, None] == expert_ids[None, :]).astype(jnp.int32), axis=0)
    padded = (counts + bm - 1) // bm * bm
    pad_end = jnp.cumsum(padded)
    pad_start = pad_end - padded
    start = jnp.cumsum(counts) - counts
    dest_sorted = pad_start[e_sorted] + jnp.arange(a, dtype=jnp.int32) - start[e_sorted]
    n_blk = -(-(a + N_EXPERTS * (bm - 1)) // bm)
    blk_start = jnp.arange(n_blk, dtype=jnp.int32) * bm
    blk_expert = jnp.minimum(jnp.sum((pad_end[None, :] <= blk_start[:, None]).astype(jnp.int32), axis=1),
                             N_EXPERTS - 1).astype(jnp.int32)
    n_used = (pad_end[-1] // bm).astype(jnp.int32).reshape(1)
    seg_start = jnp.clip(start[blk_expert] + blk_start - pad_start[blk_expert], 0, a - 1)
    _, dest = lax.sort((order, dest_sorted), num_keys=1)
    dest = dest.reshape(t, EXPERT_TOPK)
    w_tok = jnp.pad(weights.reshape(t, EXPERT_TOPK), ((0, 0), (0, LANE - EXPERT_TOPK)))
    return blk_expert, seg_start.astype(jnp.int32), order, n_used, w_tok, dest[:, 0], dest[:, 1]


ATT_SCALE = HEAD_DIM ** -0.5
_NT = (((1,), (1,)), ((), ()))


KV_COLS = 2 * NSA_KV_HEADS * HEAD_DIM


def _compress_kernel(x0_ref, x1_ref, x2_ref, x3_ref, pos_ref, w1_ref, b1_ref, w2_ref, b2_ref,
                     o_ref, *, nch):
    x_refs = (x0_ref, x1_ref, x2_ref, x3_ref)
    for kv in range(2):
        for h in range(NSA_KV_HEADS):
            c = kv * NSA_KV_HEADS + h
            first = jnp.zeros((nch, CMP_HIDDEN), F32)
            second = jnp.zeros((nch, CMP_HIDDEN), F32)
            for r in range(CMP_STRIDE):
                xr = x_refs[c][0, pl.ds(r, nch, stride=CMP_STRIDE), :]
                r2 = CMP_STRIDE + r
                first += jnp.dot((xr + pos_ref[kv, r:r + 1, :]).astype(BF16),
                                 w1_ref[kv, r * HEAD_DIM:(r + 1) * HEAD_DIM, :].astype(BF16),
                                 preferred_element_type=F32)
                second += jnp.dot((xr + pos_ref[kv, r2:r2 + 1, :]).astype(BF16),
                                  w1_ref[kv, r2 * HEAD_DIM:(r2 + 1) * HEAD_DIM, :].astype(BF16),
                                  preferred_element_type=F32)
            pre = first + pltpu.roll(second, nch - 1, axis=0) + b1_ref[kv:kv + 1, :]
            hdn = jax.nn.gelu(pre)
            o_ref[0, c] = (jnp.dot(hdn.astype(BF16), w2_ref[kv].astype(BF16),
                                   preferred_element_type=F32) + b2_ref[kv:kv + 1, :])


def _compress_prompt(proj, pos, w1, b1, w2, b2):
    b, s, _ = proj.shape
    nch = s // CMP_STRIDE
    full = lambda shape: pl.BlockSpec(shape, lambda bi: (0,) * len(shape))
    return pl.pallas_call(
        functools.partial(_compress_kernel, nch=nch),
        grid=(b,),
        in_specs=[pl.BlockSpec((1, s, HEAD_DIM), functools.partial(lambda bi, c: (bi, 0, COL_CMP + c), c=c))
                  for c in range(2 * NSA_KV_HEADS)] + [
                  full(pos.shape), full(w1.shape), full(b1.shape), full(w2.shape), full(b2.shape)],
        out_specs=pl.BlockSpec((1, 2 * NSA_KV_HEADS, nch, HEAD_DIM), lambda bi: (bi, 0, 0, 0)),
        out_shape=jax.ShapeDtypeStruct((b, 2 * NSA_KV_HEADS, nch, HEAD_DIM), F32),
        compiler_params=pltpu.CompilerParams(dimension_semantics=("arbitrary",),
                                             vmem_limit_bytes=VMEM_LIMIT_BYTES),
        name="nsa_compress",
    )(proj, proj, proj, proj, pos, w1, b1, w2, b2)


NSA_TQ = 256
NSA_KB = 256


def _flash_update_t(qb, kb, vt, mask, carry):
    m, l, acc = carry
    s = lax.dot_general(kb, qb, _NT, preferred_element_type=F32) * ATT_SCALE
    s = jnp.where(mask, s, NEG_INF)
    m_new = jnp.maximum(m, jnp.max(s, axis=0, keepdims=True))
    alpha = jnp.exp(m - m_new)
    p = jnp.exp(s - m_new)
    l = alpha * l + jnp.sum(p, axis=0, keepdims=True)
    acc = alpha * acc + jnp.dot(vt, p.astype(BF16), preferred_element_type=F32)
    return m_new, l, acc


def _flash_init_t(tq):
    return (jnp.full((1, tq), -jnp.inf, F32), jnp.zeros((1, tq), F32),
            jnp.zeros((HEAD_DIM, tq), F32))


def _stage_kv(k_ref, v_ref, kb_scr, vt_scr, nblk, blk):
    for j in range(nblk):
        kb_scr[j] = k_ref[0, j * blk:(j + 1) * blk, :].astype(BF16)
        vt_scr[j] = v_ref[0, j * blk:(j + 1) * blk, :].T.astype(BF16)


def _moba_prompt_t_kernel(q_ref, k_ref, v_ref, o_ref, kmean_scr, sel_scr, kb_scr, vt_scr, *, nb):
    i = pl.program_id(2)
    tq = MOBA_BLOCK
    nbp = kmean_scr.shape[0]

    @pl.when(i == 0)
    def _():
        kmean_scr[...] = jnp.zeros_like(kmean_scr)
        for j in range(nb):
            kmean_scr[j:j + 1, :] = jnp.mean(k_ref[0, j * tq:(j + 1) * tq, :], axis=0, keepdims=True)
        _stage_kv(k_ref, v_ref, kb_scr, vt_scr, nb, tq)

    wq = MOBA_GROUP * tq
    blk_row = lax.broadcasted_iota(jnp.int32, (nbp, tq), 0)
    qbs = []
    for g in range(MOBA_GROUP):
        qg = q_ref[0, :, g * HEAD_DIM:(g + 1) * HEAD_DIM]
        gate = lax.dot_general(kmean_scr[...], qg, _NT, precision=lax.Precision.HIGHEST,
                               preferred_element_type=F32)
        gate = jnp.where(blk_row < i, gate, NEG_INF)
        cnt = jnp.zeros((nbp, tq), F32)
        for j in range(nb):
            gj = gate[j:j + 1, :]
            cnt = cnt + jnp.where((gj > gate) | ((gj == gate) & (j < blk_row)), 1.0, 0.0)
        sel_scr[:, g * tq:(g + 1) * tq] = jnp.where((cnt < MOBA_TOPK) & (blk_row < i), 1.0, 0.0)
        qbs.append(qg.astype(BF16))
    qb = jnp.concatenate(qbs, axis=0)

    key_row = lax.broadcasted_iota(jnp.int32, (tq, wq), 0)
    q_idx = lax.broadcasted_iota(jnp.int32, (tq, wq), 1) & (tq - 1)
    carry = _flash_update_t(qb, kb_scr[i], vt_scr[i], key_row <= q_idx, _flash_init_t(wq))

    def body(j, carry):
        chosen = jnp.broadcast_to(sel_scr[pl.ds(j, 1), :], (tq, wq)) > 0.5
        return _flash_update_t(qb, kb_scr[j], vt_scr[j], chosen, carry)

    _, l, acc = lax.fori_loop(0, i, body, carry)
    o_t = acc / l
    for g in range(MOBA_GROUP):
        o_ref[0, :, g * HEAD_DIM:(g + 1) * HEAD_DIM] = o_t[:, g * tq:(g + 1) * tq].T


def _moba_prompt_t(proj):
    b, s, _ = proj.shape
    nb = s // MOBA_BLOCK
    assert s % MOBA_BLOCK == 0
    nbp = _round_up(nb, 8)
    gw = MOBA_GROUP * HEAD_DIM
    return pl.pallas_call(
        functools.partial(_moba_prompt_t_kernel, nb=nb),
        grid=(b, MOBA_KV_HEADS, nb),
        in_specs=[pl.BlockSpec((1, MOBA_BLOCK, gw), lambda bi, h, i: (bi, i, COL_QA // MOBA_GROUP + h)),
                  pl.BlockSpec((1, s, HEAD_DIM), lambda bi, h, i: (bi, 0, COL_KA + h)),
                  pl.BlockSpec((1, s, HEAD_DIM), lambda bi, h, i: (bi, 0, COL_VA + h))],
        out_specs=pl.BlockSpec((1, MOBA_BLOCK, gw), lambda bi, h, i: (bi, i, h)),
        out_shape=jax.ShapeDtypeStruct((b, s, MOBA_HEADS * HEAD_DIM), F32),
        scratch_shapes=[pltpu.VMEM((nbp, HEAD_DIM), F32),
                        pltpu.VMEM((nbp, MOBA_GROUP * MOBA_BLOCK), F32),
                        pltpu.VMEM((nb, MOBA_BLOCK, HEAD_DIM), BF16),
                        pltpu.VMEM((nb, HEAD_DIM, MOBA_BLOCK), BF16)],
        compiler_params=pltpu.CompilerParams(
            dimension_semantics=("arbitrary", "arbitrary", "arbitrary"),
            vmem_limit_bytes=VMEM_LIMIT_BYTES),
        name="moba_prompt",
    )(proj, proj, proj)


def _nsa_prompt_t_kernel(q_ref, kc_ref, vc_ref, ks_ref, vs_ref, kw_ref, vw_ref, gb_ref, o_ref,
                         ksb_scr, vst_scr, kwb_scr, vwt_scr, kcb_scr, vct_scr, gt_scr, sel_scr,
                         o_scr, *, s_len):
    kvh = pl.program_id(1)
    i = pl.program_id(2)
    tq, kb = NSA_TQ, NSA_KB
    n_cmp = s_len // CMP_STRIDE - CMP_LEN // CMP_STRIDE + 1
    n_slc = s_len // SLC_BLOCK
    n_top = min(SLC_TOPN, n_slc)
    nkb = s_len // kb
    n_rows = sel_scr.shape[0]
    per_kb = kb // SLC_BLOCK
    dh = HEAD_DIM

    @pl.when(i == 0)
    def _():
        _stage_kv(ks_ref, vs_ref, ksb_scr, vst_scr, nkb, kb)
        _stage_kv(kw_ref, vw_ref, kwb_scr, vwt_scr, nkb, kb)
        kcb_scr[...] = kc_ref[0, 0].astype(BF16)
        vct_scr[...] = vc_ref[0, 0].T.astype(BF16)

    gt_scr[...] = jax.nn.sigmoid(gb_ref[0]).T

    def gate_row(branch, g):
        return gt_scr[pl.ds(branch * NSA_HEADS + kvh * NSA_GROUP + g, 1), :]

    n_sub = lax.broadcasted_iota(jnp.int32, (LANE, tq), 0)
    t_lane = i * tq + lax.broadcasted_iota(jnp.int32, (LANE, tq), 1)
    avail = (n_sub * CMP_STRIDE + (CMP_LEN - 1) <= t_lane) & (n_sub < n_cmp)
    jj = lax.broadcasted_iota(jnp.int32, (LANE, LANE), 0)
    nn = lax.broadcasted_iota(jnp.int32, (LANE, LANE), 1)
    sel_map_t = jnp.where((nn * CMP_STRIDE < jj * SLC_BLOCK + SLC_BLOCK)
                          & (nn * CMP_STRIDE + CMP_LEN > jj * SLC_BLOCK)
                          & (nn < n_cmp) & (jj < n_slc), 1.0, 0.0)
    wq = NSA_GROUP * tq

    def gate_wide(branch):
        return jnp.concatenate([jnp.broadcast_to(gate_row(branch, g), (dh, tq))
                                for g in range(NSA_GROUP)], axis=1)

    qb = jnp.concatenate([q_ref[0, :, g * dh:(g + 1) * dh].astype(BF16)
                          for g in range(NSA_GROUP)], axis=0)
    avail_w = jnp.concatenate([avail] * NSA_GROUP, axis=1)
    s_c = lax.dot_general(kcb_scr[...], qb, _NT, preferred_element_type=F32) * ATT_SCALE
    s_c = jnp.where(avail_w, s_c, NEG_INF)
    e = jnp.where(avail_w, jnp.exp(s_c - jnp.max(s_c, axis=0, keepdims=True)), 0.0)
    den = jnp.sum(e, axis=0, keepdims=True)
    p_c = e * jnp.where(den > 0.0, 1.0 / den, 0.0)
    o_scr[...] = gate_wide(0) * jnp.dot(vct_scr[...], p_c.astype(BF16), preferred_element_type=F32)
    p_sum = sum(p_c[:, g * tq:(g + 1) * tq] for g in range(NSA_GROUP))
    imp = jnp.dot(sel_map_t, p_sum, precision=lax.Precision.HIGHEST, preferred_element_type=F32)

    imp = imp[:n_rows]
    jio = lax.broadcasted_iota(jnp.int32, (n_rows, tq), 0)
    cur = (i * tq + lax.broadcasted_iota(jnp.int32, (n_rows, tq), 1)) // SLC_BLOCK
    forced = (jio == 0) | (jio == cur) | (jio == cur - 1)
    imp = jnp.where(forced, BIG, imp)
    imp = jnp.where(jio > cur, NEG_INF, imp)
    imp = jnp.where(jio < n_slc, imp, -jnp.inf)
    cnt = jnp.zeros((n_rows, tq), F32)
    for r in range(n_slc):
        vr = imp[r:r + 1, :]
        cnt = cnt + jnp.where((vr > imp) | ((vr == imp) & (r < jio)), 1.0, 0.0)
    sel_scr[...] = jnp.where((cnt < n_top) & (jio < n_slc), 1.0, 0.0)

    def slc_mask(jb):
        rows = [jnp.broadcast_to(sel_scr[pl.ds(jb * per_kb + a, 1), :], (SLC_BLOCK, tq))
                for a in range(per_kb)]
        one = jnp.concatenate(rows, axis=0)
        return jnp.concatenate([one] * NSA_GROUP, axis=1) > 0.5

    key_row = lax.broadcasted_iota(jnp.int32, (kb, wq), 0)
    q_idx = lax.broadcasted_iota(jnp.int32, (kb, wq), 1) & (tq - 1)
    causal = key_row <= q_idx

    carry = _flash_update_t(qb, ksb_scr[i], vst_scr[i], slc_mask(i) & causal, _flash_init_t(wq))

    def slc_body(j, carry):
        return _flash_update_t(qb, ksb_scr[j], vst_scr[j], slc_mask(j), carry)

    _, l, acc = lax.fori_loop(0, i, slc_body, carry)
    o_scr[...] += gate_wide(1) * (acc / l)

    carry = _flash_update_t(qb, kwb_scr[i], vwt_scr[i], causal, _flash_init_t(wq))

    def win_body(j, carry):
        mask = (j * kb + key_row) > (i * tq + q_idx - WINDOW)
        return _flash_update_t(qb, kwb_scr[j], vwt_scr[j], mask, carry)

    _, l, acc = lax.fori_loop(jnp.maximum(i - WINDOW // kb, 0), i, win_body, carry)
    o_t = o_scr[...] + gate_wide(2) * (acc / l)
    for g in range(NSA_GROUP):
        o_ref[0, :, g * dh:(g + 1) * dh] = o_t[:, g * tq:(g + 1) * tq].T


def _nsa_prompt_t(proj, kvc, g_b):
    b, s, _ = proj.shape
    assert s % NSA_TQ == 0 and NSA_TQ == NSA_KB and s // CMP_STRIDE == LANE
    assert NSA_KB % SLC_BLOCK == 0
    gw = NSA_GROUP * HEAD_DIM
    nkh = NSA_KV_HEADS
    nkb = s // NSA_KB
    n_rows = _round_up(s // SLC_BLOCK, 8)
    seq = lambda col: pl.BlockSpec((1, s, HEAD_DIM), lambda bi, h, i: (bi, 0, col + h))
    kv_scr = [pltpu.VMEM((nkb, NSA_KB, HEAD_DIM), BF16), pltpu.VMEM((nkb, HEAD_DIM, NSA_KB), BF16)]
    return pl.pallas_call(
        functools.partial(_nsa_prompt_t_kernel, s_len=s),
        grid=(b, nkh, s // NSA_TQ),
        in_specs=[pl.BlockSpec((1, NSA_TQ, gw), lambda bi, h, i: (bi, i, COL_QB // NSA_GROUP + h)),
                  pl.BlockSpec((1, 1, LANE, HEAD_DIM), lambda bi, h, i: (bi, h, 0, 0)),
                  pl.BlockSpec((1, 1, LANE, HEAD_DIM), lambda bi, h, i: (bi, nkh + h, 0, 0)),
                  seq(COL_SLC), seq(COL_SLC + nkh), seq(COL_WIN), seq(COL_WIN + nkh),
                  pl.BlockSpec((1, NSA_TQ, LANE), lambda bi, h, i: (bi, i, 0))],
        out_specs=pl.BlockSpec((1, NSA_TQ, gw), lambda bi, h, i: (bi, i, h)),
        out_shape=jax.ShapeDtypeStruct((b, s, NSA_HEADS * HEAD_DIM), F32),
        scratch_shapes=kv_scr + kv_scr + [
            pltpu.VMEM((LANE, HEAD_DIM), BF16), pltpu.VMEM((HEAD_DIM, LANE), BF16),
            pltpu.VMEM((LANE, NSA_TQ), F32), pltpu.VMEM((n_rows, NSA_TQ), F32),
            pltpu.VMEM((HEAD_DIM, NSA_GROUP * NSA_TQ), F32)],
        compiler_params=pltpu.CompilerParams(
            dimension_semantics=("arbitrary", "arbitrary", "arbitrary"),
            vmem_limit_bytes=VMEM_LIMIT_BYTES),
        name="nsa_prompt",
    )(proj, kvc, kvc, proj, proj, proj, proj, g_b)


PAGE_SIZE = 128
PAGE_ROWS = PAGE_SIZE * 2 * NSA_KV_HEADS
PAGES_PER_STEP = 32
QROWS = 8
NQ = NSA_GROUP * QROWS


def _page_specs(n, l):
    def spec(k):
        return pl.BlockSpec((1, 1, PAGE_ROWS, HEAD_DIM), lambda b, j, pt: (l, pt[b, j * n + k], 0, 0))
    return [spec(k) for k in range(n)]


def _page_kv(ref, kv, h, rows=PAGE_SIZE):
    return ref[0, 0, pl.ds(kv * NSA_KV_HEADS + h, rows, stride=2 * NSA_KV_HEADS), :]


def _first_max_onehot(work, lane_f):
    mx = jnp.max(work, axis=-1, keepdims=True)
    first = jnp.min(jnp.where(work == mx, lane_f, 1e9), axis=-1, keepdims=True)
    return jnp.where(lane_f == first, 1.0, 0.0)


def _moba_sample_kernel(pt_ref, q_ref, knew_ref, vnew_ref, *rest, n_new, nb_past):
    pages = rest[:PAGES_PER_STEP]
    o_ref, m_scr, l_scr, acc_scr, kmean_scr = rest[PAGES_PER_STEP:]
    j = pl.program_id(1)
    nj = pl.num_programs(1)
    lane = lax.broadcasted_iota(jnp.int32, (NQ, LANE), 1)

    @pl.when(j == 0)
    def _():
        m_scr[...] = jnp.full_like(m_scr, NEG_INF)
        l_scr[...] = jnp.zeros_like(l_scr)
        kmean_scr[...] = jnp.zeros_like(kmean_scr)

    bps = PAGES_PER_STEP * PAGE_SIZE // MOBA_BLOCK
    for h in range(MOBA_KV_HEADS):
        qb = q_ref[0, h].astype(BF16)
        k_all = jnp.concatenate([_page_kv(pg, 0, h) for pg in pages], axis=0)
        v_all = jnp.concatenate([_page_kv(pg, 1, h) for pg in pages], axis=0).astype(BF16)
        s = lax.dot_general(qb, k_all.astype(BF16), _NT, preferred_element_type=F32) * ATT_SCALE
        blocks = [slice(u * MOBA_BLOCK, (u + 1) * MOBA_BLOCK) for u in range(bps)]
        m_bs = [jnp.max(s[:, sl], axis=-1, keepdims=True) for sl in blocks]
        p = jnp.exp(s - jnp.concatenate([jnp.broadcast_to(m_b, (NQ, MOBA_BLOCK)) for m_b in m_bs],
                                        axis=1))
        m_new, l_new = m_scr[h], l_scr[h]
        for u, sl in enumerate(blocks):
            blk = j * bps + u
            m_new = jnp.where(lane == blk, m_bs[u], m_new)
            l_new = jnp.where(lane == blk, jnp.sum(p[:, sl], axis=-1, keepdims=True), l_new)
            acc_scr[h, blk] = jnp.dot(p[:, sl].astype(BF16), v_all[sl], preferred_element_type=F32)
            kmean_scr[h, pl.ds(blk, 1), :] = (jnp.sum(k_all[sl], axis=0, keepdims=True)
                                             * (1.0 / MOBA_BLOCK))
        m_scr[h] = m_new
        l_scr[h] = l_new

    @pl.when(j == nj - 1)
    def _():
        lane_f = lane.astype(F32)
        trow = lax.broadcasted_iota(jnp.int32, (NQ, LANE), 0) & (QROWS - 1)
        for h in range(MOBA_KV_HEADS):
            q = q_ref[0, h]
            qb = q.astype(BF16)
            s_own = lax.dot_general(qb, knew_ref[0, h].astype(BF16), _NT,
                                    preferred_element_type=F32) * ATT_SCALE
            s_own = jnp.where((lane <= trow) & (lane < n_new), s_own, NEG_INF)
            m_own = jnp.max(s_own, axis=-1, keepdims=True)
            p_own = jnp.exp(s_own - m_own)
            l_own = jnp.sum(p_own, axis=-1, keepdims=True)
            acc_own = jnp.dot(p_own.astype(BF16), vnew_ref[0, h].astype(BF16),
                              preferred_element_type=F32)
            gate = lax.dot_general(q, kmean_scr[h], _NT, precision=lax.Precision.HIGHEST,
                                   preferred_element_type=F32)
            work = jnp.where(lane < nb_past, gate, -jnp.inf)
            sel = jnp.zeros((NQ, LANE), F32)
            for _ in range(min(MOBA_TOPK, nb_past)):
                pick = _first_max_onehot(work, lane_f)
                sel = jnp.maximum(sel, pick)
                work = jnp.where(pick > 0.5, -jnp.inf, work)
            chosen = sel > 0.5
            m_all = m_scr[h]
            m_tot = jnp.maximum(m_own, jnp.max(jnp.where(chosen, m_all, NEG_INF), axis=-1,
                                               keepdims=True))
            w = jnp.where(chosen, jnp.exp(m_all - m_tot), 0.0)
            w_own = jnp.exp(m_own - m_tot)
            l_tot = w_own * l_own + jnp.sum(w * l_scr[h], axis=-1, keepdims=True)

            def merge(blk, acc, w=w, h=h):
                wcol = jnp.sum(jnp.where(lane == blk, w, 0.0), axis=-1, keepdims=True)
                return acc + wcol * acc_scr[h, blk]

            acc = lax.fori_loop(0, nb_past, merge, w_own * acc_own)
            o_ref[0, h] = acc / l_tot


def _moba_sample(q, k_new, v_new, pool, page_table, l, n_new):
    b = q.shape[0]
    n_pages = page_table.shape[1]
    assert n_pages % PAGES_PER_STEP == 0 and (n_pages * PAGE_SIZE) % MOBA_BLOCK == 0
    nb_past = n_pages * PAGE_SIZE // MOBA_BLOCK
    assert nb_past <= LANE and n_new <= MOBA_BLOCK
    per_b = lambda shape: pl.BlockSpec(shape, lambda bi, j, pt: (bi,) + (0,) * (len(shape) - 1))
    grid_spec = pltpu.PrefetchScalarGridSpec(
        num_scalar_prefetch=1,
        grid=(b, n_pages // PAGES_PER_STEP),
        in_specs=[per_b((1, MOBA_KV_HEADS, NQ, HEAD_DIM)),
                  per_b((1, MOBA_KV_HEADS, LANE, HEAD_DIM)),
                  per_b((1, MOBA_KV_HEADS, LANE, HEAD_DIM))] + _page_specs(PAGES_PER_STEP, l),
        out_specs=per_b((1, MOBA_KV_HEADS, NQ, HEAD_DIM)),
        scratch_shapes=[pltpu.VMEM((MOBA_KV_HEADS, NQ, LANE), F32),
                        pltpu.VMEM((MOBA_KV_HEADS, NQ, LANE), F32),
                        pltpu.VMEM((MOBA_KV_HEADS, nb_past, NQ, HEAD_DIM), F32),
                        pltpu.VMEM((MOBA_KV_HEADS, LANE, HEAD_DIM), F32)])
    return pl.pallas_call(
        functools.partial(_moba_sample_kernel, n_new=n_new, nb_past=nb_past),
        grid_spec=grid_spec,
        out_shape=jax.ShapeDtypeStruct(q.shape, F32),
        compiler_params=pltpu.CompilerParams(dimension_semantics=("arbitrary", "arbitrary"),
                                             vmem_limit_bytes=VMEM_LIMIT_BYTES),
        name="moba_sample",
    )(page_table, q, k_new, v_new, *([pool] * PAGES_PER_STEP))


def _compress_sample_kernel(pt_ref, pos_ref, w1_ref, b1_ref, w2_ref, b2_ref, *rest):
    pages = rest[:PAGES_PER_STEP + 1]
    o_ref = rest[PAGES_PER_STEP + 1]
    cpp = PAGE_SIZE // CMP_STRIDE
    nch = PAGES_PER_STEP * cpp
    half = CMP_STRIDE * HEAD_DIM
    n_col = 2 * NSA_KV_HEADS
    ys = [jnp.swapaxes(pg[0, 0].reshape(cpp, CMP_STRIDE * n_col, HEAD_DIM), 0, 1) for pg in pages]
    for kv in range(2):
        pos_first = jnp.concatenate([pos_ref[kv, r:r + 1, :] for r in range(CMP_STRIDE)], axis=1)
        pos_second = jnp.concatenate([pos_ref[kv, r:r + 1, :] for r in range(CMP_STRIDE, CMP_LEN)],
                                     axis=1)
        w_first = w1_ref[kv, :half, :].astype(BF16)
        w_second = w1_ref[kv, half:, :].astype(BF16)
        w2 = w2_ref[kv].astype(BF16)
        for h in range(NSA_KV_HEADS):
            c = kv * NSA_KV_HEADS + h
            x = jnp.concatenate(
                [jnp.concatenate([y[r * n_col + c] for y in ys], axis=0)
                 for r in range(CMP_STRIDE)], axis=1)
            first = jnp.dot((x[:nch] + pos_first).astype(BF16), w_first, preferred_element_type=F32)
            second = jnp.dot((x + pos_second).astype(BF16), w_second, preferred_element_type=F32)
            pre = first + pltpu.roll(second, nch + cpp - 1, axis=0)[:nch] + b1_ref[kv:kv + 1, :]
            hdn = jax.nn.gelu(pre)
            o_ref[0, c] = (jnp.dot(hdn.astype(BF16), w2, preferred_element_type=F32)
                           + b2_ref[kv:kv + 1, :])


def _compress_sample(pool, page_table, l, pos, w1, b1, w2, b2):
    b, n_pages = page_table.shape
    nch = PAGES_PER_STEP * PAGE_SIZE // CMP_STRIDE
    full = lambda shape: pl.BlockSpec(shape, lambda bi, j, pt: (0,) * len(shape))
    next_page = pl.BlockSpec(
        (1, 1, PAGE_ROWS, HEAD_DIM),
        lambda bi, j, pt: (l, pt[bi, jnp.minimum((j + 1) * PAGES_PER_STEP, n_pages - 1)], 0, 0))
    grid_spec = pltpu.PrefetchScalarGridSpec(
        num_scalar_prefetch=1,
        grid=(b, n_pages // PAGES_PER_STEP),
        in_specs=[full(pos.shape), full(w1.shape), full(b1.shape), full(w2.shape), full(b2.shape)]
        + _page_specs(PAGES_PER_STEP, l) + [next_page],
        out_specs=pl.BlockSpec((1, 2 * NSA_KV_HEADS, nch, HEAD_DIM), lambda bi, j, pt: (bi, 0, j, 0)))
    return pl.pallas_call(
        _compress_sample_kernel,
        grid_spec=grid_spec,
        out_shape=jax.ShapeDtypeStruct(
            (b, 2 * NSA_KV_HEADS, n_pages * PAGE_SIZE // CMP_STRIDE, HEAD_DIM), F32),
        compiler_params=pltpu.CompilerParams(dimension_semantics=("arbitrary", "arbitrary"),
                                             vmem_limit_bytes=VMEM_LIMIT_BYTES),
        name="nsa_compress_sample",
    )(page_table, pos, w1, b1, w2, b2, *([pool] * (PAGES_PER_STEP + 1)))


def _nsa_cmp_sample_kernel(q_ref, kc_ref, vc_ref, oc_ref, sel_ref, *, q0, n_cmp, n_slc, n_lanes):
    ncp = kc_ref.shape[2]
    col = lax.broadcasted_iota(jnp.int32, (NQ, ncp), 1)
    trow = lax.broadcasted_iota(jnp.int32, (NQ, ncp), 0) & (QROWS - 1)
    avail = (col * CMP_STRIDE + (CMP_LEN - 1) <= q0 + trow) & (col < n_cmp)
    nn = lax.broadcasted_iota(jnp.int32, (ncp, n_lanes), 0)
    jj = lax.broadcasted_iota(jnp.int32, (ncp, n_lanes), 1)
    sel_map = jnp.where((nn * CMP_STRIDE < jj * SLC_BLOCK + SLC_BLOCK)
                        & (nn * CMP_STRIDE + CMP_LEN > jj * SLC_BLOCK)
                        & (nn < n_cmp) & (jj < n_slc), 1.0, 0.0)
    lane = lax.broadcasted_iota(jnp.int32, (QROWS, n_lanes), 1)
    lane_f = lane.astype(F32)
    cur = (q0 + lax.broadcasted_iota(jnp.int32, (QROWS, n_lanes), 0)) // SLC_BLOCK
    for h in range(NSA_KV_HEADS):
        qb = q_ref[0, h].astype(BF16)
        s = lax.dot_general(qb, kc_ref[0, h].astype(BF16), _NT,
                            preferred_element_type=F32) * ATT_SCALE
        s = jnp.where(avail, s, NEG_INF)
        e = jnp.where(avail, jnp.exp(s - jnp.max(s, axis=-1, keepdims=True)), 0.0)
        den = jnp.sum(e, axis=-1, keepdims=True)
        p = e * jnp.where(den > 0.0, 1.0 / den, 0.0)
        oc_ref[0, h] = jnp.dot(p.astype(BF16), vc_ref[0, h].astype(BF16),
                               preferred_element_type=F32)
        pg = sum(p[g * QROWS:(g + 1) * QROWS] for g in range(NSA_GROUP))
        imp = jnp.dot(pg, sel_map, precision=lax.Precision.HIGHEST, preferred_element_type=F32)
        forced = (lane == 0) | (lane == cur) | (lane == cur - 1)
        imp = jnp.where(forced, BIG, imp)
        imp = jnp.where(lane > cur, NEG_INF, imp)
        work = jnp.where(lane < n_slc, imp, -jnp.inf)
        sel = jnp.zeros((QROWS, n_lanes), F32)
        for _ in range(min(SLC_TOPN, n_slc)):
            pick = _first_max_onehot(work, lane_f)
            sel = jnp.maximum(sel, pick)
            work = jnp.where(pick > 0.5, -jnp.inf, work)
        sel_ref[0, h] = sel


def _nsa_cmp_sample(q, kvc, q0, n_new):
    b = q.shape[0]
    ncp = kvc.shape[2]
    n_cmp = ncp - CMP_LEN // CMP_STRIDE + 1
    n_slc = -(-(q0 + n_new) // SLC_BLOCK)
    n_lanes = _round_up(n_slc, LANE)
    nkh = NSA_KV_HEADS
    return pl.pallas_call(
        functools.partial(_nsa_cmp_sample_kernel, q0=q0, n_cmp=n_cmp, n_slc=n_slc, n_lanes=n_lanes),
        grid=(b,),
        in_specs=[pl.BlockSpec((1, nkh, NQ, HEAD_DIM), lambda bi: (bi, 0, 0, 0)),
                  pl.BlockSpec((1, nkh, ncp, HEAD_DIM), lambda bi: (bi, 0, 0, 0)),
                  pl.BlockSpec((1, nkh, ncp, HEAD_DIM), lambda bi: (bi, 1, 0, 0))],
        out_specs=[pl.BlockSpec((1, nkh, NQ, HEAD_DIM), lambda bi: (bi, 0, 0, 0)),
                   pl.BlockSpec((1, nkh, QROWS, n_lanes), lambda bi: (bi, 0, 0, 0))],
        out_shape=[jax.ShapeDtypeStruct(q.shape, F32),
                   jax.ShapeDtypeStruct((b, nkh, QROWS, n_lanes), F32)],
        compiler_params=pltpu.CompilerParams(dimension_semantics=("arbitrary",),
                                             vmem_limit_bytes=VMEM_LIMIT_BYTES),
        name="nsa_cmp_sample",
    )(q, kvc, kvc)


def _nsa_slc_sample_kernel(pt_ref, q_ref, sel_ref, oc_ref, g_ref, ksn_ref, vsn_ref, kwn_ref,
                           vwn_ref, win_ref, *rest, n_new, n_win, cur_blk):
    pages = rest[:PAGES_PER_STEP]
    o_ref, m_scr, l_scr, acc_scr, ow_scr = rest[PAGES_PER_STEP:]
    j = pl.program_id(1)
    nj = pl.num_programs(1)
    n_lanes = sel_ref.shape[3]
    lane = lax.broadcasted_iota(jnp.int32, (NQ, LANE), 1)
    trow = lax.broadcasted_iota(jnp.int32, (NQ, LANE), 0) & (QROWS - 1)
    new_ok = (lane <= trow) & (lane < n_new)

    @pl.when(j == 0)
    def _():
        for h in range(NSA_KV_HEADS):
            qb = q_ref[0, h].astype(BF16)
            sel_h = jnp.concatenate([sel_ref[0, h]] * NSA_GROUP, axis=0)
            lane_s = lax.broadcasted_iota(jnp.int32, (NQ, n_lanes), 1)
            sel_cur = jnp.sum(jnp.where(lane_s == cur_blk, sel_h, 0.0), axis=-1,
                              keepdims=True) > 0.5
            s = lax.dot_general(qb, ksn_ref[0, h].astype(BF16), _NT,
                                preferred_element_type=F32) * ATT_SCALE
            s = jnp.where(new_ok & sel_cur, s, NEG_INF)
            m = jnp.max(s, axis=-1, keepdims=True)
            p = jnp.exp(s - m)
            m_scr[h] = m
            l_scr[h] = jnp.sum(p, axis=-1, keepdims=True)
            acc_scr[h] = jnp.dot(p.astype(BF16), vsn_ref[0, h].astype(BF16),
                                 preferred_element_type=F32)
            kw = win_ref[0, 0, pl.ds(h, n_win, stride=2 * NSA_KV_HEADS), :]
            vw = win_ref[0, 0, pl.ds(NSA_KV_HEADS + h, n_win, stride=2 * NSA_KV_HEADS), :]
            idx = lax.broadcasted_iota(jnp.int32, (NQ, n_win), 1)
            tr = lax.broadcasted_iota(jnp.int32, (NQ, n_win), 0) & (QROWS - 1)
            s_w = lax.dot_general(qb, kw.astype(BF16), _NT, preferred_element_type=F32) * ATT_SCALE
            s_w = jnp.where(idx + (WINDOW - n_win) > tr, s_w, NEG_INF)
            s_n = lax.dot_general(qb, kwn_ref[0, h].astype(BF16), _NT,
                                  preferred_element_type=F32) * ATT_SCALE
            s_n = jnp.where(new_ok, s_n, NEG_INF)
            mw = jnp.maximum(jnp.max(s_w, axis=-1, keepdims=True),
                             jnp.max(s_n, axis=-1, keepdims=True))
            p_w = jnp.exp(s_w - mw)
            p_n = jnp.exp(s_n - mw)
            lw = jnp.sum(p_w, axis=-1, keepdims=True) + jnp.sum(p_n, axis=-1, keepdims=True)
            ow_scr[h] = (jnp.dot(p_w.astype(BF16), vw.astype(BF16), preferred_element_type=F32)
                         + jnp.dot(p_n.astype(BF16), vwn_ref[0, h].astype(BF16),
                                   preferred_element_type=F32)) / lw

    step_keys = PAGES_PER_STEP * PAGE_SIZE
    erow = lax.broadcasted_iota(jnp.int32, (n_lanes, step_keys), 0)
    ecol = lax.broadcasted_iota(jnp.int32, (n_lanes, step_keys), 1)
    expand = jnp.where((j * step_keys + ecol) // SLC_BLOCK == erow, 1.0, 0.0).astype(BF16)
    for h in range(NSA_KV_HEADS):
        qb = q_ref[0, h].astype(BF16)
        sel_h = jnp.concatenate([sel_ref[0, h]] * NSA_GROUP, axis=0).astype(BF16)
        maskf = jnp.dot(sel_h, expand, preferred_element_type=F32)
        ss = []
        for k in range(PAGES_PER_STEP):
            s = lax.dot_general(qb, _page_kv(pages[k], 0, h).astype(BF16), _NT,
                                preferred_element_type=F32) * ATT_SCALE
            ss.append(jnp.where(maskf[:, k * PAGE_SIZE:(k + 1) * PAGE_SIZE] > 0.5, s, NEG_INF))
        m_old = m_scr[h]
        m_new = functools.reduce(jnp.maximum,
                                 [jnp.max(s, axis=-1, keepdims=True) for s in ss] + [m_old])
        alpha = jnp.exp(m_old - m_new)
        ps = [jnp.exp(s - m_new) for s in ss]
        l_scr[h] = alpha * l_scr[h] + sum(jnp.sum(p, axis=-1, keepdims=True) for p in ps)
        acc_scr[h] = alpha * acc_scr[h] + sum(
            jnp.dot(p.astype(BF16), _page_kv(pages[k], 1, h).astype(BF16),
                    preferred_element_type=F32) for k, p in enumerate(ps))
        m_scr[h] = m_new

    @pl.when(j == nj - 1)
    def _():
        for h in range(NSA_KV_HEADS):
            gates = jax.nn.sigmoid(g_ref[0, h])
            o_s = acc_scr[h] / l_scr[h]
            o_ref[0, h] = (gates[:, 0:1] * oc_ref[0, h] + gates[:, 1:2] * o_s
                           + gates[:, 2:3] * ow_scr[h])


def _nsa_slc_sample(q, sel, o_c, g3, ks_new, vs_new, kw_new, vw_new, win_state, pool, page_table,
                    l, n_new):
    b = q.shape[0]
    n_pages = page_table.shape[1]
    n_win = win_state.shape[2] // (2 * NSA_KV_HEADS)
    assert (n_pages * PAGE_SIZE) % SLC_BLOCK == 0 and n_new <= SLC_BLOCK and n_win <= WINDOW
    nkh = NSA_KV_HEADS
    per_b = lambda shape: pl.BlockSpec(shape, lambda bi, j, pt: (bi,) + (0,) * (len(shape) - 1))
    grid_spec = pltpu.PrefetchScalarGridSpec(
        num_scalar_prefetch=1,
        grid=(b, n_pages // PAGES_PER_STEP),
        in_specs=[per_b((1, nkh, NQ, HEAD_DIM)), per_b((1, nkh, QROWS, sel.shape[3])),
                  per_b((1, nkh, NQ, HEAD_DIM)), per_b((1, nkh, NQ, LANE)),
                  per_b((1, nkh, LANE, HEAD_DIM)), per_b((1, nkh, LANE, HEAD_DIM)),
                  per_b((1, nkh, LANE, HEAD_DIM)), per_b((1, nkh, LANE, HEAD_DIM)),
                  pl.BlockSpec((1, 1, win_state.shape[2], HEAD_DIM), lambda bi, j, pt: (l, bi, 0, 0))]
        + _page_specs(PAGES_PER_STEP, l),
        out_specs=per_b((1, nkh, NQ, HEAD_DIM)),
        scratch_shapes=[pltpu.VMEM((nkh, NQ, 1), F32), pltpu.VMEM((nkh, NQ, 1), F32),
                        pltpu.VMEM((nkh, NQ, HEAD_DIM), F32), pltpu.VMEM((nkh, NQ, HEAD_DIM), F32)])
    return pl.pallas_call(
        functools.partial(_nsa_slc_sample_kernel, n_new=n_new, n_win=n_win,
                          cur_blk=n_pages * PAGE_SIZE // SLC_BLOCK),
        grid_spec=grid_spec,
        out_shape=jax.ShapeDtypeStruct(q.shape, F32),
        compiler_params=pltpu.CompilerParams(dimension_semantics=("arbitrary", "arbitrary"),
                                             vmem_limit_bytes=VMEM_LIMIT_BYTES),
        name="nsa_slc_sample",
    )(page_table, q, sel, o_c, g3, ks_new, vs_new, kw_new, vw_new, win_state,
      *([pool] * PAGES_PER_STEP))


PROMPT_TM_IN = 1024
PROMPT_TM_OUT = 256
MOE_TM = 128
PROMPT_MOE_BLOCK = 128
SAMPLE_MOE_BLOCK = 8


def _kv_rows(proj, col, b, s):
    lo = col * HEAD_DIM
    return proj[:, lo:lo + KV_COLS].reshape(b, s, 2, NSA_KV_HEADS, HEAD_DIM)


def _to_qrows(x, b, s):
    x = x.reshape(b, s, NSA_KV_HEADS, NSA_GROUP, HEAD_DIM).transpose(0, 2, 3, 1, 4)
    x = jnp.pad(x, ((0, 0), (0, 0), (0, 0), (0, QROWS - s), (0, 0)))
    return x.reshape(b, NSA_KV_HEADS, NQ, HEAD_DIM)


def _from_qrows(o, b, s):
    o = o.reshape(b, NSA_KV_HEADS, NSA_GROUP, QROWS, HEAD_DIM)[:, :, :, :s]
    return o.transpose(0, 3, 1, 2, 4).reshape(b * s, NSA_HEADS * HEAD_DIM)


def _new_rows(proj, col, b, s):
    lo = col * HEAD_DIM
    x = proj[:, lo:lo + NSA_KV_HEADS * HEAD_DIM].reshape(b, s, NSA_KV_HEADS, HEAD_DIM)
    return jnp.pad(x.transpose(0, 2, 1, 3), ((0, 0), (0, 0), (0, LANE - s), (0, 0)))


def _sample_mixer(proj, g_b, pools, win_state, page_table, cmp_w, l, b, s, q0):
    assert MOBA_KV_HEADS == NSA_KV_HEADS and MOBA_GROUP == NSA_GROUP and s <= QROWS
    assert q0 % CMP_STRIDE == 0 and s < CMP_STRIDE and q0 % MOBA_BLOCK == 0
    pool_moba, pool_cmp, pool_slc = pools
    nkh = NSA_KV_HEADS
    q_a = _to_qrows(proj[:, :MOBA_HEADS * HEAD_DIM], b, s)
    q_b = _to_qrows(proj[:, COL_QB * HEAD_DIM:COL_CMP * HEAD_DIM], b, s)
    o_a = _moba_sample(q_a, _new_rows(proj, COL_KA, b, s), _new_rows(proj, COL_VA, b, s),
                       pool_moba, page_table, l, s)
    kvc = _compress_sample(pool_cmp, page_table, l, *cmp_w)
    o_c, sel = _nsa_cmp_sample(q_b, kvc, q0, s)
    g3 = g_b[:, :GB_COLS].reshape(b, s, 3, nkh, NSA_GROUP).transpose(0, 3, 4, 1, 2)
    g3 = jnp.pad(g3, ((0, 0), (0, 0), (0, 0), (0, QROWS - s), (0, LANE - 3)))
    o_b = _nsa_slc_sample(q_b, sel, o_c, g3.reshape(b, nkh, NQ, LANE),
                          _new_rows(proj, COL_SLC, b, s), _new_rows(proj, COL_SLC + nkh, b, s),
                          _new_rows(proj, COL_WIN, b, s), _new_rows(proj, COL_WIN + nkh, b, s),
                          win_state, pool_slc, page_table, l, s)
    return _from_qrows(o_a, b, s), _from_qrows(o_b, b, s)


def _moe_block(x, h, logits, g2, norm_final, w_gate, w_up, w_down, l, bm, tm, rows_per_mod,
               final_norm):
    blk_expert, seg_start, order, n_used, w_tok, d0, d1 = _route(logits, bm)
    yb = _moe_ffn(h, blk_expert, seg_start, order, n_used, w_gate, w_up, w_down, l, bm)
    return _moe_combine(x, g2, w_tok, norm_final, yb, d0, d1, tm, rows_per_mod, final_norm)


def kernel(x_prompt, x_sample, cache_moba_kv, cache_cmp_kv, cache_slc_kv, state_win_kv,
           page_table, c_prompt, c_sample, w_in, w_pa, w_pb, w_out, cmp_pos, cmp_w1, cmp_b1,
           cmp_w2, cmp_b2, norm_attn, norm_ffn, norm_final, w_ada, b_ada, w_rg, b_rg, w_re,
           b_re, w_gate, w_up, w_down):
    bp, sp, d = x_prompt.shape
    bs, ss, _ = x_sample.shape
    tp, ts = bp * sp, bs * ss
    past_len = page_table.shape[1] * cache_moba_kv.shape[2]
    xp = x_prompt.reshape(tp, d)
    xs = x_sample.reshape(ts, d)
    cos_p, sin_p = _rope_tables(jnp.tile(jnp.arange(sp), bp))
    cos_s, sin_s = _rope_tables(jnp.tile(past_len + jnp.arange(ss), bs))
    n_c = _round_up(bp + bs, 8)
    c_all = jnp.pad(jnp.concatenate([c_prompt, c_sample], axis=0), ((0, n_c - bp - bs), (0, 0)))
    n_pool = cache_moba_kv.shape[1]
    pools = tuple(c.reshape(DEPTH, n_pool, PAGE_ROWS, HEAD_DIM)
                  for c in (cache_moba_kv, cache_cmp_kv, cache_slc_kv))
    n_win = state_win_kv.shape[2]
    win_view = state_win_kv.reshape(DEPTH, bs, n_win * 2 * NSA_KV_HEADS, HEAD_DIM)
    rows_p, rows_s = [], []
    for l in range(DEPTH):
        last = l == DEPTH - 1
        ada = _ada(c_all, w_ada, b_ada, l).reshape(n_c, 6, d)
        mod_p = [ada[:bp, k].reshape(bp, 1, d) for k in range(6)]
        mod_s = [jnp.repeat(ada[bp:bp + bs, k], ss, axis=0).reshape(1, ts, d) for k in range(6)]
        w_main = jnp.concatenate([w_in[l][:, :QKV_COLS], w_in[l][:, QKV_COLS + GB_COLS:]],
                                 axis=1).astype(BF16)
        w_gb = jnp.pad(w_in[l][:, QKV_COLS:QKV_COLS + GB_COLS],
                       ((0, 0), (0, LANE - GB_COLS))).astype(BF16)
        w_pa_b, w_pb_b, w_out_b = w_pa[l].astype(BF16), w_pb[l].astype(BF16), w_out[l].astype(BF16)
        w_r = jnp.pad(jnp.concatenate([w_rg[l], w_re[l]], axis=1),
                      ((0, 0), (0, LANE - N_GROUPS - N_EXPERTS)))
        b_r = jnp.pad(jnp.concatenate([b_rg[l], b_re[l]]),
                      (0, LANE - N_GROUPS - N_EXPERTS)).reshape(1, LANE)

        sh1, sc1, g1, sh2, sc2, g2 = mod_p
        proj, gates, g_b = _in_proj(xp, norm_attn[l], sc1, sh1, cos_p, sin_p, w_main, w_gb,
                                    PROMPT_TM_IN, sp)
        proj3 = proj.reshape(bp, sp, QKV_COLS)
        o_a = _moba_prompt_t(proj3)
        kvc = _compress_prompt(proj3, cmp_pos[l], cmp_w1[l], cmp_b1[l], cmp_w2[l], cmp_b2[l])
        o_b = _nsa_prompt_t(proj3, kvc, g_b.reshape(bp, sp, LANE))
        xp, h2, logits = _out_proj(o_a.reshape(tp, -1), o_b.reshape(tp, -1), gates, xp, g1,
                                   norm_ffn[l], sc2, sh2, w_pa_b, w_pb_b, w_out_b, w_r, b_r,
                                   PROMPT_TM_OUT, sp)
        xp = _moe_block(xp, h2, logits, g2, norm_final, w_gate, w_up, w_down, l,
                        PROMPT_MOE_BLOCK, MOE_TM, sp, last)
        new_win = _kv_rows(proj, COL_WIN, bp, sp)
        rows_p.append((_kv_rows(proj, COL_KA, bp, sp), _kv_rows(proj, COL_CMP, bp, sp),
                       _kv_rows(proj, COL_SLC, bp, sp), new_win[:, sp - min(WINDOW, sp):]))

        sh1, sc1, g1, sh2, sc2, g2 = mod_s
        proj, gates, g_b = _in_proj(xs, norm_attn[l], sc1, sh1, cos_s, sin_s, w_main, w_gb, ts, ss)
        cmp_w = (cmp_pos[l], cmp_w1[l], cmp_b1[l], cmp_w2[l], cmp_b2[l])
        o_a, o_b = _sample_mixer(proj, g_b, pools, win_view, page_table, cmp_w, l, bs, ss, past_len)
        xs, h2, logits = _out_proj(o_a, o_b, gates, xs, g1, norm_ffn[l], sc2, sh2,
                                   w_pa_b, w_pb_b, w_out_b, w_r, b_r, ts, ss)
        xs = _moe_block(xs, h2, logits, g2, norm_final, w_gate, w_up, w_down, l,
                        SAMPLE_MOE_BLOCK, ts, ss, last)
        win_rows = jnp.concatenate([state_win_kv[l], _kv_rows(proj, COL_WIN, bs, ss)], axis=1)
        rows_s.append((_kv_rows(proj, COL_KA, bs, ss), _kv_rows(proj, COL_CMP, bs, ss),
                       _kv_rows(proj, COL_SLC, bs, ss),
                       win_rows[:, win_rows.shape[1] - min(WINDOW, win_rows.shape[1]):]))

    return (xp.reshape(bp, sp, d), xs.reshape(bs, ss, d),
            jnp.stack([r[0] for r in rows_p]), jnp.stack([r[0] for r in rows_s]),
            jnp.stack([r[1] for r in rows_p]), jnp.stack([r[1] for r in rows_s]),
            jnp.stack([r[2] for r in rows_p]), jnp.stack([r[2] for r in rows_s]),
            jnp.stack([r[3] for r in rows_p]), jnp.stack([r[3] for r in rows_s]))
```

```python
import functools

import jax
import jax.numpy as jnp
import numpy as np
from jax import lax
from jax.experimental import pallas as pl
from jax.experimental.pallas import tpu as pltpu

D_MODEL = 2048
DEPTH = 2
HEAD_DIM = 128
ROT_DIM = HEAD_DIM // 4
ROPE_THETA = 500000.0
NORM_EPS = 1e-6
MOBA_HEADS = 8
MOBA_KV_HEADS = 2
MOBA_GROUP = MOBA_HEADS // MOBA_KV_HEADS
MOBA_BLOCK = 256
MOBA_TOPK = 3
MOBA_QCHUNK = 16
NSA_HEADS = 8
NSA_KV_HEADS = 2
NSA_GROUP = NSA_HEADS // NSA_KV_HEADS
CMP_LEN = 32
CMP_STRIDE = 16
CMP_HIDDEN = 128
SLC_BLOCK = 64
SLC_TOPN = 16
WINDOW = 512
NSA_QCHUNK = 16
WIN_QBLOCK = 128
N_GROUPS = 4
EXPERTS_PER_GROUP = 8
N_EXPERTS = N_GROUPS * EXPERTS_PER_GROUP
EXPERT_TOPK = 2
D_EXPERT = 512
MOE_BLOCK = 128

NEG_INF = -1e30
BIG = 1e30
F32 = jnp.float32
BF16 = jnp.bfloat16

LANE = 128
VMEM_LIMIT_BYTES = 48 * 1024 * 1024
QKV_TM_MAX = 512


def _round_up(n, m):
    return (n + m - 1) // m * m


QKV_COLS = 4096
GB_COLS = 3 * NSA_HEADS
GATE_COLS = 2 * D_MODEL
PROJ_COLS = QKV_COLS + GATE_COLS
COL_QA, COL_KA, COL_VA, COL_QB = 0, 8, 10, 12
COL_CMP, COL_SLC, COL_WIN = 20, 24, 28


def _mod_spec(mod, tm, rows_per_mod):
    d = mod.shape[-1]
    if mod.shape[1] == 1:
        return pl.BlockSpec((1, 1, d), lambda i, *_: (i * tm // rows_per_mod, 0, 0))
    return pl.BlockSpec((1, tm, d), lambda i, *_: (i, 0, 0))


def _ada_kernel(c_ref, w_ref, b_ref, o_ref):
    c = jax.nn.silu(c_ref[...]).astype(BF16)
    o_ref[...] = jnp.dot(c, w_ref[0].astype(BF16), preferred_element_type=F32) + b_ref[0]


def _ada(c, w_ada, b_ada, l, tn=1024):
    m, k = c.shape
    depth, _, n = w_ada.shape
    return pl.pallas_call(
        _ada_kernel,
        grid=(n // tn,),
        in_specs=[pl.BlockSpec((m, k), lambda j: (0, 0)),
                  pl.BlockSpec((1, k, tn), lambda j: (l, 0, j)),
                  pl.BlockSpec((1, 1, tn), lambda j: (l, 0, j))],
        out_specs=pl.BlockSpec((m, tn), lambda j: (0, j)),
        out_shape=jax.ShapeDtypeStruct((m, n), F32),
        compiler_params=pltpu.CompilerParams(dimension_semantics=("arbitrary",),
                                             vmem_limit_bytes=VMEM_LIMIT_BYTES),
        name="ada_ln",
    )(c, w_ada, b_ada.reshape(depth, 1, n))


def _rope_tables(pos):
    half = ROT_DIM // 2
    inv_freq = ROPE_THETA ** (-jnp.arange(half, dtype=F32) / half)
    ang = pos.astype(F32)[:, None] * inv_freq[None, :]
    cos, sin = jnp.cos(ang), jnp.sin(ang)
    n = pos.shape[0]
    cos_t = jnp.concatenate([cos, cos, jnp.ones((n, HEAD_DIM - ROT_DIM), F32)], axis=1)
    sin_t = jnp.concatenate([-sin, sin, jnp.zeros((n, HEAD_DIM - ROT_DIM), F32)], axis=1)
    return cos_t, sin_t


def _rope_column_mask():
    m = np.zeros((1, QKV_COLS), np.float32)
    for lo, hi in ((0, 1280), (1536, 2560), (3072, 3328), (3584, 3840)):
        m[0, lo:hi] = 1.0
    return jnp.asarray(m)


def _norm_modulate(x_ref, g_ref, sc_ref, sh_ref):
    x = x_ref[...]
    y = x * lax.rsqrt(jnp.mean(x * x, axis=-1, keepdims=True) + NORM_EPS) * g_ref[...]
    return (y * (1.0 + sc_ref[0]) + sh_ref[0]).astype(BF16)


def _in_proj_qkv_kernel(x_ref, g_ref, sc_ref, sh_ref, cos_ref, sin_ref, rmask_ref, w_ref, wgb_ref,
                        o_ref, ogb_ref, kvm_ref, kvc_ref, kvs_ref, kvw_ref, hb_scr, *, tn):
    j = pl.program_id(1)

    @pl.when(j == 0)
    def _():
        h = _norm_modulate(x_ref, g_ref, sc_ref, sh_ref)
        hb_scr[...] = h
        ogb_ref[...] = jnp.dot(h, wgb_ref[...], preferred_element_type=F32)

    acc = jnp.dot(hb_scr[...], w_ref[...], preferred_element_type=F32)
    lane = lax.broadcasted_iota(jnp.int32, (acc.shape[0], HEAD_DIM), 1)
    cos_t, sin_t = cos_ref[...], sin_ref[...]
    for hh in range(tn // HEAD_DIM):
        blk = acc[:, hh * HEAD_DIM:(hh + 1) * HEAD_DIM]
        on = rmask_ref[:, hh * HEAD_DIM:(hh + 1) * HEAD_DIM] > 0.5
        partner = jnp.where(lane < ROT_DIM // 2,
                            pltpu.roll(blk, HEAD_DIM - ROT_DIM // 2, axis=1),
                            pltpu.roll(blk, ROT_DIM // 2, axis=1))
        o_ref[:, hh * HEAD_DIM:(hh + 1) * HEAD_DIM] = (
            blk * jnp.where(on, cos_t, 1.0) + partner * jnp.where(on, sin_t, 0.0))

    n_col = KV_COLS // HEAD_DIM
    rows = acc.shape[0]

    def emit(dst_ref, col):
        tile, first = divmod(col, tn // HEAD_DIM)

        @pl.when(j == tile)
        def _():
            for c in range(n_col):
                dst_ref[pl.ds(c, rows, stride=n_col), :] = (
                    o_ref[:, (first + c) * HEAD_DIM:(first + c + 1) * HEAD_DIM])

    emit(kvm_ref, COL_KA)
    emit(kvc_ref, COL_CMP)
    emit(kvs_ref, COL_SLC)
    emit(kvw_ref, COL_WIN)


def _in_proj_gate_kernel(x_ref, g_ref, sc_ref, sh_ref, w_ref, o_ref, hb_scr):
    @pl.when(pl.program_id(1) == 0)
    def _():
        hb_scr[...] = _norm_modulate(x_ref, g_ref, sc_ref, sh_ref)

    o_ref[...] = jax.nn.sigmoid(jnp.dot(hb_scr[...], w_ref[...], preferred_element_type=F32))


def _in_proj(x, g, sc, sh, cos_t, sin_t, w_main, w_gb, tm, rows_per_mod, tn=1024):
    t, d = x.shape
    rmask = _rope_column_mask()

    def x_specs(rows):
        return [pl.BlockSpec((rows, d), lambda i, j: (i, 0)),
                pl.BlockSpec((1, d), lambda i, j: (0, 0)),
                _mod_spec(sc, rows, rows_per_mod), _mod_spec(sh, rows, rows_per_mod)]

    params = pltpu.CompilerParams(dimension_semantics=("arbitrary", "arbitrary"),
                                  vmem_limit_bytes=VMEM_LIMIT_BYTES)
    n_col = KV_COLS // HEAD_DIM
    tq = min(tm, QKV_TM_MAX)
    kv_spec = pl.BlockSpec((tq * n_col, HEAD_DIM), lambda i, j: (i, 0))
    kv_shape = jax.ShapeDtypeStruct((t * n_col, HEAD_DIM), F32)
    qkv, g_b, kv_moba, kv_cmp, kv_slc, kv_win = pl.pallas_call(
        functools.partial(_in_proj_qkv_kernel, tn=tn),
        grid=(t // tq, QKV_COLS // tn),
        in_specs=x_specs(tq) + [pl.BlockSpec((tq, HEAD_DIM), lambda i, j: (i, 0)),
                                pl.BlockSpec((tq, HEAD_DIM), lambda i, j: (i, 0)),
                                pl.BlockSpec((1, tn), lambda i, j: (0, j)),
                                pl.BlockSpec((d, tn), lambda i, j: (0, j)),
                                pl.BlockSpec((d, LANE), lambda i, j: (0, 0))],
        out_specs=[pl.BlockSpec((tq, tn), lambda i, j: (i, j)),
                   pl.BlockSpec((tq, LANE), lambda i, j: (i, 0)),
                   kv_spec, kv_spec, kv_spec, kv_spec],
        out_shape=[jax.ShapeDtypeStruct((t, QKV_COLS), F32),
                   jax.ShapeDtypeStruct((t, LANE), F32),
                   kv_shape, kv_shape, kv_shape, kv_shape],
        scratch_shapes=[pltpu.VMEM((tq, d), BF16)],
        compiler_params=params,
        name="in_proj_qkv",
    )(x, g.reshape(1, d), sc, sh, cos_t, sin_t, rmask, w_main, w_gb)
    gates = pl.pallas_call(
        _in_proj_gate_kernel,
        grid=(t // tm, GATE_COLS // tn),
        in_specs=x_specs(tm) + [pl.BlockSpec((d, tn), lambda i, j: (0, QKV_COLS // tn + j))],
        out_specs=pl.BlockSpec((tm, tn), lambda i, j: (i, j)),
        out_shape=jax.ShapeDtypeStruct((t, GATE_COLS), F32),
        scratch_shapes=[pltpu.VMEM((tm, d), BF16)],
        compiler_params=params,
        name="in_proj_gate",
    )(x, g.reshape(1, d), sc, sh, w_main)
    return qkv, gates, g_b, (kv_moba, kv_cmp, kv_slc, kv_win)


def _out_proj_kernel(oa_ref, ob_ref, gma_ref, gmb_ref, x_ref, g1_ref, gf_ref, sc_ref, sh_ref,
                     wpa_ref, wpb_ref, wout_ref, wr_ref, br_ref, xo_ref, h_ref, lg_ref):
    pa = jnp.dot(oa_ref[...].astype(BF16), wpa_ref[...], preferred_element_type=F32)
    pb = jnp.dot(ob_ref[...].astype(BF16), wpb_ref[...], preferred_element_type=F32)
    merged = (gma_ref[...] * pa + gmb_ref[...] * pb).astype(BF16)
    mix = jnp.dot(merged, wout_ref[...], preferred_element_type=F32)
    x = x_ref[...] + g1_ref[0] * mix
    xo_ref[...] = x
    y = x * lax.rsqrt(jnp.mean(x * x, axis=-1, keepdims=True) + NORM_EPS) * gf_ref[...]
    h = y * (1.0 + sc_ref[0]) + sh_ref[0]
    h_ref[...] = h
    lg_ref[...] = jnp.dot(h, wr_ref[...], precision=lax.Precision.HIGHEST,
                          preferred_element_type=F32) + br_ref[...]


def _out_proj(o_a, o_b, gates, x, g1, gf, sc, sh, w_pa, w_pb, w_out, w_r, b_r, tm, rows_per_mod):
    t, d = x.shape
    da = o_a.shape[1]
    const = lambda shape: pl.BlockSpec(shape, lambda i: (0,) * len(shape),
                                       pipeline_mode=pl.Buffered(1))
    gate_blk = 0
    return pl.pallas_call(
        _out_proj_kernel,
        grid=(t // tm,),
        in_specs=[pl.BlockSpec((tm, da), lambda i: (i, 0)),
                  pl.BlockSpec((tm, da), lambda i: (i, 0)),
                  pl.BlockSpec((tm, d), lambda i: (i, gate_blk)),
                  pl.BlockSpec((tm, d), lambda i: (i, gate_blk + 1)),
                  pl.BlockSpec((tm, d), lambda i: (i, 0)),
                  _mod_spec(g1, tm, rows_per_mod),
                  const((1, d)),
                  _mod_spec(sc, tm, rows_per_mod), _mod_spec(sh, tm, rows_per_mod),
                  const((da, d)), const((da, d)), const((d, d)), const((d, LANE)), const((1, LANE))],
        out_specs=[pl.BlockSpec((tm, d), lambda i: (i, 0)),
                   pl.BlockSpec((tm, d), lambda i: (i, 0)),
                   pl.BlockSpec((tm, LANE), lambda i: (i, 0))],
        out_shape=[jax.ShapeDtypeStruct((t, d), F32), jax.ShapeDtypeStruct((t, d), F32),
                   jax.ShapeDtypeStruct((t, LANE), F32)],
        compiler_params=pltpu.CompilerParams(dimension_semantics=("arbitrary",),
                                             vmem_limit_bytes=VMEM_LIMIT_BYTES),
        name="out_proj",
    )(o_a, o_b, gates, gates, x, g1, gf.reshape(1, d), sc, sh, w_pa, w_pb, w_out, w_r, b_r)


def _moe_ffn_kernel(blk_e_ref, seg_ref, order_ref, nused_ref, x_hbm, wg_ref, wu_ref, wd_ref,
                    o_ref, xbuf, sem, wg_bf, wu_bf, wd_bf, *, bm):
    i = pl.program_id(0)
    n_used = nused_ref[0]
    last = order_ref.shape[0] - 1

    def issue(blk, slot):
        base = seg_ref[blk]
        for r in range(bm):
            tok = order_ref[jnp.minimum(base + r, last)] // EXPERT_TOPK
            pltpu.make_async_copy(x_hbm.at[pl.ds(tok, 1)], xbuf.at[slot, pl.ds(r, 1)],
                                  sem.at[slot]).start()

    def wait(slot):
        pltpu.make_async_copy(x_hbm.at[pl.ds(0, bm)], xbuf.at[slot], sem.at[slot]).wait()

    slot = i % 2

    @pl.when((i == 0) & (n_used > 0))
    def _():
        issue(0, 0)

    @pl.when(i < n_used)
    def _():
        wait(slot)
        prev = blk_e_ref[jnp.maximum(i - 1, 0)]

        @pl.when((i == 0) | (blk_e_ref[i] != prev))
        def _():
            wg_bf[...] = wg_ref[0, 0].astype(BF16)
            wu_bf[...] = wu_ref[0, 0].astype(BF16)
            wd_bf[...] = wd_ref[0, 0].astype(BF16)

        issue(jnp.minimum(i + 1, n_used - 1), 1 - slot)
        x = xbuf[slot].astype(BF16)
        hg = jnp.dot(x, wg_bf[...], preferred_element_type=F32)
        hu = jnp.dot(x, wu_bf[...], preferred_element_type=F32)
        act = (jax.nn.silu(hg) * hu).astype(BF16)
        o_ref[...] = jnp.dot(act, wd_bf[...], preferred_element_type=F32)

        @pl.when(i + 1 == n_used)
        def _():
            wait(1 - slot)

    @pl.when(i >= n_used)
    def _():
        o_ref[...] = jnp.zeros_like(o_ref)


def _moe_ffn(x, blk_expert, seg_start, order, n_used, w_gate, w_up, w_down, l, bm):
    n_blk = blk_expert.shape[0]
    n_rows = n_blk * bm
    d = x.shape[1]
    de = w_gate.shape[3]
    grid_spec = pltpu.PrefetchScalarGridSpec(
        num_scalar_prefetch=4,
        grid=(n_blk,),
        in_specs=[pl.BlockSpec(memory_space=pl.ANY),
                  pl.BlockSpec((1, 1, d, de), lambda i, be, sg, od, nu: (l, be[i], 0, 0)),
                  pl.BlockSpec((1, 1, d, de), lambda i, be, sg, od, nu: (l, be[i], 0, 0)),
                  pl.BlockSpec((1, 1, de, d), lambda i, be, sg, od, nu: (l, be[i], 0, 0))],
        out_specs=pl.BlockSpec((bm, d), lambda i, be, sg, od, nu: (i, 0)),
        scratch_shapes=[pltpu.VMEM((2, bm, d), F32),
                        pltpu.SemaphoreType.DMA((2,)),
                        pltpu.VMEM((d, de), BF16), pltpu.VMEM((d, de), BF16),
                        pltpu.VMEM((de, d), BF16)])
    return pl.pallas_call(
        functools.partial(_moe_ffn_kernel, bm=bm),
        grid_spec=grid_spec,
        out_shape=jax.ShapeDtypeStruct((n_rows, d), F32),
        compiler_params=pltpu.CompilerParams(dimension_semantics=("arbitrary",),
                                             vmem_limit_bytes=VMEM_LIMIT_BYTES),
        name="moe_ffn",
    )(blk_expert, seg_start, order, n_used, x, w_gate, w_up, w_down)


def _moe_combine_kernel(d0_ref, d1_ref, x_ref, g2_ref, w_ref, nf_ref, yb_hbm, o_ref, ybuf, sem,
                        *, tm, final_norm):
    i = pl.program_id(0)
    n = pl.num_programs(0)

    def issue(blk, slot):
        for r in range(tm):
            t = blk * tm + r
            pltpu.make_async_copy(yb_hbm.at[pl.ds(d0_ref[t], 1)], ybuf.at[slot, 0, pl.ds(r, 1)],
                                  sem.at[slot]).start()
            pltpu.make_async_copy(yb_hbm.at[pl.ds(d1_ref[t], 1)], ybuf.at[slot, 1, pl.ds(r, 1)],
                                  sem.at[slot]).start()

    def wait(slot):
        pltpu.make_async_copy(yb_hbm.at[pl.ds(0, tm)], ybuf.at[slot, 0], sem.at[slot]).wait()
        pltpu.make_async_copy(yb_hbm.at[pl.ds(0, tm)], ybuf.at[slot, 1], sem.at[slot]).wait()

    slot = i % 2

    @pl.when(i == 0)
    def _():
        issue(0, 0)

    wait(slot)
    issue(jnp.minimum(i + 1, n - 1), 1 - slot)
    w = w_ref[...]
    y = x_ref[...] + g2_ref[0] * (w[:, 0:1] * ybuf[slot, 0] + w[:, 1:2] * ybuf[slot, 1])
    if final_norm:
        y = y * lax.rsqrt(jnp.mean(y * y, axis=-1, keepdims=True) + NORM_EPS) * nf_ref[...]
    o_ref[...] = y

    @pl.when(i + 1 == n)
    def _():
        wait(1 - slot)


def _moe_combine(x, g2, w_tok, norm_final, yb, d0, d1, tm, rows_per_mod, final_norm):
    t, d = x.shape
    grid_spec = pltpu.PrefetchScalarGridSpec(
        num_scalar_prefetch=2,
        grid=(t // tm,),
        in_specs=[pl.BlockSpec((tm, d), lambda i, a, b: (i, 0)),
                  _mod_spec(g2, tm, rows_per_mod),
                  pl.BlockSpec((tm, LANE), lambda i, a, b: (i, 0)),
                  pl.BlockSpec((1, d), lambda i, a, b: (0, 0)),
                  pl.BlockSpec(memory_space=pl.ANY)],
        out_specs=pl.BlockSpec((tm, d), lambda i, a, b: (i, 0)),
        scratch_shapes=[pltpu.VMEM((2, 2, tm, d), F32), pltpu.SemaphoreType.DMA((2,))])
    return pl.pallas_call(
        functools.partial(_moe_combine_kernel, tm=tm, final_norm=final_norm),
        grid_spec=grid_spec,
        out_shape=jax.ShapeDtypeStruct((t, d), F32),
        compiler_params=pltpu.CompilerParams(dimension_semantics=("arbitrary",),
                                             vmem_limit_bytes=VMEM_LIMIT_BYTES),
        name="moe_combine",
    )(d0, d1, x, g2, w_tok, norm_final.reshape(1, d), yb)


def _route(logits, bm):
    t = logits.shape[0]
    a = t * EXPERT_TOPK
    g_prob = jax.nn.softmax(logits[:, :N_GROUPS], axis=-1)
    g_idx = jnp.argmax(g_prob, axis=-1, keepdims=True).astype(jnp.int32)
    g_w = jnp.max(g_prob, axis=-1, keepdims=True)
    e_logit = logits[:, N_GROUPS:N_GROUPS + N_EXPERTS].reshape(t, N_GROUPS, EXPERTS_PER_GROUP)
    group_ids = jnp.arange(N_GROUPS, dtype=jnp.int32)[None, :, None]
    e_logit = jnp.sum(jnp.where(g_idx[:, :, None] == group_ids, e_logit, 0.0), axis=1)
    e_ids = jnp.arange(EXPERTS_PER_GROUP, dtype=jnp.int32)[None, :]
    i1 = jnp.argmax(e_logit, axis=-1, keepdims=True).astype(jnp.int32)
    rest = jnp.where(e_ids == i1, -jnp.inf, e_logit)
    i2 = jnp.argmax(rest, axis=-1, keepdims=True).astype(jnp.int32)
    e_val = jnp.concatenate([jnp.max(e_logit, axis=-1, keepdims=True),
                             jnp.max(rest, axis=-1, keepdims=True)], axis=-1)
    e_idx = jnp.concatenate([i1, i2], axis=-1)
    weights = (g_w * jax.nn.softmax(e_val, axis=-1)).reshape(a)
    flat_e = (g_idx * EXPERTS_PER_GROUP + e_idx).reshape(a)
    e_sorted, order = lax.sort((flat_e, jnp.arange(a, dtype=jnp.int32)), num_keys=1, is_stable=True)
    expert_ids = jnp.arange(N_EXPERTS, dtype=jnp.int32)
    counts = jnp.sum((flat_e[:, None] == expert_ids[None, :]).astype(jnp.int32), axis=0)
    padded = (counts + bm - 1) // bm * bm
    pad_end = jnp.cumsum(padded)
    pad_start = pad_end - padded
    start = jnp.cumsum(counts) - counts
    shift = pad_start - start
    lookup = lambda e: jnp.sum(jnp.where(e[:, None] == expert_ids[None, :], shift[None, :], 0), axis=1)
    dest_sorted = jnp.arange(a, dtype=jnp.int32) + lookup(e_sorted)
    n_blk = -(-(a + N_EXPERTS * (bm - 1)) // bm)
    blk_start = jnp.arange(n_blk, dtype=jnp.int32) * bm
    blk_expert = jnp.minimum(jnp.sum((pad_end[None, :] <= blk_start[:, None]).astype(jnp.int32), axis=1),
                             N_EXPERTS - 1).astype(jnp.int32)
    n_used = (pad_end[-1] // bm).astype(jnp.int32).reshape(1)
    seg_start = jnp.clip(blk_start - lookup(blk_expert), 0, a - 1)
    _, dest = lax.sort((order, dest_sorted), num_keys=1)
    dest = dest.reshape(t, EXPERT_TOPK)
    w_tok = jnp.pad(weights.reshape(t, EXPERT_TOPK), ((0, 0), (0, LANE - EXPERT_TOPK)))
    return blk_expert, seg_start.astype(jnp.int32), order, n_used, w_tok, dest[:, 0], dest[:, 1]


ATT_SCALE = HEAD_DIM ** -0.5
_NT = (((1,), (1,)), ((), ()))


KV_COLS = 2 * NSA_KV_HEADS * HEAD_DIM


def _compress_kernel(x0_ref, x1_ref, x2_ref, x3_ref, pos_ref, w1_ref, b1_ref, w2_ref, b2_ref,
                     o_ref, *, nch):
    x_refs = (x0_ref, x1_ref, x2_ref, x3_ref)
    for kv in range(2):
        for h in range(NSA_KV_HEADS):
            c = kv * NSA_KV_HEADS + h
            first = jnp.zeros((nch, CMP_HIDDEN), F32)
            second = jnp.zeros((nch, CMP_HIDDEN), F32)
            for r in range(CMP_STRIDE):
                xr = x_refs[c][0, pl.ds(r, nch, stride=CMP_STRIDE), :]
                r2 = CMP_STRIDE + r
                first += jnp.dot((xr + pos_ref[kv, r:r + 1, :]).astype(BF16),
                                 w1_ref[kv, r * HEAD_DIM:(r + 1) * HEAD_DIM, :].astype(BF16),
                                 preferred_element_type=F32)
                second += jnp.dot((xr + pos_ref[kv, r2:r2 + 1, :]).astype(BF16),
                                  w1_ref[kv, r2 * HEAD_DIM:(r2 + 1) * HEAD_DIM, :].astype(BF16),
                                  preferred_element_type=F32)
            pre = first + pltpu.roll(second, nch - 1, axis=0) + b1_ref[kv:kv + 1, :]
            hdn = jax.nn.gelu(pre)
            o_ref[0, c] = (jnp.dot(hdn.astype(BF16), w2_ref[kv].astype(BF16),
                                   preferred_element_type=F32) + b2_ref[kv:kv + 1, :])


def _compress_prompt(proj, pos, w1, b1, w2, b2):
    b, s, _ = proj.shape
    nch = s // CMP_STRIDE
    full = lambda shape: pl.BlockSpec(shape, lambda bi: (0,) * len(shape))
    return pl.pallas_call(
        functools.partial(_compress_kernel, nch=nch),
        grid=(b,),
        in_specs=[pl.BlockSpec((1, s, HEAD_DIM), functools.partial(lambda bi, c: (bi, 0, COL_CMP + c), c=c))
                  for c in range(2 * NSA_KV_HEADS)] + [
                  full(pos.shape), full(w1.shape), full(b1.shape), full(w2.shape), full(b2.shape)],
        out_specs=pl.BlockSpec((1, 2 * NSA_KV_HEADS, nch, HEAD_DIM), lambda bi: (bi, 0, 0, 0)),
        out_shape=jax.ShapeDtypeStruct((b, 2 * NSA_KV_HEADS, nch, HEAD_DIM), F32),
        compiler_params=pltpu.CompilerParams(dimension_semantics=("arbitrary",),
                                             vmem_limit_bytes=VMEM_LIMIT_BYTES),
        name="nsa_compress",
    )(proj, proj, proj, proj, pos, w1, b1, w2, b2)


NSA_TQ = 256
NSA_KB = 256


def _flash_update_t(qb, kb, vt, mask, carry):
    m, l, acc = carry
    s = lax.dot_general(kb, qb, _NT, preferred_element_type=F32) * ATT_SCALE
    s = jnp.where(mask, s, NEG_INF)
    m_new = jnp.maximum(m, jnp.max(s, axis=0, keepdims=True))
    alpha = jnp.exp(m - m_new)
    p = jnp.exp(s - m_new)
    l = alpha * l + jnp.sum(p, axis=0, keepdims=True)
    acc = alpha * acc + jnp.dot(vt, p.astype(BF16), preferred_element_type=F32)
    return m_new, l, acc


def _flash_init_t(tq):
    return (jnp.full((1, tq), -jnp.inf, F32), jnp.zeros((1, tq), F32),
            jnp.zeros((HEAD_DIM, tq), F32))


def _stage_kv(k_ref, v_ref, kb_scr, vt_scr, nblk, blk):
    for j in range(nblk):
        kb_scr[j] = k_ref[0, j * blk:(j + 1) * blk, :].astype(BF16)
        vt_scr[j] = v_ref[0, j * blk:(j + 1) * blk, :].T.astype(BF16)


def _moba_prompt_t_kernel(q_ref, k_ref, v_ref, o_ref, kmean_scr, sel_scr, kb_scr, vt_scr, *, nb):
    i = pl.program_id(2)
    tq = MOBA_BLOCK
    nbp = kmean_scr.shape[0]

    @pl.when(i == 0)
    def _():
        kmean_scr[...] = jnp.zeros_like(kmean_scr)
        for j in range(nb):
            kmean_scr[j:j + 1, :] = jnp.mean(k_ref[0, j * tq:(j + 1) * tq, :], axis=0, keepdims=True)
        _stage_kv(k_ref, v_ref, kb_scr, vt_scr, nb, tq)

    wq = MOBA_GROUP * tq
    blk_row = lax.broadcasted_iota(jnp.int32, (nbp, tq), 0)
    qbs = []
    for g in range(MOBA_GROUP):
        qg = q_ref[0, :, g * HEAD_DIM:(g + 1) * HEAD_DIM]
        gate = lax.dot_general(kmean_scr[...], qg, _NT, precision=lax.Precision.HIGHEST,
                               preferred_element_type=F32)
        gate = jnp.where(blk_row < i, gate, NEG_INF)
        cnt = jnp.zeros((nbp, tq), F32)
        for j in range(nb):
            gj = gate[j:j + 1, :]
            cnt = cnt + jnp.where((gj > gate) | ((gj == gate) & (j < blk_row)), 1.0, 0.0)
        sel_scr[:, g * tq:(g + 1) * tq] = jnp.where((cnt < MOBA_TOPK) & (blk_row < i), 1.0, 0.0)
        qbs.append(qg.astype(BF16))
    qb = jnp.concatenate(qbs, axis=0)

    key_row = lax.broadcasted_iota(jnp.int32, (tq, wq), 0)
    q_idx = lax.broadcasted_iota(jnp.int32, (tq, wq), 1) & (tq - 1)
    carry = _flash_update_t(qb, kb_scr[i], vt_scr[i], key_row <= q_idx, _flash_init_t(wq))

    def body(j, carry):
        chosen = jnp.broadcast_to(sel_scr[pl.ds(j, 1), :], (tq, wq)) > 0.5
        return _flash_update_t(qb, kb_scr[j], vt_scr[j], chosen, carry)

    _, l, acc = lax.fori_loop(0, i, body, carry)
    o_t = acc / l
    for g in range(MOBA_GROUP):
        o_ref[0, :, g * HEAD_DIM:(g + 1) * HEAD_DIM] = o_t[:, g * tq:(g + 1) * tq].T


def _moba_prompt_t(proj):
    b, s, _ = proj.shape
    nb = s // MOBA_BLOCK
    assert s % MOBA_BLOCK == 0
    nbp = _round_up(nb, 8)
    gw = MOBA_GROUP * HEAD_DIM
    return pl.pallas_call(
        functools.partial(_moba_prompt_t_kernel, nb=nb),
        grid=(b, MOBA_KV_HEADS, nb),
        in_specs=[pl.BlockSpec((1, MOBA_BLOCK, gw), lambda bi, h, i: (bi, i, COL_QA // MOBA_GROUP + h)),
                  pl.BlockSpec((1, s, HEAD_DIM), lambda bi, h, i: (bi, 0, COL_KA + h)),
                  pl.BlockSpec((1, s, HEAD_DIM), lambda bi, h, i: (bi, 0, COL_VA + h))],
        out_specs=pl.BlockSpec((1, MOBA_BLOCK, gw), lambda bi, h, i: (bi, i, h)),
        out_shape=jax.ShapeDtypeStruct((b, s, MOBA_HEADS * HEAD_DIM), F32),
        scratch_shapes=[pltpu.VMEM((nbp, HEAD_DIM), F32),
                        pltpu.VMEM((nbp, MOBA_GROUP * MOBA_BLOCK), F32),
                        pltpu.VMEM((nb, MOBA_BLOCK, HEAD_DIM), BF16),
                        pltpu.VMEM((nb, HEAD_DIM, MOBA_BLOCK), BF16)],
        compiler_params=pltpu.CompilerParams(
            dimension_semantics=("arbitrary", "arbitrary", "arbitrary"),
            vmem_limit_bytes=VMEM_LIMIT_BYTES),
        name="moba_prompt",
    )(proj, proj, proj)


def _nsa_prompt_t_kernel(q_ref, kc_ref, vc_ref, ks_ref, vs_ref, kw_ref, vw_ref, gb_ref, o_ref,
                         ksb_scr, vst_scr, kwb_scr, vwt_scr, kcb_scr, vct_scr, gt_scr, sel_scr,
                         o_scr, *, s_len):
    kvh = pl.program_id(1)
    i = pl.program_id(2)
    tq, kb = NSA_TQ, NSA_KB
    n_cmp = s_len // CMP_STRIDE - CMP_LEN // CMP_STRIDE + 1
    n_slc = s_len // SLC_BLOCK
    n_top = min(SLC_TOPN, n_slc)
    nkb = s_len // kb
    n_rows = sel_scr.shape[0]
    per_kb = kb // SLC_BLOCK
    dh = HEAD_DIM

    @pl.when(i == 0)
    def _():
        _stage_kv(ks_ref, vs_ref, ksb_scr, vst_scr, nkb, kb)
        _stage_kv(kw_ref, vw_ref, kwb_scr, vwt_scr, nkb, kb)
        kcb_scr[...] = kc_ref[0, 0].astype(BF16)
        vct_scr[...] = vc_ref[0, 0].T.astype(BF16)

    gt_scr[...] = jax.nn.sigmoid(gb_ref[0]).T

    def gate_row(branch, g):
        return gt_scr[pl.ds(branch * NSA_HEADS + kvh * NSA_GROUP + g, 1), :]

    n_sub = lax.broadcasted_iota(jnp.int32, (LANE, tq), 0)
    t_lane = i * tq + lax.broadcasted_iota(jnp.int32, (LANE, tq), 1)
    avail = (n_sub * CMP_STRIDE + (CMP_LEN - 1) <= t_lane) & (n_sub < n_cmp)
    jj = lax.broadcasted_iota(jnp.int32, (LANE, LANE), 0)
    nn = lax.broadcasted_iota(jnp.int32, (LANE, LANE), 1)
    sel_map_t = jnp.where((nn * CMP_STRIDE < jj * SLC_BLOCK + SLC_BLOCK)
                          & (nn * CMP_STRIDE + CMP_LEN > jj * SLC_BLOCK)
                          & (nn < n_cmp) & (jj < n_slc), 1.0, 0.0)
    wq = NSA_GROUP * tq

    def gate_wide(branch):
        return jnp.concatenate([jnp.broadcast_to(gate_row(branch, g), (dh, tq))
                                for g in range(NSA_GROUP)], axis=1)

    qb = jnp.concatenate([q_ref[0, :, g * dh:(g + 1) * dh].astype(BF16)
                          for g in range(NSA_GROUP)], axis=0)
    avail_w = jnp.concatenate([avail] * NSA_GROUP, axis=1)
    s_c = lax.dot_general(kcb_scr[...], qb, _NT, preferred_element_type=F32) * ATT_SCALE
    s_c = jnp.where(avail_w, s_c, NEG_INF)
    e = jnp.where(avail_w, jnp.exp(s_c - jnp.max(s_c, axis=0, keepdims=True)), 0.0)
    den = jnp.sum(e, axis=0, keepdims=True)
    p_c = e * jnp.where(den > 0.0, 1.0 / den, 0.0)
    o_scr[...] = gate_wide(0) * jnp.dot(vct_scr[...], p_c.astype(BF16), preferred_element_type=F32)
    p_sum = sum(p_c[:, g * tq:(g + 1) * tq] for g in range(NSA_GROUP))
    imp = jnp.dot(sel_map_t, p_sum, precision=lax.Precision.HIGHEST, preferred_element_type=F32)

    imp = imp[:n_rows]
    jio = lax.broadcasted_iota(jnp.int32, (n_rows, tq), 0)
    cur = (i * tq + lax.broadcasted_iota(jnp.int32, (n_rows, tq), 1)) // SLC_BLOCK
    forced = (jio == 0) | (jio == cur) | (jio == cur - 1)
    imp = jnp.where(forced, BIG, imp)
    imp = jnp.where(jio > cur, NEG_INF, imp)
    imp = jnp.where(jio < n_slc, imp, -jnp.inf)
    cnt = jnp.zeros((n_rows, tq), F32)
    for r in range(n_slc):
        vr = imp[r:r + 1, :]
        cnt = cnt + jnp.where((vr > imp) | ((vr == imp) & (r < jio)), 1.0, 0.0)
    sel_scr[...] = jnp.where((cnt < n_top) & (jio < n_slc), 1.0, 0.0)

    def slc_mask(jb):
        rows = [jnp.broadcast_to(sel_scr[pl.ds(jb * per_kb + a, 1), :], (SLC_BLOCK, tq))
                for a in range(per_kb)]
        one = jnp.concatenate(rows, axis=0)
        return jnp.concatenate([one] * NSA_GROUP, axis=1) > 0.5

    key_row = lax.broadcasted_iota(jnp.int32, (kb, wq), 0)
    q_idx = lax.broadcasted_iota(jnp.int32, (kb, wq), 1) & (tq - 1)
    causal = key_row <= q_idx

    carry = _flash_update_t(qb, ksb_scr[i], vst_scr[i], slc_mask(i) & causal, _flash_init_t(wq))

    def slc_body(j, carry):
        return _flash_update_t(qb, ksb_scr[j], vst_scr[j], slc_mask(j), carry)

    _, l, acc = lax.fori_loop(0, i, slc_body, carry)
    o_scr[...] += gate_wide(1) * (acc / l)

    carry = _flash_update_t(qb, kwb_scr[i], vwt_scr[i], causal, _flash_init_t(wq))

    def win_body(j, carry):
        mask = (j * kb + key_row) > (i * tq + q_idx - WINDOW)
        return _flash_update_t(qb, kwb_scr[j], vwt_scr[j], mask, carry)

    _, l, acc = lax.fori_loop(jnp.maximum(i - WINDOW // kb, 0), i, win_body, carry)
    o_t = o_scr[...] + gate_wide(2) * (acc / l)
    for g in range(NSA_GROUP):
        o_ref[0, :, g * dh:(g + 1) * dh] = o_t[:, g * tq:(g + 1) * tq].T


def _nsa_prompt_t(proj, kvc, g_b):
    b, s, _ = proj.shape
    assert s % NSA_TQ == 0 and NSA_TQ == NSA_KB and s // CMP_STRIDE == LANE
    assert NSA_KB % SLC_BLOCK == 0
    gw = NSA_GROUP * HEAD_DIM
    nkh = NSA_KV_HEADS
    nkb = s // NSA_KB
    n_rows = _round_up(s // SLC_BLOCK, 8)
    seq = lambda col: pl.BlockSpec((1, s, HEAD_DIM), lambda bi, h, i: (bi, 0, col + h))
    kv_scr = [pltpu.VMEM((nkb, NSA_KB, HEAD_DIM), BF16), pltpu.VMEM((nkb, HEAD_DIM, NSA_KB), BF16)]
    return pl.pallas_call(
        functools.partial(_nsa_prompt_t_kernel, s_len=s),
        grid=(b, nkh, s // NSA_TQ),
        in_specs=[pl.BlockSpec((1, NSA_TQ, gw), lambda bi, h, i: (bi, i, COL_QB // NSA_GROUP + h)),
                  pl.BlockSpec((1, 1, LANE, HEAD_DIM), lambda bi, h, i: (bi, h, 0, 0)),
                  pl.BlockSpec((1, 1, LANE, HEAD_DIM), lambda bi, h, i: (bi, nkh + h, 0, 0)),
                  seq(COL_SLC), seq(COL_SLC + nkh), seq(COL_WIN), seq(COL_WIN + nkh),
                  pl.BlockSpec((1, NSA_TQ, LANE), lambda bi, h, i: (bi, i, 0))],
        out_specs=pl.BlockSpec((1, NSA_TQ, gw), lambda bi, h, i: (bi, i, h)),
        out_shape=jax.ShapeDtypeStruct((b, s, NSA_HEADS * HEAD_DIM), F32),
        scratch_shapes=kv_scr + kv_scr + [
            pltpu.VMEM((LANE, HEAD_DIM), BF16), pltpu.VMEM((HEAD_DIM, LANE), BF16),
            pltpu.VMEM((LANE, NSA_TQ), F32), pltpu.VMEM((n_rows, NSA_TQ), F32),
            pltpu.VMEM((HEAD_DIM, NSA_GROUP * NSA_TQ), F32)],
        compiler_params=pltpu.CompilerParams(
            dimension_semantics=("arbitrary", "arbitrary", "arbitrary"),
            vmem_limit_bytes=VMEM_LIMIT_BYTES),
        name="nsa_prompt",
    )(proj, kvc, kvc, proj, proj, proj, proj, g_b)


PAGE_SIZE = 128
PAGE_ROWS = PAGE_SIZE * 2 * NSA_KV_HEADS
PAGES_PER_STEP = 32
QROWS = 8
NQ = NSA_GROUP * QROWS


def _page_specs(n, l):
    def spec(k):
        return pl.BlockSpec((1, 1, PAGE_ROWS, HEAD_DIM), lambda b, j, pt: (l, pt[b, j * n + k], 0, 0))
    return [spec(k) for k in range(n)]


def _page_kv(ref, kv, h, rows=PAGE_SIZE):
    return ref[0, 0, pl.ds(kv * NSA_KV_HEADS + h, rows, stride=2 * NSA_KV_HEADS), :]


def _first_max_onehot(work, lane_f):
    mx = jnp.max(work, axis=-1, keepdims=True)
    first = jnp.min(jnp.where(work == mx, lane_f, 1e9), axis=-1, keepdims=True)
    return jnp.where(lane_f == first, 1.0, 0.0)


def _moba_sample_kernel(pt_ref, q_ref, knew_ref, vnew_ref, *rest, n_new, nb_past):
    pages = rest[:PAGES_PER_STEP]
    o_ref, m_scr, l_scr, acc_scr, kmean_scr = rest[PAGES_PER_STEP:]
    j = pl.program_id(1)
    nj = pl.num_programs(1)
    lane = lax.broadcasted_iota(jnp.int32, (NQ, LANE), 1)

    @pl.when(j == 0)
    def _():
        m_scr[...] = jnp.full_like(m_scr, NEG_INF)
        l_scr[...] = jnp.zeros_like(l_scr)
        kmean_scr[...] = jnp.zeros_like(kmean_scr)

    bps = PAGES_PER_STEP * PAGE_SIZE // MOBA_BLOCK
    for h in range(MOBA_KV_HEADS):
        qb = q_ref[0, h].astype(BF16)
        k_all = jnp.concatenate([_page_kv(pg, 0, h) for pg in pages], axis=0)
        v_all = jnp.concatenate([_page_kv(pg, 1, h) for pg in pages], axis=0).astype(BF16)
        s = lax.dot_general(qb, k_all.astype(BF16), _NT, preferred_element_type=F32) * ATT_SCALE
        blocks = [slice(u * MOBA_BLOCK, (u + 1) * MOBA_BLOCK) for u in range(bps)]
        m_bs = [jnp.max(s[:, sl], axis=-1, keepdims=True) for sl in blocks]
        p = jnp.exp(s - jnp.concatenate([jnp.broadcast_to(m_b, (NQ, MOBA_BLOCK)) for m_b in m_bs],
                                        axis=1))
        m_new, l_new = m_scr[h], l_scr[h]
        for u, sl in enumerate(blocks):
            blk = j * bps + u
            m_new = jnp.where(lane == blk, m_bs[u], m_new)
            l_new = jnp.where(lane == blk, jnp.sum(p[:, sl], axis=-1, keepdims=True), l_new)
            acc_scr[h, blk] = jnp.dot(p[:, sl].astype(BF16), v_all[sl], preferred_element_type=F32)
            kmean_scr[h, pl.ds(blk, 1), :] = (jnp.sum(k_all[sl], axis=0, keepdims=True)
                                             * (1.0 / MOBA_BLOCK))
        m_scr[h] = m_new
        l_scr[h] = l_new

    @pl.when(j == nj - 1)
    def _():
        lane_f = lane.astype(F32)
        trow = lax.broadcasted_iota(jnp.int32, (NQ, LANE), 0) & (QROWS - 1)
        for h in range(MOBA_KV_HEADS):
            q = q_ref[0, h]
            qb = q.astype(BF16)
            s_own = lax.dot_general(qb, knew_ref[0, h].astype(BF16), _NT,
                                    preferred_element_type=F32) * ATT_SCALE
            s_own = jnp.where((lane <= trow) & (lane < n_new), s_own, NEG_INF)
            m_own = jnp.max(s_own, axis=-1, keepdims=True)
            p_own = jnp.exp(s_own - m_own)
            l_own = jnp.sum(p_own, axis=-1, keepdims=True)
            acc_own = jnp.dot(p_own.astype(BF16), vnew_ref[0, h].astype(BF16),
                              preferred_element_type=F32)
            gate = lax.dot_general(q, kmean_scr[h], _NT, precision=lax.Precision.HIGHEST,
                                   preferred_element_type=F32)
            work = jnp.where(lane < nb_past, gate, -jnp.inf)
            sel = jnp.zeros((NQ, LANE), F32)
            for _ in range(min(MOBA_TOPK, nb_past)):
                pick = _first_max_onehot(work, lane_f)
                sel = jnp.maximum(sel, pick)
                work = jnp.where(pick > 0.5, -jnp.inf, work)
            chosen = sel > 0.5
            m_all = m_scr[h]
            m_tot = jnp.maximum(m_own, jnp.max(jnp.where(chosen, m_all, NEG_INF), axis=-1,
                                               keepdims=True))
            w = jnp.where(chosen, jnp.exp(m_all - m_tot), 0.0)
            w_own = jnp.exp(m_own - m_tot)
            l_tot = w_own * l_own + jnp.sum(w * l_scr[h], axis=-1, keepdims=True)

            def merge(blk, acc, w=w, h=h):
                wcol = jnp.sum(jnp.where(lane == blk, w, 0.0), axis=-1, keepdims=True)
                return acc + wcol * acc_scr[h, blk]

            acc = lax.fori_loop(0, nb_past, merge, w_own * acc_own)
            o_ref[0, h] = acc / l_tot


def _moba_sample(q, k_new, v_new, pool, page_table, l, n_new):
    b = q.shape[0]
    n_pages = page_table.shape[1]
    assert n_pages % PAGES_PER_STEP == 0 and (n_pages * PAGE_SIZE) % MOBA_BLOCK == 0
    nb_past = n_pages * PAGE_SIZE // MOBA_BLOCK
    assert nb_past <= LANE and n_new <= MOBA_BLOCK
    per_b = lambda shape: pl.BlockSpec(shape, lambda bi, j, pt: (bi,) + (0,) * (len(shape) - 1))
    grid_spec = pltpu.PrefetchScalarGridSpec(
        num_scalar_prefetch=1,
        grid=(b, n_pages // PAGES_PER_STEP),
        in_specs=[per_b((1, MOBA_KV_HEADS, NQ, HEAD_DIM)),
                  per_b((1, MOBA_KV_HEADS, LANE, HEAD_DIM)),
                  per_b((1, MOBA_KV_HEADS, LANE, HEAD_DIM))] + _page_specs(PAGES_PER_STEP, l),
        out_specs=per_b((1, MOBA_KV_HEADS, NQ, HEAD_DIM)),
        scratch_shapes=[pltpu.VMEM((MOBA_KV_HEADS, NQ, LANE), F32),
                        pltpu.VMEM((MOBA_KV_HEADS, NQ, LANE), F32),
                        pltpu.VMEM((MOBA_KV_HEADS, nb_past, NQ, HEAD_DIM), F32),
                        pltpu.VMEM((MOBA_KV_HEADS, LANE, HEAD_DIM), F32)])
    return pl.pallas_call(
        functools.partial(_moba_sample_kernel, n_new=n_new, nb_past=nb_past),
        grid_spec=grid_spec,
        out_shape=jax.ShapeDtypeStruct(q.shape, F32),
        compiler_params=pltpu.CompilerParams(dimension_semantics=("arbitrary", "arbitrary"),
                                             vmem_limit_bytes=VMEM_LIMIT_BYTES),
        name="moba_sample",
    )(page_table, q, k_new, v_new, *([pool] * PAGES_PER_STEP))


def _compress_sample_kernel(pt_ref, pos_ref, w1_ref, b1_ref, w2_ref, b2_ref, *rest):
    pages = rest[:PAGES_PER_STEP + 1]
    o_ref = rest[PAGES_PER_STEP + 1]
    cpp = PAGE_SIZE // CMP_STRIDE
    nch = PAGES_PER_STEP * cpp
    half = CMP_STRIDE * HEAD_DIM
    n_col = 2 * NSA_KV_HEADS
    ys = [jnp.swapaxes(pg[0, 0].reshape(cpp, CMP_STRIDE * n_col, HEAD_DIM), 0, 1) for pg in pages]
    for kv in range(2):
        pos_first = jnp.concatenate([pos_ref[kv, r:r + 1, :] for r in range(CMP_STRIDE)], axis=1)
        pos_second = jnp.concatenate([pos_ref[kv, r:r + 1, :] for r in range(CMP_STRIDE, CMP_LEN)],
                                     axis=1)
        w_first = w1_ref[kv, :half, :].astype(BF16)
        w_second = w1_ref[kv, half:, :].astype(BF16)
        w2 = w2_ref[kv].astype(BF16)
        for h in range(NSA_KV_HEADS):
            c = kv * NSA_KV_HEADS + h
            x = jnp.concatenate(
                [jnp.concatenate([y[r * n_col + c] for y in ys], axis=0)
                 for r in range(CMP_STRIDE)], axis=1)
            first = jnp.dot((x[:nch] + pos_first).astype(BF16), w_first, preferred_element_type=F32)
            second = jnp.dot((x + pos_second).astype(BF16), w_second, preferred_element_type=F32)
            pre = first + pltpu.roll(second, nch + cpp - 1, axis=0)[:nch] + b1_ref[kv:kv + 1, :]
            hdn = jax.nn.gelu(pre)
            o_ref[0, c] = (jnp.dot(hdn.astype(BF16), w2, preferred_element_type=F32)
                           + b2_ref[kv:kv + 1, :])


def _compress_sample(pool, page_table, l, pos, w1, b1, w2, b2):
    b, n_pages = page_table.shape
    nch = PAGES_PER_STEP * PAGE_SIZE // CMP_STRIDE
    full = lambda shape: pl.BlockSpec(shape, lambda bi, j, pt: (0,) * len(shape))
    next_page = pl.BlockSpec(
        (1, 1, PAGE_ROWS, HEAD_DIM),
        lambda bi, j, pt: (l, pt[bi, jnp.minimum((j + 1) * PAGES_PER_STEP, n_pages - 1)], 0, 0))
    grid_spec = pltpu.PrefetchScalarGridSpec(
        num_scalar_prefetch=1,
        grid=(b, n_pages // PAGES_PER_STEP),
        in_specs=[full(pos.shape), full(w1.shape), full(b1.shape), full(w2.shape), full(b2.shape)]
        + _page_specs(PAGES_PER_STEP, l) + [next_page],
        out_specs=pl.BlockSpec((1, 2 * NSA_KV_HEADS, nch, HEAD_DIM), lambda bi, j, pt: (bi, 0, j, 0)))
    return pl.pallas_call(
        _compress_sample_kernel,
        grid_spec=grid_spec,
        out_shape=jax.ShapeDtypeStruct(
            (b, 2 * NSA_KV_HEADS, n_pages * PAGE_SIZE // CMP_STRIDE, HEAD_DIM), F32),
        compiler_params=pltpu.CompilerParams(dimension_semantics=("arbitrary", "arbitrary"),
                                             vmem_limit_bytes=VMEM_LIMIT_BYTES),
        name="nsa_compress_sample",
    )(page_table, pos, w1, b1, w2, b2, *([pool] * (PAGES_PER_STEP + 1)))


def _nsa_cmp_sample_kernel(q_ref, kc_ref, vc_ref, oc_ref, sel_ref, *, q0, n_cmp, n_slc, n_lanes):
    ncp = kc_ref.shape[2]
    col = lax.broadcasted_iota(jnp.int32, (NQ, ncp), 1)
    trow = lax.broadcasted_iota(jnp.int32, (NQ, ncp), 0) & (QROWS - 1)
    avail = (col * CMP_STRIDE + (CMP_LEN - 1) <= q0 + trow) & (col < n_cmp)
    nn = lax.broadcasted_iota(jnp.int32, (ncp, n_lanes), 0)
    jj = lax.broadcasted_iota(jnp.int32, (ncp, n_lanes), 1)
    sel_map = jnp.where((nn * CMP_STRIDE < jj * SLC_BLOCK + SLC_BLOCK)
                        & (nn * CMP_STRIDE + CMP_LEN > jj * SLC_BLOCK)
                        & (nn < n_cmp) & (jj < n_slc), 1.0, 0.0)
    lane = lax.broadcasted_iota(jnp.int32, (QROWS, n_lanes), 1)
    lane_f = lane.astype(F32)
    cur = (q0 + lax.broadcasted_iota(jnp.int32, (QROWS, n_lanes), 0)) // SLC_BLOCK
    for h in range(NSA_KV_HEADS):
        qb = q_ref[0, h].astype(BF16)
        s = lax.dot_general(qb, kc_ref[0, h].astype(BF16), _NT,
                            preferred_element_type=F32) * ATT_SCALE
        s = jnp.where(avail, s, NEG_INF)
        e = jnp.where(avail, jnp.exp(s - jnp.max(s, axis=-1, keepdims=True)), 0.0)
        den = jnp.sum(e, axis=-1, keepdims=True)
        p = e * jnp.where(den > 0.0, 1.0 / den, 0.0)
        oc_ref[0, h] = jnp.dot(p.astype(BF16), vc_ref[0, h].astype(BF16),
                               preferred_element_type=F32)
        pg = sum(p[g * QROWS:(g + 1) * QROWS] for g in range(NSA_GROUP))
        imp = jnp.dot(pg, sel_map, precision=lax.Precision.HIGHEST, preferred_element_type=F32)
        forced = (lane == 0) | (lane == cur) | (lane == cur - 1)
        imp = jnp.where(forced, BIG, imp)
        imp = jnp.where(lane > cur, NEG_INF, imp)
        work = jnp.where(lane < n_slc, imp, -jnp.inf)
        sel = jnp.zeros((QROWS, n_lanes), F32)
        for _ in range(min(SLC_TOPN, n_slc)):
            pick = _first_max_onehot(work, lane_f)
            sel = jnp.maximum(sel, pick)
            work = jnp.where(pick > 0.5, -jnp.inf, work)
        sel_ref[0, h] = sel


def _nsa_cmp_sample(q, kvc, q0, n_new):
    b = q.shape[0]
    ncp = kvc.shape[2]
    n_cmp = ncp - CMP_LEN // CMP_STRIDE + 1
    n_slc = -(-(q0 + n_new) // SLC_BLOCK)
    n_lanes = _round_up(n_slc, LANE)
    nkh = NSA_KV_HEADS
    return pl.pallas_call(
        functools.partial(_nsa_cmp_sample_kernel, q0=q0, n_cmp=n_cmp, n_slc=n_slc, n_lanes=n_lanes),
        grid=(b,),
        in_specs=[pl.BlockSpec((1, nkh, NQ, HEAD_DIM), lambda bi: (bi, 0, 0, 0)),
                  pl.BlockSpec((1, nkh, ncp, HEAD_DIM), lambda bi: (bi, 0, 0, 0)),
                  pl.BlockSpec((1, nkh, ncp, HEAD_DIM), lambda bi: (bi, 1, 0, 0))],
        out_specs=[pl.BlockSpec((1, nkh, NQ, HEAD_DIM), lambda bi: (bi, 0, 0, 0)),
                   pl.BlockSpec((1, nkh, QROWS, n_lanes), lambda bi: (bi, 0, 0, 0))],
        out_shape=[jax.ShapeDtypeStruct(q.shape, F32),
                   jax.ShapeDtypeStruct((b, nkh, QROWS, n_lanes), F32)],
        compiler_params=pltpu.CompilerParams(dimension_semantics=("arbitrary",),
                                             vmem_limit_bytes=VMEM_LIMIT_BYTES),
        name="nsa_cmp_sample",
    )(q, kvc, kvc)


def _nsa_slc_sample_kernel(pt_ref, q_ref, sel_ref, oc_ref, g_ref, ksn_ref, vsn_ref, kwn_ref,
                           vwn_ref, win_ref, *rest, n_new, n_win, cur_blk):
    pages = rest[:PAGES_PER_STEP]
    o_ref, m_scr, l_scr, acc_scr, ow_scr = rest[PAGES_PER_STEP:]
    j = pl.program_id(1)
    nj = pl.num_programs(1)
    n_lanes = sel_ref.shape[3]
    lane = lax.broadcasted_iota(jnp.int32, (NQ, LANE), 1)
    trow = lax.broadcasted_iota(jnp.int32, (NQ, LANE), 0) & (QROWS - 1)
    new_ok = (lane <= trow) & (lane < n_new)

    @pl.when(j == 0)
    def _():
        for h in range(NSA_KV_HEADS):
            qb = q_ref[0, h].astype(BF16)
            sel_h = jnp.concatenate([sel_ref[0, h]] * NSA_GROUP, axis=0)
            lane_s = lax.broadcasted_iota(jnp.int32, (NQ, n_lanes), 1)
            sel_cur = jnp.sum(jnp.where(lane_s == cur_blk, sel_h, 0.0), axis=-1,
                              keepdims=True) > 0.5
            s = lax.dot_general(qb, ksn_ref[0, h].astype(BF16), _NT,
                                preferred_element_type=F32) * ATT_SCALE
            s = jnp.where(new_ok & sel_cur, s, NEG_INF)
            m = jnp.max(s, axis=-1, keepdims=True)
            p = jnp.exp(s - m)
            m_scr[h] = m
            l_scr[h] = jnp.sum(p, axis=-1, keepdims=True)
            acc_scr[h] = jnp.dot(p.astype(BF16), vsn_ref[0, h].astype(BF16),
                                 preferred_element_type=F32)
            kw = win_ref[0, 0, pl.ds(h, n_win, stride=2 * NSA_KV_HEADS), :]
            vw = win_ref[0, 0, pl.ds(NSA_KV_HEADS + h, n_win, stride=2 * NSA_KV_HEADS), :]
            idx = lax.broadcasted_iota(jnp.int32, (NQ, n_win), 1)
            tr = lax.broadcasted_iota(jnp.int32, (NQ, n_win), 0) & (QROWS - 1)
            s_w = lax.dot_general(qb, kw.astype(BF16), _NT, preferred_element_type=F32) * ATT_SCALE
            s_w = jnp.where(idx + (WINDOW - n_win) > tr, s_w, NEG_INF)
            s_n = lax.dot_general(qb, kwn_ref[0, h].astype(BF16), _NT,
                                  preferred_element_type=F32) * ATT_SCALE
            s_n = jnp.where(new_ok, s_n, NEG_INF)
            mw = jnp.maximum(jnp.max(s_w, axis=-1, keepdims=True),
                             jnp.max(s_n, axis=-1, keepdims=True))
            p_w = jnp.exp(s_w - mw)
            p_n = jnp.exp(s_n - mw)
            lw = jnp.sum(p_w, axis=-1, keepdims=True) + jnp.sum(p_n, axis=-1, keepdims=True)
            ow_scr[h] = (jnp.dot(p_w.astype(BF16), vw.astype(BF16), preferred_element_type=F32)
                         + jnp.dot(p_n.astype(BF16), vwn_ref[0, h].astype(BF16),
                                   preferred_element_type=F32)) / lw

    step_keys = PAGES_PER_STEP * PAGE_SIZE
    erow = lax.broadcasted_iota(jnp.int32, (n_lanes, step_keys), 0)
    ecol = lax.broadcasted_iota(jnp.int32, (n_lanes, step_keys), 1)
    expand = jnp.where((j * step_keys + ecol) // SLC_BLOCK == erow, 1.0, 0.0).astype(BF16)
    for h in range(NSA_KV_HEADS):
        qb = q_ref[0, h].astype(BF16)
        sel_h = jnp.concatenate([sel_ref[0, h]] * NSA_GROUP, axis=0).astype(BF16)
        maskf = jnp.dot(sel_h, expand, preferred_element_type=F32)
        ss = []
        for k in range(PAGES_PER_STEP):
            s = lax.dot_general(qb, _page_kv(pages[k], 0, h).astype(BF16), _NT,
                                preferred_element_type=F32) * ATT_SCALE
            ss.append(jnp.where(maskf[:, k * PAGE_SIZE:(k + 1) * PAGE_SIZE] > 0.5, s, NEG_INF))
        m_old = m_scr[h]
        m_new = functools.reduce(jnp.maximum,
                                 [jnp.max(s, axis=-1, keepdims=True) for s in ss] + [m_old])
        alpha = jnp.exp(m_old - m_new)
        ps = [jnp.exp(s - m_new) for s in ss]
        l_scr[h] = alpha * l_scr[h] + sum(jnp.sum(p, axis=-1, keepdims=True) for p in ps)
        acc_scr[h] = alpha * acc_scr[h] + sum(
            jnp.dot(p.astype(BF16), _page_kv(pages[k], 1, h).astype(BF16),
                    preferred_element_type=F32) for k, p in enumerate(ps))
        m_scr[h] = m_new

    @pl.when(j == nj - 1)
    def _():
        for h in range(NSA_KV_HEADS):
            gates = jax.nn.sigmoid(g_ref[0, h])
            o_s = acc_scr[h] / l_scr[h]
            o_ref[0, h] = (gates[:, 0:1] * oc_ref[0, h] + gates[:, 1:2] * o_s
                           + gates[:, 2:3] * ow_scr[h])


def _nsa_slc_sample(q, sel, o_c, g3, ks_new, vs_new, kw_new, vw_new, win_state, pool, page_table,
                    l, n_new):
    b = q.shape[0]
    n_pages = page_table.shape[1]
    n_win = win_state.shape[2] // (2 * NSA_KV_HEADS)
    assert (n_pages * PAGE_SIZE) % SLC_BLOCK == 0 and n_new <= SLC_BLOCK and n_win <= WINDOW
    nkh = NSA_KV_HEADS
    per_b = lambda shape: pl.BlockSpec(shape, lambda bi, j, pt: (bi,) + (0,) * (len(shape) - 1))
    grid_spec = pltpu.PrefetchScalarGridSpec(
        num_scalar_prefetch=1,
        grid=(b, n_pages // PAGES_PER_STEP),
        in_specs=[per_b((1, nkh, NQ, HEAD_DIM)), per_b((1, nkh, QROWS, sel.shape[3])),
                  per_b((1, nkh, NQ, HEAD_DIM)), per_b((1, nkh, NQ, LANE)),
                  per_b((1, nkh, LANE, HEAD_DIM)), per_b((1, nkh, LANE, HEAD_DIM)),
                  per_b((1, nkh, LANE, HEAD_DIM)), per_b((1, nkh, LANE, HEAD_DIM)),
                  pl.BlockSpec((1, 1, win_state.shape[2], HEAD_DIM), lambda bi, j, pt: (l, bi, 0, 0))]
        + _page_specs(PAGES_PER_STEP, l),
        out_specs=per_b((1, nkh, NQ, HEAD_DIM)),
        scratch_shapes=[pltpu.VMEM((nkh, NQ, 1), F32), pltpu.VMEM((nkh, NQ, 1), F32),
                        pltpu.VMEM((nkh, NQ, HEAD_DIM), F32), pltpu.VMEM((nkh, NQ, HEAD_DIM), F32)])
    return pl.pallas_call(
        functools.partial(_nsa_slc_sample_kernel, n_new=n_new, n_win=n_win,
                          cur_blk=n_pages * PAGE_SIZE // SLC_BLOCK),
        grid_spec=grid_spec,
        out_shape=jax.ShapeDtypeStruct(q.shape, F32),
        compiler_params=pltpu.CompilerParams(dimension_semantics=("arbitrary", "arbitrary"),
                                             vmem_limit_bytes=VMEM_LIMIT_BYTES),
        name="nsa_slc_sample",
    )(page_table, q, sel, o_c, g3, ks_new, vs_new, kw_new, vw_new, win_state,
      *([pool] * PAGES_PER_STEP))


PROMPT_TM_IN = 1024
PROMPT_TM_OUT = 256
MOE_TM = 128
PROMPT_MOE_BLOCK = 128
SAMPLE_MOE_BLOCK = 8


def _kv_rows(rows, b, s):
    return rows.reshape(b, s, 2, NSA_KV_HEADS, HEAD_DIM)


def _to_qrows(x, b, s):
    x = x.reshape(b, s, NSA_KV_HEADS, NSA_GROUP, HEAD_DIM).transpose(0, 2, 3, 1, 4)
    x = jnp.pad(x, ((0, 0), (0, 0), (0, 0), (0, QROWS - s), (0, 0)))
    return x.reshape(b, NSA_KV_HEADS, NQ, HEAD_DIM)


def _from_qrows(o, b, s):
    o = o.reshape(b, NSA_KV_HEADS, NSA_GROUP, QROWS, HEAD_DIM)[:, :, :, :s]
    return o.transpose(0, 3, 1, 2, 4).reshape(b * s, NSA_HEADS * HEAD_DIM)


def _new_rows(proj, col, b, s):
    lo = col * HEAD_DIM
    x = proj[:, lo:lo + NSA_KV_HEADS * HEAD_DIM].reshape(b, s, NSA_KV_HEADS, HEAD_DIM)
    return jnp.pad(x.transpose(0, 2, 1, 3), ((0, 0), (0, 0), (0, LANE - s), (0, 0)))


def _sample_mixer(proj, g_b, pools, win_state, page_table, cmp_w, l, b, s, q0):
    assert MOBA_KV_HEADS == NSA_KV_HEADS and MOBA_GROUP == NSA_GROUP and s <= QROWS
    assert q0 % CMP_STRIDE == 0 and s < CMP_STRIDE and q0 % MOBA_BLOCK == 0
    pool_moba, pool_cmp, pool_slc = pools
    nkh = NSA_KV_HEADS
    q_a = _to_qrows(proj[:, :MOBA_HEADS * HEAD_DIM], b, s)
    q_b = _to_qrows(proj[:, COL_QB * HEAD_DIM:COL_CMP * HEAD_DIM], b, s)
    o_a = _moba_sample(q_a, _new_rows(proj, COL_KA, b, s), _new_rows(proj, COL_VA, b, s),
                       pool_moba, page_table, l, s)
    kvc = _compress_sample(pool_cmp, page_table, l, *cmp_w)
    o_c, sel = _nsa_cmp_sample(q_b, kvc, q0, s)
    g3 = g_b[:, :GB_COLS].reshape(b, s, 3, nkh, NSA_GROUP).transpose(0, 3, 4, 1, 2)
    g3 = jnp.pad(g3, ((0, 0), (0, 0), (0, 0), (0, QROWS - s), (0, LANE - 3)))
    o_b = _nsa_slc_sample(q_b, sel, o_c, g3.reshape(b, nkh, NQ, LANE),
                          _new_rows(proj, COL_SLC, b, s), _new_rows(proj, COL_SLC + nkh, b, s),
                          _new_rows(proj, COL_WIN, b, s), _new_rows(proj, COL_WIN + nkh, b, s),
                          win_state, pool_slc, page_table, l, s)
    return _from_qrows(o_a, b, s), _from_qrows(o_b, b, s)


def _moe_block(x, h, logits, g2, norm_final, w_gate, w_up, w_down, l, bm, tm, rows_per_mod,
               final_norm):
    blk_expert, seg_start, order, n_used, w_tok, d0, d1 = _route(logits, bm)
    yb = _moe_ffn(h, blk_expert, seg_start, order, n_used, w_gate, w_up, w_down, l, bm)
    return _moe_combine(x, g2, w_tok, norm_final, yb, d0, d1, tm, rows_per_mod, final_norm)


def kernel(x_prompt, x_sample, cache_moba_kv, cache_cmp_kv, cache_slc_kv, state_win_kv,
           page_table, c_prompt, c_sample, w_in, w_pa, w_pb, w_out, cmp_pos, cmp_w1, cmp_b1,
           cmp_w2, cmp_b2, norm_attn, norm_ffn, norm_final, w_ada, b_ada, w_rg, b_rg, w_re,
           b_re, w_gate, w_up, w_down):
    bp, sp, d = x_prompt.shape
    bs, ss, _ = x_sample.shape
    tp, ts = bp * sp, bs * ss
    past_len = page_table.shape[1] * cache_moba_kv.shape[2]
    xp = x_prompt.reshape(tp, d)
    xs = x_sample.reshape(ts, d)
    cos_p, sin_p = _rope_tables(jnp.tile(jnp.arange(sp), bp))
    cos_s, sin_s = _rope_tables(jnp.tile(past_len + jnp.arange(ss), bs))
    n_c = _round_up(bp + bs, 8)
    c_all = jnp.pad(jnp.concatenate([c_prompt, c_sample], axis=0), ((0, n_c - bp - bs), (0, 0)))
    n_pool = cache_moba_kv.shape[1]
    pools = tuple(c.reshape(DEPTH, n_pool, PAGE_ROWS, HEAD_DIM)
                  for c in (cache_moba_kv, cache_cmp_kv, cache_slc_kv))
    n_win = state_win_kv.shape[2]
    win_view = state_win_kv.reshape(DEPTH, bs, n_win * 2 * NSA_KV_HEADS, HEAD_DIM)
    rows_p, rows_s = [], []
    for l in range(DEPTH):
        last = l == DEPTH - 1
        ada = _ada(c_all, w_ada, b_ada, l).reshape(n_c, 6, d)
        mod_p = [ada[:bp, k].reshape(bp, 1, d) for k in range(6)]
        mod_s = [jnp.repeat(ada[bp:bp + bs, k], ss, axis=0).reshape(1, ts, d) for k in range(6)]
        w_main = jnp.concatenate([w_in[l][:, :QKV_COLS], w_in[l][:, QKV_COLS + GB_COLS:]],
                                 axis=1).astype(BF16)
        w_gb = jnp.pad(w_in[l][:, QKV_COLS:QKV_COLS + GB_COLS],
                       ((0, 0), (0, LANE - GB_COLS))).astype(BF16)
        w_pa_b, w_pb_b, w_out_b = w_pa[l].astype(BF16), w_pb[l].astype(BF16), w_out[l].astype(BF16)
        w_r = jnp.pad(jnp.concatenate([w_rg[l], w_re[l]], axis=1),
                      ((0, 0), (0, LANE - N_GROUPS - N_EXPERTS)))
        b_r = jnp.pad(jnp.concatenate([b_rg[l], b_re[l]]),
                      (0, LANE - N_GROUPS - N_EXPERTS)).reshape(1, LANE)

        sh1, sc1, g1, sh2, sc2, g2 = mod_p
        proj, gates, g_b, kv_new = _in_proj(xp, norm_attn[l], sc1, sh1, cos_p, sin_p, w_main, w_gb,
                                            PROMPT_TM_IN, sp)
        proj3 = proj.reshape(bp, sp, QKV_COLS)
        o_a = _moba_prompt_t(proj3)
        kvc = _compress_prompt(proj3, cmp_pos[l], cmp_w1[l], cmp_b1[l], cmp_w2[l], cmp_b2[l])
        o_b = _nsa_prompt_t(proj3, kvc, g_b.reshape(bp, sp, LANE))
        xp, h2, logits = _out_proj(o_a.reshape(tp, -1), o_b.reshape(tp, -1), gates, xp, g1,
                                   norm_ffn[l], sc2, sh2, w_pa_b, w_pb_b, w_out_b, w_r, b_r,
                                   PROMPT_TM_OUT, sp)
        xp = _moe_block(xp, h2, logits, g2, norm_final, w_gate, w_up, w_down, l,
                        PROMPT_MOE_BLOCK, MOE_TM, sp, last)
        new_moba, new_cmp, new_slc, new_win = (_kv_rows(r, bp, sp) for r in kv_new)
        rows_p.append((new_moba, new_cmp, new_slc, new_win[:, sp - min(WINDOW, sp):]))

        sh1, sc1, g1, sh2, sc2, g2 = mod_s
        proj, gates, g_b, kv_new = _in_proj(xs, norm_attn[l], sc1, sh1, cos_s, sin_s, w_main, w_gb,
                                            ts, ss)
        cmp_w = (cmp_pos[l], cmp_w1[l], cmp_b1[l], cmp_w2[l], cmp_b2[l])
        o_a, o_b = _sample_mixer(proj, g_b, pools, win_view, page_table, cmp_w, l, bs, ss, past_len)
        xs, h2, logits = _out_proj(o_a, o_b, gates, xs, g1, norm_ffn[l], sc2, sh2,
                                   w_pa_b, w_pb_b, w_out_b, w_r, b_r, ts, ss)
        xs = _moe_block(xs, h2, logits, g2, norm_final, w_gate, w_up, w_down, l,
                        SAMPLE_MOE_BLOCK, ts, ss, last)
        new_moba, new_cmp, new_slc, new_win = (_kv_rows(r, bs, ss) for r in kv_new)
        win_rows = jnp.concatenate([state_win_kv[l], new_win], axis=1)
        rows_s.append((new_moba, new_cmp, new_slc,
                       win_rows[:, win_rows.shape[1] - min(WINDOW, win_rows.shape[1]):]))

    return (xp.reshape(bp, sp, d), xs.reshape(bs, ss, d),
            jnp.stack([r[0] for r in rows_p]), jnp.stack([r[0] for r in rows_s]),
            jnp.stack([r[1] for r in rows_p]), jnp.stack([r[1] for r in rows_s]),
            jnp.stack([r[2] for r in rows_p]), jnp.stack([r[2] for r in rows_s]),
            jnp.stack([r[3] for r in rows_p]), jnp.stack([r[3] for r in rows_s]))
```
